```python
import math
import jax, jax.numpy as jnp
from jax import lax
import numpy as np

D_MODEL = 1024
BATCH = 8
SEQ = 4096
DEPTH = 1

GRID_W = 64
CTX_LEN = 256
NORM_EPS = 1e-6

N_HEADS = 8
HEAD_DIM = 64
V_DIM = 2 * HEAD_DIM
ATTN_QK_W = N_HEADS * 2 * HEAD_DIM
ATTN_V_W = N_HEADS * V_DIM
ROPE_AXIS_DIM = HEAD_DIM // 2
ROPE_THETA = 10000.0
Q_BLOCK = 128

HYENA_W = 1024
HYENA_ORDER = 2
FILTER_EMB = 33
FILTER_BANDS = (FILTER_EMB - 1) // 2
FILTER_HIDDEN = 64
DECAY_TARGET = 1e-2
FAST_DECAY_PCT = 0.3
SLOW_DECAY_PCT = 1.5

K_OFF = 0
V_OFF = K_OFF + ATTN_QK_W
Q_OFF = V_OFF + ATTN_V_W
HY_OFF = Q_OFF + ATTN_QK_W
GATE_OFF = HY_OFF + 3 * HYENA_W
IN_W = GATE_OFF + 2 * D_MODEL

N_EXPERTS = 64
N_GROUPS = 8
TOPK_GROUPS = 4
TOP_K = 8
EXPERT_HIDDEN = 256
SHARED_HIDDEN = 256
ROUTED_SCALE = 2.5
MOE_BLOCK = 128

kernel_name = "hybrid_diffattn_hyena_moe_dit"


def rmsnorm(x, g):
    xf = x.astype(jnp.float32)
    y = xf * lax.rsqrt(jnp.mean(xf * xf, axis=-1, keepdims=True) + NORM_EPS)
    return (y * g.astype(jnp.float32)).astype(x.dtype)


def adaln(cond, w, b, n_chunks):
    m = jax.nn.silu(cond) @ w[:, :n_chunks * D_MODEL] + b[:n_chunks * D_MODEL]
    return jnp.split(m[..., None, :], n_chunks, axis=-1)


def modulate(h, shift, scale):
    return h * (1 + scale) + shift


def rope_2d(rows):
    f32 = jnp.float32
    row = jnp.repeat(jnp.arange(rows), GRID_W).astype(f32)
    col = jnp.tile(jnp.arange(GRID_W), rows).astype(f32)
    inv = ROPE_THETA ** (-jnp.arange(0, ROPE_AXIS_DIM, 2, dtype=f32) / ROPE_AXIS_DIM)
    ang = jnp.stack([row[:, None] * inv, col[:, None] * inv], axis=1)
    return jnp.cos(ang), jnp.sin(ang)


def apply_rope_2d(x, cos, sin):
    L = x.shape[1]
    xa = x.reshape(*x.shape[:-1], 2, 2, ROPE_AXIS_DIM // 2)
    x1, x2 = xa[..., 0, :], xa[..., 1, :]
    c = cos[None, :, None, None]
    s = sin[None, :, None, None]
    out = jnp.stack([x1 * c - x2 * s, x2 * c + x1 * s], axis=-2)
    return out.reshape(x.shape).astype(x.dtype)


def qk_heads(p):
    return p.reshape(*p.shape[:-1], N_HEADS, 2, HEAD_DIM)


def v_heads(p):
    return p.reshape(*p.shape[:-1], N_HEADS, V_DIM)


def diff_attend(q, k, v, lam):
    s = jnp.einsum('bqhmd,bkhmd->bhmqk', q, k).astype(jnp.float32) * (HEAD_DIM ** -0.5)
    p = jax.nn.softmax(s, axis=-1)
    a = p[:, :, 0] - lam * p[:, :, 1]
    return jnp.einsum('bhqk,bkhe->bqhe', a.astype(v.dtype), v)


def diff_attention_blocked(q, k_all, v_all, lam):
    B, L = q.shape[:2]
    nb = L // Q_BLOCK
    qb = jnp.moveaxis(q.reshape(B, nb, Q_BLOCK, *q.shape[2:]), 1, 0)
    o = lax.map(lambda qq: diff_attend(qq, k_all, v_all, lam), qb)
    return jnp.moveaxis(o, 0, 1).reshape(B, L, N_HEADS, V_DIM)


def diff_sub_norm(o, g, lam_init):
    o = rmsnorm(o, g) * (1 - lam_init)
    return o.reshape(*o.shape[:-2], N_HEADS * V_DIM)


def implicit_filters(n, w1, b1, freq, w2, b2, w3):
    f32 = jnp.float32
    pos = jnp.arange(n, dtype=f32)[:, None]
    t = jnp.linspace(0.0, 1.0, n, dtype=f32)[:, None]
    w = 2 * math.pi * pos / n
    bands = jnp.linspace(1e-4, FILTER_BANDS - 1, FILTER_BANDS, dtype=f32)
    z = jnp.concatenate([t, jnp.cos(bands * w), -jnp.sin(bands * w)], axis=-1)
    fr = freq.astype(f32)
    hdn = jnp.sin(fr * (z @ w1.astype(f32) + b1.astype(f32)))
    hdn = jnp.sin(fr * (hdn @ w2.astype(f32) + b2.astype(f32)))
    h = (hdn @ w3.astype(f32)).reshape(n, HYENA_ORDER, 2, HYENA_W)
    deltas = jnp.abs(jnp.linspace(math.log(DECAY_TARGET) / SLOW_DECAY_PCT,
                                  math.log(DECAY_TARGET) / FAST_DECAY_PCT, HYENA_W, dtype=f32))
    h = h * jnp.exp(-t[:, :, None, None] * deltas)
    kern = jnp.concatenate([h[:, :, 0], jnp.zeros((1, HYENA_ORDER, HYENA_W), f32), h[:0:-1, :, 1]], axis=0)
    return kern / jnp.sum(jnp.abs(kern), axis=0, keepdims=True)


def long_conv(z, kern, bias):
    n = z.shape[1]
    zf32 = z.astype(jnp.float32)
    zf = jnp.fft.rfft(zf32, n=2 * n, axis=1)
    kf = jnp.fft.rfft(kern, axis=0)
    y = jnp.fft.irfft(zf * kf[None], n=2 * n, axis=1)[:, :n]
    return (y + zf32 * bias.astype(jnp.float32)).astype(z.dtype)


def hyena(u, conv_w, conv_b, filt, hy_bias):
    n = u.shape[1]
    up = jnp.pad(u, ((0, 0), (1, 1), (0, 0)))
    u = up[:, :-2] * conv_w[0] + up[:, 1:-1] * conv_w[1] + up[:, 2:] * conv_w[2] + conv_b
    v, x1, x2 = jnp.split(u, 3, axis=-1)
    kern = implicit_filters(n, *filt)
    z = x1 * long_conv(v, kern[:, 0], hy_bias[0])
    return x2 * long_conv(z, kern[:, 1], hy_bias[1])


def merge_branches(attn, hy, gates, w_pa, w_ph, w_out):
    g_a, g_h = jnp.split(gates, 2, axis=-1)
    y = jax.nn.sigmoid(g_a) * (attn @ w_pa) + jax.nn.sigmoid(g_h) * (hy @ w_ph)
    return y @ w_out


def swiglu(t, w_g, w_u, w_d):
    return (jax.nn.silu(t @ w_g) * (t @ w_u)) @ w_d


def moe(h, router_w, router_bias, w_g, w_u, w_d, s_g, s_u, s_d):
    B_, L_, D = h.shape
    t = h.reshape(-1, D)
    N = t.shape[0]
    scores = jax.nn.sigmoid((t @ router_w).astype(jnp.float32))
    choice = scores + router_bias.astype(jnp.float32)
    grp = choice.reshape(N, N_GROUPS, N_EXPERTS // N_GROUPS)
    grp_score = lax.top_k(grp, 2)[0].sum(-1)
    _, gidx = lax.top_k(grp_score, TOPK_GROUPS)
    gmask = jax.nn.one_hot(gidx, N_GROUPS, dtype=jnp.float32).sum(1) > 0
    masked = jnp.where(jnp.repeat(gmask, N_EXPERTS // N_GROUPS, axis=1), choice, -jnp.inf)
    _, eidx = lax.top_k(masked, TOP_K)
    wsel = jnp.take_along_axis(scores, eidx, axis=1)
    wsel = wsel / jnp.sum(wsel, axis=-1, keepdims=True) * ROUTED_SCALE
    NK = N * TOP_K
    flat_e = eidx.reshape(-1)
    flat_tok = jnp.repeat(jnp.arange(N, dtype=jnp.int32), TOP_K)
    flat_w = wsel.reshape(-1)
    order = jnp.argsort(flat_e)
    sorted_e = flat_e[order]
    counts = jnp.zeros((N_EXPERTS,), jnp.int32).at[flat_e].add(1)
    padded = (counts + MOE_BLOCK - 1) // MOE_BLOCK * MOE_BLOCK
    pad_end = jnp.cumsum(padded)
    pad_start = pad_end - padded
    start = jnp.cumsum(counts) - counts
    dest = pad_start[sorted_e] + jnp.arange(NK, dtype=jnp.int32) - start[sorted_e]
    NB = -(-(NK + N_EXPERTS * (MOE_BLOCK - 1)) // MOE_BLOCK)
    P = NB * MOE_BLOCK
    row_tok = jnp.full((P,), N, jnp.int32).at[dest].set(flat_tok[order])
    row_w = jnp.zeros((P,), jnp.float32).at[dest].set(flat_w[order])
    blk_e = jnp.minimum(jnp.searchsorted(pad_end, jnp.arange(NB, dtype=jnp.int32) * MOE_BLOCK, side='right'),
                        N_EXPERTS - 1)
    t_pad = jnp.concatenate([t, jnp.zeros((1, D), t.dtype)], axis=0)

    def expert_block(args):
        tok, e = args
        xb = jnp.take(t_pad, tok, axis=0)
        return swiglu(xb, w_g[e], w_u[e], w_d[e])

    yb = lax.map(expert_block, (row_tok.reshape(NB, MOE_BLOCK), blk_e))
    routed = jnp.zeros((N + 1, D), jnp.float32).at[row_tok].add(
        yb.reshape(P, D).astype(jnp.float32) * row_w[:, None])[:N]
    out = routed.astype(t.dtype) + swiglu(t, s_g, s_u, s_d)
    return out.reshape(B_, L_, D)


def setup_inputs(seed: int = 0) -> dict:
    key = jax.random.key(seed)
    ks = jax.random.split(key, 40)
    f32 = jnp.float32
    D, Lr = D_MODEL, DEPTH

    def nrm(i, shape, scale):
        return jax.random.normal(ks[i], shape, f32) * scale

    return {
        "x": nrm(0, (BATCH, SEQ, D), 1.0),
        "c": nrm(1, (BATCH, D), 1.0),
        "ctx": nrm(2, (BATCH, CTX_LEN, D), 1.0),
        "c_ctx": nrm(3, (D,), 1.0),
        "ada_w": nrm(4, (Lr, D, 6 * D), 0.5 * D ** -0.5),
        "ada_b": nrm(5, (Lr, 6 * D), 0.01),
        "norm1_g": 1.0 + nrm(6, (Lr, D), 0.02),
        "norm2_g": 1.0 + nrm(7, (Lr, D), 0.02),
        "w_in": nrm(8, (Lr, D, IN_W), D ** -0.5),
        "lam_q1": nrm(9, (Lr, HEAD_DIM), 0.1),
        "lam_k1": nrm(10, (Lr, HEAD_DIM), 0.1),
        "lam_q2": nrm(11, (Lr, HEAD_DIM), 0.1),
        "lam_k2": nrm(12, (Lr, HEAD_DIM), 0.1),
        "subln_g": 1.0 + nrm(13, (Lr, V_DIM), 0.02),
        "hy_conv_w": nrm(14, (Lr, 3, 3 * HYENA_W), 3 ** -0.5),
        "hy_conv_b": nrm(15, (Lr, 3 * HYENA_W), 0.01),
        "filt_w1": nrm(16, (Lr, FILTER_EMB, FILTER_HIDDEN), FILTER_EMB ** -0.5),
        "filt_b1": nrm(17, (Lr, FILTER_HIDDEN), 0.1),
        "filt_freq": 1.0 + nrm(18, (Lr, FILTER_HIDDEN), 0.02),
        "filt_w2": nrm(19, (Lr, FILTER_HIDDEN, FILTER_HIDDEN), FILTER_HIDDEN ** -0.5),
        "filt_b2": nrm(20, (Lr, FILTER_HIDDEN), 0.1),
        "filt_w3": nrm(21, (Lr, FILTER_HIDDEN, HYENA_ORDER * 2 * HYENA_W), FILTER_HIDDEN ** -0.5),
        "hy_bias": nrm(22, (Lr, HYENA_ORDER, HYENA_W), 1.0),
        "w_branch_attn": nrm(23, (Lr, ATTN_V_W, D), ATTN_V_W ** -0.5),
        "w_branch_hyena": nrm(24, (Lr, HYENA_W, D), HYENA_W ** -0.5),
        "w_out": nrm(25, (Lr, D, D), D ** -0.5),
        "router_w": nrm(26, (Lr, D, N_EXPERTS), D ** -0.5),
        "router_bias": nrm(27, (Lr, N_EXPERTS), 0.01),
        "exp_w_gate": nrm(28, (Lr, N_EXPERTS, D, EXPERT_HIDDEN), D ** -0.5),
        "exp_w_up": nrm(29, (Lr, N_EXPERTS, D, EXPERT_HIDDEN), D ** -0.5),
        "exp_w_down": nrm(30, (Lr, N_EXPERTS, EXPERT_HIDDEN, D), EXPERT_HIDDEN ** -0.5),
        "shared_w_gate": nrm(31, (Lr, D, SHARED_HIDDEN), D ** -0.5),
        "shared_w_up": nrm(32, (Lr, D, SHARED_HIDDEN), D ** -0.5),
        "shared_w_down": nrm(33, (Lr, SHARED_HIDDEN, D), SHARED_HIDDEN ** -0.5),
        "final_norm_g": 1.0 + nrm(34, (D,), 0.02),
    }


def reference(x, c, ctx, c_ctx, ada_w, ada_b, norm1_g, norm2_g, w_in, lam_q1, lam_k1, lam_q2, lam_k2,
              subln_g, hy_conv_w, hy_conv_b, filt_w1, filt_b1, filt_freq, filt_w2, filt_b2, filt_w3, hy_bias,
              w_branch_attn, w_branch_hyena, w_out, router_w, router_bias, exp_w_gate, exp_w_up, exp_w_down,
              shared_w_gate, shared_w_up, shared_w_down, final_norm_g):
    f32 = jnp.float32
    ROWS = x.shape[1] // GRID_W
    cos, sin = rope_2d(ROWS)
    for layer in range(DEPTH):
        last = layer == DEPTH - 1
        lam_init = 0.8 - 0.6 * math.exp(-0.3 * layer)
        lam = (jnp.exp(jnp.sum(lam_q1[layer].astype(f32) * lam_k1[layer].astype(f32)))
               - jnp.exp(jnp.sum(lam_q2[layer].astype(f32) * lam_k2[layer].astype(f32))) + lam_init)
        filt = (filt_w1[layer], filt_b1[layer], filt_freq[layer], filt_w2[layer], filt_b2[layer], filt_w3[layer])
        sh1, sc1, g1, sh2, sc2, g2 = adaln(c, ada_w[layer], ada_b[layer], 6)
        cmod = adaln(c_ctx, ada_w[layer], ada_b[layer], 2 if last else 6)
        w = w_in[layer]

        hx = modulate(rmsnorm(x, norm1_g[layer]), sh1, sc1)
        hc = modulate(rmsnorm(ctx, norm1_g[layer]), cmod[0], cmod[1])
        px = hx @ w
        pc_kv = hc @ w[:, :Q_OFF]
        kc = qk_heads(pc_kv[..., K_OFF:V_OFF])
        vc = v_heads(pc_kv[..., V_OFF:Q_OFF])
        kx = apply_rope_2d(qk_heads(px[..., K_OFF:V_OFF]), cos, sin)
        vx = v_heads(px[..., V_OFF:Q_OFF])
        qx = apply_rope_2d(qk_heads(px[..., Q_OFF:HY_OFF]), cos, sin)
        k_all = jnp.concatenate([kx, kc], axis=1)
        v_all = jnp.concatenate([vx, vc], axis=1)
        ax = diff_sub_norm(diff_attention_blocked(qx, k_all, v_all, lam), subln_g[layer], lam_init)
        yx = hyena(px[..., HY_OFF:GATE_OFF], hy_conv_w[layer], hy_conv_b[layer], filt, hy_bias[layer])
        mix_x = merge_branches(ax, yx, px[..., GATE_OFF:IN_W], w_branch_attn[layer], w_branch_hyena[layer],
                               w_out[layer])
        if not last:
            pc_rest = hc @ w[:, Q_OFF:]
            qc = qk_heads(pc_rest[..., :HY_OFF - Q_OFF])
            ac = diff_sub_norm(diff_attend(qc, kc, vc, lam), subln_g[layer], lam_init)
            yc = hyena(pc_rest[..., HY_OFF - Q_OFF:GATE_OFF - Q_OFF], hy_conv_w[layer], hy_conv_b[layer], filt,
                       hy_bias[layer])
            ctx = ctx + cmod[2] * merge_branches(ac, yc, pc_rest[..., GATE_OFF - Q_OFF:], w_branch_attn[layer],
                                                 w_branch_hyena[layer], w_out[layer])
            hc2 = modulate(rmsnorm(ctx, norm2_g[layer]), cmod[3], cmod[4])
            ctx = ctx + cmod[5] * moe(hc2, router_w[layer], router_bias[layer], exp_w_gate[layer],
                                      exp_w_up[layer], exp_w_down[layer], shared_w_gate[layer],
                                      shared_w_up[layer], shared_w_down[layer])
        x = x + g1 * mix_x

        hx2 = modulate(rmsnorm(x, norm2_g[layer]), sh2, sc2)
        x = x + g2 * moe(hx2, router_w[layer], router_bias[layer], exp_w_gate[layer], exp_w_up[layer],
                         exp_w_down[layer], shared_w_gate[layer], shared_w_up[layer], shared_w_down[layer])
    return rmsnorm(x, final_norm_g)
```

```python
import functools
import math

import jax
import jax.numpy as jnp
from jax import lax
from jax.experimental import pallas as pl
from jax.experimental.pallas import tpu as pltpu

F32 = jnp.float32
BF16 = jnp.bfloat16
HIGHEST = lax.Precision.HIGHEST

NORM_EPS = 1e-6
N_HEADS = 8
HEAD_DIM = 64
V_DIM = 2 * HEAD_DIM
GRID_W = 64
ROPE_THETA = 10000.0
ROPE_AXIS_DIM = HEAD_DIM // 2
HYENA_W = 1024
FILTER_EMB = 33
FILTER_BANDS = (FILTER_EMB - 1) // 2
DECAY_TARGET = 1e-2
FAST_DECAY_PCT = 0.3
SLOW_DECAY_PCT = 1.5
N_EXPERTS = 64
N_GROUPS = 8
TOPK_GROUPS = 4
TOP_K = 8
EXPERT_HIDDEN = 256
ROUTED_SCALE = 2.5
LAM_INIT = 0.8 - 0.6 * math.exp(-0.3 * 0)

DFT_N1 = 32
V7X_VMEM_LIMIT = 56 * 1024 * 1024
LOG2E = 1.4426950408889634


def _cparams(sem):
    return pltpu.CompilerParams(dimension_semantics=sem, vmem_limit_bytes=V7X_VMEM_LIMIT)


def _sigmoid(v):
    return 1.0 / (1.0 + jnp.exp(-v))


def _rms(v):
    return v * lax.rsqrt(jnp.mean(v * v, axis=-1, keepdims=True) + NORM_EPS)


def _adaln_kernel(c_ref, w_ref, b_ref, o_ref):
    c = c_ref[...]
    s = c * _sigmoid(c)
    o_ref[...] = jnp.dot(s, w_ref[...], precision=HIGHEST, preferred_element_type=F32) + b_ref[...]


def _adaln(cc, w, b):
    rows, d = cc.shape
    n = w.shape[1]
    tn = 1536
    return pl.pallas_call(
        _adaln_kernel,
        grid=(n // tn,),
        in_specs=[pl.BlockSpec((rows, d), lambda j: (0, 0)),
                  pl.BlockSpec((d, tn), lambda j: (0, j)),
                  pl.BlockSpec((1, tn), lambda j: (0, j))],
        out_specs=pl.BlockSpec((rows, tn), lambda j: (0, j)),
        out_shape=jax.ShapeDtypeStruct((rows, n), F32),
        compiler_params=_cparams(("arbitrary",)),
        name="adaln",
    )(cc, w, b.reshape(1, n))


def _inproj_kernel(x_ref, xp_ref, xn_ref, sh_ref, sc_ref, g_ref, w_ref, cos_ref, sin_ref, cw_ref, cb_ref,
                   k_ref, v_ref, q_ref, u_ref, gt_ref, pscr, *, tm, nt, qscale):
    i = pl.program_id(1)
    g = g_ref[...]
    sh = sh_ref[...]
    sc = 1.0 + sc_ref[...]

    def norm_mod(xx):
        return (_rms(xx) * g) * sc + sh

    h = norm_mod(x_ref[...])
    hp = jnp.where(i > 0, norm_mod(xp_ref[...]), 0.0)
    hn = jnp.where(i < nt - 1, norm_mod(xn_ref[...]), 0.0)
    hb = h.astype(BF16)
    h_ext = jnp.concatenate([hp, h, hn], axis=0).astype(BF16)

    def proj(lhs, c0):
        return jnp.dot(lhs, w_ref[:, c0:c0 + 1024], preferred_element_type=F32)

    cosv = cos_ref[...]
    sinv = sin_ref[...]
    lane = lax.broadcasted_iota(jnp.int32, (1, V_DIM), 1)
    first = (lane % ROPE_AXIS_DIM) < (ROPE_AXIS_DIM // 2)

    def rope_store(a, o_ref, scale):
        for hh in range(N_HEADS):
            s = a[:, hh * V_DIM:(hh + 1) * V_DIM]
            partner = jnp.where(first, pltpu.roll(s, V_DIM - ROPE_AXIS_DIM // 2, 1),
                                pltpu.roll(s, ROPE_AXIS_DIM // 2, 1))
            r = s * cosv + partner * sinv
            if scale != 1.0:
                r = r * scale
            o_ref[:, hh * V_DIM:(hh + 1) * V_DIM] = r.astype(BF16)

    rope_store(proj(hb, 0), k_ref, 1.0)
    v_ref[...] = proj(hb, 1024).astype(BF16)
    rope_store(proj(hb, 2048), q_ref, qscale)
    for j in range(3):
        pscr[...] = proj(h_ext, 3072 + j * 1024)
        cw = cw_ref[:, j * 1024:(j + 1) * 1024]
        cb = cb_ref[:, j * 1024:(j + 1) * 1024]
        u = (pscr[pl.ds(7, tm), :] * cw[0:1] + pscr[pl.ds(8, tm), :] * cw[1:2]
             + pscr[pl.ds(9, tm), :] * cw[2:3] + cb)
        u_ref[:, j * 1024:(j + 1) * 1024] = u.astype(BF16)
    for j in range(2):
        gt_ref[:, j * 1024:(j + 1) * 1024] = proj(hb, 6144 + j * 1024).astype(BF16)


def _inproj(x, shift, scale, g, w_bf, cos_t, sin_t, conv_w, conv_b, tm):
    B, L, D = x.shape
    nt = L // tm
    nb8 = L // 8
    qscale = (HEAD_DIM ** -0.5) * LOG2E
    kern = functools.partial(_inproj_kernel, tm=tm, nt=nt, qscale=qscale)
    row = lambda b, i: (b, i, 0)
    per_b = lambda b, i: (b, 0, 0)
    const = lambda b, i: (0, 0)
    outs = pl.pallas_call(
        kern,
        grid=(B, nt),
        in_specs=[
            pl.BlockSpec((None, tm, D), row),
            pl.BlockSpec((None, 8, D), lambda b, i: (b, jnp.maximum(i * (tm // 8) - 1, 0), 0)),
            pl.BlockSpec((None, 8, D), lambda b, i: (b, jnp.minimum((i + 1) * (tm // 8), nb8 - 1), 0)),
            pl.BlockSpec((None, 1, D), per_b),
            pl.BlockSpec((None, 1, D), per_b),
            pl.BlockSpec((1, D), const),
            pl.BlockSpec((D, 8192), const, pipeline_mode=pl.Buffered(1)),
            pl.BlockSpec((tm, V_DIM), lambda b, i: (i, 0)),
            pl.BlockSpec((tm, V_DIM), lambda b, i: (i, 0)),
            pl.BlockSpec((3, 3 * HYENA_W), const),
            pl.BlockSpec((1, 3 * HYENA_W), const),
        ],
        out_specs=[
            pl.BlockSpec((None, tm, 1024), row),
            pl.BlockSpec((None, tm, 1024), row),
            pl.BlockSpec((None, tm, 1024), row),
            pl.BlockSpec((None, tm, 3 * HYENA_W), row),
            pl.BlockSpec((None, tm, 2048), row),
        ],
        out_shape=[
            jax.ShapeDtypeStruct((B, L, 1024), BF16),
            jax.ShapeDtypeStruct((B, L, 1024), BF16),
            jax.ShapeDtypeStruct((B, L, 1024), BF16),
            jax.ShapeDtypeStruct((B, L, 3 * HYENA_W), BF16),
            jax.ShapeDtypeStruct((B, L, 2048), BF16),
        ],
        scratch_shapes=[pltpu.VMEM((tm + 16, 1024), F32)],
        compiler_params=_cparams(("parallel", "arbitrary")),
        name="inproj",
    )(x, x, x, shift, scale, g, w_bf, cos_t, sin_t, conv_w, conv_b)
    return outs


def _ctxproj_kernel(x_ref, sh_ref, sc_ref, g_ref, w_ref, k_ref, v_ref):
    h = ((_rms(x_ref[...]) * g_ref[...]) * (1.0 + sc_ref[...]) + sh_ref[...]).astype(BF16)
    k_ref[...] = jnp.dot(h, w_ref[:, 0:1024], preferred_element_type=F32).astype(BF16)
    v_ref[...] = jnp.dot(h, w_ref[:, 1024:2048], preferred_element_type=F32).astype(BF16)


def _ctxproj(ctx, shift, scale, g, w_bf):
    B, Lc, D = ctx.shape
    return pl.pallas_call(
        _ctxproj_kernel,
        grid=(B,),
        in_specs=[pl.BlockSpec((None, Lc, D), lambda b: (b, 0, 0)),
                  pl.BlockSpec((1, D), lambda b: (0, 0)),
                  pl.BlockSpec((1, D), lambda b: (0, 0)),
                  pl.BlockSpec((1, D), lambda b: (0, 0)),
                  pl.BlockSpec((D, 2048), lambda b: (0, 0))],
        out_specs=[pl.BlockSpec((None, Lc, 1024), lambda b: (b, 0, 0)),
                   pl.BlockSpec((None, Lc, 1024), lambda b: (b, 0, 0))],
        out_shape=[jax.ShapeDtypeStruct((B, Lc, 1024), BF16),
                   jax.ShapeDtypeStruct((B, Lc, 1024), BF16)],
        compiler_params=_cparams(("arbitrary",)),
        name="ctxproj",
    )(ctx, shift, scale, g, w_bf)


def _attn_kernel(lam_ref, q_ref, k_ref, v_ref, g_ref, o_ref, *, tk, nk):
    q = q_ref[...]
    tq = q.shape[0]
    lane = lax.broadcasted_iota(jnp.int32, (1, V_DIM), 1)
    zero = jnp.zeros_like(q)
    qs = (jnp.where(lane < HEAD_DIM, q, zero), jnp.where(lane >= HEAD_DIM, q, zero))

    def body(j, carry):
        kk = k_ref[pl.ds(pl.multiple_of(j * tk, tk), tk), :]
        vv = v_ref[pl.ds(pl.multiple_of(j * tk, tk), tk), :]
        new = []
        for m in range(2):
            mx, l, acc = carry[m]
            s = lax.dot_general(qs[m], kk, (((1,), (1,)), ((), ())), preferred_element_type=F32)
            mn = jnp.maximum(mx, jnp.max(s, axis=-1, keepdims=True))
            alpha = jnp.exp2(mx - mn)
            p = jnp.exp2(s - mn)
            l = alpha * l + jnp.sum(p, axis=-1, keepdims=True)
            acc = alpha * acc + jnp.dot(p.astype(BF16), vv, preferred_element_type=F32)
            new.append((mn, l, acc))
        return tuple(new)

    init = tuple((jnp.full((tq, 1), -1e30, F32), jnp.zeros((tq, 1), F32), jnp.zeros((tq, V_DIM), F32))
                 for _ in range(2))
    (m0, l0, a0), (m1, l1, a1) = lax.fori_loop(0, nk, body, init)
    lam = lam_ref[0, 0]
    o = a0 / l0 - lam * (a1 / l1)
    o = _rms(o) * g_ref[...] * (1.0 - LAM_INIT)
    o_ref[...] = o.astype(BF16)


def _attention(lam, q, k_all, v_all, subln_g, tq, tk):
    B, L, _ = q.shape
    Lk = k_all.shape[1]
    kern = functools.partial(_attn_kernel, tk=tk, nk=Lk // tk)
    return pl.pallas_call(
        kern,
        grid=(B, N_HEADS, L // tq),
        in_specs=[
            pl.BlockSpec(memory_space=pltpu.SMEM),
            pl.BlockSpec((None, tq, V_DIM), lambda b, h, i: (b, i, h)),
            pl.BlockSpec((None, Lk, V_DIM), lambda b, h, i: (b, 0, h)),
            pl.BlockSpec((None, Lk, V_DIM), lambda b, h, i: (b, 0, h)),
            pl.BlockSpec((1, V_DIM), lambda b, h, i: (0, 0)),
        ],
        out_specs=pl.BlockSpec((None, tq, V_DIM), lambda b, h, i: (b, i, h)),
        out_shape=jax.ShapeDtypeStruct((B, L, N_HEADS * V_DIM), BF16),
        compiler_params=_cparams(("parallel", "parallel", "arbitrary")),
        name="diffattn",
    )(lam, q, k_all, v_all, subln_g)


def _dft_tables(n2):
    n1 = DFT_N1
    N = n1 * n2
    half = n2 // 2
    s1 = jnp.arange(n1, dtype=jnp.int32)[:, None, None]
    f2 = jnp.arange(n2, dtype=jnp.int32)[None, :, None]
    s2 = jnp.arange(half, dtype=jnp.int32)[None, None, :]
    ang = ((n1 * s2 * f2 + s1 * f2) % N).astype(F32) * (2.0 * math.pi / N)
    c, s = jnp.cos(ang), jnp.sin(ang)
    fwd = jnp.concatenate([jnp.concatenate([c, s], -1), jnp.concatenate([-s, c], -1)], axis=1)
    ct, st = jnp.swapaxes(c, 1, 2), jnp.swapaxes(s, 1, 2)
    inv = jnp.concatenate([jnp.concatenate([ct, -st], -1), jnp.concatenate([st, ct], -1)], axis=1) / N
    a = jnp.arange(n1, dtype=jnp.int32)
    ang2 = ((a[:, None] * a[None, :]) % n1).astype(F32) * (2.0 * math.pi / n1)
    c2, s2m = jnp.cos(ang2), jnp.sin(ang2)
    g_f = jnp.concatenate([jnp.concatenate([c2, s2m], -1), jnp.concatenate([-s2m, c2], -1)], axis=0)
    g_i = jnp.concatenate([jnp.concatenate([c2, -s2m], -1), jnp.concatenate([s2m, c2], -1)], axis=0)
    return fwd.astype(BF16), inv.astype(BF16), g_f.astype(BF16), g_i.astype(BF16)


def _filter_features(n):
    pos = jnp.arange(n, dtype=F32)[:, None]
    t = jnp.linspace(0.0, 1.0, n, dtype=F32)[:, None]
    w = 2 * math.pi * pos / n
    bands = jnp.linspace(1e-4, FILTER_BANDS - 1, FILTER_BANDS, dtype=F32)
    z = jnp.concatenate([t, jnp.cos(bands * w), -jnp.sin(bands * w)], axis=-1)
    return jnp.pad(z, ((0, 0), (0, 128 - FILTER_EMB)))


def _filt_stage1_kernel(z_ref, w1_ref, b1_ref, fr_ref, w2_ref, b2_ref, w3_ref, dl_ref, f_ref,
                        ofr, ofi, obr, obi, sum_ref, *, n2):
    s1 = pl.program_id(0)
    z = z_ref[...]
    fr = fr_ref[...]
    hd = jnp.sin(fr * (jnp.dot(z, w1_ref[...], precision=HIGHEST, preferred_element_type=F32) + b1_ref[...]))
    hd = jnp.sin(fr * (jnp.dot(hd, w2_ref[...], precision=HIGHEST, preferred_element_type=F32) + b2_ref[...]))
    decay = jnp.exp(-z[:, 0:1] * dl_ref[...])
    rows = lax.broadcasted_iota(jnp.int32, (z.shape[0], 1), 0)
    not_origin = jnp.logical_or(rows > 0, s1 > 0)

    @pl.when(s1 == 0)
    def _():
        sum_ref[...] = jnp.zeros_like(sum_ref)

    fmat = f_ref[...]
    for o in range(2):
        for d, (ore, oim) in enumerate(((ofr, ofi), (obr, obi))):
            col = (o * 2 + d) * HYENA_W
            hh = jnp.dot(hd, w3_ref[:, col:col + HYENA_W], precision=HIGHEST, preferred_element_type=F32) * decay
            if d == 1:
                hh = jnp.where(not_origin, hh, 0.0)
            sum_ref[0:1, col:col + HYENA_W] += jnp.sum(jnp.abs(hh), axis=0, keepdims=True)
            zz = jnp.dot(fmat, hh.astype(BF16), preferred_element_type=F32)
            ore[o] = zz[:n2].astype(BF16)
            oim[o] = zz[n2:].astype(BF16)


def _filt_stage2_kernel(fr_ref, fi_ref, br_ref, bi_ref, inv_ref, g_ref, kr_ref, ki_ref):
    g = g_ref[...]
    bf = jnp.dot(g, jnp.concatenate([fr_ref[...], fi_ref[...]], axis=0), preferred_element_type=F32)
    bb = jnp.dot(g, jnp.concatenate([br_ref[...], bi_ref[...]], axis=0), preferred_element_type=F32)
    n1 = DFT_N1
    inv = inv_ref[...]
    kr_ref[...] = (bf[:n1] + bb[:n1]) * inv
    ki_ref[...] = (bf[n1:] - bb[n1:]) * inv


def _hyena_filters(n, filt, fwd_tab, g_f, n2):
    w1, b1, freq, w2, b2, w3 = filt
    half = n2 // 2
    W = HYENA_W
    z = _filter_features(n).reshape(half, DFT_N1 * 128)
    w1p = jnp.pad(w1.astype(F32), ((0, 128 - FILTER_EMB), (0, 0)))
    deltas = jnp.abs(jnp.linspace(math.log(DECAY_TARGET) / SLOW_DECAY_PCT,
                                  math.log(DECAY_TARGET) / FAST_DECAY_PCT, W, dtype=F32)).reshape(1, W)
    hid = w1.shape[1]
    const = lambda s: (0, 0)
    spec_o = pl.BlockSpec((2, None, n2, W), lambda s: (0, s, 0, 0))
    shp_o = jax.ShapeDtypeStruct((2, DFT_N1, n2, W), BF16)
    ofr, ofi, obr, obi, sums = pl.pallas_call(
        functools.partial(_filt_stage1_kernel, n2=n2),
        grid=(DFT_N1,),
        in_specs=[pl.BlockSpec((half, 128), lambda s: (0, s)),
                  pl.BlockSpec((128, hid), const), pl.BlockSpec((1, hid), const), pl.BlockSpec((1, hid), const),
                  pl.BlockSpec((hid, hid), const), pl.BlockSpec((1, hid), const),
                  pl.BlockSpec((hid, 4 * W), const), pl.BlockSpec((1, W), const),
                  pl.BlockSpec((None, 2 * n2, half), lambda s: (s, 0, 0))],
        out_specs=[spec_o, spec_o, spec_o, spec_o, pl.BlockSpec((8, 4 * W), const)],
        out_shape=[shp_o, shp_o, shp_o, shp_o, jax.ShapeDtypeStruct((8, 4 * W), F32)],
        compiler_params=_cparams(("arbitrary",)),
        name="hyena_filter_stage1",
    )(z, w1p, b1.reshape(1, hid), freq.reshape(1, hid), w2, b2.reshape(1, hid), w3, deltas, fwd_tab[:, :, :half])
    tot = sums[0].reshape(2, 2, W).sum(axis=1)
    wl = min(8192, n2 * W)
    inv = jnp.tile((1.0 / tot).reshape(2, 1, W), (1, 1, wl // W))
    flat = lambda a: a.reshape(2, DFT_N1, n2 * W)
    spec_i = pl.BlockSpec((None, DFT_N1, wl), lambda o, j: (o, 0, j))
    kr, ki = pl.pallas_call(
        _filt_stage2_kernel,
        grid=(2, n2 * W // wl),
        in_specs=[spec_i, spec_i, spec_i, spec_i,
                  pl.BlockSpec((None, 1, wl), lambda o, j: (o, 0, 0)),
                  pl.BlockSpec((2 * DFT_N1, 2 * DFT_N1), lambda o, j: (0, 0))],
        out_specs=[spec_i, spec_i],
        out_shape=[jax.ShapeDtypeStruct((2, DFT_N1, n2 * W), F32)] * 2,
        compiler_params=_cparams(("parallel", "parallel")),
        name="hyena_filter_stage2",
    )(flat(ofr), flat(ofi), flat(obr), flat(obi), inv, g_f)
    return kr, ki


def _fwd1_kernel(v_ref, f_ref, ar_ref, ai_ref, *, n2):
    x = v_ref[...]
    x = x.reshape(x.shape[0] * x.shape[1], x.shape[2])
    z = jnp.dot(f_ref[...], x, preferred_element_type=F32)
    ar_ref[...] = z[:n2].astype(BF16)
    ai_ref[...] = z[n2:].astype(BF16)


def _mid_kernel(ar_ref, ai_ref, kr_ref, ki_ref, gf_ref, gi_ref, cr_ref, ci_ref):
    n1 = DFT_N1
    b = jnp.dot(gf_ref[...], jnp.concatenate([ar_ref[...], ai_ref[...]], axis=0), preferred_element_type=F32)
    br, bi = b[:n1], b[n1:]
    kr, ki = kr_ref[...], ki_ref[...]
    y = jnp.concatenate([br * kr - bi * ki, br * ki + bi * kr], axis=0).astype(BF16)
    c = jnp.dot(gi_ref[...], y, preferred_element_type=F32)
    cr_ref[...] = c[:n1].astype(BF16)
    ci_ref[...] = c[n1:].astype(BF16)


def _inv1_gate_fwd1_kernel(cr_ref, ci_ref, fi_ref, f_ref, v_ref, x_ref, bias_ref, z_ref, ar_ref, ai_ref, *, n2):
    cc = jnp.concatenate([cr_ref[...], ci_ref[...]], axis=0)
    y = jnp.dot(fi_ref[...], cc, preferred_element_type=F32)
    shp = v_ref.shape
    v = v_ref[...].reshape(shp[0] * shp[1], shp[2]).astype(F32)
    x = x_ref[...].reshape(shp[0] * shp[1], shp[2]).astype(F32)
    z = (x * (y + v * bias_ref[...])).astype(BF16)
    z_ref[...] = z.reshape(shp)
    zz = jnp.dot(f_ref[...], z, preferred_element_type=F32)
    ar_ref[...] = zz[:n2].astype(BF16)
    ai_ref[...] = zz[n2:].astype(BF16)


def _inv1_gate_kernel(cr_ref, ci_ref, fi_ref, v_ref, x_ref, bias_ref, o_ref):
    cc = jnp.concatenate([cr_ref[...], ci_ref[...]], axis=0)
    y = jnp.dot(fi_ref[...], cc, preferred_element_type=F32)
    shp = v_ref.shape
    v = v_ref[...].reshape(shp[0] * shp[1], shp[2]).astype(F32)
    x = x_ref[...].reshape(shp[0] * shp[1], shp[2]).astype(F32)
    o_ref[...] = (x * (y + v * bias_ref[...])).astype(BF16).reshape(shp)


def _hyena(u, filt, hy_bias):
    B, L, _ = u.shape
    W = HYENA_W
    P = B // 2
    n1 = DFT_N1
    half = L // n1
    n2 = 2 * half
    fwd_tab, inv_tab, g_f, g_i = _dft_tables(n2)
    kr, ki = _hyena_filters(L, filt, fwd_tab, g_f, n2)
    uv = u.reshape(2, P, half, n1 * 3 * W)
    spec_u = lambda cb: pl.BlockSpec((2, None, half, W), lambda p, s: (0, p, 0, s * 3 + cb))
    spec_z = pl.BlockSpec((2, None, half, W), lambda p, s: (0, p, 0, s))
    spec_a = pl.BlockSpec((None, None, n2, W), lambda p, s: (p, s, 0, 0))
    spec_f = pl.BlockSpec((None, 2 * n2, 2 * half), lambda p, s: (s, 0, 0))
    spec_fi = pl.BlockSpec((None, 2 * half, 2 * n2), lambda p, s: (s, 0, 0))
    spec_b = pl.BlockSpec((1, W), lambda p, s: (0, 0))
    shp_a = jax.ShapeDtypeStruct((P, n1, n2, W), BF16)
    shp_z = jax.ShapeDtypeStruct((2, P, half, n1 * W), BF16)
    sem2 = _cparams(("parallel", "arbitrary"))

    ar, ai = pl.pallas_call(
        functools.partial(_fwd1_kernel, n2=n2), grid=(P, n1),
        in_specs=[spec_u(0), spec_f], out_specs=[spec_a, spec_a], out_shape=[shp_a, shp_a],
        compiler_params=sem2, name="hyena_fwd1")(uv, fwd_tab)

    wl = min(8192, n2 * W)
    nj = n2 * W // wl
    spec_m = pl.BlockSpec((None, n1, wl), lambda j, p: (p, 0, j))
    spec_g = pl.BlockSpec((2 * n1, 2 * n1), lambda j, p: (0, 0))
    shp_m = jax.ShapeDtypeStruct((P, n1, n2 * W), BF16)

    def mid(ar, ai, o):
        spec_k = pl.BlockSpec((None, n1, wl), lambda j, p: (o, 0, j))
        cr, ci = pl.pallas_call(
            _mid_kernel, grid=(nj, P),
            in_specs=[spec_m, spec_m, spec_k, spec_k, spec_g, spec_g],
            out_specs=[spec_m, spec_m], out_shape=[shp_m, shp_m],
            compiler_params=_cparams(("parallel", "arbitrary")), name="hyena_mid")(
                ar.reshape(P, n1, n2 * W), ai.reshape(P, n1, n2 * W), kr, ki, g_f, g_i)
        return cr.reshape(P, n1, n2, W), ci.reshape(P, n1, n2, W)

    cr, ci = mid(ar, ai, 0)
    z, ar, ai = pl.pallas_call(
        functools.partial(_inv1_gate_fwd1_kernel, n2=n2), grid=(P, n1),
        in_specs=[spec_a, spec_a, spec_fi, spec_f, spec_u(0), spec_u(1), spec_b],
        out_specs=[spec_z, spec_a, spec_a], out_shape=[shp_z, shp_a, shp_a],
        compiler_params=sem2, name="hyena_inv1_fwd1")(cr, ci, inv_tab, fwd_tab, uv, uv, hy_bias[0:1].astype(F32))
    cr, ci = mid(ar, ai, 1)
    y = pl.pallas_call(
        _inv1_gate_kernel, grid=(P, n1),
        in_specs=[spec_a, spec_a, spec_fi, spec_z, spec_u(2), spec_b],
        out_specs=spec_z, out_shape=shp_z,
        compiler_params=sem2, name="hyena_inv1")(cr, ci, inv_tab, z, uv, hy_bias[1:2].astype(F32))
    return y.reshape(B, L, W)


def _merge_kernel(a_ref, h_ref, gt_ref, x_ref, g1_ref, sh_ref, sc_ref, n2g_ref, wpa_ref, wph_ref, wo_ref, rw_ref,
                  x1_ref, hx_ref, lg_ref):
    ya = jnp.dot(a_ref[...], wpa_ref[...], preferred_element_type=F32)
    yh = jnp.dot(h_ref[...], wph_ref[...], preferred_element_type=F32)
    ga = gt_ref[:, 0:1024].astype(F32)
    gh = gt_ref[:, 1024:2048].astype(F32)
    y = _sigmoid(ga) * ya + _sigmoid(gh) * yh
    mix = jnp.dot(y.astype(BF16), wo_ref[...], preferred_element_type=F32)
    x1 = x_ref[...] + g1_ref[...] * mix
    x1_ref[...] = x1
    hx = (_rms(x1) * n2g_ref[...]) * (1.0 + sc_ref[...]) + sh_ref[...]
    hx_ref[...] = hx.astype(BF16)
    lg_ref[...] = lax.dot_general(rw_ref[...], hx, (((1,), (1,)), ((), ())), precision=HIGHEST,
                                  preferred_element_type=F32)


def _merge(attn, hy, gates, x, g1, sh2, sc2, n2g, wpa, wph, wo, rwT, tm):
    B, L, D = x.shape
    nt = L // tm
    row = lambda b, i: (b, i, 0)
    per_b = lambda b, i: (b, 0, 0)
    const = lambda b, i: (0, 0)
    return pl.pallas_call(
        _merge_kernel,
        grid=(B, nt),
        in_specs=[pl.BlockSpec((None, tm, 1024), row), pl.BlockSpec((None, tm, 1024), row),
                  pl.BlockSpec((None, tm, 2048), row), pl.BlockSpec((None, tm, D), row),
                  pl.BlockSpec((None, 1, D), per_b), pl.BlockSpec((None, 1, D), per_b),
                  pl.BlockSpec((None, 1, D), per_b), pl.BlockSpec((1, D), const),
                  pl.BlockSpec((1024, D), const), pl.BlockSpec((1024, D), const), pl.BlockSpec((D, D), const),
                  pl.BlockSpec((N_EXPERTS, D), const)],
        out_specs=[pl.BlockSpec((None, tm, D), row), pl.BlockSpec((None, tm, D), row),
                   pl.BlockSpec((N_EXPERTS, tm), lambda b, i: (0, b * nt + i))],
        out_shape=[jax.ShapeDtypeStruct((B, L, D), F32), jax.ShapeDtypeStruct((B, L, D), BF16),
                   jax.ShapeDtypeStruct((N_EXPERTS, B * L), F32)],
        compiler_params=_cparams(("parallel", "arbitrary")),
        name="merge",
    )(attn, hy, gates, x, g1, sh2, sc2, n2g, wpa, wph, wo, rwT)


def _router_kernel(lg_ref, rb_ref, gt_ref):
    scores = _sigmoid(lg_ref[...])
    choice = scores + rb_ref[...]
    E, T = scores.shape
    gsz = E // N_GROUPS
    neg = jnp.float32(-jnp.inf)
    big = jnp.int32(1 << 30)
    rows8 = lax.broadcasted_iota(jnp.int32, (gsz, T), 0)
    gscore = []
    for gi in range(N_GROUPS):
        c = choice[gi * gsz:(gi + 1) * gsz]
        m1 = jnp.max(c, axis=0, keepdims=True)
        i1 = jnp.min(jnp.where(c == m1, rows8, big), axis=0, keepdims=True)
        m2 = jnp.max(jnp.where(rows8 == i1, neg, c), axis=0, keepdims=True)
        gscore.append(m1 + m2)
    gs = jnp.concatenate(gscore, axis=0)
    rowsg = lax.broadcasted_iota(jnp.int32, (N_GROUPS, T), 0)
    gsel = jnp.zeros((N_GROUPS, T), jnp.bool_)
    for _ in range(TOPK_GROUPS):
        m = jnp.max(gs, axis=0, keepdims=True)
        ix = jnp.min(jnp.where(gs == m, rowsg, big), axis=0, keepdims=True)
        hit = rowsg == ix
        gsel = jnp.logical_or(gsel, hit)
        gs = jnp.where(hit, neg, gs)
    masked = jnp.concatenate(
        [jnp.where(gsel[gi:gi + 1], choice[gi * gsz:(gi + 1) * gsz], neg) for gi in range(N_GROUPS)], axis=0)
    rows = lax.broadcasted_iota(jnp.int32, (E, T), 0)
    sel = jnp.zeros((E, T), jnp.bool_)
    for _ in range(TOP_K):
        m = jnp.max(masked, axis=0, keepdims=True)
        ix = jnp.min(jnp.where(masked == m, rows, big), axis=0, keepdims=True)
        hit = rows == ix
        sel = jnp.logical_or(sel, hit)
        masked = jnp.where(hit, neg, masked)
    w = jnp.where(sel, scores, 0.0)
    gt_ref[...] = w / jnp.sum(w, axis=0, keepdims=True) * ROUTED_SCALE


def _router(logits_t, router_bias, tt):
    E, N = logits_t.shape
    return pl.pallas_call(
        _router_kernel,
        grid=(N // tt,),
        in_specs=[pl.BlockSpec((E, tt), lambda i: (0, i)), pl.BlockSpec((E, 1), lambda i: (0, 0))],
        out_specs=pl.BlockSpec((E, tt), lambda i: (0, i)),
        out_shape=jax.ShapeDtypeStruct((E, N), F32),
        compiler_params=_cparams(("parallel",)),
        name="router",
    )(logits_t, router_bias.reshape(E, 1).astype(F32))


def _moe_kernel(hx_ref, gate_ref, wgu_ref, wd_ref, x1_ref, g2_ref, fg_ref, o_ref, acc_ref, *, ne):
    e = pl.program_id(1)

    @pl.when(e == 0)
    def _():
        acc_ref[...] = jnp.zeros_like(acc_ref)

    hu = jnp.dot(hx_ref[...], wgu_ref[...], preferred_element_type=F32)
    hg = hu[:, :EXPERT_HIDDEN]
    act = hg * _sigmoid(hg) * hu[:, EXPERT_HIDDEN:]
    gate = gate_ref[...]
    g_hi = gate.astype(BF16)
    g_lo = (gate - g_hi.astype(F32)).astype(BF16)
    onehot = (lax.broadcasted_iota(jnp.int32, (gate.shape[1], EXPERT_HIDDEN), 0) == e).astype(BF16)
    gcol = (jnp.dot(g_hi, onehot, preferred_element_type=F32) + jnp.dot(g_lo, onehot, preferred_element_type=F32))
    acc_ref[...] += jnp.dot((act * gcol).astype(BF16), wd_ref[...], preferred_element_type=F32)

    @pl.when(e == ne - 1)
    def _():
        xo = x1_ref[...] + g2_ref[...] * acc_ref[...]
        o_ref[...] = _rms(xo) * fg_ref[...]


def _moe(hx, gate, wgu, wd, x1, g2, fg, tm, tiles_per_batch):
    N, D = hx.shape
    ne = wgu.shape[0]
    return pl.pallas_call(
        functools.partial(_moe_kernel, ne=ne),
        grid=(N // tm, ne),
        in_specs=[pl.BlockSpec((tm, D), lambda i, e: (i, 0)),
                  pl.BlockSpec((tm, gate.shape[1]), lambda i, e: (i, 0)),
                  pl.BlockSpec((None, D, 2 * EXPERT_HIDDEN), lambda i, e: (e, 0, 0)),
                  pl.BlockSpec((None, EXPERT_HIDDEN, D), lambda i, e: (e, 0, 0)),
                  pl.BlockSpec((tm, D), lambda i, e: (i, 0)),
                  pl.BlockSpec((None, 1, D), lambda i, e: (i // tiles_per_batch, 0, 0)),
                  pl.BlockSpec((1, D), lambda i, e: (0, 0))],
        out_specs=pl.BlockSpec((tm, D), lambda i, e: (i, 0)),
        out_shape=jax.ShapeDtypeStruct((N, D), F32),
        scratch_shapes=[pltpu.VMEM((tm, D), F32)],
        compiler_params=_cparams(("parallel", "arbitrary")),
        name="moe",
    )(hx, gate, wgu, wd, x1, g2, fg)


def _rope_tables(L):
    rows = L // GRID_W
    row = jnp.repeat(jnp.arange(rows), GRID_W).astype(F32)
    col = jnp.tile(jnp.arange(GRID_W), rows).astype(F32)
    inv = ROPE_THETA ** (-jnp.arange(0, ROPE_AXIS_DIM, 2, dtype=F32) / ROPE_AXIS_DIM)
    ang = jnp.stack([row[:, None] * inv, col[:, None] * inv], axis=1)
    c, s = jnp.cos(ang), jnp.sin(ang)
    cos_h = jnp.stack([c, c], axis=2).reshape(L, HEAD_DIM)
    sin_h = jnp.stack([-s, s], axis=2).reshape(L, HEAD_DIM)
    return jnp.tile(cos_h, (1, 2)), jnp.tile(sin_h, (1, 2))


def _tile(n, pref):
    t = min(pref, n)
    while n % t:
        t //= 2
    return t


def kernel(x, c, ctx, c_ctx, ada_w, ada_b, norm1_g, norm2_g, w_in, lam_q1, lam_k1, lam_q2, lam_k2, subln_g, hy_conv_w, hy_conv_b, filt_w1, filt_b1, filt_freq, filt_w2, filt_b2, filt_w3, hy_bias, w_branch_attn, w_branch_hyena, w_out, router_w, router_bias, exp_w_gate, exp_w_up, exp_w_down, shared_w_gate, shared_w_up, shared_w_down, final_norm_g):
    B, L, D = x.shape
    assert ada_w.shape[0] == 1 and B % 2 == 0 and L % (2 * DFT_N1 * 8) == 0
    lam = (jnp.exp(jnp.sum(lam_q1[0].astype(F32) * lam_k1[0].astype(F32)))
           - jnp.exp(jnp.sum(lam_q2[0].astype(F32) * lam_k2[0].astype(F32))) + LAM_INIT).reshape(1, 1)

    cc = jnp.zeros((16, D), F32).at[:B].set(c).at[B].set(c_ctx)
    mod = _adaln(cc, ada_w[0], ada_b[0])
    chunk = lambda r0, r1, j: mod[r0:r1, j * D:(j + 1) * D]
    sh1, sc1, g1, sh2, sc2, g2 = (chunk(0, B, j).reshape(B, 1, D) for j in range(6))
    csh, csc = chunk(B, B + 1, 0), chunk(B, B + 1, 1)

    w_bf = w_in[0].astype(BF16)
    n1g = norm1_g[0].reshape(1, D)
    cos_t, sin_t = _rope_tables(L)
    k, v, q, u, gates = _inproj(x, sh1, sc1, n1g, w_bf, cos_t, sin_t, hy_conv_w[0], hy_conv_b[0].reshape(1, -1),
                                _tile(L, 512))
    kc, vc = _ctxproj(ctx, csh, csc, n1g, w_bf)
    k_all = jnp.concatenate([k, kc], axis=1)
    v_all = jnp.concatenate([v, vc], axis=1)
    attn = _attention(lam, q, k_all, v_all, subln_g[0].reshape(1, V_DIM), _tile(L, 256), _tile(k_all.shape[1], 256))

    filt = (filt_w1[0], filt_b1[0], filt_freq[0], filt_w2[0], filt_b2[0], filt_w3[0])
    hy = _hyena(u, filt, hy_bias[0])

    x1, hx2, logits_t = _merge(attn, hy, gates, x, g1, sh2, sc2, norm2_g[0].reshape(1, D),
                               w_branch_attn[0].astype(BF16), w_branch_hyena[0].astype(BF16),
                               w_out[0].astype(BF16), router_w[0].T.astype(F32), _tile(L, 512))
    gates_t = _router(logits_t, router_bias[0], _tile(B * L, 1024))

    N = B * L
    gate = jnp.concatenate([gates_t.T, jnp.ones((N, 1), F32), jnp.zeros((N, 128 - N_EXPERTS - 1), F32)], axis=1)
    wgu = jnp.concatenate([jnp.concatenate([exp_w_gate[0], exp_w_up[0]], axis=-1),
                           jnp.concatenate([shared_w_gate[0], shared_w_up[0]], axis=-1)[None]], axis=0).astype(BF16)
    wd = jnp.concatenate([exp_w_down[0], shared_w_down[0][None]], axis=0).astype(BF16)
    tm = _tile(L, 1024)
    out = _moe(hx2.reshape(N, D), gate, wgu, wd, x1.reshape(N, D), g2, final_norm_g.reshape(1, D), tm, L // tm)
    return out.reshape(B, L, D)
```

```python
import functools
import math

import jax
import jax.numpy as jnp
from jax import lax
from jax.experimental import pallas as pl
from jax.experimental.pallas import tpu as pltpu

F32 = jnp.float32
BF16 = jnp.bfloat16
HIGHEST = lax.Precision.HIGHEST

NORM_EPS = 1e-6
N_HEADS = 8
HEAD_DIM = 64
V_DIM = 2 * HEAD_DIM
GRID_W = 64
ROPE_THETA = 10000.0
ROPE_AXIS_DIM = HEAD_DIM // 2
HYENA_W = 1024
FILTER_EMB = 33
FILTER_BANDS = (FILTER_EMB - 1) // 2
DECAY_TARGET = 1e-2
FAST_DECAY_PCT = 0.3
SLOW_DECAY_PCT = 1.5
N_EXPERTS = 64
N_GROUPS = 8
TOPK_GROUPS = 4
TOP_K = 8
EXPERT_HIDDEN = 256
ROUTED_SCALE = 2.5
LAM_INIT = 0.8 - 0.6 * math.exp(-0.3 * 0)

V7X_VMEM_LIMIT = 56 * 1024 * 1024
LOG2E = 1.4426950408889634


def _cparams(sem):
    return pltpu.CompilerParams(dimension_semantics=sem, vmem_limit_bytes=V7X_VMEM_LIMIT)


def _sigmoid(v):
    return 1.0 / (1.0 + jnp.exp(-v))


def _rms(v):
    return v * lax.rsqrt(jnp.mean(v * v, axis=-1, keepdims=True) + NORM_EPS)


def _adaln_kernel(c_ref, w_ref, b_ref, o_ref):
    c = c_ref[...]
    s = c * _sigmoid(c)
    o_ref[...] = jnp.dot(s, w_ref[...], precision=HIGHEST, preferred_element_type=F32) + b_ref[...]


def _adaln(cc, w, b):
    rows, d = cc.shape
    n = w.shape[1]
    tn = 1536
    return pl.pallas_call(
        _adaln_kernel,
        grid=(n // tn,),
        in_specs=[pl.BlockSpec((rows, d), lambda j: (0, 0)),
                  pl.BlockSpec((d, tn), lambda j: (0, j)),
                  pl.BlockSpec((1, tn), lambda j: (0, j))],
        out_specs=pl.BlockSpec((rows, tn), lambda j: (0, j)),
        out_shape=jax.ShapeDtypeStruct((rows, n), F32),
        compiler_params=_cparams(("arbitrary",)),
        name="adaln",
    )(cc, w, b.reshape(1, n))


def _inproj_kernel(x_ref, xp_ref, xn_ref, sh_ref, sc_ref, g_ref, w_ref, cos_ref, sin_ref, cw_ref, cb_ref,
                   k_ref, v_ref, q_ref, u_ref, gt_ref, pscr, *, tm, nt, qscale):
    i = pl.program_id(1)
    g = g_ref[...]
    sh = sh_ref[...]
    sc = 1.0 + sc_ref[...]

    def norm_mod(xx):
        return (_rms(xx) * g) * sc + sh

    h = norm_mod(x_ref[...])
    hp = jnp.where(i > 0, norm_mod(xp_ref[...]), 0.0)
    hn = jnp.where(i < nt - 1, norm_mod(xn_ref[...]), 0.0)
    hb = h.astype(BF16)
    h_ext = jnp.concatenate([hp, h, hn], axis=0).astype(BF16)

    def proj(lhs, c0):
        return jnp.dot(lhs, w_ref[:, c0:c0 + 1024], preferred_element_type=F32)

    cosv = cos_ref[...]
    sinv = sin_ref[...]
    lane = lax.broadcasted_iota(jnp.int32, (1, V_DIM), 1)
    first = (lane % ROPE_AXIS_DIM) < (ROPE_AXIS_DIM // 2)

    def rope_store(a, o_ref, scale):
        for hh in range(N_HEADS):
            s = a[:, hh * V_DIM:(hh + 1) * V_DIM]
            partner = jnp.where(first, pltpu.roll(s, V_DIM - ROPE_AXIS_DIM // 2, 1),
                                pltpu.roll(s, ROPE_AXIS_DIM // 2, 1))
            r = s * cosv + partner * sinv
            if scale != 1.0:
                r = r * scale
            o_ref[:, hh * V_DIM:(hh + 1) * V_DIM] = r.astype(BF16)

    rope_store(proj(hb, 0), k_ref, 1.0)
    v_ref[...] = proj(hb, 1024).astype(BF16)
    rope_store(proj(hb, 2048), q_ref, qscale)
    for j in range(3):
        pscr[...] = proj(h_ext, 3072 + j * 1024)
        cw = cw_ref[:, j * 1024:(j + 1) * 1024]
        cb = cb_ref[:, j * 1024:(j + 1) * 1024]
        u = (pscr[pl.ds(7, tm), :] * cw[0:1] + pscr[pl.ds(8, tm), :] * cw[1:2]
             + pscr[pl.ds(9, tm), :] * cw[2:3] + cb)
        u_ref[:, j * 1024:(j + 1) * 1024] = u.astype(BF16)
    for j in range(2):
        gt_ref[:, j * 1024:(j + 1) * 1024] = proj(hb, 6144 + j * 1024).astype(BF16)


def _inproj(x, shift, scale, g, w_bf, cos_t, sin_t, conv_w, conv_b, tm):
    B, L, D = x.shape
    nt = L // tm
    nb8 = L // 8
    qscale = (HEAD_DIM ** -0.5) * LOG2E
    kern = functools.partial(_inproj_kernel, tm=tm, nt=nt, qscale=qscale)
    row = lambda b, i: (b, i, 0)
    per_b = lambda b, i: (b, 0, 0)
    const = lambda b, i: (0, 0)
    outs = pl.pallas_call(
        kern,
        grid=(B, nt),
        in_specs=[
            pl.BlockSpec((None, tm, D), row),
            pl.BlockSpec((None, 8, D), lambda b, i: (b, jnp.maximum(i * (tm // 8) - 1, 0), 0)),
            pl.BlockSpec((None, 8, D), lambda b, i: (b, jnp.minimum((i + 1) * (tm // 8), nb8 - 1), 0)),
            pl.BlockSpec((None, 1, D), per_b),
            pl.BlockSpec((None, 1, D), per_b),
            pl.BlockSpec((1, D), const),
            pl.BlockSpec((D, 8192), const, pipeline_mode=pl.Buffered(1)),
            pl.BlockSpec((tm, V_DIM), lambda b, i: (i, 0)),
            pl.BlockSpec((tm, V_DIM), lambda b, i: (i, 0)),
            pl.BlockSpec((3, 3 * HYENA_W), const),
            pl.BlockSpec((1, 3 * HYENA_W), const),
        ],
        out_specs=[
            pl.BlockSpec((None, tm, 1024), row),
            pl.BlockSpec((None, tm, 1024), row),
            pl.BlockSpec((None, tm, 1024), row),
            pl.BlockSpec((None, tm, 3 * HYENA_W), row),
            pl.BlockSpec((None, tm, 2048), row),
        ],
        out_shape=[
            jax.ShapeDtypeStruct((B, L, 1024), BF16),
            jax.ShapeDtypeStruct((B, L, 1024), BF16),
            jax.ShapeDtypeStruct((B, L, 1024), BF16),
            jax.ShapeDtypeStruct((B, L, 3 * HYENA_W), BF16),
            jax.ShapeDtypeStruct((B, L, 2048), BF16),
        ],
        scratch_shapes=[pltpu.VMEM((tm + 16, 1024), F32)],
        compiler_params=_cparams(("parallel", "arbitrary")),
        name="inproj",
    )(x, x, x, shift, scale, g, w_bf, cos_t, sin_t, conv_w, conv_b)
    return outs


def _ctxproj_kernel(x_ref, sh_ref, sc_ref, g_ref, w_ref, k_ref, v_ref):
    h = ((_rms(x_ref[...]) * g_ref[...]) * (1.0 + sc_ref[...]) + sh_ref[...]).astype(BF16)
    k_ref[...] = jnp.dot(h, w_ref[:, 0:1024], preferred_element_type=F32).astype(BF16)
    v_ref[...] = jnp.dot(h, w_ref[:, 1024:2048], preferred_element_type=F32).astype(BF16)


def _ctxproj(ctx, shift, scale, g, w_bf):
    B, Lc, D = ctx.shape
    return pl.pallas_call(
        _ctxproj_kernel,
        grid=(B,),
        in_specs=[pl.BlockSpec((None, Lc, D), lambda b: (b, 0, 0)),
                  pl.BlockSpec((1, D), lambda b: (0, 0)),
                  pl.BlockSpec((1, D), lambda b: (0, 0)),
                  pl.BlockSpec((1, D), lambda b: (0, 0)),
                  pl.BlockSpec((D, 2048), lambda b: (0, 0))],
        out_specs=[pl.BlockSpec((None, Lc, 1024), lambda b: (b, 0, 0)),
                   pl.BlockSpec((None, Lc, 1024), lambda b: (b, 0, 0))],
        out_shape=[jax.ShapeDtypeStruct((B, Lc, 1024), BF16),
                   jax.ShapeDtypeStruct((B, Lc, 1024), BF16)],
        compiler_params=_cparams(("arbitrary",)),
        name="ctxproj",
    )(ctx, shift, scale, g, w_bf)


def _attn_kernel(lam_ref, q_ref, k_ref, v_ref, kc_ref, vc_ref, g_ref, o_ref, m_s, l_s, acc_s, *, tk, nk):
    q = q_ref[...]
    lane = lax.broadcasted_iota(jnp.int32, (1, V_DIM), 1)
    zero = jnp.zeros_like(q)
    qs = (jnp.where(lane < HEAD_DIM, q, zero), jnp.where(lane >= HEAD_DIM, q, zero))
    m_s[...] = jnp.full(m_s.shape, -1e30, F32)
    l_s[...] = jnp.zeros(l_s.shape, F32)
    acc_s[...] = jnp.zeros(acc_s.shape, F32)

    def chunk(kk, vv):
        reps = kk.shape[0] // V_DIM
        for m in range(2):
            s = lax.dot_general(qs[m], kk, (((1,), (1,)), ((), ())), preferred_element_type=F32)
            m_prev = m_s[m]
            m_new = jnp.maximum(m_prev, jnp.max(s, axis=-1, keepdims=True))
            alpha = jnp.exp2(m_prev - m_new)
            p = jnp.exp2(s - pltpu.repeat(m_new, reps, axis=1))
            l_s[m] = alpha * l_s[m] + jnp.sum(p, axis=-1, keepdims=True)
            acc_s[m] = alpha * acc_s[m] + jnp.dot(p.astype(BF16), vv, preferred_element_type=F32)
            m_s[m] = m_new

    def body(j, carry):
        off = pl.multiple_of(j * tk, tk)
        chunk(k_ref[pl.ds(off, tk), :], v_ref[pl.ds(off, tk), :])
        return carry

    lax.fori_loop(0, nk, body, 0, unroll=2 if nk % 2 == 0 else 1)
    chunk(kc_ref[...], vc_ref[...])
    lam = lam_ref[0, 0]
    o = acc_s[0] / l_s[0] - lam * (acc_s[1] / l_s[1])
    o = _rms(o) * g_ref[...] * (1.0 - LAM_INIT)
    o_ref[...] = o.astype(BF16)


def _attention(lam, q, k, v, kc, vc, subln_g, tq, tk):
    B, L, _ = q.shape
    Lc = kc.shape[1]
    kern = functools.partial(_attn_kernel, tk=tk, nk=L // tk)
    kv = lambda b, h, i: (b, 0, h)
    return pl.pallas_call(
        kern,
        grid=(B, N_HEADS, L // tq),
        in_specs=[
            pl.BlockSpec(memory_space=pltpu.SMEM),
            pl.BlockSpec((None, tq, V_DIM), lambda b, h, i: (b, i, h)),
            pl.BlockSpec((None, L, V_DIM), kv),
            pl.BlockSpec((None, L, V_DIM), kv),
            pl.BlockSpec((None, Lc, V_DIM), kv),
            pl.BlockSpec((None, Lc, V_DIM), kv),
            pl.BlockSpec((1, V_DIM), lambda b, h, i: (0, 0)),
        ],
        out_specs=pl.BlockSpec((None, tq, V_DIM), lambda b, h, i: (b, i, h)),
        out_shape=jax.ShapeDtypeStruct((B, L, N_HEADS * V_DIM), BF16),
        scratch_shapes=[pltpu.VMEM((2, tq, V_DIM), F32), pltpu.VMEM((2, tq, V_DIM), F32),
                        pltpu.VMEM((2, tq, V_DIM), F32)],
        compiler_params=_cparams(("parallel", "parallel", "arbitrary")),
        name="diffattn",
    )(lam, q, k, v, kc, vc, subln_g)


FFT_SLABS = 64
FFT_STRIP = 16


def _dft_tables(nb):
    N = FFT_SLABS * nb
    fa = jnp.arange(FFT_SLABS, dtype=jnp.int32)[:, None, None]
    fb = jnp.arange(nb, dtype=jnp.int32)[None, :, None]
    b = jnp.arange(nb, dtype=jnp.int32)[None, None, :]
    ang = ((FFT_SLABS * b * fb + b * fa) % N).astype(F32) * (2.0 * math.pi / N)
    c, s = jnp.cos(ang), jnp.sin(ang)
    fwd = jnp.concatenate([jnp.concatenate([c, s], -1), jnp.concatenate([-s, c], -1)], axis=1)
    inv = jnp.swapaxes(fwd, 1, 2) / N
    return fwd.astype(BF16), inv.astype(BF16)


def _vadd(a, b):
    return b if a is None else (a if b is None else a + b)


def _vsub(a, b):
    if b is None:
        return a
    return -b if a is None else a - b


def _vscale(a, k):
    if a is None or k == 1.0:
        return a
    return -a if k == -1.0 else a * k


def _cmulc(x, c, s):
    re, im = x
    eps = 1e-9
    if abs(s) < eps:
        k = 1.0 if c > 0 else -1.0
        return _vscale(re, k), _vscale(im, k)
    if abs(c) < eps:
        k = 1.0 if s > 0 else -1.0
        return _vscale(im, -k), _vscale(re, k)
    if abs(abs(c) - abs(s)) < eps:
        kc = 1.0 if c > 0 else -1.0
        ks = 1.0 if s > 0 else -1.0
        return (_vscale(_vsub(_vscale(re, kc), _vscale(im, ks)), abs(c)),
                _vscale(_vadd(_vscale(re, ks), _vscale(im, kc)), abs(c)))
    return _vsub(_vscale(re, c), _vscale(im, s)), _vadd(_vscale(re, s), _vscale(im, c))


def _fft(xs, sign):
    n = len(xs)
    if n == 1:
        return xs
    ev, od = _fft(xs[0::2], sign), _fft(xs[1::2], sign)
    out = [None] * n
    for k in range(n // 2):
        ang = sign * 2.0 * math.pi * k / n
        t = _cmulc(od[k], math.cos(ang), math.sin(ang))
        out[k] = (_vadd(ev[k][0], t[0]), _vadd(ev[k][1], t[1]))
        out[k + n // 2] = (_vsub(ev[k][0], t[0]), _vsub(ev[k][1], t[1]))
    return out


def _put(ref, slot, rows, val):
    ref[slot, rows, :] = jnp.zeros((FFT_STRIP, ref.shape[2]), F32) if val is None else val


def _slab_fft_forward(load, s_ref, nb, n_in):
    def strip(i, carry):
        r0 = pl.multiple_of(i * FFT_STRIP, FFT_STRIP)
        rows_re, rows_im = pl.ds(r0, FFT_STRIP), pl.ds(nb + r0, FFT_STRIP)
        for a0 in range(8):
            xs = [load(8 * a1 + a0, r0) if a1 < n_in else (None, None) for a1 in range(8)]
            ys = _fft(xs, -1.0)
            for f1 in range(8):
                ang = -2.0 * math.pi * a0 * f1 / FFT_SLABS
                yr, yi = _cmulc(ys[f1], math.cos(ang), math.sin(ang))
                _put(s_ref, a0 * 8 + f1, rows_re, yr)
                _put(s_ref, a0 * 8 + f1, rows_im, yi)
        for f1 in range(8):
            xs = [(s_ref[a0 * 8 + f1, rows_re, :], s_ref[a0 * 8 + f1, rows_im, :]) for a0 in range(8)]
            ys = _fft(xs, -1.0)
            for f0 in range(8):
                s_ref[f1 + 8 * f0, rows_re, :] = ys[f0][0]
                s_ref[f1 + 8 * f0, rows_im, :] = ys[f0][1]
        return carry

    lax.fori_loop(0, nb // FFT_STRIP, strip, 0)


def _slab_fft_inverse(s_ref, nb, emit):
    def strip(i, carry):
        r0 = pl.multiple_of(i * FFT_STRIP, FFT_STRIP)
        rows_re, rows_im = pl.ds(r0, FFT_STRIP), pl.ds(nb + r0, FFT_STRIP)
        for q in range(8):
            xs = [(s_ref[8 * p + q, rows_re, :], s_ref[8 * p + q, rows_im, :]) for p in range(8)]
            us = _fft(xs, 1.0)
            for r in range(8):
                ang = 2.0 * math.pi * r * q / FFT_SLABS
                ur, ui = _cmulc(us[r], math.cos(ang), math.sin(ang))
                s_ref[8 * r + q, rows_re, :] = ur
                s_ref[8 * r + q, rows_im, :] = ui
        for r in range(8):
            xs = [(s_ref[8 * r + q, rows_re, :], s_ref[8 * r + q, rows_im, :]) for q in range(8)]
            ys = _fft(xs, 1.0)
            for sidx in range(4):
                emit(r + 8 * sidx, r0, ys[sidx][0], ys[sidx][1])
        return carry

    lax.fori_loop(0, nb // FFT_STRIP, strip, 0)


def _hyena_conv_kernel(v_ref, x_ref, k_ref, ff_ref, fi_ref, bias_ref, o_ref, s_ref, *, nb):
    def load(a, r0):
        rows = pl.ds(pl.multiple_of(a * nb + r0, FFT_STRIP), FFT_STRIP)
        return v_ref[0, rows, :].astype(F32), v_ref[1, rows, :].astype(F32)

    _slab_fft_forward(load, s_ref, nb, FFT_SLABS // 16)

    def freq(f, carry):
        z = jnp.dot(ff_ref[f], s_ref[f].astype(BF16), preferred_element_type=F32)
        zr, zi = z[:nb], z[nb:]
        kr, ki = k_ref[0, f], k_ref[1, f]
        y = jnp.concatenate([zr * kr - zi * ki, zr * ki + zi * kr], axis=0).astype(BF16)
        s_ref[f] = jnp.dot(fi_ref[f], y, preferred_element_type=F32)
        return carry

    lax.fori_loop(0, FFT_SLABS, freq, 0, unroll=8)
    bias = bias_ref[...]

    def emit(a, r0, re, im):
        rows = pl.ds(pl.multiple_of(a * nb + r0, FFT_STRIP), FFT_STRIP)
        for half, y in enumerate((re, im)):
            v = v_ref[half, rows, :].astype(F32)
            o_ref[half, rows, :] = (x_ref[half, rows, :].astype(F32) * (y + v * bias)).astype(BF16)

    _slab_fft_inverse(s_ref, nb, emit)


def _hyena_conv(uv, zv, k_hat, ff, fi, bias, order, in_col, gate_col, nb):
    _, P, L, _ = uv.shape
    W = HYENA_W
    nct = W // 128
    src, src_off = (uv, in_col * nct) if zv is None else (zv, 0)
    blk = (2, None, L, 128)
    return pl.pallas_call(
        functools.partial(_hyena_conv_kernel, nb=nb),
        grid=(nct, P),
        in_specs=[pl.BlockSpec(blk, lambda c, p: (0, p, 0, src_off + c)),
                  pl.BlockSpec(blk, lambda c, p: (0, p, 0, gate_col * nct + c)),
                  pl.BlockSpec((2, FFT_SLABS, nb, 128), lambda c, p: (0, 0, 0, order * nct + c),
                               pipeline_mode=pl.Buffered(1)),
                  pl.BlockSpec((FFT_SLABS, 2 * nb, 2 * nb), lambda c, p: (0, 0, 0), pipeline_mode=pl.Buffered(1)),
                  pl.BlockSpec((FFT_SLABS, 2 * nb, 2 * nb), lambda c, p: (0, 0, 0), pipeline_mode=pl.Buffered(1)),
                  pl.BlockSpec((1, 128), lambda c, p: (0, c))],
        out_specs=pl.BlockSpec(blk, lambda c, p: (0, p, 0, c)),
        out_shape=jax.ShapeDtypeStruct((2, P, L, W), BF16),
        scratch_shapes=[pltpu.VMEM((FFT_SLABS, 2 * nb, 128), F32)],
        compiler_params=_cparams(("parallel", "arbitrary")),
        name="hyena_conv%d" % order,
    )(src, uv, k_hat, ff, fi, bias)


def _filter_features(n):
    pos = jnp.arange(n, dtype=F32)[:, None]
    t = jnp.linspace(0.0, 1.0, n, dtype=F32)[:, None]
    w = 2 * math.pi * pos / n
    bands = jnp.linspace(1e-4, FILTER_BANDS - 1, FILTER_BANDS, dtype=F32)
    z = jnp.concatenate([t, jnp.cos(bands * w), -jnp.sin(bands * w)], axis=-1)
    z = jnp.pad(z, ((0, 0), (0, 128 - FILTER_EMB)))
    tap = jnp.arange(2 * n)
    return z[jnp.where(tap < n, tap, (2 * n - tap) % n)]


def _filt_taps_kernel(z_ref, w1_ref, b1_ref, fr_ref, w2_ref, b2_ref, w3_ref, dl_ref, k_ref, sum_ref):
    a = pl.program_id(0)
    z = z_ref[...]
    fr = fr_ref[...]
    hd = jnp.sin(fr * (jnp.dot(z, w1_ref[...], precision=HIGHEST, preferred_element_type=F32) + b1_ref[...]))
    hd = jnp.sin(fr * (jnp.dot(hd, w2_ref[...], precision=HIGHEST, preferred_element_type=F32) + b2_ref[...]))
    h = jnp.dot(hd, w3_ref[...], precision=HIGHEST, preferred_element_type=F32) * jnp.exp(-z[:, 0:1] * dl_ref[...])
    rows = lax.broadcasted_iota(jnp.int32, (z.shape[0], 1), 0)
    h = jnp.where(jnp.logical_and(a == FFT_SLABS // 2, rows == 0), 0.0, h)
    k_ref[...] = h

    @pl.when(a == 0)
    def _():
        sum_ref[...] = jnp.zeros_like(sum_ref)

    sum_ref[0:1, :] += jnp.sum(jnp.abs(h), axis=0, keepdims=True)


def _filt_spectrum_kernel(t_ref, inv_ref, ff_ref, k_ref, s_ref, *, nb):
    def load(a, r0):
        return t_ref[pl.ds(pl.multiple_of(a * nb + r0, FFT_STRIP), FFT_STRIP), :], None

    _slab_fft_forward(load, s_ref, nb, FFT_SLABS // 8)
    inv = inv_ref[...]

    def freq(f, carry):
        z = jnp.dot(ff_ref[f], s_ref[f].astype(BF16), preferred_element_type=F32)
        k_ref[0, f] = z[:nb] * inv
        k_ref[1, f] = z[nb:] * inv
        return carry

    lax.fori_loop(0, FFT_SLABS, freq, 0, unroll=8)


def _hyena_filters(n, filt, ff, nb):
    w1, b1, freq, w2, b2, w3 = filt
    W = HYENA_W
    hid = w1.shape[1]
    z = _filter_features(n)
    w1p = jnp.pad(w1.astype(F32), ((0, 128 - FILTER_EMB), (0, 0)))
    w3d = w3.astype(F32).reshape(hid, 2, 2, W)
    w3sel = jnp.stack([w3d[:, :, d, :].reshape(hid, 2 * W) for d in range(2)])
    deltas = jnp.abs(jnp.linspace(math.log(DECAY_TARGET) / SLOW_DECAY_PCT,
                                  math.log(DECAY_TARGET) / FAST_DECAY_PCT, W, dtype=F32))
    deltas = jnp.tile(deltas.reshape(1, W), (1, 2))
    const = lambda a: (0, 0)
    taps, sums = pl.pallas_call(
        _filt_taps_kernel,
        grid=(FFT_SLABS,),
        in_specs=[pl.BlockSpec((nb, 128), lambda a: (a, 0)),
                  pl.BlockSpec((128, hid), const), pl.BlockSpec((1, hid), const), pl.BlockSpec((1, hid), const),
                  pl.BlockSpec((hid, hid), const), pl.BlockSpec((1, hid), const),
                  pl.BlockSpec((None, hid, 2 * W), lambda a: (a // (FFT_SLABS // 2), 0, 0)),
                  pl.BlockSpec((1, 2 * W), const)],
        out_specs=[pl.BlockSpec((nb, 2 * W), lambda a: (a, 0)), pl.BlockSpec((8, 2 * W), const)],
        out_shape=[jax.ShapeDtypeStruct((2 * n, 2 * W), F32), jax.ShapeDtypeStruct((8, 2 * W), F32)],
        compiler_params=_cparams(("arbitrary",)),
        name="hyena_filter_taps",
    )(z, w1p, b1.reshape(1, hid), freq.reshape(1, hid), w2, b2.reshape(1, hid), w3sel, deltas)
    inv = 1.0 / sums[0:1]
    return pl.pallas_call(
        functools.partial(_filt_spectrum_kernel, nb=nb),
        grid=(2 * W // 128,),
        in_specs=[pl.BlockSpec((2 * n, 128), lambda j: (0, j)),
                  pl.BlockSpec((1, 128), lambda j: (0, j)),
                  pl.BlockSpec((FFT_SLABS, 2 * nb, 2 * nb), lambda j: (0, 0, 0), pipeline_mode=pl.Buffered(1))],
        out_specs=pl.BlockSpec((2, FFT_SLABS, nb, 128), lambda j: (0, 0, 0, j)),
        out_shape=jax.ShapeDtypeStruct((2, FFT_SLABS, nb, 2 * W), F32),
        scratch_shapes=[pltpu.VMEM((FFT_SLABS, 2 * nb, 128), F32)],
        compiler_params=_cparams(("parallel",)),
        name="hyena_filter_spectrum",
    )(taps, inv, ff)


def _hyena(u, filt, hy_bias):
    B, L, _ = u.shape
    P = B // 2
    nb = 2 * L // FFT_SLABS
    ff, fi = _dft_tables(nb)
    k_hat = _hyena_filters(L, filt, ff, nb)
    uv = u.reshape(2, P, L, 3 * HYENA_W)
    bias = hy_bias.astype(F32)
    z = _hyena_conv(uv, None, k_hat, ff, fi, bias[0:1], 0, 0, 1, nb)
    y = _hyena_conv(uv, z, k_hat, ff, fi, bias[1:2], 1, 0, 2, nb)
    return y.reshape(B, L, HYENA_W)


def _merge_kernel(a_ref, h_ref, gt_ref, x_ref, g1_ref, sh_ref, sc_ref, n2g_ref, wpa_ref, wph_ref, wo_ref, rw_ref,
                  x1_ref, hx_ref, lg_ref):
    ya = jnp.dot(a_ref[...], wpa_ref[...], preferred_element_type=F32)
    yh = jnp.dot(h_ref[...], wph_ref[...], preferred_element_type=F32)
    ga = gt_ref[:, 0:1024].astype(F32)
    gh = gt_ref[:, 1024:2048].astype(F32)
    y = _sigmoid(ga) * ya + _sigmoid(gh) * yh
    mix = jnp.dot(y.astype(BF16), wo_ref[...], preferred_element_type=F32)
    x1 = x_ref[...] + g1_ref[...] * mix
    x1_ref[...] = x1
    hx = (_rms(x1) * n2g_ref[...]) * (1.0 + sc_ref[...]) + sh_ref[...]
    hx_ref[...] = hx.astype(BF16)
    lg_ref[...] = lax.dot_general(rw_ref[...], hx, (((1,), (1,)), ((), ())), precision=HIGHEST,
                                  preferred_element_type=F32)


def _merge(attn, hy, gates, x, g1, sh2, sc2, n2g, wpa, wph, wo, rwT, tm):
    B, L, D = x.shape
    nt = L // tm
    row = lambda b, i: (b, i, 0)
    per_b = lambda b, i: (b, 0, 0)
    const = lambda b, i: (0, 0)
    return pl.pallas_call(
        _merge_kernel,
        grid=(B, nt),
        in_specs=[pl.BlockSpec((None, tm, 1024), row), pl.BlockSpec((None, tm, 1024), row),
                  pl.BlockSpec((None, tm, 2048), row), pl.BlockSpec((None, tm, D), row),
                  pl.BlockSpec((None, 1, D), per_b), pl.BlockSpec((None, 1, D), per_b),
                  pl.BlockSpec((None, 1, D), per_b), pl.BlockSpec((1, D), const),
                  pl.BlockSpec((1024, D), const), pl.BlockSpec((1024, D), const), pl.BlockSpec((D, D), const),
                  pl.BlockSpec((N_EXPERTS, D), const)],
        out_specs=[pl.BlockSpec((None, tm, D), row), pl.BlockSpec((None, tm, D), row),
                   pl.BlockSpec((N_EXPERTS, tm), lambda b, i: (0, b * nt + i))],
        out_shape=[jax.ShapeDtypeStruct((B, L, D), F32), jax.ShapeDtypeStruct((B, L, D), BF16),
                   jax.ShapeDtypeStruct((N_EXPERTS, B * L), F32)],
        compiler_params=_cparams(("parallel", "arbitrary")),
        name="merge",
    )(attn, hy, gates, x, g1, sh2, sc2, n2g, wpa, wph, wo, rwT)


def _router_kernel(lg_ref, rb_ref, gt_ref):
    scores = _sigmoid(lg_ref[...])
    choice = scores + rb_ref[...]
    E, T = scores.shape
    gsz = E // N_GROUPS
    neg = jnp.float32(-jnp.inf)
    big = jnp.int32(1 << 30)
    rows8 = lax.broadcasted_iota(jnp.int32, (gsz, T), 0)
    gscore = []
    for gi in range(N_GROUPS):
        c = choice[gi * gsz:(gi + 1) * gsz]
        m1 = jnp.max(c, axis=0, keepdims=True)
        i1 = jnp.min(jnp.where(c == m1, rows8, big), axis=0, keepdims=True)
        m2 = jnp.max(jnp.where(rows8 == i1, neg, c), axis=0, keepdims=True)
        gscore.append(m1 + m2)
    gs = jnp.concatenate(gscore, axis=0)
    rowsg = lax.broadcasted_iota(jnp.int32, (N_GROUPS, T), 0)
    gsel = jnp.zeros((N_GROUPS, T), jnp.bool_)
    for _ in range(TOPK_GROUPS):
        m = jnp.max(gs, axis=0, keepdims=True)
        ix = jnp.min(jnp.where(gs == m, rowsg, big), axis=0, keepdims=True)
        hit = rowsg == ix
        gsel = jnp.logical_or(gsel, hit)
        gs = jnp.where(hit, neg, gs)
    masked = jnp.concatenate(
        [jnp.where(gsel[gi:gi + 1], choice[gi * gsz:(gi + 1) * gsz], neg) for gi in range(N_GROUPS)], axis=0)
    rows = lax.broadcasted_iota(jnp.int32, (E, T), 0)
    sel = jnp.zeros((E, T), jnp.bool_)
    for _ in range(TOP_K):
        m = jnp.max(masked, axis=0, keepdims=True)
        ix = jnp.min(jnp.where(masked == m, rows, big), axis=0, keepdims=True)
        hit = rows == ix
        sel = jnp.logical_or(sel, hit)
        masked = jnp.where(hit, neg, masked)
    w = jnp.where(sel, scores, 0.0)
    gt_ref[...] = w / jnp.sum(w, axis=0, keepdims=True) * ROUTED_SCALE


def _router(logits_t, router_bias, tt):
    E, N = logits_t.shape
    return pl.pallas_call(
        _router_kernel,
        grid=(N // tt,),
        in_specs=[pl.BlockSpec((E, tt), lambda i: (0, i)), pl.BlockSpec((E, 1), lambda i: (0, 0))],
        out_specs=pl.BlockSpec((E, tt), lambda i: (0, i)),
        out_shape=jax.ShapeDtypeStruct((E, N), F32),
        compiler_params=_cparams(("parallel",)),
        name="router",
    )(logits_t, router_bias.reshape(E, 1).astype(F32))


def _moe_kernel(hx_ref, gate_ref, wgu_ref, wd_ref, x1_ref, g2_ref, fg_ref, o_ref, acc_ref, *, ne):
    e = pl.program_id(1)

    @pl.when(e == 0)
    def _():
        acc_ref[...] = jnp.zeros_like(acc_ref)

    hu = jnp.dot(hx_ref[...], wgu_ref[...], preferred_element_type=F32)
    hg = hu[:, :EXPERT_HIDDEN]
    act = hg * _sigmoid(hg) * hu[:, EXPERT_HIDDEN:]
    gate = gate_ref[...]
    g_hi = gate.astype(BF16)
    g_lo = (gate - g_hi.astype(F32)).astype(BF16)
    onehot = (lax.broadcasted_iota(jnp.int32, (gate.shape[1], EXPERT_HIDDEN), 0) == e).astype(BF16)
    gcol = (jnp.dot(g_hi, onehot, preferred_element_type=F32) + jnp.dot(g_lo, onehot, preferred_element_type=F32))
    acc_ref[...] += jnp.dot((act * gcol).astype(BF16), wd_ref[...], preferred_element_type=F32)

    @pl.when(e == ne - 1)
    def _():
        xo = x1_ref[...] + g2_ref[...] * acc_ref[...]
        o_ref[...] = _rms(xo) * fg_ref[...]


def _moe(hx, gate, wgu, wd, x1, g2, fg, tm, tiles_per_batch):
    N, D = hx.shape
    ne = wgu.shape[0]
    return pl.pallas_call(
        functools.partial(_moe_kernel, ne=ne),
        grid=(N // tm, ne),
        in_specs=[pl.BlockSpec((tm, D), lambda i, e: (i, 0)),
                  pl.BlockSpec((tm, gate.shape[1]), lambda i, e: (i, 0)),
                  pl.BlockSpec((None, D, 2 * EXPERT_HIDDEN), lambda i, e: (e, 0, 0)),
                  pl.BlockSpec((None, EXPERT_HIDDEN, D), lambda i, e: (e, 0, 0)),
                  pl.BlockSpec((tm, D), lambda i, e: (i, 0)),
                  pl.BlockSpec((None, 1, D), lambda i, e: (i // tiles_per_batch, 0, 0)),
                  pl.BlockSpec((1, D), lambda i, e: (0, 0))],
        out_specs=pl.BlockSpec((tm, D), lambda i, e: (i, 0)),
        out_shape=jax.ShapeDtypeStruct((N, D), F32),
        scratch_shapes=[pltpu.VMEM((tm, D), F32)],
        compiler_params=_cparams(("parallel", "arbitrary")),
        name="moe",
    )(hx, gate, wgu, wd, x1, g2, fg)


def _rope_tables(L):
    rows = L // GRID_W
    row = jnp.repeat(jnp.arange(rows), GRID_W).astype(F32)
    col = jnp.tile(jnp.arange(GRID_W), rows).astype(F32)
    inv = ROPE_THETA ** (-jnp.arange(0, ROPE_AXIS_DIM, 2, dtype=F32) / ROPE_AXIS_DIM)
    ang = jnp.stack([row[:, None] * inv, col[:, None] * inv], axis=1)
    c, s = jnp.cos(ang), jnp.sin(ang)
    cos_h = jnp.stack([c, c], axis=2).reshape(L, HEAD_DIM)
    sin_h = jnp.stack([-s, s], axis=2).reshape(L, HEAD_DIM)
    return jnp.tile(cos_h, (1, 2)), jnp.tile(sin_h, (1, 2))


def _tile(n, pref):
    t = min(pref, n)
    while n % t:
        t //= 2
    return t


def kernel(x, c, ctx, c_ctx, ada_w, ada_b, norm1_g, norm2_g, w_in, lam_q1, lam_k1, lam_q2, lam_k2, subln_g, hy_conv_w, hy_conv_b, filt_w1, filt_b1, filt_freq, filt_w2, filt_b2, filt_w3, hy_bias, w_branch_attn, w_branch_hyena, w_out, router_w, router_bias, exp_w_gate, exp_w_up, exp_w_down, shared_w_gate, shared_w_up, shared_w_down, final_norm_g):
    B, L, D = x.shape
    assert ada_w.shape[0] == 1 and B % 2 == 0 and (2 * L) % (FFT_SLABS * FFT_STRIP) == 0
    lam = (jnp.exp(jnp.sum(lam_q1[0].astype(F32) * lam_k1[0].astype(F32)))
           - jnp.exp(jnp.sum(lam_q2[0].astype(F32) * lam_k2[0].astype(F32))) + LAM_INIT).reshape(1, 1)

    cc = jnp.zeros((16, D), F32).at[:B].set(c).at[B].set(c_ctx)
    mod = _adaln(cc, ada_w[0], ada_b[0])
    chunk = lambda r0, r1, j: mod[r0:r1, j * D:(j + 1) * D]
    sh1, sc1, g1, sh2, sc2, g2 = (chunk(0, B, j).reshape(B, 1, D) for j in range(6))
    csh, csc = chunk(B, B + 1, 0), chunk(B, B + 1, 1)

    w_bf = w_in[0].astype(BF16)
    n1g = norm1_g[0].reshape(1, D)
    cos_t, sin_t = _rope_tables(L)
    k, v, q, u, gates = _inproj(x, sh1, sc1, n1g, w_bf, cos_t, sin_t, hy_conv_w[0], hy_conv_b[0].reshape(1, -1),
                                _tile(L, 512))
    kc, vc = _ctxproj(ctx, csh, csc, n1g, w_bf)
    attn = _attention(lam, q, k, v, kc, vc, subln_g[0].reshape(1, V_DIM), _tile(L, 512), _tile(L, 512))

    filt = (filt_w1[0], filt_b1[0], filt_freq[0], filt_w2[0], filt_b2[0], filt_w3[0])
    hy = _hyena(u, filt, hy_bias[0])

    x1, hx2, logits_t = _merge(attn, hy, gates, x, g1, sh2, sc2, norm2_g[0].reshape(1, D),
                               w_branch_attn[0].astype(BF16), w_branch_hyena[0].astype(BF16),
                               w_out[0].astype(BF16), router_w[0].T.astype(F32), _tile(L, 512))
    gates_t = _router(logits_t, router_bias[0], _tile(B * L, 1024))

    N = B * L
    gate = jnp.concatenate([gates_t.T, jnp.ones((N, 1), F32), jnp.zeros((N, 128 - N_EXPERTS - 1), F32)], axis=1)
    wgu = jnp.concatenate([jnp.concatenate([exp_w_gate[0], exp_w_up[0]], axis=-1),
                           jnp.concatenate([shared_w_gate[0], shared_w_up[0]], axis=-1)[None]], axis=0).astype(BF16)
    wd = jnp.concatenate([exp_w_down[0], shared_w_down[0][None]], axis=0).astype(BF16)
    tm = _tile(L, 1024)
    out = _moe(hx2.reshape(N, D), gate, wgu, wd, x1.reshape(N, D), g2, final_norm_g.reshape(1, D), tm, L // tm)
    return out.reshape(B, L, D)
```

```python
import functools
import math

import jax
import jax.numpy as jnp
from jax import lax
from jax.experimental import pallas as pl
from jax.experimental.pallas import tpu as pltpu

F32 = jnp.float32
BF16 = jnp.bfloat16
HIGHEST = lax.Precision.HIGHEST

NORM_EPS = 1e-6
N_HEADS = 8
HEAD_DIM = 64
V_DIM = 2 * HEAD_DIM
GRID_W = 64
ROPE_THETA = 10000.0
ROPE_AXIS_DIM = HEAD_DIM // 2
HYENA_W = 1024
FILTER_EMB = 33
FILTER_BANDS = (FILTER_EMB - 1) // 2
DECAY_TARGET = 1e-2
FAST_DECAY_PCT = 0.3
SLOW_DECAY_PCT = 1.5
N_EXPERTS = 64
N_GROUPS = 8
TOPK_GROUPS = 4
TOP_K = 8
EXPERT_HIDDEN = 256
ROUTED_SCALE = 2.5
LAM_INIT = 0.8 - 0.6 * math.exp(-0.3 * 0)

V7X_VMEM_LIMIT = 56 * 1024 * 1024
LOG2E = 1.4426950408889634


def _cparams(sem):
    return pltpu.CompilerParams(dimension_semantics=sem, vmem_limit_bytes=V7X_VMEM_LIMIT)


def _sigmoid(v):
    return 1.0 / (1.0 + jnp.exp(-v))


def _rms(v):
    return v * lax.rsqrt(jnp.mean(v * v, axis=-1, keepdims=True) + NORM_EPS)


def _adaln_kernel(c_ref, w_ref, b_ref, o_ref):
    c = c_ref[...]
    s = c * _sigmoid(c)
    o_ref[...] = jnp.dot(s, w_ref[...], precision=HIGHEST, preferred_element_type=F32) + b_ref[...]


def _adaln(cc, w, b):
    rows, d = cc.shape
    n = w.shape[1]
    tn = 1536
    return pl.pallas_call(
        _adaln_kernel,
        grid=(n // tn,),
        in_specs=[pl.BlockSpec((rows, d), lambda j: (0, 0)),
                  pl.BlockSpec((d, tn), lambda j: (0, j)),
                  pl.BlockSpec((1, tn), lambda j: (0, j))],
        out_specs=pl.BlockSpec((rows, tn), lambda j: (0, j)),
        out_shape=jax.ShapeDtypeStruct((rows, n), F32),
        compiler_params=_cparams(("arbitrary",)),
        name="adaln",
    )(cc, w, b.reshape(1, n))


def _inproj_kernel(x_ref, xp_ref, xn_ref, sh_ref, sc_ref, g_ref, w_ref, cos_ref, sin_ref, cw_ref, cb_ref,
                   k_ref, v_ref, q_ref, u_ref, gt_ref, pscr, *, tm, nt, qscale):
    i = pl.program_id(1)
    g = g_ref[...]
    sh = sh_ref[...]
    sc = 1.0 + sc_ref[...]

    def norm_mod(xx):
        return (_rms(xx) * g) * sc + sh

    h = norm_mod(x_ref[...])
    hp = jnp.where(i > 0, norm_mod(xp_ref[...]), 0.0)
    hn = jnp.where(i < nt - 1, norm_mod(xn_ref[...]), 0.0)
    hb = h.astype(BF16)
    h_ext = jnp.concatenate([hp, h, hn], axis=0).astype(BF16)

    def proj(lhs, c0):
        return jnp.dot(lhs, w_ref[:, c0:c0 + 1024], preferred_element_type=F32)

    cosv = cos_ref[...]
    sinv = sin_ref[...]
    lane = lax.broadcasted_iota(jnp.int32, (1, V_DIM), 1)
    first = (lane % ROPE_AXIS_DIM) < (ROPE_AXIS_DIM // 2)

    def rope_store(a, o_ref, scale):
        for hh in range(N_HEADS):
            s = a[:, hh * V_DIM:(hh + 1) * V_DIM]
            partner = jnp.where(first, pltpu.roll(s, V_DIM - ROPE_AXIS_DIM // 2, 1),
                                pltpu.roll(s, ROPE_AXIS_DIM // 2, 1))
            r = s * cosv + partner * sinv
            if scale != 1.0:
                r = r * scale
            o_ref[:, hh * V_DIM:(hh + 1) * V_DIM] = r.astype(BF16)

    rope_store(proj(hb, 0), k_ref, 1.0)
    v_ref[...] = proj(hb, 1024).astype(BF16)
    rope_store(proj(hb, 2048), q_ref, qscale)
    for j in range(3):
        pscr[...] = proj(h_ext, 3072 + j * 1024)
        cw = cw_ref[:, j * 1024:(j + 1) * 1024]
        cb = cb_ref[:, j * 1024:(j + 1) * 1024]
        u = (pscr[pl.ds(7, tm), :] * cw[0:1] + pscr[pl.ds(8, tm), :] * cw[1:2]
             + pscr[pl.ds(9, tm), :] * cw[2:3] + cb)
        u_ref[:, j * 1024:(j + 1) * 1024] = u.astype(BF16)
    for j in range(2):
        gt_ref[:, j * 1024:(j + 1) * 1024] = proj(hb, 6144 + j * 1024).astype(BF16)


def _inproj(x, shift, scale, g, w_bf, cos_t, sin_t, conv_w, conv_b, tm):
    B, L, D = x.shape
    nt = L // tm
    nb8 = L // 8
    qscale = (HEAD_DIM ** -0.5) * LOG2E
    kern = functools.partial(_inproj_kernel, tm=tm, nt=nt, qscale=qscale)
    row = lambda b, i: (b, i, 0)
    per_b = lambda b, i: (b, 0, 0)
    const = lambda b, i: (0, 0)
    outs = pl.pallas_call(
        kern,
        grid=(B, nt),
        in_specs=[
            pl.BlockSpec((None, tm, D), row),
            pl.BlockSpec((None, 8, D), lambda b, i: (b, jnp.maximum(i * (tm // 8) - 1, 0), 0)),
            pl.BlockSpec((None, 8, D), lambda b, i: (b, jnp.minimum((i + 1) * (tm // 8), nb8 - 1), 0)),
            pl.BlockSpec((None, 1, D), per_b),
            pl.BlockSpec((None, 1, D), per_b),
            pl.BlockSpec((1, D), const),
            pl.BlockSpec((D, 8192), const, pipeline_mode=pl.Buffered(1)),
            pl.BlockSpec((tm, V_DIM), lambda b, i: (i, 0)),
            pl.BlockSpec((tm, V_DIM), lambda b, i: (i, 0)),
            pl.BlockSpec((3, 3 * HYENA_W), const),
            pl.BlockSpec((1, 3 * HYENA_W), const),
        ],
        out_specs=[
            pl.BlockSpec((None, tm, 1024), row),
            pl.BlockSpec((None, tm, 1024), row),
            pl.BlockSpec((None, tm, 1024), row),
            pl.BlockSpec((None, tm, 3 * HYENA_W), row),
            pl.BlockSpec((None, tm, 2048), row),
        ],
        out_shape=[
            jax.ShapeDtypeStruct((B, L, 1024), BF16),
            jax.ShapeDtypeStruct((B, L, 1024), BF16),
            jax.ShapeDtypeStruct((B, L, 1024), BF16),
            jax.ShapeDtypeStruct((B, L, 3 * HYENA_W), BF16),
            jax.ShapeDtypeStruct((B, L, 2048), BF16),
        ],
        scratch_shapes=[pltpu.VMEM((tm + 16, 1024), F32)],
        compiler_params=_cparams(("parallel", "arbitrary")),
        name="inproj",
    )(x, x, x, shift, scale, g, w_bf, cos_t, sin_t, conv_w, conv_b)
    return outs


def _ctxproj_kernel(x_ref, sh_ref, sc_ref, g_ref, w_ref, k_ref, v_ref):
    h = ((_rms(x_ref[...]) * g_ref[...]) * (1.0 + sc_ref[...]) + sh_ref[...]).astype(BF16)
    k_ref[...] = jnp.dot(h, w_ref[:, 0:1024], preferred_element_type=F32).astype(BF16)
    v_ref[...] = jnp.dot(h, w_ref[:, 1024:2048], preferred_element_type=F32).astype(BF16)


def _ctxproj(ctx, shift, scale, g, w_bf):
    B, Lc, D = ctx.shape
    return pl.pallas_call(
        _ctxproj_kernel,
        grid=(B,),
        in_specs=[pl.BlockSpec((None, Lc, D), lambda b: (b, 0, 0)),
                  pl.BlockSpec((1, D), lambda b: (0, 0)),
                  pl.BlockSpec((1, D), lambda b: (0, 0)),
                  pl.BlockSpec((1, D), lambda b: (0, 0)),
                  pl.BlockSpec((D, 2048), lambda b: (0, 0))],
        out_specs=[pl.BlockSpec((None, Lc, 1024), lambda b: (b, 0, 0)),
                   pl.BlockSpec((None, Lc, 1024), lambda b: (b, 0, 0))],
        out_shape=[jax.ShapeDtypeStruct((B, Lc, 1024), BF16),
                   jax.ShapeDtypeStruct((B, Lc, 1024), BF16)],
        compiler_params=_cparams(("arbitrary",)),
        name="ctxproj",
    )(ctx, shift, scale, g, w_bf)


def _attn_kernel(lam_ref, q_ref, k_ref, v_ref, kc_ref, vc_ref, g_ref, o_ref, m_s, l_s, acc_s, *, tk, nk):
    q = q_ref[...]
    lane = lax.broadcasted_iota(jnp.int32, (1, V_DIM), 1)
    zero = jnp.zeros_like(q)
    qs = (jnp.where(lane < HEAD_DIM, q, zero), jnp.where(lane >= HEAD_DIM, q, zero))
    m_s[...] = jnp.full(m_s.shape, -1e30, F32)
    l_s[...] = jnp.zeros(l_s.shape, F32)
    acc_s[...] = jnp.zeros(acc_s.shape, F32)

    def chunk(kk, vv):
        reps = kk.shape[0] // V_DIM
        for m in range(2):
            s = lax.dot_general(qs[m], kk, (((1,), (1,)), ((), ())), preferred_element_type=F32)
            m_prev = m_s[m]
            m_new = jnp.maximum(m_prev, jnp.max(s, axis=-1, keepdims=True))
            alpha = jnp.exp2(m_prev - m_new)
            p = jnp.exp2(s - jnp.concatenate([m_new] * reps, axis=1))
            l_s[m] = alpha * l_s[m] + jnp.sum(p, axis=-1, keepdims=True)
            acc_s[m] = alpha * acc_s[m] + jnp.dot(p.astype(BF16), vv, preferred_element_type=F32)
            m_s[m] = m_new

    def body(j, carry):
        off = pl.multiple_of(j * tk, tk)
        chunk(k_ref[pl.ds(off, tk), :], v_ref[pl.ds(off, tk), :])
        return carry

    lax.fori_loop(0, nk, body, 0, unroll=2 if nk % 2 == 0 else 1)
    chunk(kc_ref[...], vc_ref[...])
    lam = lam_ref[0, 0]
    o = acc_s[0] / l_s[0] - lam * (acc_s[1] / l_s[1])
    o = _rms(o) * g_ref[...] * (1.0 - LAM_INIT)
    o_ref[...] = o.astype(BF16)


def _attention(lam, q, k, v, kc, vc, subln_g, tq, tk):
    B, L, _ = q.shape
    Lc = kc.shape[1]
    kern = functools.partial(_attn_kernel, tk=tk, nk=L // tk)
    kv = lambda b, h, i: (b, 0, h)
    return pl.pallas_call(
        kern,
        grid=(B, N_HEADS, L // tq),
        in_specs=[
            pl.BlockSpec(memory_space=pltpu.SMEM),
            pl.BlockSpec((None, tq, V_DIM), lambda b, h, i: (b, i, h)),
            pl.BlockSpec((None, L, V_DIM), kv),
            pl.BlockSpec((None, L, V_DIM), kv),
            pl.BlockSpec((None, Lc, V_DIM), kv),
            pl.BlockSpec((None, Lc, V_DIM), kv),
            pl.BlockSpec((1, V_DIM), lambda b, h, i: (0, 0)),
        ],
        out_specs=pl.BlockSpec((None, tq, V_DIM), lambda b, h, i: (b, i, h)),
        out_shape=jax.ShapeDtypeStruct((B, L, N_HEADS * V_DIM), BF16),
        scratch_shapes=[pltpu.VMEM((2, tq, V_DIM), F32), pltpu.VMEM((2, tq, V_DIM), F32),
                        pltpu.VMEM((2, tq, V_DIM), F32)],
        compiler_params=_cparams(("parallel", "parallel", "arbitrary")),
        name="diffattn",
    )(lam, q, k, v, kc, vc, subln_g)


FFT_SLABS = 64
FFT_STRIP = 16


def _dft_tables(nb):
    N = FFT_SLABS * nb
    fa = jnp.arange(FFT_SLABS, dtype=jnp.int32)[:, None, None]
    fb = jnp.arange(nb, dtype=jnp.int32)[None, :, None]
    b = jnp.arange(nb, dtype=jnp.int32)[None, None, :]
    ang = ((FFT_SLABS * b * fb + b * fa) % N).astype(F32) * (2.0 * math.pi / N)
    c, s = jnp.cos(ang), jnp.sin(ang)
    fwd = jnp.concatenate([jnp.concatenate([c, s], -1), jnp.concatenate([-s, c], -1)], axis=1)
    inv = jnp.swapaxes(fwd, 1, 2) / N
    return fwd.astype(BF16), inv.astype(BF16)


def _vadd(a, b):
    return b if a is None else (a if b is None else a + b)


def _vsub(a, b):
    if b is None:
        return a
    return -b if a is None else a - b


def _vscale(a, k):
    if a is None or k == 1.0:
        return a
    return -a if k == -1.0 else a * k


def _cmulc(x, c, s):
    re, im = x
    eps = 1e-9
    if abs(s) < eps:
        k = 1.0 if c > 0 else -1.0
        return _vscale(re, k), _vscale(im, k)
    if abs(c) < eps:
        k = 1.0 if s > 0 else -1.0
        return _vscale(im, -k), _vscale(re, k)
    if abs(abs(c) - abs(s)) < eps:
        kc = 1.0 if c > 0 else -1.0
        ks = 1.0 if s > 0 else -1.0
        return (_vscale(_vsub(_vscale(re, kc), _vscale(im, ks)), abs(c)),
                _vscale(_vadd(_vscale(re, ks), _vscale(im, kc)), abs(c)))
    return _vsub(_vscale(re, c), _vscale(im, s)), _vadd(_vscale(re, s), _vscale(im, c))


def _fft(xs, sign):
    n = len(xs)
    if n == 1:
        return xs
    ev, od = _fft(xs[0::2], sign), _fft(xs[1::2], sign)
    out = [None] * n
    for k in range(n // 2):
        ang = sign * 2.0 * math.pi * k / n
        t = _cmulc(od[k], math.cos(ang), math.sin(ang))
        out[k] = (_vadd(ev[k][0], t[0]), _vadd(ev[k][1], t[1]))
        out[k + n // 2] = (_vsub(ev[k][0], t[0]), _vsub(ev[k][1], t[1]))
    return out


def _put(ref, slot, rows, val):
    ref[slot, rows, :] = jnp.zeros((FFT_STRIP, ref.shape[2]), F32) if val is None else val


def _slab_fft_forward(load, s_ref, nb, n_in):
    def strip(i, carry):
        r0 = pl.multiple_of(i * FFT_STRIP, FFT_STRIP)
        rows_re, rows_im = pl.ds(r0, FFT_STRIP), pl.ds(nb + r0, FFT_STRIP)
        for a0 in range(8):
            xs = [load(8 * a1 + a0, r0) if a1 < n_in else (None, None) for a1 in range(8)]
            ys = _fft(xs, -1.0)
            for f1 in range(8):
                ang = -2.0 * math.pi * a0 * f1 / FFT_SLABS
                yr, yi = _cmulc(ys[f1], math.cos(ang), math.sin(ang))
                _put(s_ref, a0 * 8 + f1, rows_re, yr)
                _put(s_ref, a0 * 8 + f1, rows_im, yi)
        for f1 in range(8):
            xs = [(s_ref[a0 * 8 + f1, rows_re, :], s_ref[a0 * 8 + f1, rows_im, :]) for a0 in range(8)]
            ys = _fft(xs, -1.0)
            for f0 in range(8):
                s_ref[f1 + 8 * f0, rows_re, :] = ys[f0][0]
                s_ref[f1 + 8 * f0, rows_im, :] = ys[f0][1]
        return carry

    lax.fori_loop(0, nb // FFT_STRIP, strip, 0)


def _slab_fft_inverse(s_ref, nb, emit):
    def strip(i, carry):
        r0 = pl.multiple_of(i * FFT_STRIP, FFT_STRIP)
        rows_re, rows_im = pl.ds(r0, FFT_STRIP), pl.ds(nb + r0, FFT_STRIP)
        for q in range(8):
            xs = [(s_ref[8 * p + q, rows_re, :], s_ref[8 * p + q, rows_im, :]) for p in range(8)]
            us = _fft(xs, 1.0)
            for r in range(8):
                ang = 2.0 * math.pi * r * q / FFT_SLABS
                ur, ui = _cmulc(us[r], math.cos(ang), math.sin(ang))
                s_ref[8 * r + q, rows_re, :] = ur
                s_ref[8 * r + q, rows_im, :] = ui
        for r in range(8):
            xs = [(s_ref[8 * r + q, rows_re, :], s_ref[8 * r + q, rows_im, :]) for q in range(8)]
            ys = _fft(xs, 1.0)
            for sidx in range(4):
                emit(r + 8 * sidx, r0, ys[sidx][0], ys[sidx][1])
        return carry

    lax.fori_loop(0, nb // FFT_STRIP, strip, 0)


def _hyena_conv_kernel(v_ref, x_ref, k_ref, ff_ref, fi_ref, bias_ref, o_ref, s_ref, *, nb):
    def load(a, r0):
        rows = pl.ds(pl.multiple_of(a * nb + r0, FFT_STRIP), FFT_STRIP)
        return v_ref[0, rows, :].astype(F32), v_ref[1, rows, :].astype(F32)

    _slab_fft_forward(load, s_ref, nb, FFT_SLABS // 16)

    def freq(f, carry):
        z = jnp.dot(ff_ref[f], s_ref[f].astype(BF16), preferred_element_type=F32)
        zr, zi = z[:nb], z[nb:]
        kr, ki = k_ref[0, f], k_ref[1, f]
        y = jnp.concatenate([zr * kr - zi * ki, zr * ki + zi * kr], axis=0).astype(BF16)
        s_ref[f] = jnp.dot(fi_ref[f], y, preferred_element_type=F32)
        return carry

    lax.fori_loop(0, FFT_SLABS, freq, 0, unroll=8)
    bias = bias_ref[...]

    def emit(a, r0, re, im):
        rows = pl.ds(pl.multiple_of(a * nb + r0, FFT_STRIP), FFT_STRIP)
        for half, y in enumerate((re, im)):
            v = v_ref[half, rows, :].astype(F32)
            o_ref[half, rows, :] = (x_ref[half, rows, :].astype(F32) * (y + v * bias)).astype(BF16)

    _slab_fft_inverse(s_ref, nb, emit)


def _hyena_conv(uv, zv, k_hat, ff, fi, bias, order, in_col, gate_col, nb):
    _, P, L, _ = uv.shape
    W = HYENA_W
    nct = W // 128
    src, src_off = (uv, in_col * nct) if zv is None else (zv, 0)
    blk = (2, None, L, 128)
    return pl.pallas_call(
        functools.partial(_hyena_conv_kernel, nb=nb),
        grid=(nct, P),
        in_specs=[pl.BlockSpec(blk, lambda c, p: (0, p, 0, src_off + c)),
                  pl.BlockSpec(blk, lambda c, p: (0, p, 0, gate_col * nct + c)),
                  pl.BlockSpec((2, FFT_SLABS, nb, 128), lambda c, p: (0, 0, 0, order * nct + c),
                               pipeline_mode=pl.Buffered(1)),
                  pl.BlockSpec((FFT_SLABS, 2 * nb, 2 * nb), lambda c, p: (0, 0, 0), pipeline_mode=pl.Buffered(1)),
                  pl.BlockSpec((FFT_SLABS, 2 * nb, 2 * nb), lambda c, p: (0, 0, 0), pipeline_mode=pl.Buffered(1)),
                  pl.BlockSpec((1, 128), lambda c, p: (0, c))],
        out_specs=pl.BlockSpec(blk, lambda c, p: (0, p, 0, c)),
        out_shape=jax.ShapeDtypeStruct((2, P, L, W), BF16),
        scratch_shapes=[pltpu.VMEM((FFT_SLABS, 2 * nb, 128), F32)],
        compiler_params=_cparams(("parallel", "arbitrary")),
        name="hyena_conv%d" % order,
    )(src, uv, k_hat, ff, fi, bias)


def _filter_features(n):
    pos = jnp.arange(n, dtype=F32)[:, None]
    t = jnp.linspace(0.0, 1.0, n, dtype=F32)[:, None]
    w = 2 * math.pi * pos / n
    bands = jnp.linspace(1e-4, FILTER_BANDS - 1, FILTER_BANDS, dtype=F32)
    z = jnp.concatenate([t, jnp.cos(bands * w), -jnp.sin(bands * w)], axis=-1)
    z = jnp.pad(z, ((0, 0), (0, 128 - FILTER_EMB)))
    tap = jnp.arange(2 * n)
    return z[jnp.where(tap < n, tap, (2 * n - tap) % n)]


def _filt_taps_kernel(z_ref, w1_ref, b1_ref, fr_ref, w2_ref, b2_ref, w3_ref, dl_ref, k_ref, sum_ref):
    a = pl.program_id(0)
    z = z_ref[...]
    fr = fr_ref[...]
    hd = jnp.sin(fr * (jnp.dot(z, w1_ref[...], precision=HIGHEST, preferred_element_type=F32) + b1_ref[...]))
    hd = jnp.sin(fr * (jnp.dot(hd, w2_ref[...], precision=HIGHEST, preferred_element_type=F32) + b2_ref[...]))
    h = jnp.dot(hd, w3_ref[...], precision=HIGHEST, preferred_element_type=F32) * jnp.exp(-z[:, 0:1] * dl_ref[...])
    rows = lax.broadcasted_iota(jnp.int32, (z.shape[0], 1), 0)
    h = jnp.where(jnp.logical_and(a == FFT_SLABS // 2, rows == 0), 0.0, h)
    k_ref[...] = h

    @pl.when(a == 0)
    def _():
        sum_ref[...] = jnp.zeros_like(sum_ref)

    sum_ref[0:1, :] += jnp.sum(jnp.abs(h), axis=0, keepdims=True)


def _filt_spectrum_kernel(t_ref, inv_ref, ff_ref, k_ref, s_ref, *, nb):
    def load(a, r0):
        return t_ref[pl.ds(pl.multiple_of(a * nb + r0, FFT_STRIP), FFT_STRIP), :], None

    _slab_fft_forward(load, s_ref, nb, FFT_SLABS // 8)
    inv = inv_ref[...]

    def freq(f, carry):
        z = jnp.dot(ff_ref[f], s_ref[f].astype(BF16), preferred_element_type=F32)
        k_ref[0, f] = z[:nb] * inv
        k_ref[1, f] = z[nb:] * inv
        return carry

    lax.fori_loop(0, FFT_SLABS, freq, 0, unroll=8)


def _hyena_filters(n, filt, ff, nb):
    w1, b1, freq, w2, b2, w3 = filt
    W = HYENA_W
    hid = w1.shape[1]
    z = _filter_features(n)
    w1p = jnp.pad(w1.astype(F32), ((0, 128 - FILTER_EMB), (0, 0)))
    w3d = w3.astype(F32).reshape(hid, 2, 2, W)
    w3sel = jnp.stack([w3d[:, :, d, :].reshape(hid, 2 * W) for d in range(2)])
    deltas = jnp.abs(jnp.linspace(math.log(DECAY_TARGET) / SLOW_DECAY_PCT,
                                  math.log(DECAY_TARGET) / FAST_DECAY_PCT, W, dtype=F32))
    deltas = jnp.tile(deltas.reshape(1, W), (1, 2))
    const = lambda a: (0, 0)
    taps, sums = pl.pallas_call(
        _filt_taps_kernel,
        grid=(FFT_SLABS,),
        in_specs=[pl.BlockSpec((nb, 128), lambda a: (a, 0)),
                  pl.BlockSpec((128, hid), const), pl.BlockSpec((1, hid), const), pl.BlockSpec((1, hid), const),
                  pl.BlockSpec((hid, hid), const), pl.BlockSpec((1, hid), const),
                  pl.BlockSpec((None, hid, 2 * W), lambda a: (a // (FFT_SLABS // 2), 0, 0)),
                  pl.BlockSpec((1, 2 * W), const)],
        out_specs=[pl.BlockSpec((nb, 2 * W), lambda a: (a, 0)), pl.BlockSpec((8, 2 * W), const)],
        out_shape=[jax.ShapeDtypeStruct((2 * n, 2 * W), F32), jax.ShapeDtypeStruct((8, 2 * W), F32)],
        compiler_params=_cparams(("arbitrary",)),
        name="hyena_filter_taps",
    )(z, w1p, b1.reshape(1, hid), freq.reshape(1, hid), w2, b2.reshape(1, hid), w3sel, deltas)
    inv = 1.0 / sums[0:1]
    return pl.pallas_call(
        functools.partial(_filt_spectrum_kernel, nb=nb),
        grid=(2 * W // 128,),
        in_specs=[pl.BlockSpec((2 * n, 128), lambda j: (0, j)),
                  pl.BlockSpec((1, 128), lambda j: (0, j)),
                  pl.BlockSpec((FFT_SLABS, 2 * nb, 2 * nb), lambda j: (0, 0, 0), pipeline_mode=pl.Buffered(1))],
        out_specs=pl.BlockSpec((2, FFT_SLABS, nb, 128), lambda j: (0, 0, 0, j)),
        out_shape=jax.ShapeDtypeStruct((2, FFT_SLABS, nb, 2 * W), F32),
        scratch_shapes=[pltpu.VMEM((FFT_SLABS, 2 * nb, 128), F32)],
        compiler_params=_cparams(("parallel",)),
        name="hyena_filter_spectrum",
    )(taps, inv, ff)


def _hyena(u, filt, hy_bias):
    B, L, _ = u.shape
    P = B // 2
    nb = 2 * L // FFT_SLABS
    ff, fi = _dft_tables(nb)
    k_hat = _hyena_filters(L, filt, ff, nb)
    uv = u.reshape(2, P, L, 3 * HYENA_W)
    bias = hy_bias.astype(F32)
    z = _hyena_conv(uv, None, k_hat, ff, fi, bias[0:1], 0, 0, 1, nb)
    y = _hyena_conv(uv, z, k_hat, ff, fi, bias[1:2], 1, 0, 2, nb)
    return y.reshape(B, L, HYENA_W)


def _merge_kernel(a_ref, h_ref, gt_ref, x_ref, g1_ref, sh_ref, sc_ref, n2g_ref, wpa_ref, wph_ref, wo_ref, rw_ref,
                  x1_ref, hx_ref, lg_ref):
    ya = jnp.dot(a_ref[...], wpa_ref[...], preferred_element_type=F32)
    yh = jnp.dot(h_ref[...], wph_ref[...], preferred_element_type=F32)
    ga = gt_ref[:, 0:1024].astype(F32)
    gh = gt_ref[:, 1024:2048].astype(F32)
    y = _sigmoid(ga) * ya + _sigmoid(gh) * yh
    mix = jnp.dot(y.astype(BF16), wo_ref[...], preferred_element_type=F32)
    x1 = x_ref[...] + g1_ref[...] * mix
    x1_ref[...] = x1
    hx = (_rms(x1) * n2g_ref[...]) * (1.0 + sc_ref[...]) + sh_ref[...]
    _to_tiles(hx_ref, hx)
    lg_ref[...] = lax.dot_general(rw_ref[...], hx, (((1,), (1,)), ((), ())), precision=HIGHEST,
                                  preferred_element_type=F32)


def _merge(attn, hy, gates, x, g1, sh2, sc2, n2g, wpa, wph, wo, rwT, tm):
    B, L, D = x.shape
    nt = L // tm
    row = lambda b, i: (b, i, 0)
    per_b = lambda b, i: (b, 0, 0)
    const = lambda b, i: (0, 0)
    return pl.pallas_call(
        _merge_kernel,
        grid=(B, nt),
        in_specs=[pl.BlockSpec((None, tm, 1024), row), pl.BlockSpec((None, tm, 1024), row),
                  pl.BlockSpec((None, tm, 2048), row), pl.BlockSpec((None, tm, D), row),
                  pl.BlockSpec((None, 1, D), per_b), pl.BlockSpec((None, 1, D), per_b),
                  pl.BlockSpec((None, 1, D), per_b), pl.BlockSpec((1, D), const),
                  pl.BlockSpec((1024, D), const), pl.BlockSpec((1024, D), const), pl.BlockSpec((D, D), const),
                  pl.BlockSpec((N_EXPERTS, D), const)],
        out_specs=[pl.BlockSpec((None, tm, D), row),
                   pl.BlockSpec((tm * TILE_ROWS, 128), lambda b, i: (b * nt + i, 0)),
                   pl.BlockSpec((N_EXPERTS, tm), lambda b, i: (0, b * nt + i))],
        out_shape=[jax.ShapeDtypeStruct((B, L, D), F32),
                   jax.ShapeDtypeStruct((B * L * TILE_ROWS, 128), F32),
                   jax.ShapeDtypeStruct((N_EXPERTS, B * L), F32)],
        compiler_params=_cparams(("parallel", "arbitrary")),
        name="merge",
    )(attn, hy, gates, x, g1, sh2, sc2, n2g, wpa, wph, wo, rwT)


def _router_kernel(lg_ref, rb_ref, ei_ref, wt_ref):
    scores = _sigmoid(lg_ref[...])
    choice = scores + rb_ref[...]
    E, T = scores.shape
    gsz = E // N_GROUPS
    neg = jnp.float32(-jnp.inf)
    big = jnp.int32(1 << 30)
    rows8 = lax.broadcasted_iota(jnp.int32, (gsz, T), 0)
    gscore = []
    for gi in range(N_GROUPS):
        c = choice[gi * gsz:(gi + 1) * gsz]
        m1 = jnp.max(c, axis=0, keepdims=True)
        i1 = jnp.min(jnp.where(c == m1, rows8, big), axis=0, keepdims=True)
        m2 = jnp.max(jnp.where(rows8 == i1, neg, c), axis=0, keepdims=True)
        gscore.append(m1 + m2)
    gs = jnp.concatenate(gscore, axis=0)
    rowsg = lax.broadcasted_iota(jnp.int32, (N_GROUPS, T), 0)
    gsel = jnp.zeros((N_GROUPS, T), jnp.bool_)
    for _ in range(TOPK_GROUPS):
        m = jnp.max(gs, axis=0, keepdims=True)
        ix = jnp.min(jnp.where(gs == m, rowsg, big), axis=0, keepdims=True)
        hit = rowsg == ix
        gsel = jnp.logical_or(gsel, hit)
        gs = jnp.where(hit, neg, gs)
    masked = jnp.concatenate(
        [jnp.where(gsel[gi:gi + 1], choice[gi * gsz:(gi + 1) * gsz], neg) for gi in range(N_GROUPS)], axis=0)
    rows = lax.broadcasted_iota(jnp.int32, (E, T), 0)
    ids, ws = [], []
    for _ in range(TOP_K):
        m = jnp.max(masked, axis=0, keepdims=True)
        ix = jnp.min(jnp.where(masked == m, rows, big), axis=0, keepdims=True)
        hit = rows == ix
        ids.append(ix)
        ws.append(jnp.sum(jnp.where(hit, scores, 0.0), axis=0, keepdims=True))
        masked = jnp.where(hit, neg, masked)
    w = jnp.concatenate(ws, axis=0)
    ei_ref[...] = jnp.concatenate(ids, axis=0)
    wt_ref[...] = w / jnp.sum(w, axis=0, keepdims=True) * ROUTED_SCALE


def _router(logits_t, router_bias, tt):
    E, N = logits_t.shape
    return pl.pallas_call(
        _router_kernel,
        grid=(N // tt,),
        in_specs=[pl.BlockSpec((E, tt), lambda i: (0, i)), pl.BlockSpec((E, 1), lambda i: (0, 0))],
        out_specs=[pl.BlockSpec((TOP_K, tt), lambda i: (0, i)), pl.BlockSpec((TOP_K, tt), lambda i: (0, i))],
        out_shape=[jax.ShapeDtypeStruct((TOP_K, N), jnp.int32), jax.ShapeDtypeStruct((TOP_K, N), F32)],
        compiler_params=_cparams(("parallel",)),
        name="router",
    )(logits_t, router_bias.reshape(E, 1).astype(F32))


MOE_ROWS = 256
COMBINE_TOKENS = 128


def _dispatch_plan(eidx, wsel, bm):
    K, N = eidx.shape
    E = N_EXPERTS
    sel = jnp.any(eidx[None] == jnp.arange(E, dtype=jnp.int32)[:, None, None], axis=1).astype(jnp.int32)
    csum = jnp.cumsum(sel, axis=1)
    cnt = csum[:, -1]
    padded = (cnt + bm - 1) // bm * bm
    pad_end = jnp.cumsum(padded)
    pad_start = pad_end - padded
    pos = pad_start[eidx] + jnp.take_along_axis(csum - sel, eidx, axis=0)
    nb = -(-(N * K + E * (bm - 1)) // bm)
    flat = pos.reshape(-1)
    tok = jnp.broadcast_to(jnp.arange(N, dtype=jnp.int32)[None], (K, N)).reshape(-1)
    row_tok = jnp.zeros((nb * bm,), jnp.int32).at[flat].set(tok)
    row_w = jnp.zeros((nb * bm,), F32).at[flat].set(wsel.reshape(-1))
    blk_e = jnp.minimum(jnp.searchsorted(pad_end, jnp.arange(nb, dtype=jnp.int32) * bm, side='right'), E - 1)
    return row_tok, row_w, blk_e.astype(jnp.int32), pos, nb


TILE_ROWS = 8


def _to_tiles(ref, val):
    m = val.shape[0]
    for j in range(TILE_ROWS):
        ref[pl.ds(j, m, stride=TILE_ROWS), :] = val[:, j * 128:(j + 1) * 128]


def _from_tiles(ref, start, m):
    return jnp.concatenate([ref[pl.ds(start + j, m, stride=TILE_ROWS), :] for j in range(TILE_ROWS)], axis=1)


def _row_copy(idx_ref, r, src_hbm, buf, base, sem):
    src = pl.multiple_of(idx_ref[0, r] * TILE_ROWS, TILE_ROWS)
    dst = pl.multiple_of(base + r * TILE_ROWS, TILE_ROWS)
    return pltpu.make_async_copy(src_hbm.at[pl.ds(src, TILE_ROWS), :], buf.at[pl.ds(dst, TILE_ROWS), :], sem)


def _pipelined_gather(cur_ref, nxt_ref, src_hbm, buf, sem, nrows):
    i = pl.program_id(0)
    slot = i % 2
    span = nrows * TILE_ROWS

    @pl.when(i == 0)
    def _():
        def body(r, carry):
            _row_copy(cur_ref, r, src_hbm, buf, 0, sem.at[0]).start()
            return carry
        lax.fori_loop(0, nrows, body, 0)

    @pl.when(i + 1 < pl.num_programs(0))
    def _():
        base = (1 - slot) * span
        for r in range(nrows):
            _row_copy(nxt_ref, r, src_hbm, buf, base, sem.at[1 - slot]).start(priority=r % 2)

    here = pl.multiple_of(slot * span, span)
    pltpu.make_async_copy(src_hbm.at[pl.ds(0, span), :], buf.at[pl.ds(here, span), :], sem.at[slot]).wait()
    return here


def _expert_kernel(blk_e_ref, cur_ref, nxt_ref, hx_hbm, wg_ref, wu_ref, wd_ref, rw_ref, y_ref, xbuf, sem, *, bm):
    del blk_e_ref
    here = _pipelined_gather(cur_ref, nxt_ref, hx_hbm, xbuf, sem, bm)
    x = _from_tiles(xbuf, here, bm).astype(BF16)
    hg = jnp.dot(x, wg_ref[...].astype(BF16), preferred_element_type=F32)
    hu = jnp.dot(x, wu_ref[...].astype(BF16), preferred_element_type=F32)
    act = hg * _sigmoid(hg) * hu
    rw = jnp.concatenate([rw_ref[...]] * (EXPERT_HIDDEN // 128), axis=1)
    y = jnp.dot((act * rw).astype(BF16), wd_ref[...].astype(BF16), preferred_element_type=F32)
    _to_tiles(y_ref, y)


def _experts(hx_tiles, row_tok, row_w, blk_e, wg, wu, wd, nb, bm):
    D = wg.shape[1]
    tok3 = row_tok.reshape(nb, 1, bm)
    rw = jnp.broadcast_to(row_w[:, None], (nb * bm, 128))
    smem_blk = lambda f: pl.BlockSpec((None, 1, bm), f, memory_space=pltpu.SMEM)
    grid_spec = pltpu.PrefetchScalarGridSpec(
        num_scalar_prefetch=1,
        grid=(nb,),
        in_specs=[smem_blk(lambda i, be: (i, 0, 0)),
                  smem_blk(lambda i, be: (jnp.minimum(i + 1, nb - 1), 0, 0)),
                  pl.BlockSpec(memory_space=pl.ANY),
                  pl.BlockSpec((None, D, EXPERT_HIDDEN), lambda i, be: (be[i], 0, 0)),
                  pl.BlockSpec((None, D, EXPERT_HIDDEN), lambda i, be: (be[i], 0, 0)),
                  pl.BlockSpec((None, EXPERT_HIDDEN, D), lambda i, be: (be[i], 0, 0)),
                  pl.BlockSpec((bm, 128), lambda i, be: (i, 0))],
        out_specs=pl.BlockSpec((bm * TILE_ROWS, 128), lambda i, be: (i, 0)),
        scratch_shapes=[pltpu.VMEM((2 * bm * TILE_ROWS, 128), F32), pltpu.SemaphoreType.DMA((2,))],
    )
    return pl.pallas_call(
        functools.partial(_expert_kernel, bm=bm),
        grid_spec=grid_spec,
        out_shape=jax.ShapeDtypeStruct((nb * bm * TILE_ROWS, 128), F32),
        compiler_params=_cparams(("arbitrary",)),
        name="moe_experts",
    )(blk_e, tok3, tok3, hx_tiles, wg, wu, wd, rw)


def _combine_kernel(cur_ref, nxt_ref, y_hbm, hx_ref, sgu_ref, sd_ref, x1_ref, g2_ref, fg_ref, o_ref, ybuf, sem, *, tm):
    here = _pipelined_gather(cur_ref, nxt_ref, y_hbm, ybuf, sem, TOP_K * tm)
    routed = _from_tiles(ybuf, here, tm)
    for k in range(1, TOP_K):
        routed = routed + _from_tiles(ybuf, here + k * tm * TILE_ROWS, tm)
    hu = jnp.dot(_from_tiles(hx_ref, 0, tm).astype(BF16), sgu_ref[...], preferred_element_type=F32)
    hg = hu[:, :EXPERT_HIDDEN]
    act = hg * _sigmoid(hg) * hu[:, EXPERT_HIDDEN:]
    shared = jnp.dot(act.astype(BF16), sd_ref[...], preferred_element_type=F32)
    xo = x1_ref[...] + g2_ref[...] * (routed + shared)
    o_ref[...] = _rms(xo) * fg_ref[...]


def _combine(pos, y_tiles, hx_tiles, sgu, sd, x1, g2, fg, tm, tiles_per_batch):
    N, D = x1.shape
    nt = N // tm
    pos3 = pos.reshape(TOP_K, nt, tm).transpose(1, 0, 2).reshape(nt, 1, TOP_K * tm)
    smem_blk = lambda f: pl.BlockSpec((None, 1, TOP_K * tm), f, memory_space=pltpu.SMEM)
    row = lambda i: (i, 0)
    const = lambda i: (0, 0)
    return pl.pallas_call(
        functools.partial(_combine_kernel, tm=tm),
        grid=(nt,),
        in_specs=[smem_blk(lambda i: (i, 0, 0)),
                  smem_blk(lambda i: (jnp.minimum(i + 1, nt - 1), 0, 0)),
                  pl.BlockSpec(memory_space=pl.ANY),
                  pl.BlockSpec((tm * TILE_ROWS, 128), row),
                  pl.BlockSpec((D, 2 * EXPERT_HIDDEN), const),
                  pl.BlockSpec((EXPERT_HIDDEN, D), const),
                  pl.BlockSpec((tm, D), row),
                  pl.BlockSpec((None, 1, D), lambda i: (i // tiles_per_batch, 0, 0)),
                  pl.BlockSpec((1, D), const)],
        out_specs=pl.BlockSpec((tm, D), row),
        out_shape=jax.ShapeDtypeStruct((N, D), F32),
        scratch_shapes=[pltpu.VMEM((2 * TOP_K * tm * TILE_ROWS, 128), F32), pltpu.SemaphoreType.DMA((2,))],
        compiler_params=_cparams(("arbitrary",)),
        name="moe_combine",
    )(pos3, pos3, y_tiles, hx_tiles, sgu, sd, x1, g2, fg)


def _rope_tables(L):
    rows = L // GRID_W
    row = jnp.repeat(jnp.arange(rows), GRID_W).astype(F32)
    col = jnp.tile(jnp.arange(GRID_W), rows).astype(F32)
    inv = ROPE_THETA ** (-jnp.arange(0, ROPE_AXIS_DIM, 2, dtype=F32) / ROPE_AXIS_DIM)
    ang = jnp.stack([row[:, None] * inv, col[:, None] * inv], axis=1)
    c, s = jnp.cos(ang), jnp.sin(ang)
    cos_h = jnp.stack([c, c], axis=2).reshape(L, HEAD_DIM)
    sin_h = jnp.stack([-s, s], axis=2).reshape(L, HEAD_DIM)
    return jnp.tile(cos_h, (1, 2)), jnp.tile(sin_h, (1, 2))


def _tile(n, pref):
    t = min(pref, n)
    while n % t:
        t //= 2
    return t


def kernel(x, c, ctx, c_ctx, ada_w, ada_b, norm1_g, norm2_g, w_in, lam_q1, lam_k1, lam_q2, lam_k2, subln_g, hy_conv_w, hy_conv_b, filt_w1, filt_b1, filt_freq, filt_w2, filt_b2, filt_w3, hy_bias, w_branch_attn, w_branch_hyena, w_out, router_w, router_bias, exp_w_gate, exp_w_up, exp_w_down, shared_w_gate, shared_w_up, shared_w_down, final_norm_g):
    B, L, D = x.shape
    assert ada_w.shape[0] == 1 and B % 2 == 0 and (2 * L) % (FFT_SLABS * FFT_STRIP) == 0
    lam = (jnp.exp(jnp.sum(lam_q1[0].astype(F32) * lam_k1[0].astype(F32)))
           - jnp.exp(jnp.sum(lam_q2[0].astype(F32) * lam_k2[0].astype(F32))) + LAM_INIT).reshape(1, 1)

    cc = jnp.zeros((16, D), F32).at[:B].set(c).at[B].set(c_ctx)
    mod = _adaln(cc, ada_w[0], ada_b[0])
    chunk = lambda r0, r1, j: mod[r0:r1, j * D:(j + 1) * D]
    sh1, sc1, g1, sh2, sc2, g2 = (chunk(0, B, j).reshape(B, 1, D) for j in range(6))
    csh, csc = chunk(B, B + 1, 0), chunk(B, B + 1, 1)

    w_bf = w_in[0].astype(BF16)
    n1g = norm1_g[0].reshape(1, D)
    cos_t, sin_t = _rope_tables(L)
    k, v, q, u, gates = _inproj(x, sh1, sc1, n1g, w_bf, cos_t, sin_t, hy_conv_w[0], hy_conv_b[0].reshape(1, -1),
                                _tile(L, 512))
    kc, vc = _ctxproj(ctx, csh, csc, n1g, w_bf)
    attn = _attention(lam, q, k, v, kc, vc, subln_g[0].reshape(1, V_DIM), _tile(L, 512), _tile(L, 512))

    filt = (filt_w1[0], filt_b1[0], filt_freq[0], filt_w2[0], filt_b2[0], filt_w3[0])
    hy = _hyena(u, filt, hy_bias[0])

    x1, hx2, logits_t = _merge(attn, hy, gates, x, g1, sh2, sc2, norm2_g[0].reshape(1, D),
                               w_branch_attn[0].astype(BF16), w_branch_hyena[0].astype(BF16),
                               w_out[0].astype(BF16), router_w[0].T.astype(F32), _tile(L, 512))
    eidx, wsel = _router(logits_t, router_bias[0], _tile(B * L, 1024))

    N = B * L
    row_tok, row_w, blk_e, pos, nb = _dispatch_plan(eidx, wsel, MOE_ROWS)
    y = _experts(hx2, row_tok, row_w, blk_e, exp_w_gate[0], exp_w_up[0], exp_w_down[0], nb, MOE_ROWS)
    sgu = jnp.concatenate([shared_w_gate[0], shared_w_up[0]], axis=-1).astype(BF16)
    tm = _tile(L, COMBINE_TOKENS)
    out = _combine(pos, y, hx2, sgu, shared_w_down[0].astype(BF16), x1.reshape(N, D), g2,
                   final_norm_g.reshape(1, D), tm, L // tm)
    return out.reshape(B, L, D)
```

```python
import functools
import math

import jax
import jax.numpy as jnp
from jax import lax
from jax.experimental import pallas as pl
from jax.experimental.pallas import tpu as pltpu

F32 = jnp.float32
BF16 = jnp.bfloat16
HIGHEST = lax.Precision.HIGHEST

NORM_EPS = 1e-6
N_HEADS = 8
HEAD_DIM = 64
V_DIM = 2 * HEAD_DIM
GRID_W = 64
ROPE_THETA = 10000.0
ROPE_AXIS_DIM = HEAD_DIM // 2
HYENA_W = 1024
FILTER_EMB = 33
FILTER_BANDS = (FILTER_EMB - 1) // 2
DECAY_TARGET = 1e-2
FAST_DECAY_PCT = 0.3
SLOW_DECAY_PCT = 1.5
N_EXPERTS = 64
N_GROUPS = 8
TOPK_GROUPS = 4
TOP_K = 8
EXPERT_HIDDEN = 256
ROUTED_SCALE = 2.5
LAM_INIT = 0.8 - 0.6 * math.exp(-0.3 * 0)

V7X_VMEM_LIMIT = 56 * 1024 * 1024
LOG2E = 1.4426950408889634


def _cparams(sem):
    return pltpu.CompilerParams(dimension_semantics=sem, vmem_limit_bytes=V7X_VMEM_LIMIT)


def _sigmoid(v):
    return 1.0 / (1.0 + jnp.exp(-v))


def _rms(v):
    return v * lax.rsqrt(jnp.mean(v * v, axis=-1, keepdims=True) + NORM_EPS)


def _adaln_kernel(c_ref, w_ref, b_ref, o_ref):
    c = c_ref[...]
    s = c * _sigmoid(c)
    o_ref[...] = jnp.dot(s, w_ref[...], precision=HIGHEST, preferred_element_type=F32) + b_ref[...]


def _adaln(cc, w, b):
    rows, d = cc.shape
    n = w.shape[1]
    tn = 1536
    return pl.pallas_call(
        _adaln_kernel,
        grid=(n // tn,),
        in_specs=[pl.BlockSpec((rows, d), lambda j: (0, 0)),
                  pl.BlockSpec((d, tn), lambda j: (0, j)),
                  pl.BlockSpec((1, tn), lambda j: (0, j))],
        out_specs=pl.BlockSpec((rows, tn), lambda j: (0, j)),
        out_shape=jax.ShapeDtypeStruct((rows, n), F32),
        compiler_params=_cparams(("arbitrary",)),
        name="adaln",
    )(cc, w, b.reshape(1, n))


def _inproj_kernel(x_ref, xp_ref, xn_ref, sh_ref, sc_ref, g_ref, w_ref, cos_ref, sin_ref, cw_ref, cb_ref,
                   k_ref, v_ref, q_ref, u_ref, gt_ref, pscr, *, tm, nt, qscale):
    i = pl.program_id(1)
    g = g_ref[...]
    sh = sh_ref[...]
    sc = 1.0 + sc_ref[...]

    def norm_mod(xx):
        return (_rms(xx) * g) * sc + sh

    h = norm_mod(x_ref[...])
    hp = jnp.where(i > 0, norm_mod(xp_ref[...]), 0.0)
    hn = jnp.where(i < nt - 1, norm_mod(xn_ref[...]), 0.0)
    hb = h.astype(BF16)
    h_ext = jnp.concatenate([hp, h, hn], axis=0).astype(BF16)

    def proj(lhs, c0):
        return jnp.dot(lhs, w_ref[:, c0:c0 + 1024], preferred_element_type=F32)

    cosv = cos_ref[...]
    sinv = sin_ref[...]
    lane = lax.broadcasted_iota(jnp.int32, (1, V_DIM), 1)
    first = (lane % ROPE_AXIS_DIM) < (ROPE_AXIS_DIM // 2)

    def rope_store(a, o_ref, scale):
        for hh in range(N_HEADS):
            s = a[:, hh * V_DIM:(hh + 1) * V_DIM]
            partner = jnp.where(first, pltpu.roll(s, V_DIM - ROPE_AXIS_DIM // 2, 1),
                                pltpu.roll(s, ROPE_AXIS_DIM // 2, 1))
            r = s * cosv + partner * sinv
            if scale != 1.0:
                r = r * scale
            o_ref[:, hh * V_DIM:(hh + 1) * V_DIM] = r.astype(BF16)

    rope_store(proj(hb, 0), k_ref, 1.0)
    v_ref[...] = proj(hb, 1024).astype(BF16)
    rope_store(proj(hb, 2048), q_ref, qscale)
    for j in range(3):
        pscr[...] = proj(h_ext, 3072 + j * 1024)
        cw = cw_ref[:, j * 1024:(j + 1) * 1024]
        cb = cb_ref[:, j * 1024:(j + 1) * 1024]
        u = (pscr[pl.ds(7, tm), :] * cw[0:1] + pscr[pl.ds(8, tm), :] * cw[1:2]
             + pscr[pl.ds(9, tm), :] * cw[2:3] + cb)
        u_ref[:, j * 1024:(j + 1) * 1024] = u.astype(BF16)
    for j in range(2):
        gt_ref[:, j * 1024:(j + 1) * 1024] = proj(hb, 6144 + j * 1024).astype(BF16)


def _inproj(x, shift, scale, g, w_bf, cos_t, sin_t, conv_w, conv_b, tm):
    B, L, D = x.shape
    nt = L // tm
    nb8 = L // 8
    qscale = (HEAD_DIM ** -0.5) * LOG2E
    kern = functools.partial(_inproj_kernel, tm=tm, nt=nt, qscale=qscale)
    row = lambda b, i: (b, i, 0)
    per_b = lambda b, i: (b, 0, 0)
    const = lambda b, i: (0, 0)
    outs = pl.pallas_call(
        kern,
        grid=(B, nt),
        in_specs=[
            pl.BlockSpec((None, tm, D), row),
            pl.BlockSpec((None, 8, D), lambda b, i: (b, jnp.maximum(i * (tm // 8) - 1, 0), 0)),
            pl.BlockSpec((None, 8, D), lambda b, i: (b, jnp.minimum((i + 1) * (tm // 8), nb8 - 1), 0)),
            pl.BlockSpec((None, 1, D), per_b),
            pl.BlockSpec((None, 1, D), per_b),
            pl.BlockSpec((1, D), const),
            pl.BlockSpec((D, 8192), const, pipeline_mode=pl.Buffered(1)),
            pl.BlockSpec((tm, V_DIM), lambda b, i: (i, 0)),
            pl.BlockSpec((tm, V_DIM), lambda b, i: (i, 0)),
            pl.BlockSpec((3, 3 * HYENA_W), const),
            pl.BlockSpec((1, 3 * HYENA_W), const),
        ],
        out_specs=[
            pl.BlockSpec((None, tm, 1024), row),
            pl.BlockSpec((None, tm, 1024), row),
            pl.BlockSpec((None, tm, 1024), row),
            pl.BlockSpec((None, tm, 3 * HYENA_W), row),
            pl.BlockSpec((None, tm, 2048), row),
        ],
        out_shape=[
            jax.ShapeDtypeStruct((B, L, 1024), BF16),
            jax.ShapeDtypeStruct((B, L, 1024), BF16),
            jax.ShapeDtypeStruct((B, L, 1024), BF16),
            jax.ShapeDtypeStruct((B, L, 3 * HYENA_W), BF16),
            jax.ShapeDtypeStruct((B, L, 2048), BF16),
        ],
        scratch_shapes=[pltpu.VMEM((tm + 16, 1024), F32)],
        compiler_params=_cparams(("parallel", "arbitrary")),
        name="inproj",
    )(x, x, x, shift, scale, g, w_bf, cos_t, sin_t, conv_w, conv_b)
    return outs


def _ctxproj_kernel(x_ref, sh_ref, sc_ref, g_ref, w_ref, k_ref, v_ref):
    h = ((_rms(x_ref[...]) * g_ref[...]) * (1.0 + sc_ref[...]) + sh_ref[...]).astype(BF16)
    k_ref[...] = jnp.dot(h, w_ref[:, 0:1024], preferred_element_type=F32).astype(BF16)
    v_ref[...] = jnp.dot(h, w_ref[:, 1024:2048], preferred_element_type=F32).astype(BF16)


def _ctxproj(ctx, shift, scale, g, w_bf):
    B, Lc, D = ctx.shape
    return pl.pallas_call(
        _ctxproj_kernel,
        grid=(B,),
        in_specs=[pl.BlockSpec((None, Lc, D), lambda b: (b, 0, 0)),
                  pl.BlockSpec((1, D), lambda b: (0, 0)),
                  pl.BlockSpec((1, D), lambda b: (0, 0)),
                  pl.BlockSpec((1, D), lambda b: (0, 0)),
                  pl.BlockSpec((D, 2048), lambda b: (0, 0))],
        out_specs=[pl.BlockSpec((None, Lc, 1024), lambda b: (b, 0, 0)),
                   pl.BlockSpec((None, Lc, 1024), lambda b: (b, 0, 0))],
        out_shape=[jax.ShapeDtypeStruct((B, Lc, 1024), BF16),
                   jax.ShapeDtypeStruct((B, Lc, 1024), BF16)],
        compiler_params=_cparams(("arbitrary",)),
        name="ctxproj",
    )(ctx, shift, scale, g, w_bf)


def _attn_kernel(lam_ref, q_ref, k_ref, v_ref, kc_ref, vc_ref, g_ref, o_ref, m_s, l_s, acc_s, *, tk, nk):
    q = q_ref[...]
    lane = lax.broadcasted_iota(jnp.int32, (1, V_DIM), 1)
    zero = jnp.zeros_like(q)
    qs = (jnp.where(lane < HEAD_DIM, q, zero), jnp.where(lane >= HEAD_DIM, q, zero))
    m_s[...] = jnp.full(m_s.shape, -1e30, F32)
    l_s[...] = jnp.zeros(l_s.shape, F32)
    acc_s[...] = jnp.zeros(acc_s.shape, F32)

    def chunk(kk, vv):
        reps = kk.shape[0] // V_DIM
        for m in range(2):
            s = lax.dot_general(qs[m], kk, (((1,), (1,)), ((), ())), preferred_element_type=F32)
            m_prev = m_s[m]
            m_new = jnp.maximum(m_prev, jnp.max(s, axis=-1, keepdims=True))
            alpha = jnp.exp2(m_prev - m_new)
            p = jnp.exp2(s - jnp.concatenate([m_new] * reps, axis=1))
            l_s[m] = alpha * l_s[m] + jnp.sum(p, axis=-1, keepdims=True)
            acc_s[m] = alpha * acc_s[m] + jnp.dot(p.astype(BF16), vv, preferred_element_type=F32)
            m_s[m] = m_new

    def body(j, carry):
        off = pl.multiple_of(j * tk, tk)
        chunk(k_ref[pl.ds(off, tk), :], v_ref[pl.ds(off, tk), :])
        return carry

    lax.fori_loop(0, nk, body, 0, unroll=2 if nk % 2 == 0 else 1)
    chunk(kc_ref[...], vc_ref[...])
    lam = lam_ref[0, 0]
    o = acc_s[0] / l_s[0] - lam * (acc_s[1] / l_s[1])
    o = _rms(o) * g_ref[...] * (1.0 - LAM_INIT)
    o_ref[...] = o.astype(BF16)


def _attention(lam, q, k, v, kc, vc, subln_g, tq, tk):
    B, L, _ = q.shape
    Lc = kc.shape[1]
    kern = functools.partial(_attn_kernel, tk=tk, nk=L // tk)
    kv = lambda b, h, i: (b, 0, h)
    return pl.pallas_call(
        kern,
        grid=(B, N_HEADS, L // tq),
        in_specs=[
            pl.BlockSpec(memory_space=pltpu.SMEM),
            pl.BlockSpec((None, tq, V_DIM), lambda b, h, i: (b, i, h)),
            pl.BlockSpec((None, L, V_DIM), kv),
            pl.BlockSpec((None, L, V_DIM), kv),
            pl.BlockSpec((None, Lc, V_DIM), kv),
            pl.BlockSpec((None, Lc, V_DIM), kv),
            pl.BlockSpec((1, V_DIM), lambda b, h, i: (0, 0)),
        ],
        out_specs=pl.BlockSpec((None, tq, V_DIM), lambda b, h, i: (b, i, h)),
        out_shape=jax.ShapeDtypeStruct((B, L, N_HEADS * V_DIM), BF16),
        scratch_shapes=[pltpu.VMEM((2, tq, V_DIM), F32), pltpu.VMEM((2, tq, V_DIM), F32),
                        pltpu.VMEM((2, tq, V_DIM), F32)],
        compiler_params=_cparams(("parallel", "parallel", "arbitrary")),
        name="diffattn",
    )(lam, q, k, v, kc, vc, subln_g)


FFT_SLABS = 64
FFT_STRIP = 16


def _dft_tables(nb):
    N = FFT_SLABS * nb
    fa = jnp.arange(FFT_SLABS, dtype=jnp.int32)[:, None, None]
    fb = jnp.arange(nb, dtype=jnp.int32)[None, :, None]
    b = jnp.arange(nb, dtype=jnp.int32)[None, None, :]
    ang = ((FFT_SLABS * b * fb + b * fa) % N).astype(F32) * (2.0 * math.pi / N)
    c, s = jnp.cos(ang), jnp.sin(ang)
    fwd = jnp.concatenate([jnp.concatenate([c, s], -1), jnp.concatenate([-s, c], -1)], axis=1)
    inv = jnp.swapaxes(fwd, 1, 2) / N
    return fwd.astype(BF16), inv.astype(BF16)


def _vadd(a, b):
    return b if a is None else (a if b is None else a + b)


def _vsub(a, b):
    if b is None:
        return a
    return -b if a is None else a - b


def _vscale(a, k):
    if a is None or k == 1.0:
        return a
    return -a if k == -1.0 else a * k


def _cmulc(x, c, s):
    re, im = x
    eps = 1e-9
    if abs(s) < eps:
        k = 1.0 if c > 0 else -1.0
        return _vscale(re, k), _vscale(im, k)
    if abs(c) < eps:
        k = 1.0 if s > 0 else -1.0
        return _vscale(im, -k), _vscale(re, k)
    if abs(abs(c) - abs(s)) < eps:
        kc = 1.0 if c > 0 else -1.0
        ks = 1.0 if s > 0 else -1.0
        return (_vscale(_vsub(_vscale(re, kc), _vscale(im, ks)), abs(c)),
                _vscale(_vadd(_vscale(re, ks), _vscale(im, kc)), abs(c)))
    return _vsub(_vscale(re, c), _vscale(im, s)), _vadd(_vscale(re, s), _vscale(im, c))


def _fft(xs, sign):
    n = len(xs)
    if n == 1:
        return xs
    ev, od = _fft(xs[0::2], sign), _fft(xs[1::2], sign)
    out = [None] * n
    for k in range(n // 2):
        ang = sign * 2.0 * math.pi * k / n
        t = _cmulc(od[k], math.cos(ang), math.sin(ang))
        out[k] = (_vadd(ev[k][0], t[0]), _vadd(ev[k][1], t[1]))
        out[k + n // 2] = (_vsub(ev[k][0], t[0]), _vsub(ev[k][1], t[1]))
    return out


def _put(ref, slot, rows, val):
    ref[slot, rows, :] = jnp.zeros((FFT_STRIP, ref.shape[2]), F32) if val is None else val


def _slab_fft_forward(load, s_ref, nb, n_in):
    def strip(i, carry):
        r0 = pl.multiple_of(i * FFT_STRIP, FFT_STRIP)
        rows_re, rows_im = pl.ds(r0, FFT_STRIP), pl.ds(nb + r0, FFT_STRIP)
        for a0 in range(8):
            xs = [load(8 * a1 + a0, r0) if a1 < n_in else (None, None) for a1 in range(8)]
            ys = _fft(xs, -1.0)
            for f1 in range(8):
                ang = -2.0 * math.pi * a0 * f1 / FFT_SLABS
                yr, yi = _cmulc(ys[f1], math.cos(ang), math.sin(ang))
                _put(s_ref, a0 * 8 + f1, rows_re, yr)
                _put(s_ref, a0 * 8 + f1, rows_im, yi)
        for f1 in range(8):
            xs = [(s_ref[a0 * 8 + f1, rows_re, :], s_ref[a0 * 8 + f1, rows_im, :]) for a0 in range(8)]
            ys = _fft(xs, -1.0)
            for f0 in range(8):
                s_ref[f1 + 8 * f0, rows_re, :] = ys[f0][0]
                s_ref[f1 + 8 * f0, rows_im, :] = ys[f0][1]
        return carry

    lax.fori_loop(0, nb // FFT_STRIP, strip, 0)


def _slab_fft_inverse(s_ref, nb, emit):
    def strip(i, carry):
        r0 = pl.multiple_of(i * FFT_STRIP, FFT_STRIP)
        rows_re, rows_im = pl.ds(r0, FFT_STRIP), pl.ds(nb + r0, FFT_STRIP)
        for q in range(8):
            xs = [(s_ref[8 * p + q, rows_re, :], s_ref[8 * p + q, rows_im, :]) for p in range(8)]
            us = _fft(xs, 1.0)
            for r in range(8):
                ang = 2.0 * math.pi * r * q / FFT_SLABS
                ur, ui = _cmulc(us[r], math.cos(ang), math.sin(ang))
                s_ref[8 * r + q, rows_re, :] = ur
                s_ref[8 * r + q, rows_im, :] = ui
        for r in range(8):
            xs = [(s_ref[8 * r + q, rows_re, :], s_ref[8 * r + q, rows_im, :]) for q in range(8)]
            ys = _fft(xs, 1.0)
            for sidx in range(4):
                emit(r + 8 * sidx, r0, ys[sidx][0], ys[sidx][1])
        return carry

    lax.fori_loop(0, nb // FFT_STRIP, strip, 0)


def _hyena_conv_kernel(v_ref, x_ref, k_ref, ff_ref, fi_ref, bias_ref, o_ref, s_ref, *, nb):
    def load(a, r0):
        rows = pl.ds(pl.multiple_of(a * nb + r0, FFT_STRIP), FFT_STRIP)
        return v_ref[0, rows, :].astype(F32), v_ref[1, rows, :].astype(F32)

    _slab_fft_forward(load, s_ref, nb, FFT_SLABS // 16)

    def freq(f, carry):
        z = jnp.dot(ff_ref[f], s_ref[f].astype(BF16), preferred_element_type=F32)
        zr, zi = z[:nb], z[nb:]
        kr, ki = k_ref[0, f], k_ref[1, f]
        y = jnp.concatenate([zr * kr - zi * ki, zr * ki + zi * kr], axis=0).astype(BF16)
        s_ref[f] = jnp.dot(fi_ref[f], y, preferred_element_type=F32)
        return carry

    lax.fori_loop(0, FFT_SLABS, freq, 0, unroll=8)
    bias = bias_ref[...]

    def emit(a, r0, re, im):
        rows = pl.ds(pl.multiple_of(a * nb + r0, FFT_STRIP), FFT_STRIP)
        for half, y in enumerate((re, im)):
            v = v_ref[half, rows, :].astype(F32)
            o_ref[half, rows, :] = (x_ref[half, rows, :].astype(F32) * (y + v * bias)).astype(BF16)

    _slab_fft_inverse(s_ref, nb, emit)


def _hyena_conv(uv, zv, k_hat, ff, fi, bias, order, in_col, gate_col, nb):
    _, P, L, _ = uv.shape
    W = HYENA_W
    nct = W // 128
    src, src_off = (uv, in_col * nct) if zv is None else (zv, 0)
    blk = (2, None, L, 128)
    return pl.pallas_call(
        functools.partial(_hyena_conv_kernel, nb=nb),
        grid=(nct, P),
        in_specs=[pl.BlockSpec(blk, lambda c, p: (0, p, 0, src_off + c)),
                  pl.BlockSpec(blk, lambda c, p: (0, p, 0, gate_col * nct + c)),
                  pl.BlockSpec((2, FFT_SLABS, nb, 128), lambda c, p: (0, 0, 0, order * nct + c),
                               pipeline_mode=pl.Buffered(1)),
                  pl.BlockSpec((FFT_SLABS, 2 * nb, 2 * nb), lambda c, p: (0, 0, 0), pipeline_mode=pl.Buffered(1)),
                  pl.BlockSpec((FFT_SLABS, 2 * nb, 2 * nb), lambda c, p: (0, 0, 0), pipeline_mode=pl.Buffered(1)),
                  pl.BlockSpec((1, 128), lambda c, p: (0, c))],
        out_specs=pl.BlockSpec(blk, lambda c, p: (0, p, 0, c)),
        out_shape=jax.ShapeDtypeStruct((2, P, L, W), BF16),
        scratch_shapes=[pltpu.VMEM((FFT_SLABS, 2 * nb, 128), F32)],
        compiler_params=_cparams(("parallel", "arbitrary")),
        name="hyena_conv%d" % order,
    )(src, uv, k_hat, ff, fi, bias)


def _filter_features(n):
    pos = jnp.arange(n, dtype=F32)[:, None]
    t = jnp.linspace(0.0, 1.0, n, dtype=F32)[:, None]
    w = 2 * math.pi * pos / n
    bands = jnp.linspace(1e-4, FILTER_BANDS - 1, FILTER_BANDS, dtype=F32)
    z = jnp.concatenate([t, jnp.cos(bands * w), -jnp.sin(bands * w)], axis=-1)
    z = jnp.pad(z, ((0, 0), (0, 128 - FILTER_EMB)))
    tap = jnp.arange(2 * n)
    return z[jnp.where(tap < n, tap, (2 * n - tap) % n)]


def _filt_taps_kernel(z_ref, w1_ref, b1_ref, fr_ref, w2_ref, b2_ref, w3_ref, dl_ref, k_ref, sum_ref):
    a = pl.program_id(0)
    z = z_ref[...]
    fr = fr_ref[...]
    hd = jnp.sin(fr * (jnp.dot(z, w1_ref[...], precision=HIGHEST, preferred_element_type=F32) + b1_ref[...]))
    hd = jnp.sin(fr * (jnp.dot(hd, w2_ref[...], precision=HIGHEST, preferred_element_type=F32) + b2_ref[...]))
    h = jnp.dot(hd, w3_ref[...], precision=HIGHEST, preferred_element_type=F32) * jnp.exp(-z[:, 0:1] * dl_ref[...])
    rows = lax.broadcasted_iota(jnp.int32, (z.shape[0], 1), 0)
    h = jnp.where(jnp.logical_and(a == FFT_SLABS // 2, rows == 0), 0.0, h)
    k_ref[...] = h

    @pl.when(a == 0)
    def _():
        sum_ref[...] = jnp.zeros_like(sum_ref)

    sum_ref[0:1, :] += jnp.sum(jnp.abs(h), axis=0, keepdims=True)


def _filt_spectrum_kernel(t_ref, inv_ref, ff_ref, k_ref, s_ref, *, nb):
    def load(a, r0):
        return t_ref[pl.ds(pl.multiple_of(a * nb + r0, FFT_STRIP), FFT_STRIP), :], None

    _slab_fft_forward(load, s_ref, nb, FFT_SLABS // 8)
    inv = inv_ref[...]

    def freq(f, carry):
        z = jnp.dot(ff_ref[f], s_ref[f].astype(BF16), preferred_element_type=F32)
        k_ref[0, f] = z[:nb] * inv
        k_ref[1, f] = z[nb:] * inv
        return carry

    lax.fori_loop(0, FFT_SLABS, freq, 0, unroll=8)


def _hyena_filters(n, filt, ff, nb):
    w1, b1, freq, w2, b2, w3 = filt
    W = HYENA_W
    hid = w1.shape[1]
    z = _filter_features(n)
    w1p = jnp.pad(w1.astype(F32), ((0, 128 - FILTER_EMB), (0, 0)))
    w3d = w3.astype(F32).reshape(hid, 2, 2, W)
    w3sel = jnp.stack([w3d[:, :, d, :].reshape(hid, 2 * W) for d in range(2)])
    deltas = jnp.abs(jnp.linspace(math.log(DECAY_TARGET) / SLOW_DECAY_PCT,
                                  math.log(DECAY_TARGET) / FAST_DECAY_PCT, W, dtype=F32))
    deltas = jnp.tile(deltas.reshape(1, W), (1, 2))
    const = lambda a: (0, 0)
    taps, sums = pl.pallas_call(
        _filt_taps_kernel,
        grid=(FFT_SLABS,),
        in_specs=[pl.BlockSpec((nb, 128), lambda a: (a, 0)),
                  pl.BlockSpec((128, hid), const), pl.BlockSpec((1, hid), const), pl.BlockSpec((1, hid), const),
                  pl.BlockSpec((hid, hid), const), pl.BlockSpec((1, hid), const),
                  pl.BlockSpec((None, hid, 2 * W), lambda a: (a // (FFT_SLABS // 2), 0, 0)),
                  pl.BlockSpec((1, 2 * W), const)],
        out_specs=[pl.BlockSpec((nb, 2 * W), lambda a: (a, 0)), pl.BlockSpec((8, 2 * W), const)],
        out_shape=[jax.ShapeDtypeStruct((2 * n, 2 * W), F32), jax.ShapeDtypeStruct((8, 2 * W), F32)],
        compiler_params=_cparams(("arbitrary",)),
        name="hyena_filter_taps",
    )(z, w1p, b1.reshape(1, hid), freq.reshape(1, hid), w2, b2.reshape(1, hid), w3sel, deltas)
    inv = 1.0 / sums[0:1]
    return pl.pallas_call(
        functools.partial(_filt_spectrum_kernel, nb=nb),
        grid=(2 * W // 128,),
        in_specs=[pl.BlockSpec((2 * n, 128), lambda j: (0, j)),
                  pl.BlockSpec((1, 128), lambda j: (0, j)),
                  pl.BlockSpec((FFT_SLABS, 2 * nb, 2 * nb), lambda j: (0, 0, 0), pipeline_mode=pl.Buffered(1))],
        out_specs=pl.BlockSpec((2, FFT_SLABS, nb, 128), lambda j: (0, 0, 0, j)),
        out_shape=jax.ShapeDtypeStruct((2, FFT_SLABS, nb, 2 * W), F32),
        scratch_shapes=[pltpu.VMEM((FFT_SLABS, 2 * nb, 128), F32)],
        compiler_params=_cparams(("parallel",)),
        name="hyena_filter_spectrum",
    )(taps, inv, ff)


def _hyena(u, filt, hy_bias):
    B, L, _ = u.shape
    P = B // 2
    nb = 2 * L // FFT_SLABS
    ff, fi = _dft_tables(nb)
    k_hat = _hyena_filters(L, filt, ff, nb)
    uv = u.reshape(2, P, L, 3 * HYENA_W)
    bias = hy_bias.astype(F32)
    z = _hyena_conv(uv, None, k_hat, ff, fi, bias[0:1], 0, 0, 1, nb)
    y = _hyena_conv(uv, z, k_hat, ff, fi, bias[1:2], 1, 0, 2, nb)
    return y.reshape(B, L, HYENA_W)


def _merge_kernel(a_ref, h_ref, gt_ref, x_ref, g1_ref, sh_ref, sc_ref, n2g_ref, wpa_ref, wph_ref, wo_ref, rw_ref,
                  x1_ref, hx_ref, lg_ref):
    ya = jnp.dot(a_ref[...], wpa_ref[...], preferred_element_type=F32)
    yh = jnp.dot(h_ref[...], wph_ref[...], preferred_element_type=F32)
    ga = gt_ref[:, 0:1024].astype(F32)
    gh = gt_ref[:, 1024:2048].astype(F32)
    y = _sigmoid(ga) * ya + _sigmoid(gh) * yh
    mix = jnp.dot(y.astype(BF16), wo_ref[...], preferred_element_type=F32)
    x1 = x_ref[...] + g1_ref[...] * mix
    x1_ref[...] = x1
    hx = (_rms(x1) * n2g_ref[...]) * (1.0 + sc_ref[...]) + sh_ref[...]
    _to_tiles(hx_ref, hx)
    lg_ref[...] = lax.dot_general(rw_ref[...], hx, (((1,), (1,)), ((), ())), precision=HIGHEST,
                                  preferred_element_type=F32)


def _merge(attn, hy, gates, x, g1, sh2, sc2, n2g, wpa, wph, wo, rwT, tm):
    B, L, D = x.shape
    nt = L // tm
    row = lambda b, i: (b, i, 0)
    per_b = lambda b, i: (b, 0, 0)
    const = lambda b, i: (0, 0)
    return pl.pallas_call(
        _merge_kernel,
        grid=(B, nt),
        in_specs=[pl.BlockSpec((None, tm, 1024), row), pl.BlockSpec((None, tm, 1024), row),
                  pl.BlockSpec((None, tm, 2048), row), pl.BlockSpec((None, tm, D), row),
                  pl.BlockSpec((None, 1, D), per_b), pl.BlockSpec((None, 1, D), per_b),
                  pl.BlockSpec((None, 1, D), per_b), pl.BlockSpec((1, D), const),
                  pl.BlockSpec((1024, D), const), pl.BlockSpec((1024, D), const), pl.BlockSpec((D, D), const),
                  pl.BlockSpec((N_EXPERTS, D), const)],
        out_specs=[pl.BlockSpec((None, tm, D), row),
                   pl.BlockSpec((tm * TILE_ROWS, 128), lambda b, i: (b * nt + i, 0)),
                   pl.BlockSpec((N_EXPERTS, tm), lambda b, i: (0, b * nt + i))],
        out_shape=[jax.ShapeDtypeStruct((B, L, D), F32),
                   jax.ShapeDtypeStruct((B * L * TILE_ROWS, 128), F32),
                   jax.ShapeDtypeStruct((N_EXPERTS, B * L), F32)],
        compiler_params=_cparams(("parallel", "arbitrary")),
        name="merge",
    )(attn, hy, gates, x, g1, sh2, sc2, n2g, wpa, wph, wo, rwT)


def _router_kernel(lg_ref, rb_ref, tri_ref, ei_ref, wt_ref, rk_ref, cnt_ref):
    scores = _sigmoid(lg_ref[...])
    choice = scores + rb_ref[...]
    E, T = scores.shape
    gsz = E // N_GROUPS
    neg = jnp.float32(-jnp.inf)
    big = jnp.int32(1 << 30)
    rows8 = lax.broadcasted_iota(jnp.int32, (gsz, T), 0)
    gscore = []
    for gi in range(N_GROUPS):
        c = choice[gi * gsz:(gi + 1) * gsz]
        m1 = jnp.max(c, axis=0, keepdims=True)
        i1 = jnp.min(jnp.where(c == m1, rows8, big), axis=0, keepdims=True)
        m2 = jnp.max(jnp.where(rows8 == i1, neg, c), axis=0, keepdims=True)
        gscore.append(m1 + m2)
    gs = jnp.concatenate(gscore, axis=0)
    rowsg = lax.broadcasted_iota(jnp.int32, (N_GROUPS, T), 0)
    gsel = jnp.zeros((N_GROUPS, T), jnp.bool_)
    for _ in range(TOPK_GROUPS):
        m = jnp.max(gs, axis=0, keepdims=True)
        ix = jnp.min(jnp.where(gs == m, rowsg, big), axis=0, keepdims=True)
        hit = rowsg == ix
        gsel = jnp.logical_or(gsel, hit)
        gs = jnp.where(hit, neg, gs)
    masked = jnp.concatenate(
        [jnp.where(gsel[gi:gi + 1], choice[gi * gsz:(gi + 1) * gsz], neg) for gi in range(N_GROUPS)], axis=0)
    rows = lax.broadcasted_iota(jnp.int32, (E, T), 0)
    ids, ws, hits = [], [], []
    sel = jnp.zeros((E, T), F32)
    for _ in range(TOP_K):
        m = jnp.max(masked, axis=0, keepdims=True)
        ix = jnp.min(jnp.where(masked == m, rows, big), axis=0, keepdims=True)
        hit = rows == ix
        ids.append(ix)
        hits.append(hit)
        ws.append(jnp.sum(jnp.where(hit, scores, 0.0), axis=0, keepdims=True))
        sel = jnp.where(hit, 1.0, sel)
        masked = jnp.where(hit, neg, masked)
    w = jnp.concatenate(ws, axis=0)
    w = w / jnp.sum(w, axis=0, keepdims=True) * ROUTED_SCALE
    ei_ref[...] = jnp.concatenate(ids, axis=0)
    wt_ref[...] = jnp.concatenate([w, jnp.zeros((128 - TOP_K, T), F32)], axis=0).T

    @pl.when(pl.program_id(0) == 0)
    def _():
        cnt_ref[...] = jnp.zeros_like(cnt_ref)

    before = cnt_ref[...][:, 0:1]
    csum = jnp.dot(sel.astype(BF16), tri_ref[...], preferred_element_type=F32)
    rank = before + csum - sel
    rk_ref[...] = jnp.concatenate(
        [jnp.sum(jnp.where(h, rank, 0.0), axis=0, keepdims=True) for h in hits], axis=0).astype(jnp.int32)
    cnt_ref[...] = jnp.broadcast_to(before + csum[:, T - 1:T], cnt_ref.shape)


def _router(logits_t, router_bias, tt):
    E, N = logits_t.shape
    tri = jnp.triu(jnp.ones((tt, tt), BF16))
    per_tok = pl.BlockSpec((TOP_K, tt), lambda i: (0, i))
    return pl.pallas_call(
        _router_kernel,
        grid=(N // tt,),
        in_specs=[pl.BlockSpec((E, tt), lambda i: (0, i)), pl.BlockSpec((E, 1), lambda i: (0, 0)),
                  pl.BlockSpec((tt, tt), lambda i: (0, 0))],
        out_specs=[per_tok, pl.BlockSpec((tt, 128), lambda i: (i, 0)), per_tok,
                   pl.BlockSpec((E, 128), lambda i: (0, 0))],
        out_shape=[jax.ShapeDtypeStruct((TOP_K, N), jnp.int32), jax.ShapeDtypeStruct((N, 128), F32),
                   jax.ShapeDtypeStruct((TOP_K, N), jnp.int32), jax.ShapeDtypeStruct((E, 128), F32)],
        compiler_params=_cparams(("arbitrary",)),
        name="router",
    )(logits_t, router_bias.reshape(E, 1).astype(F32), tri)


MOE_ROWS = 256
MOE_TOKENS = 128


def _plan_kernel(ei_ref, rk_ref, ps_ref, pos_ref):
    ei = ei_ref[...]
    ps = ps_ref[...]
    rows = lax.broadcasted_iota(jnp.int32, (ps.shape[0], ei.shape[1]), 0)
    start = [jnp.sum(jnp.where(rows == ei[k:k + 1], ps, 0.0), axis=0, keepdims=True) for k in range(ei.shape[0])]
    pos_ref[...] = rk_ref[...] + jnp.concatenate(start, axis=0).astype(jnp.int32)


def _dispatch_plan(eidx, rank, counts, bm, tm):
    K, N = eidx.shape
    E = N_EXPERTS
    cnt = counts[:, 0].astype(jnp.int32)
    padded = (cnt + bm - 1) // bm * bm
    pad_end = jnp.cumsum(padded)
    pad_start = pad_end - padded
    nb = -(-(N * K + E * (bm - 1)) // bm)
    tt = _tile(N, 2048)
    per_tok = pl.BlockSpec((K, tt), lambda i: (0, i))
    pos = pl.pallas_call(
        _plan_kernel,
        grid=(N // tt,),
        in_specs=[per_tok, per_tok, pl.BlockSpec((E, 1), lambda i: (0, 0))],
        out_specs=per_tok,
        out_shape=jax.ShapeDtypeStruct((K, N), jnp.int32),
        compiler_params=_cparams(("parallel",)),
        name="moe_plan",
    )(eidx, rank, pad_start.astype(F32).reshape(E, 1))
    blk_e = jnp.minimum(jnp.searchsorted(pad_end, jnp.arange(nb, dtype=jnp.int32) * bm, side='right'), E - 1)
    n_pad = nb * bm - N * K
    assert n_pad % (K * tm) == 0
    cpad = jnp.cumsum(padded - cnt)
    q = jnp.arange(n_pad, dtype=jnp.int32)
    e_of = jnp.searchsorted(cpad, q, side='right')
    first = jnp.concatenate([pad_start + cnt, pad_end[-1:]])[e_of]
    skipped = jnp.concatenate([jnp.zeros((1,), cpad.dtype), cpad])[e_of]
    pad_pos = (first + q - skipped).astype(jnp.int32)
    nt = N // tm
    pos3 = pos.reshape(K, nt, tm).transpose(1, 0, 2).reshape(nt, 1, K * tm)
    return pos3, pad_pos.reshape(-1, 1, K * tm), blk_e.astype(jnp.int32), nb


TILE_ROWS = 8


def _to_tiles(ref, val):
    m = val.shape[0]
    for j in range(TILE_ROWS):
        ref[pl.ds(j, m, stride=TILE_ROWS), :] = val[:, j * 128:(j + 1) * 128]


def _from_tiles(ref, start, m):
    return jnp.concatenate([ref[pl.ds(start + j, m, stride=TILE_ROWS), :] for j in range(TILE_ROWS)], axis=1)


def _row_copy(idx_ref, r, src_hbm, buf, base, sem):
    src = pl.multiple_of(idx_ref[0, r] * TILE_ROWS, TILE_ROWS)
    dst = pl.multiple_of(base + r * TILE_ROWS, TILE_ROWS)
    return pltpu.make_async_copy(src_hbm.at[pl.ds(src, TILE_ROWS), :], buf.at[pl.ds(dst, TILE_ROWS), :], sem)


def _pipelined_gather(cur_ref, nxt_ref, src_hbm, buf, sem, nrows):
    i = pl.program_id(0)
    slot = i % 2
    span = nrows * TILE_ROWS

    @pl.when(i == 0)
    def _():
        def body(r, carry):
            _row_copy(cur_ref, r, src_hbm, buf, 0, sem.at[0]).start()
            return carry
        lax.fori_loop(0, nrows, body, 0)

    @pl.when(i + 1 < pl.num_programs(0))
    def _():
        base = (1 - slot) * span
        for r in range(nrows):
            _row_copy(nxt_ref, r, src_hbm, buf, base, sem.at[1 - slot]).start(priority=r % 2)

    here = pl.multiple_of(slot * span, span)
    pltpu.make_async_copy(src_hbm.at[pl.ds(0, span), :], buf.at[pl.ds(here, span), :], sem.at[slot]).wait()
    return here


def _dispatch_kernel(pos_ref, hx_ref, xs_hbm, stage, zero_tile, sem, *, tm, nt):
    i = pl.program_id(0)
    slot = i % 2
    span = tm * TILE_ROWS

    def wait_step(s):
        for _ in range(TOP_K):
            pltpu.make_async_copy(stage.at[pl.ds(0, span), :], xs_hbm.at[pl.ds(0, span), :], sem.at[s]).wait()

    def dst(r):
        return xs_hbm.at[pl.ds(pl.multiple_of(pos_ref[0, r] * TILE_ROWS, TILE_ROWS), TILE_ROWS), :]

    @pl.when(i < nt)
    def _():
        base = pl.multiple_of(slot * span, span)
        stage[pl.ds(base, span), :] = hx_ref[...]
        for r in range(TOP_K * tm):
            src = stage.at[pl.ds(pl.multiple_of(base + (r % tm) * TILE_ROWS, TILE_ROWS), TILE_ROWS), :]
            pltpu.make_async_copy(src, dst(r), sem.at[slot]).start(priority=r % 2)

    @pl.when(i == 0)
    def _():
        zero_tile[...] = jnp.zeros_like(zero_tile)

    @pl.when(i >= nt)
    def _():
        for r in range(TOP_K * tm):
            pltpu.make_async_copy(zero_tile, dst(r), sem.at[slot]).start(priority=r % 2)

    @pl.when(i > 0)
    def _():
        wait_step(1 - slot)

    @pl.when(i == pl.num_programs(0) - 1)
    def _():
        wait_step(slot)


def _dispatch(pos3, pad_pos, hx_tiles, rows_total, tm):
    nt = pos3.shape[0]
    idx = jnp.concatenate([pos3, pad_pos], axis=0)
    steps = idx.shape[0]
    return pl.pallas_call(
        functools.partial(_dispatch_kernel, tm=tm, nt=nt),
        grid=(steps,),
        in_specs=[pl.BlockSpec((None, 1, TOP_K * tm), lambda i: (i, 0, 0), memory_space=pltpu.SMEM),
                  pl.BlockSpec((tm * TILE_ROWS, 128), lambda i: (jnp.minimum(i, nt - 1), 0))],
        out_specs=pl.BlockSpec(memory_space=pl.ANY),
        out_shape=jax.ShapeDtypeStruct((rows_total * TILE_ROWS, 128), F32),
        scratch_shapes=[pltpu.VMEM((2 * tm * TILE_ROWS, 128), F32), pltpu.VMEM((TILE_ROWS, 128), F32),
                        pltpu.SemaphoreType.DMA((2,))],
        compiler_params=_cparams(("arbitrary",)),
        name="moe_dispatch",
    )(idx, hx_tiles)


def _expert_kernel(blk_e_ref, xs_ref, wg_ref, wu_ref, wd_ref, y_ref, *, bm):
    del blk_e_ref
    x = _from_tiles(xs_ref, 0, bm).astype(BF16)
    hg = jnp.dot(x, wg_ref[...].astype(BF16), preferred_element_type=F32)
    hu = jnp.dot(x, wu_ref[...].astype(BF16), preferred_element_type=F32)
    act = hg * _sigmoid(hg) * hu
    _to_tiles(y_ref, jnp.dot(act.astype(BF16), wd_ref[...].astype(BF16), preferred_element_type=F32))


def _experts(xs_tiles, blk_e, wg, wu, wd, nb, bm):
    D = wg.shape[1]
    grid_spec = pltpu.PrefetchScalarGridSpec(
        num_scalar_prefetch=1,
        grid=(nb,),
        in_specs=[pl.BlockSpec((bm * TILE_ROWS, 128), lambda i, be: (i, 0)),
                  pl.BlockSpec((None, D, EXPERT_HIDDEN), lambda i, be: (be[i], 0, 0)),
                  pl.BlockSpec((None, D, EXPERT_HIDDEN), lambda i, be: (be[i], 0, 0)),
                  pl.BlockSpec((None, EXPERT_HIDDEN, D), lambda i, be: (be[i], 0, 0))],
        out_specs=pl.BlockSpec((bm * TILE_ROWS, 128), lambda i, be: (i, 0)),
    )
    return pl.pallas_call(
        functools.partial(_expert_kernel, bm=bm),
        grid_spec=grid_spec,
        out_shape=jax.ShapeDtypeStruct((nb * bm * TILE_ROWS, 128), F32),
        compiler_params=_cparams(("arbitrary",)),
        name="moe_experts",
    )(blk_e, xs_tiles, wg, wu, wd)


def _combine_kernel(cur_ref, nxt_ref, y_hbm, hx_ref, wt_ref, sgu_ref, sd_ref, x1_ref, g2_ref, fg_ref, o_ref, ybuf,
                    sem, *, tm):
    here = _pipelined_gather(cur_ref, nxt_ref, y_hbm, ybuf, sem, TOP_K * tm)
    wt = wt_ref[...]
    routed = None
    for k in range(TOP_K):
        wk = jnp.broadcast_to(wt[:, k:k + 1], (tm, 128))
        wk = jnp.concatenate([wk] * TILE_ROWS, axis=1)
        term = wk * _from_tiles(ybuf, here + k * tm * TILE_ROWS, tm)
        routed = term if routed is None else routed + term
    hu = jnp.dot(_from_tiles(hx_ref, 0, tm).astype(BF16), sgu_ref[...], preferred_element_type=F32)
    hg = hu[:, :EXPERT_HIDDEN]
    act = hg * _sigmoid(hg) * hu[:, EXPERT_HIDDEN:]
    shared = jnp.dot(act.astype(BF16), sd_ref[...], preferred_element_type=F32)
    xo = x1_ref[...] + g2_ref[...] * (routed + shared)
    o_ref[...] = _rms(xo) * fg_ref[...]


def _combine(pos3, y_tiles, hx_tiles, w_tok, sgu, sd, x1, g2, fg, tm, tiles_per_batch):
    N, D = x1.shape
    nt = N // tm
    smem_blk = lambda f: pl.BlockSpec((None, 1, TOP_K * tm), f, memory_space=pltpu.SMEM)
    row = lambda i: (i, 0)
    const = lambda i: (0, 0)
    return pl.pallas_call(
        functools.partial(_combine_kernel, tm=tm),
        grid=(nt,),
        in_specs=[smem_blk(lambda i: (i, 0, 0)),
                  smem_blk(lambda i: (jnp.minimum(i + 1, nt - 1), 0, 0)),
                  pl.BlockSpec(memory_space=pl.ANY),
                  pl.BlockSpec((tm * TILE_ROWS, 128), row),
                  pl.BlockSpec((tm, 128), row),
                  pl.BlockSpec((D, 2 * EXPERT_HIDDEN), const),
                  pl.BlockSpec((EXPERT_HIDDEN, D), const),
                  pl.BlockSpec((tm, D), row),
                  pl.BlockSpec((None, 1, D), lambda i: (i // tiles_per_batch, 0, 0)),
                  pl.BlockSpec((1, D), const)],
        out_specs=pl.BlockSpec((tm, D), row),
        out_shape=jax.ShapeDtypeStruct((N, D), F32),
        scratch_shapes=[pltpu.VMEM((2 * TOP_K * tm * TILE_ROWS, 128), F32), pltpu.SemaphoreType.DMA((2,))],
        compiler_params=_cparams(("arbitrary",)),
        name="moe_combine",
    )(pos3, pos3, y_tiles, hx_tiles, w_tok, sgu, sd, x1, g2, fg)


def _rope_tables(L):
    rows = L // GRID_W
    row = jnp.repeat(jnp.arange(rows), GRID_W).astype(F32)
    col = jnp.tile(jnp.arange(GRID_W), rows).astype(F32)
    inv = ROPE_THETA ** (-jnp.arange(0, ROPE_AXIS_DIM, 2, dtype=F32) / ROPE_AXIS_DIM)
    ang = jnp.stack([row[:, None] * inv, col[:, None] * inv], axis=1)
    c, s = jnp.cos(ang), jnp.sin(ang)
    cos_h = jnp.stack([c, c], axis=2).reshape(L, HEAD_DIM)
    sin_h = jnp.stack([-s, s], axis=2).reshape(L, HEAD_DIM)
    return jnp.tile(cos_h, (1, 2)), jnp.tile(sin_h, (1, 2))


def _tile(n, pref):
    t = min(pref, n)
    while n % t:
        t //= 2
    return t


def kernel(x, c, ctx, c_ctx, ada_w, ada_b, norm1_g, norm2_g, w_in, lam_q1, lam_k1, lam_q2, lam_k2, subln_g, hy_conv_w, hy_conv_b, filt_w1, filt_b1, filt_freq, filt_w2, filt_b2, filt_w3, hy_bias, w_branch_attn, w_branch_hyena, w_out, router_w, router_bias, exp_w_gate, exp_w_up, exp_w_down, shared_w_gate, shared_w_up, shared_w_down, final_norm_g):
    B, L, D = x.shape
    assert ada_w.shape[0] == 1 and B % 2 == 0 and (2 * L) % (FFT_SLABS * FFT_STRIP) == 0
    lam = (jnp.exp(jnp.sum(lam_q1[0].astype(F32) * lam_k1[0].astype(F32)))
           - jnp.exp(jnp.sum(lam_q2[0].astype(F32) * lam_k2[0].astype(F32))) + LAM_INIT).reshape(1, 1)

    cc = jnp.zeros((16, D), F32).at[:B].set(c).at[B].set(c_ctx)
    mod = _adaln(cc, ada_w[0], ada_b[0])
    chunk = lambda r0, r1, j: mod[r0:r1, j * D:(j + 1) * D]
    sh1, sc1, g1, sh2, sc2, g2 = (chunk(0, B, j).reshape(B, 1, D) for j in range(6))
    csh, csc = chunk(B, B + 1, 0), chunk(B, B + 1, 1)

    w_bf = w_in[0].astype(BF16)
    n1g = norm1_g[0].reshape(1, D)
    cos_t, sin_t = _rope_tables(L)
    k, v, q, u, gates = _inproj(x, sh1, sc1, n1g, w_bf, cos_t, sin_t, hy_conv_w[0], hy_conv_b[0].reshape(1, -1),
                                _tile(L, 512))
    kc, vc = _ctxproj(ctx, csh, csc, n1g, w_bf)
    attn = _attention(lam, q, k, v, kc, vc, subln_g[0].reshape(1, V_DIM), _tile(L, 512), _tile(L, 512))

    filt = (filt_w1[0], filt_b1[0], filt_freq[0], filt_w2[0], filt_b2[0], filt_w3[0])
    hy = _hyena(u, filt, hy_bias[0])

    x1, hx2, logits_t = _merge(attn, hy, gates, x, g1, sh2, sc2, norm2_g[0].reshape(1, D),
                               w_branch_attn[0].astype(BF16), w_branch_hyena[0].astype(BF16),
                               w_out[0].astype(BF16), router_w[0].T.astype(F32), _tile(L, 512))
    eidx, w_tok, rank, counts = _router(logits_t, router_bias[0], _tile(B * L, 1024))

    N = B * L
    tm = _tile(L, MOE_TOKENS)
    pos3, pad_pos, blk_e, nb = _dispatch_plan(eidx, rank, counts, MOE_ROWS, tm)
    xs = _dispatch(pos3, pad_pos, hx2, nb * MOE_ROWS, tm)
    y = _experts(xs, blk_e, exp_w_gate[0], exp_w_up[0], exp_w_down[0], nb, MOE_ROWS)
    sgu = jnp.concatenate([shared_w_gate[0], shared_w_up[0]], axis=-1).astype(BF16)
    out = _combine(pos3, y, hx2, w_tok, sgu, shared_w_down[0].astype(BF16), x1.reshape(N, D), g2,
                   final_norm_g.reshape(1, D), tm, L // tm)
    return out.reshape(B, L, D)
```

```python
import functools
import math

import jax
import jax.numpy as jnp
from jax import lax
from jax.experimental import pallas as pl
from jax.experimental.pallas import tpu as pltpu

F32 = jnp.float32
BF16 = jnp.bfloat16
HIGHEST = lax.Precision.HIGHEST

NORM_EPS = 1e-6
N_HEADS = 8
HEAD_DIM = 64
V_DIM = 2 * HEAD_DIM
GRID_W = 64
ROPE_THETA = 10000.0
ROPE_AXIS_DIM = HEAD_DIM // 2
HYENA_W = 1024
FILTER_EMB = 33
FILTER_BANDS = (FILTER_EMB - 1) // 2
DECAY_TARGET = 1e-2
FAST_DECAY_PCT = 0.3
SLOW_DECAY_PCT = 1.5
N_EXPERTS = 64
N_GROUPS = 8
TOPK_GROUPS = 4
TOP_K = 8
EXPERT_HIDDEN = 256
ROUTED_SCALE = 2.5
LAM_INIT = 0.8 - 0.6 * math.exp(-0.3 * 0)

V7X_VMEM_LIMIT = 56 * 1024 * 1024
LOG2E = 1.4426950408889634


def _cparams(sem):
    return pltpu.CompilerParams(dimension_semantics=sem, vmem_limit_bytes=V7X_VMEM_LIMIT)


def _sigmoid(v):
    return 1.0 / (1.0 + jnp.exp(-v))


def _rms(v):
    return v * lax.rsqrt(jnp.mean(v * v, axis=-1, keepdims=True) + NORM_EPS)


def _adaln_kernel(c_ref, w_ref, b_ref, o_ref):
    c = c_ref[...]
    s = c * _sigmoid(c)
    o_ref[...] = jnp.dot(s, w_ref[...], precision=HIGHEST, preferred_element_type=F32) + b_ref[...]


def _adaln(cc, w, b):
    rows, d = cc.shape
    n = w.shape[1]
    tn = 1536
    return pl.pallas_call(
        _adaln_kernel,
        grid=(n // tn,),
        in_specs=[pl.BlockSpec((rows, d), lambda j: (0, 0)),
                  pl.BlockSpec((d, tn), lambda j: (0, j)),
                  pl.BlockSpec((1, tn), lambda j: (0, j))],
        out_specs=pl.BlockSpec((rows, tn), lambda j: (0, j)),
        out_shape=jax.ShapeDtypeStruct((rows, n), F32),
        compiler_params=_cparams(("arbitrary",)),
        name="adaln",
    )(cc, w, b.reshape(1, n))


def _inproj_kernel(x_ref, xp_ref, xn_ref, sh_ref, sc_ref, g_ref, w_ref, cos_ref, sin_ref, cw_ref, cb_ref,
                   k_ref, v_ref, q_ref, u_ref, gt_ref, pscr, *, tm, nt, qscale):
    i = pl.program_id(1)
    g = g_ref[...]
    sh = sh_ref[...]
    sc = 1.0 + sc_ref[...]

    def norm_mod(xx):
        return (_rms(xx) * g) * sc + sh

    h = norm_mod(x_ref[...])
    hp = jnp.where(i > 0, norm_mod(xp_ref[...]), 0.0)
    hn = jnp.where(i < nt - 1, norm_mod(xn_ref[...]), 0.0)
    hb = h.astype(BF16)
    h_ext = jnp.concatenate([hp, h, hn], axis=0).astype(BF16)

    def proj(lhs, c0):
        return jnp.dot(lhs, w_ref[:, c0:c0 + 1024], preferred_element_type=F32)

    cosv = cos_ref[...]
    sinv = sin_ref[...]
    lane = lax.broadcasted_iota(jnp.int32, (1, V_DIM), 1)
    first = (lane % ROPE_AXIS_DIM) < (ROPE_AXIS_DIM // 2)

    def rope_store(a, o_ref, scale):
        for hh in range(N_HEADS):
            s = a[:, hh * V_DIM:(hh + 1) * V_DIM]
            partner = jnp.where(first, pltpu.roll(s, V_DIM - ROPE_AXIS_DIM // 2, 1),
                                pltpu.roll(s, ROPE_AXIS_DIM // 2, 1))
            r = s * cosv + partner * sinv
            if scale != 1.0:
                r = r * scale
            o_ref[:, hh * V_DIM:(hh + 1) * V_DIM] = r.astype(BF16)

    rope_store(proj(hb, 0), k_ref, 1.0)
    v_ref[...] = proj(hb, 1024).astype(BF16)
    rope_store(proj(hb, 2048), q_ref, qscale)
    for j in range(3):
        pscr[...] = proj(h_ext, 3072 + j * 1024)
        cw = cw_ref[:, j * 1024:(j + 1) * 1024]
        cb = cb_ref[:, j * 1024:(j + 1) * 1024]
        u = (pscr[pl.ds(7, tm), :] * cw[0:1] + pscr[pl.ds(8, tm), :] * cw[1:2]
             + pscr[pl.ds(9, tm), :] * cw[2:3] + cb)
        u_ref[:, j * 1024:(j + 1) * 1024] = u.astype(BF16)
    for j in range(2):
        gt_ref[:, j * 1024:(j + 1) * 1024] = proj(hb, 6144 + j * 1024).astype(BF16)


def _inproj(x, shift, scale, g, w_bf, cos_t, sin_t, conv_w, conv_b, tm):
    B, L, D = x.shape
    nt = L // tm
    nb8 = L // 8
    qscale = (HEAD_DIM ** -0.5) * LOG2E
    kern = functools.partial(_inproj_kernel, tm=tm, nt=nt, qscale=qscale)
    row = lambda b, i: (b, i, 0)
    per_b = lambda b, i: (b, 0, 0)
    const = lambda b, i: (0, 0)
    outs = pl.pallas_call(
        kern,
        grid=(B, nt),
        in_specs=[
            pl.BlockSpec((None, tm, D), row),
            pl.BlockSpec((None, 8, D), lambda b, i: (b, jnp.maximum(i * (tm // 8) - 1, 0), 0)),
            pl.BlockSpec((None, 8, D), lambda b, i: (b, jnp.minimum((i + 1) * (tm // 8), nb8 - 1), 0)),
            pl.BlockSpec((None, 1, D), per_b),
            pl.BlockSpec((None, 1, D), per_b),
            pl.BlockSpec((1, D), const),
            pl.BlockSpec((D, 8192), const, pipeline_mode=pl.Buffered(1)),
            pl.BlockSpec((tm, V_DIM), lambda b, i: (i, 0)),
            pl.BlockSpec((tm, V_DIM), lambda b, i: (i, 0)),
            pl.BlockSpec((3, 3 * HYENA_W), const),
            pl.BlockSpec((1, 3 * HYENA_W), const),
        ],
        out_specs=[
            pl.BlockSpec((None, tm, 1024), row),
            pl.BlockSpec((None, tm, 1024), row),
            pl.BlockSpec((None, tm, 1024), row),
            pl.BlockSpec((None, tm, 3 * HYENA_W), row),
            pl.BlockSpec((None, tm, 2048), row),
        ],
        out_shape=[
            jax.ShapeDtypeStruct((B, L, 1024), BF16),
            jax.ShapeDtypeStruct((B, L, 1024), BF16),
            jax.ShapeDtypeStruct((B, L, 1024), BF16),
            jax.ShapeDtypeStruct((B, L, 3 * HYENA_W), BF16),
            jax.ShapeDtypeStruct((B, L, 2048), BF16),
        ],
        scratch_shapes=[pltpu.VMEM((tm + 16, 1024), F32)],
        compiler_params=_cparams(("parallel", "arbitrary")),
        name="inproj",
    )(x, x, x, shift, scale, g, w_bf, cos_t, sin_t, conv_w, conv_b)
    return outs


def _ctxproj_kernel(x_ref, sh_ref, sc_ref, g_ref, w_ref, k_ref, v_ref):
    h = ((_rms(x_ref[...]) * g_ref[...]) * (1.0 + sc_ref[...]) + sh_ref[...]).astype(BF16)
    k_ref[...] = jnp.dot(h, w_ref[:, 0:1024], preferred_element_type=F32).astype(BF16)
    v_ref[...] = jnp.dot(h, w_ref[:, 1024:2048], preferred_element_type=F32).astype(BF16)


def _ctxproj(ctx, shift, scale, g, w_bf):
    B, Lc, D = ctx.shape
    return pl.pallas_call(
        _ctxproj_kernel,
        grid=(B,),
        in_specs=[pl.BlockSpec((None, Lc, D), lambda b: (b, 0, 0)),
                  pl.BlockSpec((1, D), lambda b: (0, 0)),
                  pl.BlockSpec((1, D), lambda b: (0, 0)),
                  pl.BlockSpec((1, D), lambda b: (0, 0)),
                  pl.BlockSpec((D, 2048), lambda b: (0, 0))],
        out_specs=[pl.BlockSpec((None, Lc, 1024), lambda b: (b, 0, 0)),
                   pl.BlockSpec((None, Lc, 1024), lambda b: (b, 0, 0))],
        out_shape=[jax.ShapeDtypeStruct((B, Lc, 1024), BF16),
                   jax.ShapeDtypeStruct((B, Lc, 1024), BF16)],
        compiler_params=_cparams(("arbitrary",)),
        name="ctxproj",
    )(ctx, shift, scale, g, w_bf)


def _attn_kernel(lam_ref, q_ref, k_ref, v_ref, kc_ref, vc_ref, g_ref, o_ref, m_s, acc_s, *, tk, nk):
    q = q_ref[...]
    lane = lax.broadcasted_iota(jnp.int32, (1, V_DIM), 1)
    zero = jnp.zeros_like(q)
    qs = (jnp.where(lane < HEAD_DIM, q, zero), jnp.where(lane >= HEAD_DIM, q, zero))
    m_s[...] = jnp.full(m_s.shape, -1e30, F32)
    acc_s[...] = jnp.zeros(acc_s.shape, F32)

    def chunk(kk, vv):
        reps = kk.shape[0] // V_DIM
        v1 = jnp.concatenate([vv, jnp.ones_like(vv)], axis=1)
        for m in range(2):
            s = lax.dot_general(qs[m], kk, (((1,), (1,)), ((), ())), preferred_element_type=F32)
            m_prev = m_s[m]
            m_new = jnp.maximum(m_prev, jnp.max(s, axis=-1, keepdims=True))
            alpha = jnp.exp2(m_prev - m_new)
            p = jnp.exp2(s - jnp.concatenate([m_new] * reps, axis=1))
            acc_s[m] = (jnp.concatenate([alpha, alpha], axis=1) * acc_s[m]
                        + jnp.dot(p.astype(BF16), v1, preferred_element_type=F32))
            m_s[m] = m_new

    def body(j, carry):
        off = pl.multiple_of(j * tk, tk)
        chunk(k_ref[pl.ds(off, tk), :], v_ref[pl.ds(off, tk), :])
        return carry

    lax.fori_loop(0, nk, body, 0, unroll=2 if nk % 2 == 0 else 1)
    chunk(kc_ref[...], vc_ref[...])
    lam = lam_ref[0, 0]
    a0, a1 = acc_s[0], acc_s[1]
    o = a0[:, :V_DIM] / a0[:, V_DIM:] - lam * (a1[:, :V_DIM] / a1[:, V_DIM:])
    o = _rms(o) * g_ref[...] * (1.0 - LAM_INIT)
    o_ref[...] = o.astype(BF16)


def _attention(lam, q, k, v, kc, vc, subln_g, tq, tk):
    B, L, _ = q.shape
    Lc = kc.shape[1]
    kern = functools.partial(_attn_kernel, tk=tk, nk=L // tk)
    kv = lambda b, h, i: (b, 0, h)
    return pl.pallas_call(
        kern,
        grid=(B, N_HEADS, L // tq),
        in_specs=[
            pl.BlockSpec(memory_space=pltpu.SMEM),
            pl.BlockSpec((None, tq, V_DIM), lambda b, h, i: (b, i, h)),
            pl.BlockSpec((None, L, V_DIM), kv),
            pl.BlockSpec((None, L, V_DIM), kv),
            pl.BlockSpec((None, Lc, V_DIM), kv),
            pl.BlockSpec((None, Lc, V_DIM), kv),
            pl.BlockSpec((1, V_DIM), lambda b, h, i: (0, 0)),
        ],
        out_specs=pl.BlockSpec((None, tq, V_DIM), lambda b, h, i: (b, i, h)),
        out_shape=jax.ShapeDtypeStruct((B, L, N_HEADS * V_DIM), BF16),
        scratch_shapes=[pltpu.VMEM((2, tq, V_DIM), F32), pltpu.VMEM((2, tq, 2 * V_DIM), F32)],
        compiler_params=_cparams(("parallel", "parallel", "arbitrary")),
        name="diffattn",
    )(lam, q, k, v, kc, vc, subln_g)


FFT_SLABS = 64
FFT_STRIP = 16


def _dft_tables(nb):
    N = FFT_SLABS * nb
    fa = jnp.arange(FFT_SLABS, dtype=jnp.int32)[:, None, None]
    fb = jnp.arange(nb, dtype=jnp.int32)[None, :, None]
    b = jnp.arange(nb, dtype=jnp.int32)[None, None, :]
    ang = ((FFT_SLABS * b * fb + b * fa) % N).astype(F32) * (2.0 * math.pi / N)
    c, s = jnp.cos(ang), jnp.sin(ang)
    fwd = jnp.concatenate([jnp.concatenate([c, s], -1), jnp.concatenate([-s, c], -1)], axis=1)
    inv = jnp.swapaxes(fwd, 1, 2) / N
    return fwd.astype(BF16), inv.astype(BF16)


def _vadd(a, b):
    return b if a is None else (a if b is None else a + b)


def _vsub(a, b):
    if b is None:
        return a
    return -b if a is None else a - b


def _vscale(a, k):
    if a is None or k == 1.0:
        return a
    return -a if k == -1.0 else a * k


def _cmulc(x, c, s):
    re, im = x
    eps = 1e-9
    if abs(s) < eps:
        k = 1.0 if c > 0 else -1.0
        return _vscale(re, k), _vscale(im, k)
    if abs(c) < eps:
        k = 1.0 if s > 0 else -1.0
        return _vscale(im, -k), _vscale(re, k)
    if abs(abs(c) - abs(s)) < eps:
        kc = 1.0 if c > 0 else -1.0
        ks = 1.0 if s > 0 else -1.0
        return (_vscale(_vsub(_vscale(re, kc), _vscale(im, ks)), abs(c)),
                _vscale(_vadd(_vscale(re, ks), _vscale(im, kc)), abs(c)))
    return _vsub(_vscale(re, c), _vscale(im, s)), _vadd(_vscale(re, s), _vscale(im, c))


def _fft(xs, sign):
    n = len(xs)
    if n == 1:
        return xs
    ev, od = _fft(xs[0::2], sign), _fft(xs[1::2], sign)
    out = [None] * n
    for k in range(n // 2):
        ang = sign * 2.0 * math.pi * k / n
        t = _cmulc(od[k], math.cos(ang), math.sin(ang))
        out[k] = (_vadd(ev[k][0], t[0]), _vadd(ev[k][1], t[1]))
        out[k + n // 2] = (_vsub(ev[k][0], t[0]), _vsub(ev[k][1], t[1]))
    return out


def _put(ref, slot, rows, val):
    ref[slot, rows, :] = jnp.zeros((FFT_STRIP, ref.shape[2]), F32) if val is None else val


def _slab_fft_forward(load, s_ref, nb, n_in):
    def strip(i, carry):
        r0 = pl.multiple_of(i * FFT_STRIP, FFT_STRIP)
        rows_re, rows_im = pl.ds(r0, FFT_STRIP), pl.ds(nb + r0, FFT_STRIP)
        for a0 in range(8):
            xs = [load(8 * a1 + a0, r0) if a1 < n_in else (None, None) for a1 in range(8)]
            ys = _fft(xs, -1.0)
            for f1 in range(8):
                ang = -2.0 * math.pi * a0 * f1 / FFT_SLABS
                yr, yi = _cmulc(ys[f1], math.cos(ang), math.sin(ang))
                _put(s_ref, a0 * 8 + f1, rows_re, yr)
                _put(s_ref, a0 * 8 + f1, rows_im, yi)
        for f1 in range(8):
            xs = [(s_ref[a0 * 8 + f1, rows_re, :], s_ref[a0 * 8 + f1, rows_im, :]) for a0 in range(8)]
            ys = _fft(xs, -1.0)
            for f0 in range(8):
                s_ref[f1 + 8 * f0, rows_re, :] = ys[f0][0]
                s_ref[f1 + 8 * f0, rows_im, :] = ys[f0][1]
        return carry

    lax.fori_loop(0, nb // FFT_STRIP, strip, 0)


def _slab_fft_inverse(s_ref, nb, emit):
    def strip(i, carry):
        r0 = pl.multiple_of(i * FFT_STRIP, FFT_STRIP)
        rows_re, rows_im = pl.ds(r0, FFT_STRIP), pl.ds(nb + r0, FFT_STRIP)
        for q in range(8):
            xs = [(s_ref[8 * p + q, rows_re, :], s_ref[8 * p + q, rows_im, :]) for p in range(8)]
            us = _fft(xs, 1.0)
            for r in range(8):
                ang = 2.0 * math.pi * r * q / FFT_SLABS
                ur, ui = _cmulc(us[r], math.cos(ang), math.sin(ang))
                s_ref[8 * r + q, rows_re, :] = ur
                s_ref[8 * r + q, rows_im, :] = ui
        for r in range(8):
            xs = [(s_ref[8 * r + q, rows_re, :], s_ref[8 * r + q, rows_im, :]) for q in range(8)]
            ys = _fft(xs, 1.0)
            for sidx in range(4):
                emit(r + 8 * sidx, r0, ys[sidx][0], ys[sidx][1])
        return carry

    lax.fori_loop(0, nb // FFT_STRIP, strip, 0)


def _hyena_conv_kernel(v_ref, x_ref, k_ref, ff_ref, fi_ref, bias_ref, o_ref, s_ref, *, nb):
    def load(a, r0):
        rows = pl.ds(pl.multiple_of(a * nb + r0, FFT_STRIP), FFT_STRIP)
        return v_ref[0, rows, :].astype(F32), v_ref[1, rows, :].astype(F32)

    _slab_fft_forward(load, s_ref, nb, FFT_SLABS // 16)

    def freq(f, carry):
        z = jnp.dot(ff_ref[f], s_ref[f].astype(BF16), preferred_element_type=F32)
        zr, zi = z[:nb], z[nb:]
        kr, ki = k_ref[0, f], k_ref[1, f]
        y = jnp.concatenate([zr * kr - zi * ki, zr * ki + zi * kr], axis=0).astype(BF16)
        s_ref[f] = jnp.dot(fi_ref[f], y, preferred_element_type=F32)
        return carry

    lax.fori_loop(0, FFT_SLABS, freq, 0, unroll=8)
    bias = bias_ref[...]

    def emit(a, r0, re, im):
        rows = pl.ds(pl.multiple_of(a * nb + r0, FFT_STRIP), FFT_STRIP)
        for half, y in enumerate((re, im)):
            v = v_ref[half, rows, :].astype(F32)
            o_ref[half, rows, :] = (x_ref[half, rows, :].astype(F32) * (y + v * bias)).astype(BF16)

    _slab_fft_inverse(s_ref, nb, emit)


def _hyena_conv(uv, zv, k_hat, ff, fi, bias, order, in_col, gate_col, nb):
    _, P, L, _ = uv.shape
    W = HYENA_W
    nct = W // 128
    src, src_off = (uv, in_col * nct) if zv is None else (zv, 0)
    blk = (2, None, L, 128)
    return pl.pallas_call(
        functools.partial(_hyena_conv_kernel, nb=nb),
        grid=(nct, P),
        in_specs=[pl.BlockSpec(blk, lambda c, p: (0, p, 0, src_off + c)),
                  pl.BlockSpec(blk, lambda c, p: (0, p, 0, gate_col * nct + c)),
                  pl.BlockSpec((2, FFT_SLABS, nb, 128), lambda c, p: (0, 0, 0, order * nct + c),
                               pipeline_mode=pl.Buffered(1)),
                  pl.BlockSpec((FFT_SLABS, 2 * nb, 2 * nb), lambda c, p: (0, 0, 0), pipeline_mode=pl.Buffered(1)),
                  pl.BlockSpec((FFT_SLABS, 2 * nb, 2 * nb), lambda c, p: (0, 0, 0), pipeline_mode=pl.Buffered(1)),
                  pl.BlockSpec((1, 128), lambda c, p: (0, c))],
        out_specs=pl.BlockSpec(blk, lambda c, p: (0, p, 0, c)),
        out_shape=jax.ShapeDtypeStruct((2, P, L, W), BF16),
        scratch_shapes=[pltpu.VMEM((FFT_SLABS, 2 * nb, 128), F32)],
        compiler_params=_cparams(("parallel", "arbitrary")),
        name="hyena_conv%d" % order,
    )(src, uv, k_hat, ff, fi, bias)


def _filter_features(n):
    pos = jnp.arange(n, dtype=F32)[:, None]
    t = jnp.linspace(0.0, 1.0, n, dtype=F32)[:, None]
    w = 2 * math.pi * pos / n
    bands = jnp.linspace(1e-4, FILTER_BANDS - 1, FILTER_BANDS, dtype=F32)
    z = jnp.concatenate([t, jnp.cos(bands * w), -jnp.sin(bands * w)], axis=-1)
    z = jnp.pad(z, ((0, 0), (0, 128 - FILTER_EMB)))
    tap = jnp.arange(2 * n)
    return z[jnp.where(tap < n, tap, (2 * n - tap) % n)]


def _filt_taps_kernel(z_ref, w1_ref, b1_ref, fr_ref, w2_ref, b2_ref, w3_ref, dl_ref, k_ref, sum_ref):
    a = pl.program_id(0)
    z = z_ref[...]
    fr = fr_ref[...]
    hd = jnp.sin(fr * (jnp.dot(z, w1_ref[...], precision=HIGHEST, preferred_element_type=F32) + b1_ref[...]))
    hd = jnp.sin(fr * (jnp.dot(hd, w2_ref[...], precision=HIGHEST, preferred_element_type=F32) + b2_ref[...]))
    h = jnp.dot(hd, w3_ref[...], precision=HIGHEST, preferred_element_type=F32) * jnp.exp(-z[:, 0:1] * dl_ref[...])
    rows = lax.broadcasted_iota(jnp.int32, (z.shape[0], 1), 0)
    h = jnp.where(jnp.logical_and(a == FFT_SLABS // 2, rows == 0), 0.0, h)
    k_ref[...] = h

    @pl.when(a == 0)
    def _():
        sum_ref[...] = jnp.zeros_like(sum_ref)

    sum_ref[0:1, :] += jnp.sum(jnp.abs(h), axis=0, keepdims=True)


def _filt_spectrum_kernel(t_ref, inv_ref, ff_ref, k_ref, s_ref, *, nb):
    def load(a, r0):
        return t_ref[pl.ds(pl.multiple_of(a * nb + r0, FFT_STRIP), FFT_STRIP), :], None

    _slab_fft_forward(load, s_ref, nb, FFT_SLABS // 8)
    inv = inv_ref[...]

    def freq(f, carry):
        z = jnp.dot(ff_ref[f], s_ref[f].astype(BF16), preferred_element_type=F32)
        k_ref[0, f] = z[:nb] * inv
        k_ref[1, f] = z[nb:] * inv
        return carry

    lax.fori_loop(0, FFT_SLABS, freq, 0, unroll=8)


def _hyena_filters(n, filt, ff, nb):
    w1, b1, freq, w2, b2, w3 = filt
    W = HYENA_W
    hid = w1.shape[1]
    z = _filter_features(n)
    w1p = jnp.pad(w1.astype(F32), ((0, 128 - FILTER_EMB), (0, 0)))
    w3d = w3.astype(F32).reshape(hid, 2, 2, W)
    w3sel = jnp.stack([w3d[:, :, d, :].reshape(hid, 2 * W) for d in range(2)])
    deltas = jnp.abs(jnp.linspace(math.log(DECAY_TARGET) / SLOW_DECAY_PCT,
                                  math.log(DECAY_TARGET) / FAST_DECAY_PCT, W, dtype=F32))
    deltas = jnp.tile(deltas.reshape(1, W), (1, 2))
    const = lambda a: (0, 0)
    taps, sums = pl.pallas_call(
        _filt_taps_kernel,
        grid=(FFT_SLABS,),
        in_specs=[pl.BlockSpec((nb, 128), lambda a: (a, 0)),
                  pl.BlockSpec((128, hid), const), pl.BlockSpec((1, hid), const), pl.BlockSpec((1, hid), const),
                  pl.BlockSpec((hid, hid), const), pl.BlockSpec((1, hid), const),
                  pl.BlockSpec((None, hid, 2 * W), lambda a: (a // (FFT_SLABS // 2), 0, 0)),
                  pl.BlockSpec((1, 2 * W), const)],
        out_specs=[pl.BlockSpec((nb, 2 * W), lambda a: (a, 0)), pl.BlockSpec((8, 2 * W), const)],
        out_shape=[jax.ShapeDtypeStruct((2 * n, 2 * W), F32), jax.ShapeDtypeStruct((8, 2 * W), F32)],
        compiler_params=_cparams(("arbitrary",)),
        name="hyena_filter_taps",
    )(z, w1p, b1.reshape(1, hid), freq.reshape(1, hid), w2, b2.reshape(1, hid), w3sel, deltas)
    inv = 1.0 / sums[0:1]
    return pl.pallas_call(
        functools.partial(_filt_spectrum_kernel, nb=nb),
        grid=(2 * W // 128,),
        in_specs=[pl.BlockSpec((2 * n, 128), lambda j: (0, j)),
                  pl.BlockSpec((1, 128), lambda j: (0, j)),
                  pl.BlockSpec((FFT_SLABS, 2 * nb, 2 * nb), lambda j: (0, 0, 0), pipeline_mode=pl.Buffered(1))],
        out_specs=pl.BlockSpec((2, FFT_SLABS, nb, 128), lambda j: (0, 0, 0, j)),
        out_shape=jax.ShapeDtypeStruct((2, FFT_SLABS, nb, 2 * W), F32),
        scratch_shapes=[pltpu.VMEM((FFT_SLABS, 2 * nb, 128), F32)],
        compiler_params=_cparams(("parallel",)),
        name="hyena_filter_spectrum",
    )(taps, inv, ff)


def _hyena(u, filt, hy_bias):
    B, L, _ = u.shape
    P = B // 2
    nb = 2 * L // FFT_SLABS
    ff, fi = _dft_tables(nb)
    k_hat = _hyena_filters(L, filt, ff, nb)
    uv = u.reshape(2, P, L, 3 * HYENA_W)
    bias = hy_bias.astype(F32)
    z = _hyena_conv(uv, None, k_hat, ff, fi, bias[0:1], 0, 0, 1, nb)
    y = _hyena_conv(uv, z, k_hat, ff, fi, bias[1:2], 1, 0, 2, nb)
    return y.reshape(B, L, HYENA_W)


def _merge_kernel(a_ref, h_ref, gt_ref, x_ref, g1_ref, sh_ref, sc_ref, n2g_ref, wpa_ref, wph_ref, wo_ref, rw_ref,
                  x1_ref, hx_ref, lg_ref):
    ya = jnp.dot(a_ref[...], wpa_ref[...], preferred_element_type=F32)
    yh = jnp.dot(h_ref[...], wph_ref[...], preferred_element_type=F32)
    ga = gt_ref[:, 0:1024].astype(F32)
    gh = gt_ref[:, 1024:2048].astype(F32)
    y = _sigmoid(ga) * ya + _sigmoid(gh) * yh
    mix = jnp.dot(y.astype(BF16), wo_ref[...], preferred_element_type=F32)
    x1 = x_ref[...] + g1_ref[...] * mix
    x1_ref[...] = x1
    hx = (_rms(x1) * n2g_ref[...]) * (1.0 + sc_ref[...]) + sh_ref[...]
    _to_tiles(hx_ref, hx)
    lg_ref[...] = lax.dot_general(rw_ref[...], hx, (((1,), (1,)), ((), ())), precision=HIGHEST,
                                  preferred_element_type=F32)


def _merge(attn, hy, gates, x, g1, sh2, sc2, n2g, wpa, wph, wo, rwT, tm):
    B, L, D = x.shape
    nt = L // tm
    row = lambda b, i: (b, i, 0)
    per_b = lambda b, i: (b, 0, 0)
    const = lambda b, i: (0, 0)
    return pl.pallas_call(
        _merge_kernel,
        grid=(B, nt),
        in_specs=[pl.BlockSpec((None, tm, 1024), row), pl.BlockSpec((None, tm, 1024), row),
                  pl.BlockSpec((None, tm, 2048), row), pl.BlockSpec((None, tm, D), row),
                  pl.BlockSpec((None, 1, D), per_b), pl.BlockSpec((None, 1, D), per_b),
                  pl.BlockSpec((None, 1, D), per_b), pl.BlockSpec((1, D), const),
                  pl.BlockSpec((1024, D), const), pl.BlockSpec((1024, D), const), pl.BlockSpec((D, D), const),
                  pl.BlockSpec((N_EXPERTS, D), const)],
        out_specs=[pl.BlockSpec((None, tm, D), row),
                   pl.BlockSpec((tm * TILE_ROWS, 128), lambda b, i: (b * nt + i, 0)),
                   pl.BlockSpec((N_EXPERTS, tm), lambda b, i: (0, b * nt + i))],
        out_shape=[jax.ShapeDtypeStruct((B, L, D), F32),
                   jax.ShapeDtypeStruct((B * L * TILE_ROWS, 128), F32),
                   jax.ShapeDtypeStruct((N_EXPERTS, B * L), F32)],
        compiler_params=_cparams(("parallel", "arbitrary")),
        name="merge",
    )(attn, hy, gates, x, g1, sh2, sc2, n2g, wpa, wph, wo, rwT)


def _router_kernel(lg_ref, rb_ref, tri_ref, ei_ref, wt_ref, rk_ref, cnt_ref):
    scores = _sigmoid(lg_ref[...])
    choice = scores + rb_ref[...]
    E, T = scores.shape
    gsz = E // N_GROUPS
    neg = jnp.float32(-jnp.inf)
    big = jnp.int32(1 << 30)
    rows8 = lax.broadcasted_iota(jnp.int32, (gsz, T), 0)
    gscore = []
    for gi in range(N_GROUPS):
        c = choice[gi * gsz:(gi + 1) * gsz]
        m1 = jnp.max(c, axis=0, keepdims=True)
        i1 = jnp.min(jnp.where(c == m1, rows8, big), axis=0, keepdims=True)
        m2 = jnp.max(jnp.where(rows8 == i1, neg, c), axis=0, keepdims=True)
        gscore.append(m1 + m2)
    gs = jnp.concatenate(gscore, axis=0)
    rowsg = lax.broadcasted_iota(jnp.int32, (N_GROUPS, T), 0)
    gsel = jnp.zeros((N_GROUPS, T), jnp.bool_)
    for _ in range(TOPK_GROUPS):
        m = jnp.max(gs, axis=0, keepdims=True)
        ix = jnp.min(jnp.where(gs == m, rowsg, big), axis=0, keepdims=True)
        hit = rowsg == ix
        gsel = jnp.logical_or(gsel, hit)
        gs = jnp.where(hit, neg, gs)
    masked = jnp.concatenate(
        [jnp.where(gsel[gi:gi + 1], choice[gi * gsz:(gi + 1) * gsz], neg) for gi in range(N_GROUPS)], axis=0)
    rows = lax.broadcasted_iota(jnp.int32, (E, T), 0)
    ids, ws, hits = [], [], []
    sel = jnp.zeros((E, T), F32)
    for _ in range(TOP_K):
        m = jnp.max(masked, axis=0, keepdims=True)
        ix = jnp.min(jnp.where(masked == m, rows, big), axis=0, keepdims=True)
        hit = rows == ix
        ids.append(ix)
        hits.append(hit)
        ws.append(jnp.sum(jnp.where(hit, scores, 0.0), axis=0, keepdims=True))
        sel = jnp.where(hit, 1.0, sel)
        masked = jnp.where(hit, neg, masked)
    w = jnp.concatenate(ws, axis=0)
    w = w / jnp.sum(w, axis=0, keepdims=True) * ROUTED_SCALE
    ei_ref[...] = jnp.concatenate(ids, axis=0)
    wt_ref[...] = jnp.concatenate([w, jnp.zeros((128 - TOP_K, T), F32)], axis=0).T

    @pl.when(pl.program_id(0) == 0)
    def _():
        cnt_ref[...] = jnp.zeros_like(cnt_ref)

    before = cnt_ref[...][:, 0:1]
    csum = jnp.dot(sel.astype(BF16), tri_ref[...], preferred_element_type=F32)
    rank = before + csum - sel
    rk_ref[...] = jnp.concatenate(
        [jnp.sum(jnp.where(h, rank, 0.0), axis=0, keepdims=True) for h in hits], axis=0).astype(jnp.int32)
    cnt_ref[...] = jnp.broadcast_to(before + csum[:, T - 1:T], cnt_ref.shape)


def _router(logits_t, router_bias, tt):
    E, N = logits_t.shape
    tri = jnp.triu(jnp.ones((tt, tt), BF16))
    per_tok = pl.BlockSpec((TOP_K, tt), lambda i: (0, i))
    return pl.pallas_call(
        _router_kernel,
        grid=(N // tt,),
        in_specs=[pl.BlockSpec((E, tt), lambda i: (0, i)), pl.BlockSpec((E, 1), lambda i: (0, 0)),
                  pl.BlockSpec((tt, tt), lambda i: (0, 0))],
        out_specs=[per_tok, pl.BlockSpec((tt, 128), lambda i: (i, 0)), per_tok,
                   pl.BlockSpec((E, 128), lambda i: (0, 0))],
        out_shape=[jax.ShapeDtypeStruct((TOP_K, N), jnp.int32), jax.ShapeDtypeStruct((N, 128), F32),
                   jax.ShapeDtypeStruct((TOP_K, N), jnp.int32), jax.ShapeDtypeStruct((E, 128), F32)],
        compiler_params=_cparams(("arbitrary",)),
        name="router",
    )(logits_t, router_bias.reshape(E, 1).astype(F32), tri)


MOE_ROWS = 512
MOE_TOKENS = 128


def _plan_kernel(ei_ref, rk_ref, ps_ref, pos_ref):
    ei = ei_ref[...]
    ps = ps_ref[...]
    rows = lax.broadcasted_iota(jnp.int32, (ps.shape[0], ei.shape[1]), 0)
    start = [jnp.sum(jnp.where(rows == ei[k:k + 1], ps, 0.0), axis=0, keepdims=True) for k in range(ei.shape[0])]
    pos_ref[...] = rk_ref[...] + jnp.concatenate(start, axis=0).astype(jnp.int32)


def _dispatch_plan(eidx, rank, counts, bm, tm):
    K, N = eidx.shape
    E = N_EXPERTS
    cnt = counts[:, 0].astype(jnp.int32)
    padded = (cnt + bm - 1) // bm * bm
    pad_end = jnp.cumsum(padded)
    pad_start = pad_end - padded
    nb = -(-(N * K + E * (bm - 1)) // bm)
    tt = _tile(N, 2048)
    per_tok = pl.BlockSpec((K, tt), lambda i: (0, i))
    pos = pl.pallas_call(
        _plan_kernel,
        grid=(N // tt,),
        in_specs=[per_tok, per_tok, pl.BlockSpec((E, 1), lambda i: (0, 0))],
        out_specs=per_tok,
        out_shape=jax.ShapeDtypeStruct((K, N), jnp.int32),
        compiler_params=_cparams(("parallel",)),
        name="moe_plan",
    )(eidx, rank, pad_start.astype(F32).reshape(E, 1))
    blk_start = jnp.arange(nb, dtype=jnp.int32) * bm
    blk_e = jnp.minimum(jnp.sum(blk_start[:, None] >= pad_end[None, :], axis=1), E - 1)
    n_pad = nb * bm - N * K
    assert n_pad % (K * tm) == 0
    cpad = jnp.cumsum(padded - cnt)
    q = jnp.arange(n_pad, dtype=jnp.int32)
    seg = jnp.sum(q[:, None] >= cpad[None, :], axis=1)
    onehot = seg[:, None] == jnp.arange(E + 1, dtype=jnp.int32)[None, :]
    first = jnp.concatenate([pad_start + cnt, pad_end[-1:]])
    skipped = jnp.concatenate([jnp.zeros((1,), cpad.dtype), cpad])
    pad_pos = (q + jnp.sum(jnp.where(onehot, (first - skipped)[None, :], 0), axis=1)).astype(jnp.int32)
    nt = N // tm
    pos3 = pos.reshape(K, nt, tm).transpose(1, 0, 2).reshape(nt, 1, K * tm)
    return pos3, pad_pos.reshape(-1, 1, K * tm), blk_e.astype(jnp.int32), nb


TILE_ROWS = 8


def _to_tiles(ref, val):
    m = val.shape[0]
    for j in range(TILE_ROWS):
        ref[pl.ds(j, m, stride=TILE_ROWS), :] = val[:, j * 128:(j + 1) * 128]


def _from_tiles(ref, start, m):
    return jnp.concatenate([ref[pl.ds(start + j, m, stride=TILE_ROWS), :] for j in range(TILE_ROWS)], axis=1)


def _row_copy(idx_ref, r, src_hbm, buf, base, sem):
    src = pl.multiple_of(idx_ref[0, r] * TILE_ROWS, TILE_ROWS)
    dst = pl.multiple_of(base + r * TILE_ROWS, TILE_ROWS)
    return pltpu.make_async_copy(src_hbm.at[pl.ds(src, TILE_ROWS), :], buf.at[pl.ds(dst, TILE_ROWS), :], sem)


def _pipelined_gather(cur_ref, nxt_ref, src_hbm, buf, sem, nrows):
    i = pl.program_id(0)
    slot = i % 2
    span = nrows * TILE_ROWS

    @pl.when(i == 0)
    def _():
        def body(r, carry):
            _row_copy(cur_ref, r, src_hbm, buf, 0, sem.at[0]).start()
            return carry
        lax.fori_loop(0, nrows, body, 0)

    @pl.when(i + 1 < pl.num_programs(0))
    def _():
        base = (1 - slot) * span
        for r in range(nrows):
            _row_copy(nxt_ref, r, src_hbm, buf, base, sem.at[1 - slot]).start(priority=r % 2)

    here = pl.multiple_of(slot * span, span)
    pltpu.make_async_copy(src_hbm.at[pl.ds(0, span), :], buf.at[pl.ds(here, span), :], sem.at[slot]).wait()
    return here


def _dispatch_kernel(pos_ref, hx_ref, xs_hbm, stage, zero_tile, sem, *, tm, nt):
    i = pl.program_id(0)
    slot = i % 2
    span = tm * TILE_ROWS

    def wait_step(s):
        for _ in range(TOP_K):
            pltpu.make_async_copy(stage.at[pl.ds(0, span), :], xs_hbm.at[pl.ds(0, span), :], sem.at[s]).wait()

    def dst(r):
        return xs_hbm.at[pl.ds(pl.multiple_of(pos_ref[0, r] * TILE_ROWS, TILE_ROWS), TILE_ROWS), :]

    @pl.when(i < nt)
    def _():
        base = pl.multiple_of(slot * span, span)
        stage[pl.ds(base, span), :] = hx_ref[...]
        for r in range(TOP_K * tm):
            src = stage.at[pl.ds(pl.multiple_of(base + (r % tm) * TILE_ROWS, TILE_ROWS), TILE_ROWS), :]
            pltpu.make_async_copy(src, dst(r), sem.at[slot]).start(priority=r % 2)

    @pl.when(i == 0)
    def _():
        zero_tile[...] = jnp.zeros_like(zero_tile)

    @pl.when(i >= nt)
    def _():
        for r in range(TOP_K * tm):
            pltpu.make_async_copy(zero_tile, dst(r), sem.at[slot]).start(priority=r % 2)

    @pl.when(i > 0)
    def _():
        wait_step(1 - slot)

    @pl.when(i == pl.num_programs(0) - 1)
    def _():
        wait_step(slot)


def _dispatch(pos3, pad_pos, hx_tiles, rows_total, tm):
    nt = pos3.shape[0]
    idx = jnp.concatenate([pos3, pad_pos], axis=0)
    steps = idx.shape[0]
    return pl.pallas_call(
        functools.partial(_dispatch_kernel, tm=tm, nt=nt),
        grid=(steps,),
        in_specs=[pl.BlockSpec((None, 1, TOP_K * tm), lambda i: (i, 0, 0), memory_space=pltpu.SMEM),
                  pl.BlockSpec((tm * TILE_ROWS, 128), lambda i: (jnp.minimum(i, nt - 1), 0))],
        out_specs=pl.BlockSpec(memory_space=pl.ANY),
        out_shape=jax.ShapeDtypeStruct((rows_total * TILE_ROWS, 128), F32),
        scratch_shapes=[pltpu.VMEM((2 * tm * TILE_ROWS, 128), F32), pltpu.VMEM((TILE_ROWS, 128), F32),
                        pltpu.SemaphoreType.DMA((2,))],
        compiler_params=_cparams(("arbitrary",)),
        name="moe_dispatch",
    )(idx, hx_tiles)


def _expert_kernel(blk_e_ref, xs_ref, wg_ref, wu_ref, wd_ref, y_ref, *, bm):
    del blk_e_ref
    x = _from_tiles(xs_ref, 0, bm).astype(BF16)
    hg = jnp.dot(x, wg_ref[...].astype(BF16), preferred_element_type=F32)
    hu = jnp.dot(x, wu_ref[...].astype(BF16), preferred_element_type=F32)
    act = hg * _sigmoid(hg) * hu
    _to_tiles(y_ref, jnp.dot(act.astype(BF16), wd_ref[...].astype(BF16), preferred_element_type=F32))


def _experts(xs_tiles, blk_e, wg, wu, wd, nb, bm):
    D = wg.shape[1]
    grid_spec = pltpu.PrefetchScalarGridSpec(
        num_scalar_prefetch=1,
        grid=(nb,),
        in_specs=[pl.BlockSpec((bm * TILE_ROWS, 128), lambda i, be: (i, 0)),
                  pl.BlockSpec((None, D, EXPERT_HIDDEN), lambda i, be: (be[i], 0, 0)),
                  pl.BlockSpec((None, D, EXPERT_HIDDEN), lambda i, be: (be[i], 0, 0)),
                  pl.BlockSpec((None, EXPERT_HIDDEN, D), lambda i, be: (be[i], 0, 0))],
        out_specs=pl.BlockSpec((bm * TILE_ROWS, 128), lambda i, be: (i, 0)),
    )
    return pl.pallas_call(
        functools.partial(_expert_kernel, bm=bm),
        grid_spec=grid_spec,
        out_shape=jax.ShapeDtypeStruct((nb * bm * TILE_ROWS, 128), F32),
        compiler_params=_cparams(("arbitrary",)),
        name="moe_experts",
    )(blk_e, xs_tiles, wg, wu, wd)


def _combine_kernel(cur_ref, nxt_ref, y_hbm, hx_ref, wt_ref, sgu_ref, sd_ref, x1_ref, g2_ref, fg_ref, o_ref, ybuf,
                    sem, *, tm):
    here = _pipelined_gather(cur_ref, nxt_ref, y_hbm, ybuf, sem, TOP_K * tm)
    wt = wt_ref[...]
    routed = None
    for k in range(TOP_K):
        wk = jnp.broadcast_to(wt[:, k:k + 1], (tm, 128))
        wk = jnp.concatenate([wk] * TILE_ROWS, axis=1)
        term = wk * _from_tiles(ybuf, here + k * tm * TILE_ROWS, tm)
        routed = term if routed is None else routed + term
    hu = jnp.dot(_from_tiles(hx_ref, 0, tm).astype(BF16), sgu_ref[...], preferred_element_type=F32)
    hg = hu[:, :EXPERT_HIDDEN]
    act = hg * _sigmoid(hg) * hu[:, EXPERT_HIDDEN:]
    shared = jnp.dot(act.astype(BF16), sd_ref[...], preferred_element_type=F32)
    xo = x1_ref[...] + g2_ref[...] * (routed + shared)
    o_ref[...] = _rms(xo) * fg_ref[...]


def _combine(pos3, y_tiles, hx_tiles, w_tok, sgu, sd, x1, g2, fg, tm, tiles_per_batch):
    N, D = x1.shape
    nt = N // tm
    smem_blk = lambda f: pl.BlockSpec((None, 1, TOP_K * tm), f, memory_space=pltpu.SMEM)
    row = lambda i: (i, 0)
    const = lambda i: (0, 0)
    return pl.pallas_call(
        functools.partial(_combine_kernel, tm=tm),
        grid=(nt,),
        in_specs=[smem_blk(lambda i: (i, 0, 0)),
                  smem_blk(lambda i: (jnp.minimum(i + 1, nt - 1), 0, 0)),
                  pl.BlockSpec(memory_space=pl.ANY),
                  pl.BlockSpec((tm * TILE_ROWS, 128), row),
                  pl.BlockSpec((tm, 128), row),
                  pl.BlockSpec((D, 2 * EXPERT_HIDDEN), const),
                  pl.BlockSpec((EXPERT_HIDDEN, D), const),
                  pl.BlockSpec((tm, D), row),
                  pl.BlockSpec((None, 1, D), lambda i: (i // tiles_per_batch, 0, 0)),
                  pl.BlockSpec((1, D), const)],
        out_specs=pl.BlockSpec((tm, D), row),
        out_shape=jax.ShapeDtypeStruct((N, D), F32),
        scratch_shapes=[pltpu.VMEM((2 * TOP_K * tm * TILE_ROWS, 128), F32), pltpu.SemaphoreType.DMA((2,))],
        compiler_params=_cparams(("arbitrary",)),
        name="moe_combine",
    )(pos3, pos3, y_tiles, hx_tiles, w_tok, sgu, sd, x1, g2, fg)


def _rope_tables(L):
    rows = L // GRID_W
    row = jnp.repeat(jnp.arange(rows), GRID_W).astype(F32)
    col = jnp.tile(jnp.arange(GRID_W), rows).astype(F32)
    inv = ROPE_THETA ** (-jnp.arange(0, ROPE_AXIS_DIM, 2, dtype=F32) / ROPE_AXIS_DIM)
    ang = jnp.stack([row[:, None] * inv, col[:, None] * inv], axis=1)
    c, s = jnp.cos(ang), jnp.sin(ang)
    cos_h = jnp.stack([c, c], axis=2).reshape(L, HEAD_DIM)
    sin_h = jnp.stack([-s, s], axis=2).reshape(L, HEAD_DIM)
    return jnp.tile(cos_h, (1, 2)), jnp.tile(sin_h, (1, 2))


def _tile(n, pref):
    t = min(pref, n)
    while n % t:
        t //= 2
    return t


def kernel(x, c, ctx, c_ctx, ada_w, ada_b, norm1_g, norm2_g, w_in, lam_q1, lam_k1, lam_q2, lam_k2, subln_g, hy_conv_w, hy_conv_b, filt_w1, filt_b1, filt_freq, filt_w2, filt_b2, filt_w3, hy_bias, w_branch_attn, w_branch_hyena, w_out, router_w, router_bias, exp_w_gate, exp_w_up, exp_w_down, shared_w_gate, shared_w_up, shared_w_down, final_norm_g):
    B, L, D = x.shape
    assert ada_w.shape[0] == 1 and B % 2 == 0 and (2 * L) % (FFT_SLABS * FFT_STRIP) == 0
    lam = (jnp.exp(jnp.sum(lam_q1[0].astype(F32) * lam_k1[0].astype(F32)))
           - jnp.exp(jnp.sum(lam_q2[0].astype(F32) * lam_k2[0].astype(F32))) + LAM_INIT).reshape(1, 1)

    cc = jnp.zeros((16, D), F32).at[:B].set(c).at[B].set(c_ctx)
    mod = _adaln(cc, ada_w[0], ada_b[0])
    chunk = lambda r0, r1, j: mod[r0:r1, j * D:(j + 1) * D]
    sh1, sc1, g1, sh2, sc2, g2 = (chunk(0, B, j).reshape(B, 1, D) for j in range(6))
    csh, csc = chunk(B, B + 1, 0), chunk(B, B + 1, 1)

    w_bf = w_in[0].astype(BF16)
    n1g = norm1_g[0].reshape(1, D)
    cos_t, sin_t = _rope_tables(L)
    k, v, q, u, gates = _inproj(x, sh1, sc1, n1g, w_bf, cos_t, sin_t, hy_conv_w[0], hy_conv_b[0].reshape(1, -1),
                                _tile(L, 512))
    kc, vc = _ctxproj(ctx, csh, csc, n1g, w_bf)
    attn = _attention(lam, q, k, v, kc, vc, subln_g[0].reshape(1, V_DIM), _tile(L, 512), _tile(L, 512))

    filt = (filt_w1[0], filt_b1[0], filt_freq[0], filt_w2[0], filt_b2[0], filt_w3[0])
    hy = _hyena(u, filt, hy_bias[0])

    x1, hx2, logits_t = _merge(attn, hy, gates, x, g1, sh2, sc2, norm2_g[0].reshape(1, D),
                               w_branch_attn[0].astype(BF16), w_branch_hyena[0].astype(BF16),
                               w_out[0].astype(BF16), router_w[0].T.astype(F32), _tile(L, 512))
    eidx, w_tok, rank, counts = _router(logits_t, router_bias[0], _tile(B * L, 1024))

    N = B * L
    tm = _tile(L, MOE_TOKENS)
    pos3, pad_pos, blk_e, nb = _dispatch_plan(eidx, rank, counts, MOE_ROWS, tm)
    xs = _dispatch(pos3, pad_pos, hx2, nb * MOE_ROWS, tm)
    y = _experts(xs, blk_e, exp_w_gate[0], exp_w_up[0], exp_w_down[0], nb, MOE_ROWS)
    sgu = jnp.concatenate([shared_w_gate[0], shared_w_up[0]], axis=-1).astype(BF16)
    out = _combine(pos3, y, hx2, w_tok, sgu, shared_w_down[0].astype(BF16), x1.reshape(N, D), g2,
                   final_norm_g.reshape(1, D), tm, L // tm)
    return out.reshape(B, L, D)
```

```python
import functools
import math

import jax
import jax.numpy as jnp
from jax import lax
from jax.experimental import pallas as pl
from jax.experimental.pallas import tpu as pltpu

F32 = jnp.float32
BF16 = jnp.bfloat16
HIGHEST = lax.Precision.HIGHEST

NORM_EPS = 1e-6
N_HEADS = 8
HEAD_DIM = 64
V_DIM = 2 * HEAD_DIM
GRID_W = 64
ROPE_THETA = 10000.0
ROPE_AXIS_DIM = HEAD_DIM // 2
HYENA_W = 1024
FILTER_EMB = 33
FILTER_BANDS = (FILTER_EMB - 1) // 2
DECAY_TARGET = 1e-2
FAST_DECAY_PCT = 0.3
SLOW_DECAY_PCT = 1.5
N_EXPERTS = 64
N_GROUPS = 8
TOPK_GROUPS = 4
TOP_K = 8
EXPERT_HIDDEN = 256
ROUTED_SCALE = 2.5
LAM_INIT = 0.8 - 0.6 * math.exp(-0.3 * 0)

V7X_VMEM_LIMIT = 56 * 1024 * 1024
LOG2E = 1.4426950408889634


def _cparams(sem):
    return pltpu.CompilerParams(dimension_semantics=sem, vmem_limit_bytes=V7X_VMEM_LIMIT)


def _sigmoid(v):
    return 1.0 / (1.0 + jnp.exp(-v))


def _rms(v):
    return v * lax.rsqrt(jnp.mean(v * v, axis=-1, keepdims=True) + NORM_EPS)


def _adaln_kernel(c_ref, w_ref, b_ref, o_ref):
    c = c_ref[...]
    s = c * _sigmoid(c)
    o_ref[...] = jnp.dot(s, w_ref[...], precision=HIGHEST, preferred_element_type=F32) + b_ref[...]


def _adaln(cc, w, b):
    rows, d = cc.shape
    n = w.shape[1]
    tn = 1536
    return pl.pallas_call(
        _adaln_kernel,
        grid=(n // tn,),
        in_specs=[pl.BlockSpec((rows, d), lambda j: (0, 0)),
                  pl.BlockSpec((d, tn), lambda j: (0, j)),
                  pl.BlockSpec((1, tn), lambda j: (0, j))],
        out_specs=pl.BlockSpec((rows, tn), lambda j: (0, j)),
        out_shape=jax.ShapeDtypeStruct((rows, n), F32),
        compiler_params=_cparams(("arbitrary",)),
        name="adaln",
    )(cc, w, b.reshape(1, n))


def _inproj_kernel(x_ref, xp_ref, xn_ref, sh_ref, sc_ref, g_ref, w_ref, cos_ref, sin_ref, cw_ref, cb_ref,
                   k_ref, v_ref, q_ref, u_ref, gt_ref, pscr, *, tm, nt, qscale):
    i = pl.program_id(1)
    g = g_ref[...]
    sh = sh_ref[...]
    sc = 1.0 + sc_ref[...]

    def norm_mod(xx):
        return (_rms(xx) * g) * sc + sh

    h = norm_mod(x_ref[...])
    hp = jnp.where(i > 0, norm_mod(xp_ref[...]), 0.0)
    hn = jnp.where(i < nt - 1, norm_mod(xn_ref[...]), 0.0)
    hb = h.astype(BF16)
    h_ext = jnp.concatenate([hp, h, hn], axis=0).astype(BF16)

    def proj(lhs, c0):
        return jnp.dot(lhs, w_ref[:, c0:c0 + 1024], preferred_element_type=F32)

    cosv = cos_ref[...]
    sinv = sin_ref[...]
    lane = lax.broadcasted_iota(jnp.int32, (1, V_DIM), 1)
    first = (lane % ROPE_AXIS_DIM) < (ROPE_AXIS_DIM // 2)

    def rope_store(a, o_ref, scale):
        for hh in range(N_HEADS):
            s = a[:, hh * V_DIM:(hh + 1) * V_DIM]
            partner = jnp.where(first, pltpu.roll(s, V_DIM - ROPE_AXIS_DIM // 2, 1),
                                pltpu.roll(s, ROPE_AXIS_DIM // 2, 1))
            r = s * cosv + partner * sinv
            if scale != 1.0:
                r = r * scale
            o_ref[:, hh * V_DIM:(hh + 1) * V_DIM] = r.astype(BF16)

    rope_store(proj(hb, 0), k_ref, 1.0)
    v_ref[...] = proj(hb, 1024).astype(BF16)
    rope_store(proj(hb, 2048), q_ref, qscale)
    for j in range(3):
        pscr[...] = proj(h_ext, 3072 + j * 1024)
        cw = cw_ref[:, j * 1024:(j + 1) * 1024]
        cb = cb_ref[:, j * 1024:(j + 1) * 1024]
        u = (pscr[pl.ds(7, tm), :] * cw[0:1] + pscr[pl.ds(8, tm), :] * cw[1:2]
             + pscr[pl.ds(9, tm), :] * cw[2:3] + cb)
        u_ref[:, j * 1024:(j + 1) * 1024] = u.astype(BF16)
    for j in range(2):
        gt_ref[:, j * 1024:(j + 1) * 1024] = proj(hb, 6144 + j * 1024).astype(BF16)


def _inproj(x, shift, scale, g, w_bf, cos_t, sin_t, conv_w, conv_b, tm):
    B, L, D = x.shape
    nt = L // tm
    nb8 = L // 8
    qscale = (HEAD_DIM ** -0.5) * LOG2E
    kern = functools.partial(_inproj_kernel, tm=tm, nt=nt, qscale=qscale)
    row = lambda b, i: (b, i, 0)
    per_b = lambda b, i: (b, 0, 0)
    const = lambda b, i: (0, 0)
    outs = pl.pallas_call(
        kern,
        grid=(B, nt),
        in_specs=[
            pl.BlockSpec((None, tm, D), row),
            pl.BlockSpec((None, 8, D), lambda b, i: (b, jnp.maximum(i * (tm // 8) - 1, 0), 0)),
            pl.BlockSpec((None, 8, D), lambda b, i: (b, jnp.minimum((i + 1) * (tm // 8), nb8 - 1), 0)),
            pl.BlockSpec((None, 1, D), per_b),
            pl.BlockSpec((None, 1, D), per_b),
            pl.BlockSpec((1, D), const),
            pl.BlockSpec((D, 8192), const, pipeline_mode=pl.Buffered(1)),
            pl.BlockSpec((tm, V_DIM), lambda b, i: (i, 0)),
            pl.BlockSpec((tm, V_DIM), lambda b, i: (i, 0)),
            pl.BlockSpec((3, 3 * HYENA_W), const),
            pl.BlockSpec((1, 3 * HYENA_W), const),
        ],
        out_specs=[
            pl.BlockSpec((None, tm, 1024), row),
            pl.BlockSpec((None, tm, 1024), row),
            pl.BlockSpec((None, tm, 1024), row),
            pl.BlockSpec((None, tm, 3 * HYENA_W), row),
            pl.BlockSpec((None, tm, 2048), row),
        ],
        out_shape=[
            jax.ShapeDtypeStruct((B, L, 1024), BF16),
            jax.ShapeDtypeStruct((B, L, 1024), BF16),
            jax.ShapeDtypeStruct((B, L, 1024), BF16),
            jax.ShapeDtypeStruct((B, L, 3 * HYENA_W), BF16),
            jax.ShapeDtypeStruct((B, L, 2048), BF16),
        ],
        scratch_shapes=[pltpu.VMEM((tm + 16, 1024), F32)],
        compiler_params=_cparams(("parallel", "arbitrary")),
        name="inproj",
    )(x, x, x, shift, scale, g, w_bf, cos_t, sin_t, conv_w, conv_b)
    return outs


def _ctxproj_kernel(x_ref, sh_ref, sc_ref, g_ref, w_ref, k_ref, v_ref):
    h = ((_rms(x_ref[...]) * g_ref[...]) * (1.0 + sc_ref[...]) + sh_ref[...]).astype(BF16)
    k_ref[...] = jnp.dot(h, w_ref[:, 0:1024], preferred_element_type=F32).astype(BF16)
    v_ref[...] = jnp.dot(h, w_ref[:, 1024:2048], preferred_element_type=F32).astype(BF16)


def _ctxproj(ctx, shift, scale, g, w_bf):
    B, Lc, D = ctx.shape
    return pl.pallas_call(
        _ctxproj_kernel,
        grid=(B,),
        in_specs=[pl.BlockSpec((None, Lc, D), lambda b: (b, 0, 0)),
                  pl.BlockSpec((1, D), lambda b: (0, 0)),
                  pl.BlockSpec((1, D), lambda b: (0, 0)),
                  pl.BlockSpec((1, D), lambda b: (0, 0)),
                  pl.BlockSpec((D, 2048), lambda b: (0, 0))],
        out_specs=[pl.BlockSpec((None, Lc, 1024), lambda b: (b, 0, 0)),
                   pl.BlockSpec((None, Lc, 1024), lambda b: (b, 0, 0))],
        out_shape=[jax.ShapeDtypeStruct((B, Lc, 1024), BF16),
                   jax.ShapeDtypeStruct((B, Lc, 1024), BF16)],
        compiler_params=_cparams(("arbitrary",)),
        name="ctxproj",
    )(ctx, shift, scale, g, w_bf)


def _attn_kernel(lam_ref, q_ref, k_ref, v_ref, kc_ref, vc_ref, g_ref, o_ref, m_s, acc_s, *, tk, nk):
    q = q_ref[...]
    lane = lax.broadcasted_iota(jnp.int32, (1, V_DIM), 1)
    zero = jnp.zeros_like(q)
    qs = (jnp.where(lane < HEAD_DIM, q, zero), jnp.where(lane >= HEAD_DIM, q, zero))
    m_s[...] = jnp.full(m_s.shape, -1e30, F32)
    acc_s[...] = jnp.zeros(acc_s.shape, F32)

    def chunk(kk, vv):
        reps = kk.shape[0] // V_DIM
        v1 = jnp.concatenate([vv, jnp.ones_like(vv)], axis=1)
        for m in range(2):
            s = lax.dot_general(qs[m], kk, (((1,), (1,)), ((), ())), preferred_element_type=F32)
            m_prev = m_s[m]
            m_new = jnp.maximum(m_prev, jnp.max(s, axis=-1, keepdims=True))
            alpha = jnp.exp2(m_prev - m_new)
            p = jnp.exp2(s - jnp.concatenate([m_new] * reps, axis=1))
            acc_s[m] = (jnp.concatenate([alpha, alpha], axis=1) * acc_s[m]
                        + jnp.dot(p.astype(BF16), v1, preferred_element_type=F32))
            m_s[m] = m_new

    def body(j, carry):
        off = pl.multiple_of(j * tk, tk)
        chunk(k_ref[pl.ds(off, tk), :], v_ref[pl.ds(off, tk), :])
        return carry

    lax.fori_loop(0, nk, body, 0, unroll=2 if nk % 2 == 0 else 1)
    chunk(kc_ref[...], vc_ref[...])
    lam = lam_ref[0, 0]
    a0, a1 = acc_s[0], acc_s[1]
    o = a0[:, :V_DIM] / a0[:, V_DIM:] - lam * (a1[:, :V_DIM] / a1[:, V_DIM:])
    o = _rms(o) * g_ref[...] * (1.0 - LAM_INIT)
    o_ref[...] = o.astype(BF16)


def _attention(lam, q, k, v, kc, vc, subln_g, tq, tk):
    B, L, _ = q.shape
    Lc = kc.shape[1]
    kern = functools.partial(_attn_kernel, tk=tk, nk=L // tk)
    kv = lambda b, h, i: (b, 0, h)
    return pl.pallas_call(
        kern,
        grid=(B, N_HEADS, L // tq),
        in_specs=[
            pl.BlockSpec(memory_space=pltpu.SMEM),
            pl.BlockSpec((None, tq, V_DIM), lambda b, h, i: (b, i, h)),
            pl.BlockSpec((None, L, V_DIM), kv),
            pl.BlockSpec((None, L, V_DIM), kv),
            pl.BlockSpec((None, Lc, V_DIM), kv),
            pl.BlockSpec((None, Lc, V_DIM), kv),
            pl.BlockSpec((1, V_DIM), lambda b, h, i: (0, 0)),
        ],
        out_specs=pl.BlockSpec((None, tq, V_DIM), lambda b, h, i: (b, i, h)),
        out_shape=jax.ShapeDtypeStruct((B, L, N_HEADS * V_DIM), BF16),
        scratch_shapes=[pltpu.VMEM((2, tq, V_DIM), F32), pltpu.VMEM((2, tq, 2 * V_DIM), F32)],
        compiler_params=_cparams(("parallel", "parallel", "arbitrary")),
        name="diffattn",
    )(lam, q, k, v, kc, vc, subln_g)


FFT_SLABS = 64
FFT_STRIP = 16


def _dft_tables(nb):
    N = FFT_SLABS * nb
    fa = jnp.arange(FFT_SLABS, dtype=jnp.int32)[:, None, None]
    fb = jnp.arange(nb, dtype=jnp.int32)[None, :, None]
    b = jnp.arange(nb, dtype=jnp.int32)[None, None, :]
    ang = ((FFT_SLABS * b * fb + b * fa) % N).astype(F32) * (2.0 * math.pi / N)
    c, s = jnp.cos(ang), jnp.sin(ang)
    fwd = jnp.concatenate([jnp.concatenate([c, s], -1), jnp.concatenate([-s, c], -1)], axis=1)
    inv = jnp.swapaxes(fwd, 1, 2) / N
    return fwd.astype(BF16), inv.astype(BF16)


def _vadd(a, b):
    return b if a is None else (a if b is None else a + b)


def _vsub(a, b):
    if b is None:
        return a
    return -b if a is None else a - b


def _vscale(a, k):
    if a is None or k == 1.0:
        return a
    return -a if k == -1.0 else a * k


def _cmulc(x, c, s):
    re, im = x
    eps = 1e-9
    if abs(s) < eps:
        k = 1.0 if c > 0 else -1.0
        return _vscale(re, k), _vscale(im, k)
    if abs(c) < eps:
        k = 1.0 if s > 0 else -1.0
        return _vscale(im, -k), _vscale(re, k)
    if abs(abs(c) - abs(s)) < eps:
        kc = 1.0 if c > 0 else -1.0
        ks = 1.0 if s > 0 else -1.0
        return (_vscale(_vsub(_vscale(re, kc), _vscale(im, ks)), abs(c)),
                _vscale(_vadd(_vscale(re, ks), _vscale(im, kc)), abs(c)))
    return _vsub(_vscale(re, c), _vscale(im, s)), _vadd(_vscale(re, s), _vscale(im, c))


def _fft(xs, sign):
    n = len(xs)
    if n == 1:
        return xs
    ev, od = _fft(xs[0::2], sign), _fft(xs[1::2], sign)
    out = [None] * n
    for k in range(n // 2):
        ang = sign * 2.0 * math.pi * k / n
        t = _cmulc(od[k], math.cos(ang), math.sin(ang))
        out[k] = (_vadd(ev[k][0], t[0]), _vadd(ev[k][1], t[1]))
        out[k + n // 2] = (_vsub(ev[k][0], t[0]), _vsub(ev[k][1], t[1]))
    return out


def _put(ref, slot, rows, val):
    ref[slot, rows, :] = jnp.zeros((FFT_STRIP, ref.shape[2]), F32) if val is None else val


def _slab_fft_forward(load, s_ref, nb, n_in):
    def strip(i, carry):
        r0 = pl.multiple_of(i * FFT_STRIP, FFT_STRIP)
        rows_re, rows_im = pl.ds(r0, FFT_STRIP), pl.ds(nb + r0, FFT_STRIP)
        for a0 in range(8):
            xs = [load(8 * a1 + a0, r0) if a1 < n_in else (None, None) for a1 in range(8)]
            ys = _fft(xs, -1.0)
            for f1 in range(8):
                ang = -2.0 * math.pi * a0 * f1 / FFT_SLABS
                yr, yi = _cmulc(ys[f1], math.cos(ang), math.sin(ang))
                _put(s_ref, a0 * 8 + f1, rows_re, yr)
                _put(s_ref, a0 * 8 + f1, rows_im, yi)
        for f1 in range(8):
            xs = [(s_ref[a0 * 8 + f1, rows_re, :], s_ref[a0 * 8 + f1, rows_im, :]) for a0 in range(8)]
            ys = _fft(xs, -1.0)
            for f0 in range(8):
                s_ref[f1 + 8 * f0, rows_re, :] = ys[f0][0]
                s_ref[f1 + 8 * f0, rows_im, :] = ys[f0][1]
        return carry

    lax.fori_loop(0, nb // FFT_STRIP, strip, 0)


def _slab_fft_inverse(s_ref, nb, emit):
    def strip(i, carry):
        r0 = pl.multiple_of(i * FFT_STRIP, FFT_STRIP)
        rows_re, rows_im = pl.ds(r0, FFT_STRIP), pl.ds(nb + r0, FFT_STRIP)
        for q in range(8):
            xs = [(s_ref[8 * p + q, rows_re, :], s_ref[8 * p + q, rows_im, :]) for p in range(8)]
            us = _fft(xs, 1.0)
            for r in range(8):
                ang = 2.0 * math.pi * r * q / FFT_SLABS
                ur, ui = _cmulc(us[r], math.cos(ang), math.sin(ang))
                s_ref[8 * r + q, rows_re, :] = ur
                s_ref[8 * r + q, rows_im, :] = ui
        for r in range(8):
            xs = [(s_ref[8 * r + q, rows_re, :], s_ref[8 * r + q, rows_im, :]) for q in range(8)]
            ys = _fft(xs, 1.0)
            for sidx in range(4):
                emit(r + 8 * sidx, r0, ys[sidx][0], ys[sidx][1])
        return carry

    lax.fori_loop(0, nb // FFT_STRIP, strip, 0)


def _hyena_conv_kernel(v_ref, x_ref, k_ref, ff_ref, fi_ref, bias_ref, o_ref, s_ref, *, nb):
    def load(a, r0):
        rows = pl.ds(pl.multiple_of(a * nb + r0, FFT_STRIP), FFT_STRIP)
        return v_ref[0, rows, :].astype(F32), v_ref[1, rows, :].astype(F32)

    _slab_fft_forward(load, s_ref, nb, FFT_SLABS // 16)

    def freq(f, carry):
        z = jnp.dot(ff_ref[f], s_ref[f].astype(BF16), preferred_element_type=F32)
        zr, zi = z[:nb], z[nb:]
        kr, ki = k_ref[0, f], k_ref[1, f]
        y = jnp.concatenate([zr * kr - zi * ki, zr * ki + zi * kr], axis=0).astype(BF16)
        s_ref[f] = jnp.dot(fi_ref[f], y, preferred_element_type=F32)
        return carry

    lax.fori_loop(0, FFT_SLABS, freq, 0, unroll=8)
    bias = bias_ref[...]

    def emit(a, r0, re, im):
        rows = pl.ds(pl.multiple_of(a * nb + r0, FFT_STRIP), FFT_STRIP)
        for half, y in enumerate((re, im)):
            v = v_ref[half, rows, :].astype(F32)
            o_ref[half, rows, :] = (x_ref[half, rows, :].astype(F32) * (y + v * bias)).astype(BF16)

    _slab_fft_inverse(s_ref, nb, emit)


def _hyena_conv(uv, zv, k_hat, ff, fi, bias, order, in_col, gate_col, nb):
    _, P, L, _ = uv.shape
    W = HYENA_W
    nct = W // 128
    src, src_off = (uv, in_col * nct) if zv is None else (zv, 0)
    blk = (2, None, L, 128)
    return pl.pallas_call(
        functools.partial(_hyena_conv_kernel, nb=nb),
        grid=(nct, P),
        in_specs=[pl.BlockSpec(blk, lambda c, p: (0, p, 0, src_off + c)),
                  pl.BlockSpec(blk, lambda c, p: (0, p, 0, gate_col * nct + c)),
                  pl.BlockSpec((2, FFT_SLABS, nb, 128), lambda c, p: (0, 0, 0, order * nct + c),
                               pipeline_mode=pl.Buffered(1)),
                  pl.BlockSpec((FFT_SLABS, 2 * nb, 2 * nb), lambda c, p: (0, 0, 0), pipeline_mode=pl.Buffered(1)),
                  pl.BlockSpec((FFT_SLABS, 2 * nb, 2 * nb), lambda c, p: (0, 0, 0), pipeline_mode=pl.Buffered(1)),
                  pl.BlockSpec((1, 128), lambda c, p: (0, c))],
        out_specs=pl.BlockSpec(blk, lambda c, p: (0, p, 0, c)),
        out_shape=jax.ShapeDtypeStruct((2, P, L, W), BF16),
        scratch_shapes=[pltpu.VMEM((FFT_SLABS, 2 * nb, 128), F32)],
        compiler_params=_cparams(("parallel", "arbitrary")),
        name="hyena_conv%d" % order,
    )(src, uv, k_hat, ff, fi, bias)


def _filter_features(n):
    pos = jnp.arange(n, dtype=F32)[:, None]
    t = jnp.linspace(0.0, 1.0, n, dtype=F32)[:, None]
    w = 2 * math.pi * pos / n
    bands = jnp.linspace(1e-4, FILTER_BANDS - 1, FILTER_BANDS, dtype=F32)
    z = jnp.concatenate([t, jnp.cos(bands * w), -jnp.sin(bands * w)], axis=-1)
    z = jnp.pad(z, ((0, 0), (0, 128 - FILTER_EMB)))
    tap = jnp.arange(2 * n)
    return z[jnp.where(tap < n, tap, (2 * n - tap) % n)]


def _filt_taps_kernel(z_ref, w1_ref, b1_ref, fr_ref, w2_ref, b2_ref, w3_ref, dl_ref, k_ref, sum_ref):
    a = pl.program_id(0)
    z = z_ref[...]
    fr = fr_ref[...]
    hd = jnp.sin(fr * (jnp.dot(z, w1_ref[...], precision=HIGHEST, preferred_element_type=F32) + b1_ref[...]))
    hd = jnp.sin(fr * (jnp.dot(hd, w2_ref[...], precision=HIGHEST, preferred_element_type=F32) + b2_ref[...]))
    h = jnp.dot(hd, w3_ref[...], precision=HIGHEST, preferred_element_type=F32) * jnp.exp(-z[:, 0:1] * dl_ref[...])
    rows = lax.broadcasted_iota(jnp.int32, (z.shape[0], 1), 0)
    h = jnp.where(jnp.logical_and(a == FFT_SLABS // 2, rows == 0), 0.0, h)
    k_ref[...] = h

    @pl.when(a == 0)
    def _():
        sum_ref[...] = jnp.zeros_like(sum_ref)

    sum_ref[0:1, :] += jnp.sum(jnp.abs(h), axis=0, keepdims=True)


def _filt_spectrum_kernel(t_ref, inv_ref, ff_ref, k_ref, s_ref, *, nb):
    def load(a, r0):
        return t_ref[pl.ds(pl.multiple_of(a * nb + r0, FFT_STRIP), FFT_STRIP), :], None

    _slab_fft_forward(load, s_ref, nb, FFT_SLABS // 8)
    inv = inv_ref[...]

    def freq(f, carry):
        z = jnp.dot(ff_ref[f], s_ref[f].astype(BF16), preferred_element_type=F32)
        k_ref[0, f] = z[:nb] * inv
        k_ref[1, f] = z[nb:] * inv
        return carry

    lax.fori_loop(0, FFT_SLABS, freq, 0, unroll=8)


def _hyena_filters(n, filt, ff, nb):
    w1, b1, freq, w2, b2, w3 = filt
    W = HYENA_W
    hid = w1.shape[1]
    z = _filter_features(n)
    w1p = jnp.pad(w1.astype(F32), ((0, 128 - FILTER_EMB), (0, 0)))
    w3d = w3.astype(F32).reshape(hid, 2, 2, W)
    w3sel = jnp.stack([w3d[:, :, d, :].reshape(hid, 2 * W) for d in range(2)])
    deltas = jnp.abs(jnp.linspace(math.log(DECAY_TARGET) / SLOW_DECAY_PCT,
                                  math.log(DECAY_TARGET) / FAST_DECAY_PCT, W, dtype=F32))
    deltas = jnp.tile(deltas.reshape(1, W), (1, 2))
    const = lambda a: (0, 0)
    taps, sums = pl.pallas_call(
        _filt_taps_kernel,
        grid=(FFT_SLABS,),
        in_specs=[pl.BlockSpec((nb, 128), lambda a: (a, 0)),
                  pl.BlockSpec((128, hid), const), pl.BlockSpec((1, hid), const), pl.BlockSpec((1, hid), const),
                  pl.BlockSpec((hid, hid), const), pl.BlockSpec((1, hid), const),
                  pl.BlockSpec((None, hid, 2 * W), lambda a: (a // (FFT_SLABS // 2), 0, 0)),
                  pl.BlockSpec((1, 2 * W), const)],
        out_specs=[pl.BlockSpec((nb, 2 * W), lambda a: (a, 0)), pl.BlockSpec((8, 2 * W), const)],
        out_shape=[jax.ShapeDtypeStruct((2 * n, 2 * W), F32), jax.ShapeDtypeStruct((8, 2 * W), F32)],
        compiler_params=_cparams(("arbitrary",)),
        name="hyena_filter_taps",
    )(z, w1p, b1.reshape(1, hid), freq.reshape(1, hid), w2, b2.reshape(1, hid), w3sel, deltas)
    inv = 1.0 / sums[0:1]
    return pl.pallas_call(
        functools.partial(_filt_spectrum_kernel, nb=nb),
        grid=(2 * W // 128,),
        in_specs=[pl.BlockSpec((2 * n, 128), lambda j: (0, j)),
                  pl.BlockSpec((1, 128), lambda j: (0, j)),
                  pl.BlockSpec((FFT_SLABS, 2 * nb, 2 * nb), lambda j: (0, 0, 0), pipeline_mode=pl.Buffered(1))],
        out_specs=pl.BlockSpec((2, FFT_SLABS, nb, 128), lambda j: (0, 0, 0, j)),
        out_shape=jax.ShapeDtypeStruct((2, FFT_SLABS, nb, 2 * W), F32),
        scratch_shapes=[pltpu.VMEM((FFT_SLABS, 2 * nb, 128), F32)],
        compiler_params=_cparams(("parallel",)),
        name="hyena_filter_spectrum",
    )(taps, inv, ff)


def _hyena(u, filt, hy_bias):
    B, L, _ = u.shape
    P = B // 2
    nb = 2 * L // FFT_SLABS
    ff, fi = _dft_tables(nb)
    k_hat = _hyena_filters(L, filt, ff, nb)
    uv = u.reshape(2, P, L, 3 * HYENA_W)
    bias = hy_bias.astype(F32)
    z = _hyena_conv(uv, None, k_hat, ff, fi, bias[0:1], 0, 0, 1, nb)
    y = _hyena_conv(uv, z, k_hat, ff, fi, bias[1:2], 1, 0, 2, nb)
    return y.reshape(B, L, HYENA_W)


def _merge_kernel(a_ref, h_ref, gt_ref, x_ref, g1_ref, sh_ref, sc_ref, n2g_ref, wpa_ref, wph_ref, wo_ref, rw_ref,
                  x1_ref, hx_ref, lg_ref):
    ya = jnp.dot(a_ref[...], wpa_ref[...], preferred_element_type=F32)
    yh = jnp.dot(h_ref[...], wph_ref[...], preferred_element_type=F32)
    ga = gt_ref[:, 0:1024].astype(F32)
    gh = gt_ref[:, 1024:2048].astype(F32)
    y = _sigmoid(ga) * ya + _sigmoid(gh) * yh
    mix = jnp.dot(y.astype(BF16), wo_ref[...], preferred_element_type=F32)
    x1 = x_ref[...] + g1_ref[...] * mix
    x1_ref[...] = x1
    hx = (_rms(x1) * n2g_ref[...]) * (1.0 + sc_ref[...]) + sh_ref[...]
    _to_tiles(hx_ref, hx)
    lg_ref[...] = lax.dot_general(rw_ref[...], hx, (((1,), (1,)), ((), ())), precision=HIGHEST,
                                  preferred_element_type=F32)


def _merge(attn, hy, gates, x, g1, sh2, sc2, n2g, wpa, wph, wo, rwT, tm):
    B, L, D = x.shape
    nt = L // tm
    row = lambda b, i: (b, i, 0)
    per_b = lambda b, i: (b, 0, 0)
    const = lambda b, i: (0, 0)
    return pl.pallas_call(
        _merge_kernel,
        grid=(B, nt),
        in_specs=[pl.BlockSpec((None, tm, 1024), row), pl.BlockSpec((None, tm, 1024), row),
                  pl.BlockSpec((None, tm, 2048), row), pl.BlockSpec((None, tm, D), row),
                  pl.BlockSpec((None, 1, D), per_b), pl.BlockSpec((None, 1, D), per_b),
                  pl.BlockSpec((None, 1, D), per_b), pl.BlockSpec((1, D), const),
                  pl.BlockSpec((1024, D), const), pl.BlockSpec((1024, D), const), pl.BlockSpec((D, D), const),
                  pl.BlockSpec((N_EXPERTS, D), const)],
        out_specs=[pl.BlockSpec((None, tm, D), row),
                   pl.BlockSpec((tm * TILE_ROWS, 128), lambda b, i: (b * nt + i, 0)),
                   pl.BlockSpec((N_EXPERTS, tm), lambda b, i: (0, b * nt + i))],
        out_shape=[jax.ShapeDtypeStruct((B, L, D), F32),
                   jax.ShapeDtypeStruct((B * L * TILE_ROWS, 128), F32),
                   jax.ShapeDtypeStruct((N_EXPERTS, B * L), F32)],
        compiler_params=_cparams(("parallel", "arbitrary")),
        name="merge",
    )(attn, hy, gates, x, g1, sh2, sc2, n2g, wpa, wph, wo, rwT)


def _router_kernel(lg_ref, rb_ref, tri_ref, ei_ref, wt_ref, rk_ref, cnt_ref):
    scores = _sigmoid(lg_ref[...])
    choice = scores + rb_ref[...]
    E, T = scores.shape
    gsz = E // N_GROUPS
    neg = jnp.float32(-jnp.inf)
    big = jnp.int32(1 << 30)
    rows8 = lax.broadcasted_iota(jnp.int32, (gsz, T), 0)
    gscore = []
    for gi in range(N_GROUPS):
        c = choice[gi * gsz:(gi + 1) * gsz]
        m1 = jnp.max(c, axis=0, keepdims=True)
        i1 = jnp.min(jnp.where(c == m1, rows8, big), axis=0, keepdims=True)
        m2 = jnp.max(jnp.where(rows8 == i1, neg, c), axis=0, keepdims=True)
        gscore.append(m1 + m2)
    gs = jnp.concatenate(gscore, axis=0)
    rowsg = lax.broadcasted_iota(jnp.int32, (N_GROUPS, T), 0)
    gsel = jnp.zeros((N_GROUPS, T), jnp.bool_)
    for _ in range(TOPK_GROUPS):
        m = jnp.max(gs, axis=0, keepdims=True)
        ix = jnp.min(jnp.where(gs == m, rowsg, big), axis=0, keepdims=True)
        hit = rowsg == ix
        gsel = jnp.logical_or(gsel, hit)
        gs = jnp.where(hit, neg, gs)
    masked = jnp.concatenate(
        [jnp.where(gsel[gi:gi + 1], choice[gi * gsz:(gi + 1) * gsz], neg) for gi in range(N_GROUPS)], axis=0)
    rows = lax.broadcasted_iota(jnp.int32, (E, T), 0)
    ids, ws, hits = [], [], []
    sel = jnp.zeros((E, T), F32)
    for _ in range(TOP_K):
        m = jnp.max(masked, axis=0, keepdims=True)
        ix = jnp.min(jnp.where(masked == m, rows, big), axis=0, keepdims=True)
        hit = rows == ix
        ids.append(ix)
        hits.append(hit)
        ws.append(jnp.sum(jnp.where(hit, scores, 0.0), axis=0, keepdims=True))
        sel = jnp.where(hit, 1.0, sel)
        masked = jnp.where(hit, neg, masked)
    w = jnp.concatenate(ws, axis=0)
    w = w / jnp.sum(w, axis=0, keepdims=True) * ROUTED_SCALE
    ei_ref[...] = jnp.concatenate(ids, axis=0)
    wt_ref[...] = jnp.concatenate([w, jnp.zeros((128 - TOP_K, T), F32)], axis=0).T

    @pl.when(pl.program_id(0) == 0)
    def _():
        cnt_ref[...] = jnp.zeros_like(cnt_ref)

    before = cnt_ref[...][:, 0:1]
    csum = jnp.dot(sel.astype(BF16), tri_ref[...], preferred_element_type=F32)
    rank = before + csum - sel
    rk_ref[...] = jnp.concatenate(
        [jnp.sum(jnp.where(h, rank, 0.0), axis=0, keepdims=True) for h in hits], axis=0).astype(jnp.int32)
    cnt_ref[...] = jnp.broadcast_to(before + csum[:, T - 1:T], cnt_ref.shape)


def _router(logits_t, router_bias, tt):
    E, N = logits_t.shape
    tri = jnp.triu(jnp.ones((tt, tt), BF16))
    per_tok = pl.BlockSpec((TOP_K, tt), lambda i: (0, i))
    return pl.pallas_call(
        _router_kernel,
        grid=(N // tt,),
        in_specs=[pl.BlockSpec((E, tt), lambda i: (0, i)), pl.BlockSpec((E, 1), lambda i: (0, 0)),
                  pl.BlockSpec((tt, tt), lambda i: (0, 0))],
        out_specs=[per_tok, pl.BlockSpec((tt, 128), lambda i: (i, 0)), per_tok,
                   pl.BlockSpec((E, 128), lambda i: (0, 0))],
        out_shape=[jax.ShapeDtypeStruct((TOP_K, N), jnp.int32), jax.ShapeDtypeStruct((N, 128), F32),
                   jax.ShapeDtypeStruct((TOP_K, N), jnp.int32), jax.ShapeDtypeStruct((E, 128), F32)],
        compiler_params=_cparams(("arbitrary",)),
        name="router",
    )(logits_t, router_bias.reshape(E, 1).astype(F32), tri)


MOE_ROWS = 512
MOE_TOKENS = 128


def _plan_kernel(ei_ref, rk_ref, ps_ref, pos_ref):
    ei = ei_ref[...]
    ps = ps_ref[...]
    rows = lax.broadcasted_iota(jnp.int32, (ps.shape[0], ei.shape[1]), 0)
    start = [jnp.sum(jnp.where(rows == ei[k:k + 1], ps, 0.0), axis=0, keepdims=True) for k in range(ei.shape[0])]
    pos_ref[...] = rk_ref[...] + jnp.concatenate(start, axis=0).astype(jnp.int32)


def _dispatch_plan(eidx, rank, counts, bm, tm):
    K, N = eidx.shape
    E = N_EXPERTS
    cnt = counts[:, 0].astype(jnp.int32)
    padded = (cnt + bm - 1) // bm * bm
    pad_end = jnp.cumsum(padded)
    pad_start = pad_end - padded
    nb = -(-(N * K + E * (bm - 1)) // bm)
    tt = _tile(N, 2048)
    per_tok = pl.BlockSpec((K, tt), lambda i: (0, i))
    pos = pl.pallas_call(
        _plan_kernel,
        grid=(N // tt,),
        in_specs=[per_tok, per_tok, pl.BlockSpec((E, 1), lambda i: (0, 0))],
        out_specs=per_tok,
        out_shape=jax.ShapeDtypeStruct((K, N), jnp.int32),
        compiler_params=_cparams(("parallel",)),
        name="moe_plan",
    )(eidx, rank, pad_start.astype(F32).reshape(E, 1))
    blk_start = jnp.arange(nb, dtype=jnp.int32) * bm
    blk_e = jnp.minimum(jnp.sum(blk_start[:, None] >= pad_end[None, :], axis=1), E - 1)
    n_pad = nb * bm - N * K
    assert n_pad % (K * tm) == 0
    cpad = jnp.cumsum(padded - cnt)
    q = jnp.arange(n_pad, dtype=jnp.int32)
    seg = jnp.sum(q[:, None] >= cpad[None, :], axis=1)
    onehot = seg[:, None] == jnp.arange(E + 1, dtype=jnp.int32)[None, :]
    first = jnp.concatenate([pad_start + cnt, pad_end[-1:]])
    skipped = jnp.concatenate([jnp.zeros((1,), cpad.dtype), cpad])
    pad_pos = (q + jnp.sum(jnp.where(onehot, (first - skipped)[None, :], 0), axis=1)).astype(jnp.int32)
    nt = N // tm
    pos3 = pos.reshape(K, nt, tm).transpose(1, 0, 2).reshape(nt, 1, K * tm)
    return pos3, pad_pos.reshape(-1, 1, K * tm), blk_e.astype(jnp.int32), nb


TILE_ROWS = 8


def _to_tiles(ref, val):
    m = val.shape[0]
    for j in range(TILE_ROWS):
        ref[pl.ds(j, m, stride=TILE_ROWS), :] = val[:, j * 128:(j + 1) * 128]


def _from_tiles(ref, start, m):
    return jnp.concatenate([ref[pl.ds(start + j, m, stride=TILE_ROWS), :] for j in range(TILE_ROWS)], axis=1)


def _row_copy(idx_ref, r, src_hbm, buf, base, sem):
    src = pl.multiple_of(idx_ref[0, r] * TILE_ROWS, TILE_ROWS)
    dst = base + r * TILE_ROWS
    if not isinstance(dst, int):
        dst = pl.multiple_of(dst, TILE_ROWS)
    return pltpu.make_async_copy(src_hbm.at[pl.ds(src, TILE_ROWS), :], buf.at[pl.ds(dst, TILE_ROWS), :], sem)


def _pipelined_gather(cur_ref, nxt_ref, src_hbm, buf, sem, nrows, consume):
    i = pl.program_id(0)
    span = nrows * TILE_ROWS

    def wait(slot):
        pltpu.make_async_copy(src_hbm.at[pl.ds(0, span), :], buf.at[pl.ds(slot * span, span), :], sem.at[slot]).wait()

    @pl.when(i == 0)
    def _():
        def body(r, carry):
            _row_copy(cur_ref, r, src_hbm, buf, 0, sem.at[0]).start()
            return carry
        lax.fori_loop(0, nrows, body, 0)

    for slot in range(2):
        @pl.when(i % 2 == slot)
        def _():
            for r in range(nrows):
                _row_copy(nxt_ref, r, src_hbm, buf, (1 - slot) * span, sem.at[1 - slot]).start(priority=r % 2)
            wait(slot)
            consume(slot * span)

            @pl.when(i == pl.num_programs(0) - 1)
            def _():
                wait(1 - slot)


def _dispatch_kernel(pos_ref, hx_ref, xs_hbm, stage, zero_tile, sem, *, tm, nt):
    i = pl.program_id(0)
    slot = i % 2
    span = tm * TILE_ROWS

    def wait_step(s):
        for _ in range(TOP_K):
            pltpu.make_async_copy(stage.at[pl.ds(0, span), :], xs_hbm.at[pl.ds(0, span), :], sem.at[s]).wait()

    def dst(r):
        return xs_hbm.at[pl.ds(pl.multiple_of(pos_ref[0, r] * TILE_ROWS, TILE_ROWS), TILE_ROWS), :]

    @pl.when(i < nt)
    def _():
        base = pl.multiple_of(slot * span, span)
        stage[pl.ds(base, span), :] = hx_ref[...]
        for r in range(TOP_K * tm):
            src = stage.at[pl.ds(pl.multiple_of(base + (r % tm) * TILE_ROWS, TILE_ROWS), TILE_ROWS), :]
            pltpu.make_async_copy(src, dst(r), sem.at[slot]).start(priority=r % 2)

    @pl.when(i == 0)
    def _():
        zero_tile[...] = jnp.zeros_like(zero_tile)

    @pl.when(i >= nt)
    def _():
        for r in range(TOP_K * tm):
            pltpu.make_async_copy(zero_tile, dst(r), sem.at[slot]).start(priority=r % 2)

    @pl.when(i > 0)
    def _():
        wait_step(1 - slot)

    @pl.when(i == pl.num_programs(0) - 1)
    def _():
        wait_step(slot)


def _dispatch(pos3, pad_pos, hx_tiles, rows_total, tm):
    nt = pos3.shape[0]
    idx = jnp.concatenate([pos3, pad_pos], axis=0)
    steps = idx.shape[0]
    return pl.pallas_call(
        functools.partial(_dispatch_kernel, tm=tm, nt=nt),
        grid=(steps,),
        in_specs=[pl.BlockSpec((None, 1, TOP_K * tm), lambda i: (i, 0, 0), memory_space=pltpu.SMEM),
                  pl.BlockSpec((tm * TILE_ROWS, 128), lambda i: (jnp.minimum(i, nt - 1), 0))],
        out_specs=pl.BlockSpec(memory_space=pl.ANY),
        out_shape=jax.ShapeDtypeStruct((rows_total * TILE_ROWS, 128), F32),
        scratch_shapes=[pltpu.VMEM((2 * tm * TILE_ROWS, 128), F32), pltpu.VMEM((TILE_ROWS, 128), F32),
                        pltpu.SemaphoreType.DMA((2,))],
        compiler_params=_cparams(("arbitrary",)),
        name="moe_dispatch",
    )(idx, hx_tiles)


def _expert_kernel(blk_e_ref, xs_ref, wg_ref, wu_ref, wd_ref, y_ref, *, bm):
    del blk_e_ref
    x = _from_tiles(xs_ref, 0, bm).astype(BF16)
    hg = jnp.dot(x, wg_ref[...].astype(BF16), preferred_element_type=F32)
    hu = jnp.dot(x, wu_ref[...].astype(BF16), preferred_element_type=F32)
    act = hg * _sigmoid(hg) * hu
    _to_tiles(y_ref, jnp.dot(act.astype(BF16), wd_ref[...].astype(BF16), preferred_element_type=F32))


def _experts(xs_tiles, blk_e, wg, wu, wd, nb, bm):
    D = wg.shape[1]
    grid_spec = pltpu.PrefetchScalarGridSpec(
        num_scalar_prefetch=1,
        grid=(nb,),
        in_specs=[pl.BlockSpec((bm * TILE_ROWS, 128), lambda i, be: (i, 0)),
                  pl.BlockSpec((None, D, EXPERT_HIDDEN), lambda i, be: (be[i], 0, 0)),
                  pl.BlockSpec((None, D, EXPERT_HIDDEN), lambda i, be: (be[i], 0, 0)),
                  pl.BlockSpec((None, EXPERT_HIDDEN, D), lambda i, be: (be[i], 0, 0))],
        out_specs=pl.BlockSpec((bm * TILE_ROWS, 128), lambda i, be: (i, 0)),
    )
    return pl.pallas_call(
        functools.partial(_expert_kernel, bm=bm),
        grid_spec=grid_spec,
        out_shape=jax.ShapeDtypeStruct((nb * bm * TILE_ROWS, 128), F32),
        compiler_params=_cparams(("arbitrary",)),
        name="moe_experts",
    )(blk_e, xs_tiles, wg, wu, wd)


def _combine_kernel(cur_ref, nxt_ref, y_hbm, hx_ref, wt_ref, sgu_ref, sd_ref, x1_ref, g2_ref, fg_ref, o_ref, ybuf,
                    sem, *, tm):
    def consume(here):
        wt = wt_ref[...]
        routed = None
        for k in range(TOP_K):
            wk = jnp.broadcast_to(wt[:, k:k + 1], (tm, 128))
            wk = jnp.concatenate([wk] * TILE_ROWS, axis=1)
            term = wk * _from_tiles(ybuf, here + k * tm * TILE_ROWS, tm)
            routed = term if routed is None else routed + term
        hu = jnp.dot(_from_tiles(hx_ref, 0, tm).astype(BF16), sgu_ref[...], preferred_element_type=F32)
        hg = hu[:, :EXPERT_HIDDEN]
        act = hg * _sigmoid(hg) * hu[:, EXPERT_HIDDEN:]
        shared = jnp.dot(act.astype(BF16), sd_ref[...], preferred_element_type=F32)
        xo = x1_ref[...] + g2_ref[...] * (routed + shared)
        o_ref[...] = _rms(xo) * fg_ref[...]

    _pipelined_gather(cur_ref, nxt_ref, y_hbm, ybuf, sem, TOP_K * tm, consume)


def _combine(pos3, y_tiles, hx_tiles, w_tok, sgu, sd, x1, g2, fg, tm, tiles_per_batch):
    N, D = x1.shape
    nt = N // tm
    smem_blk = lambda f: pl.BlockSpec((None, 1, TOP_K * tm), f, memory_space=pltpu.SMEM)
    row = lambda i: (i, 0)
    const = lambda i: (0, 0)
    return pl.pallas_call(
        functools.partial(_combine_kernel, tm=tm),
        grid=(nt,),
        in_specs=[smem_blk(lambda i: (i, 0, 0)),
                  smem_blk(lambda i: (jnp.minimum(i + 1, nt - 1), 0, 0)),
                  pl.BlockSpec(memory_space=pl.ANY),
                  pl.BlockSpec((tm * TILE_ROWS, 128), row),
                  pl.BlockSpec((tm, 128), row),
                  pl.BlockSpec((D, 2 * EXPERT_HIDDEN), const),
                  pl.BlockSpec((EXPERT_HIDDEN, D), const),
                  pl.BlockSpec((tm, D), row),
                  pl.BlockSpec((None, 1, D), lambda i: (i // tiles_per_batch, 0, 0)),
                  pl.BlockSpec((1, D), const)],
        out_specs=pl.BlockSpec((tm, D), row),
        out_shape=jax.ShapeDtypeStruct((N, D), F32),
        scratch_shapes=[pltpu.VMEM((2 * TOP_K * tm * TILE_ROWS, 128), F32), pltpu.SemaphoreType.DMA((2,))],
        compiler_params=_cparams(("arbitrary",)),
        name="moe_combine",
    )(pos3, pos3, y_tiles, hx_tiles, w_tok, sgu, sd, x1, g2, fg)


def _rope_tables(L):
    rows = L // GRID_W
    row = jnp.repeat(jnp.arange(rows), GRID_W).astype(F32)
    col = jnp.tile(jnp.arange(GRID_W), rows).astype(F32)
    inv = ROPE_THETA ** (-jnp.arange(0, ROPE_AXIS_DIM, 2, dtype=F32) / ROPE_AXIS_DIM)
    ang = jnp.stack([row[:, None] * inv, col[:, None] * inv], axis=1)
    c, s = jnp.cos(ang), jnp.sin(ang)
    cos_h = jnp.stack([c, c], axis=2).reshape(L, HEAD_DIM)
    sin_h = jnp.stack([-s, s], axis=2).reshape(L, HEAD_DIM)
    return jnp.tile(cos_h, (1, 2)), jnp.tile(sin_h, (1, 2))


def _tile(n, pref):
    t = min(pref, n)
    while n % t:
        t //= 2
    return t


def kernel(x, c, ctx, c_ctx, ada_w, ada_b, norm1_g, norm2_g, w_in, lam_q1, lam_k1, lam_q2, lam_k2, subln_g, hy_conv_w, hy_conv_b, filt_w1, filt_b1, filt_freq, filt_w2, filt_b2, filt_w3, hy_bias, w_branch_attn, w_branch_hyena, w_out, router_w, router_bias, exp_w_gate, exp_w_up, exp_w_down, shared_w_gate, shared_w_up, shared_w_down, final_norm_g):
    B, L, D = x.shape
    assert ada_w.shape[0] == 1 and B % 2 == 0 and (2 * L) % (FFT_SLABS * FFT_STRIP) == 0
    lam = (jnp.exp(jnp.sum(lam_q1[0].astype(F32) * lam_k1[0].astype(F32)))
           - jnp.exp(jnp.sum(lam_q2[0].astype(F32) * lam_k2[0].astype(F32))) + LAM_INIT).reshape(1, 1)

    cc = jnp.zeros((16, D), F32).at[:B].set(c).at[B].set(c_ctx)
    mod = _adaln(cc, ada_w[0], ada_b[0])
    chunk = lambda r0, r1, j: mod[r0:r1, j * D:(j + 1) * D]
    sh1, sc1, g1, sh2, sc2, g2 = (chunk(0, B, j).reshape(B, 1, D) for j in range(6))
    csh, csc = chunk(B, B + 1, 0), chunk(B, B + 1, 1)

    w_bf = w_in[0].astype(BF16)
    n1g = norm1_g[0].reshape(1, D)
    cos_t, sin_t = _rope_tables(L)
    k, v, q, u, gates = _inproj(x, sh1, sc1, n1g, w_bf, cos_t, sin_t, hy_conv_w[0], hy_conv_b[0].reshape(1, -1),
                                _tile(L, 512))
    kc, vc = _ctxproj(ctx, csh, csc, n1g, w_bf)
    attn = _attention(lam, q, k, v, kc, vc, subln_g[0].reshape(1, V_DIM), _tile(L, 1024), _tile(L, 512))

    filt = (filt_w1[0], filt_b1[0], filt_freq[0], filt_w2[0], filt_b2[0], filt_w3[0])
    hy = _hyena(u, filt, hy_bias[0])

    x1, hx2, logits_t = _merge(attn, hy, gates, x, g1, sh2, sc2, norm2_g[0].reshape(1, D),
                               w_branch_attn[0].astype(BF16), w_branch_hyena[0].astype(BF16),
                               w_out[0].astype(BF16), router_w[0].T.astype(F32), _tile(L, 512))
    eidx, w_tok, rank, counts = _router(logits_t, router_bias[0], _tile(B * L, 1024))

    N = B * L
    tm = _tile(L, MOE_TOKENS)
    pos3, pad_pos, blk_e, nb = _dispatch_plan(eidx, rank, counts, MOE_ROWS, tm)
    xs = _dispatch(pos3, pad_pos, hx2, nb * MOE_ROWS, tm)
    y = _experts(xs, blk_e, exp_w_gate[0], exp_w_up[0], exp_w_down[0], nb, MOE_ROWS)
    sgu = jnp.concatenate([shared_w_gate[0], shared_w_up[0]], axis=-1).astype(BF16)
    out = _combine(pos3, y, hx2, w_tok, sgu, shared_w_down[0].astype(BF16), x1.reshape(N, D), g2,
                   final_norm_g.reshape(1, D), tm, L // tm)
    return out.reshape(B, L, D)
```

```python
import functools
import math

import jax
import jax.numpy as jnp
from jax import lax
from jax.experimental import pallas as pl
from jax.experimental.pallas import tpu as pltpu

F32 = jnp.float32
BF16 = jnp.bfloat16
U32 = jnp.uint32
HIGHEST = lax.Precision.HIGHEST

NORM_EPS = 1e-6
N_HEADS = 8
HEAD_DIM = 64
V_DIM = 2 * HEAD_DIM
GRID_W = 64
ROPE_THETA = 10000.0
ROPE_AXIS_DIM = HEAD_DIM // 2
HYENA_W = 1024
FILTER_EMB = 33
FILTER_BANDS = (FILTER_EMB - 1) // 2
DECAY_TARGET = 1e-2
FAST_DECAY_PCT = 0.3
SLOW_DECAY_PCT = 1.5
N_EXPERTS = 64
N_GROUPS = 8
TOPK_GROUPS = 4
TOP_K = 8
EXPERT_HIDDEN = 256
ROUTED_SCALE = 2.5
LAM_INIT = 0.8 - 0.6 * math.exp(-0.3 * 0)

V7X_VMEM_LIMIT = 56 * 1024 * 1024
LOG2E = 1.4426950408889634


def _cparams(sem):
    return pltpu.CompilerParams(dimension_semantics=sem, vmem_limit_bytes=V7X_VMEM_LIMIT)


def _sigmoid(v):
    return 1.0 / (1.0 + jnp.exp(-v))


def _rms(v):
    return v * lax.rsqrt(jnp.mean(v * v, axis=-1, keepdims=True) + NORM_EPS)


def _adaln_kernel(c_ref, w_ref, b_ref, o_ref):
    c = c_ref[...]
    s = c * _sigmoid(c)
    o_ref[...] = jnp.dot(s, w_ref[...], precision=HIGHEST, preferred_element_type=F32) + b_ref[...]


def _adaln(cc, w, b):
    rows, d = cc.shape
    n = w.shape[1]
    tn = 1536
    return pl.pallas_call(
        _adaln_kernel,
        grid=(n // tn,),
        in_specs=[pl.BlockSpec((rows, d), lambda j: (0, 0)),
                  pl.BlockSpec((d, tn), lambda j: (0, j)),
                  pl.BlockSpec((1, tn), lambda j: (0, j))],
        out_specs=pl.BlockSpec((rows, tn), lambda j: (0, j)),
        out_shape=jax.ShapeDtypeStruct((rows, n), F32),
        compiler_params=_cparams(("arbitrary",)),
        name="adaln",
    )(cc, w, b.reshape(1, n))


def _inproj_kernel(x_ref, xp_ref, xn_ref, sh_ref, sc_ref, g_ref, w_ref, cos_ref, sin_ref, cw_ref, cb_ref,
                   k_ref, v_ref, q_ref, u_ref, gt_ref, pscr, *, tm, nt, qscale):
    i = pl.program_id(1)
    g = g_ref[...]
    sh = sh_ref[...]
    sc = 1.0 + sc_ref[...]

    def norm_mod(xx):
        return (_rms(xx) * g) * sc + sh

    h = norm_mod(x_ref[...])
    hp = jnp.where(i > 0, norm_mod(xp_ref[...]), 0.0)
    hn = jnp.where(i < nt - 1, norm_mod(xn_ref[...]), 0.0)
    hb = h.astype(BF16)
    h_ext = jnp.concatenate([hp, h, hn], axis=0).astype(BF16)

    def proj(lhs, c0):
        return jnp.dot(lhs, w_ref[:, c0:c0 + 1024], preferred_element_type=F32)

    cosv = cos_ref[...]
    sinv = sin_ref[...]
    lane = lax.broadcasted_iota(jnp.int32, (1, V_DIM), 1)
    first = (lane % ROPE_AXIS_DIM) < (ROPE_AXIS_DIM // 2)

    def rope_store(a, o_ref, scale):
        for hh in range(N_HEADS):
            s = a[:, hh * V_DIM:(hh + 1) * V_DIM]
            partner = jnp.where(first, pltpu.roll(s, V_DIM - ROPE_AXIS_DIM // 2, 1),
                                pltpu.roll(s, ROPE_AXIS_DIM // 2, 1))
            r = s * cosv + partner * sinv
            if scale != 1.0:
                r = r * scale
            o_ref[:, hh * V_DIM:(hh + 1) * V_DIM] = r.astype(BF16)

    rope_store(proj(hb, 0), k_ref, 1.0)
    v_ref[...] = proj(hb, 1024).astype(BF16)
    rope_store(proj(hb, 2048), q_ref, qscale)
    for j in range(3):
        pscr[...] = proj(h_ext, 3072 + j * 1024)
        cw = cw_ref[:, j * 1024:(j + 1) * 1024]
        cb = cb_ref[:, j * 1024:(j + 1) * 1024]
        u = (pscr[pl.ds(7, tm), :] * cw[0:1] + pscr[pl.ds(8, tm), :] * cw[1:2]
             + pscr[pl.ds(9, tm), :] * cw[2:3] + cb)
        u_ref[:, j * 1024:(j + 1) * 1024] = u.astype(BF16)
    for j in range(2):
        gt_ref[:, j * 1024:(j + 1) * 1024] = proj(hb, 6144 + j * 1024).astype(BF16)


def _inproj(x, shift, scale, g, w_bf, cos_t, sin_t, conv_w, conv_b, tm):
    B, L, D = x.shape
    nt = L // tm
    nb8 = L // 8
    qscale = (HEAD_DIM ** -0.5) * LOG2E
    kern = functools.partial(_inproj_kernel, tm=tm, nt=nt, qscale=qscale)
    row = lambda b, i: (b, i, 0)
    per_b = lambda b, i: (b, 0, 0)
    const = lambda b, i: (0, 0)
    outs = pl.pallas_call(
        kern,
        grid=(B, nt),
        in_specs=[
            pl.BlockSpec((None, tm, D), row),
            pl.BlockSpec((None, 8, D), lambda b, i: (b, jnp.maximum(i * (tm // 8) - 1, 0), 0)),
            pl.BlockSpec((None, 8, D), lambda b, i: (b, jnp.minimum((i + 1) * (tm // 8), nb8 - 1), 0)),
            pl.BlockSpec((None, 1, D), per_b),
            pl.BlockSpec((None, 1, D), per_b),
            pl.BlockSpec((1, D), const),
            pl.BlockSpec((D, 8192), const, pipeline_mode=pl.Buffered(1)),
            pl.BlockSpec((tm, V_DIM), lambda b, i: (i, 0)),
            pl.BlockSpec((tm, V_DIM), lambda b, i: (i, 0)),
            pl.BlockSpec((3, 3 * HYENA_W), const),
            pl.BlockSpec((1, 3 * HYENA_W), const),
        ],
        out_specs=[
            pl.BlockSpec((None, tm, 1024), row),
            pl.BlockSpec((None, tm, 1024), row),
            pl.BlockSpec((None, tm, 1024), row),
            pl.BlockSpec((None, tm, 3 * HYENA_W), row),
            pl.BlockSpec((None, tm, 2048), row),
        ],
        out_shape=[
            jax.ShapeDtypeStruct((B, L, 1024), BF16),
            jax.ShapeDtypeStruct((B, L, 1024), BF16),
            jax.ShapeDtypeStruct((B, L, 1024), BF16),
            jax.ShapeDtypeStruct((B, L, 3 * HYENA_W), BF16),
            jax.ShapeDtypeStruct((B, L, 2048), BF16),
        ],
        scratch_shapes=[pltpu.VMEM((tm + 16, 1024), F32)],
        compiler_params=_cparams(("parallel", "arbitrary")),
        name="inproj",
    )(x, x, x, shift, scale, g, w_bf, cos_t, sin_t, conv_w, conv_b)
    return outs


def _ctxproj_kernel(x_ref, sh_ref, sc_ref, g_ref, w_ref, k_ref, v_ref):
    h = ((_rms(x_ref[...]) * g_ref[...]) * (1.0 + sc_ref[...]) + sh_ref[...]).astype(BF16)
    k_ref[...] = jnp.dot(h, w_ref[:, 0:1024], preferred_element_type=F32).astype(BF16)
    v_ref[...] = jnp.dot(h, w_ref[:, 1024:2048], preferred_element_type=F32).astype(BF16)


def _ctxproj(ctx, shift, scale, g, w_bf):
    B, Lc, D = ctx.shape
    return pl.pallas_call(
        _ctxproj_kernel,
        grid=(B,),
        in_specs=[pl.BlockSpec((None, Lc, D), lambda b: (b, 0, 0)),
                  pl.BlockSpec((1, D), lambda b: (0, 0)),
                  pl.BlockSpec((1, D), lambda b: (0, 0)),
                  pl.BlockSpec((1, D), lambda b: (0, 0)),
                  pl.BlockSpec((D, 2048), lambda b: (0, 0))],
        out_specs=[pl.BlockSpec((None, Lc, 1024), lambda b: (b, 0, 0)),
                   pl.BlockSpec((None, Lc, 1024), lambda b: (b, 0, 0))],
        out_shape=[jax.ShapeDtypeStruct((B, Lc, 1024), BF16),
                   jax.ShapeDtypeStruct((B, Lc, 1024), BF16)],
        compiler_params=_cparams(("arbitrary",)),
        name="ctxproj",
    )(ctx, shift, scale, g, w_bf)


def _attn_kernel(lam_ref, q_ref, k_ref, v_ref, kc_ref, vc_ref, g_ref, o_ref, m_s, acc_s, *, tk, nk):
    q = q_ref[...]
    lane = lax.broadcasted_iota(jnp.int32, (1, V_DIM), 1)
    zero = jnp.zeros_like(q)
    qs = (jnp.where(lane < HEAD_DIM, q, zero), jnp.where(lane >= HEAD_DIM, q, zero))
    m_s[...] = jnp.full(m_s.shape, -1e30, F32)
    acc_s[...] = jnp.zeros(acc_s.shape, F32)

    def chunk(kk, vv):
        reps = kk.shape[0] // V_DIM
        v1 = jnp.concatenate([vv, jnp.ones_like(vv)], axis=1)
        for m in range(2):
            s = lax.dot_general(qs[m], kk, (((1,), (1,)), ((), ())), preferred_element_type=F32)
            m_prev = m_s[m]
            m_new = jnp.maximum(m_prev, jnp.max(s, axis=-1, keepdims=True))
            alpha = jnp.exp2(m_prev - m_new)
            p = jnp.exp2(s - jnp.concatenate([m_new] * reps, axis=1))
            acc_s[m] = (jnp.concatenate([alpha, alpha], axis=1) * acc_s[m]
                        + jnp.dot(p.astype(BF16), v1, preferred_element_type=F32))
            m_s[m] = m_new

    def body(j, carry):
        off = pl.multiple_of(j * tk, tk)
        chunk(k_ref[pl.ds(off, tk), :], v_ref[pl.ds(off, tk), :])
        return carry

    lax.fori_loop(0, nk, body, 0, unroll=2 if nk % 2 == 0 else 1)
    chunk(kc_ref[...], vc_ref[...])
    lam = lam_ref[0, 0]
    a0, a1 = acc_s[0], acc_s[1]
    o = a0[:, :V_DIM] / a0[:, V_DIM:] - lam * (a1[:, :V_DIM] / a1[:, V_DIM:])
    o = _rms(o) * g_ref[...] * (1.0 - LAM_INIT)
    o_ref[...] = o.astype(BF16)


def _attention(lam, q, k, v, kc, vc, subln_g, tq, tk):
    B, L, _ = q.shape
    Lc = kc.shape[1]
    kern = functools.partial(_attn_kernel, tk=tk, nk=L // tk)
    kv = lambda b, h, i: (b, 0, h)
    return pl.pallas_call(
        kern,
        grid=(B, N_HEADS, L // tq),
        in_specs=[
            pl.BlockSpec(memory_space=pltpu.SMEM),
            pl.BlockSpec((None, tq, V_DIM), lambda b, h, i: (b, i, h)),
            pl.BlockSpec((None, L, V_DIM), kv),
            pl.BlockSpec((None, L, V_DIM), kv),
            pl.BlockSpec((None, Lc, V_DIM), kv),
            pl.BlockSpec((None, Lc, V_DIM), kv),
            pl.BlockSpec((1, V_DIM), lambda b, h, i: (0, 0)),
        ],
        out_specs=pl.BlockSpec((None, tq, V_DIM), lambda b, h, i: (b, i, h)),
        out_shape=jax.ShapeDtypeStruct((B, L, N_HEADS * V_DIM), BF16),
        scratch_shapes=[pltpu.VMEM((2, tq, V_DIM), F32), pltpu.VMEM((2, tq, 2 * V_DIM), F32)],
        compiler_params=_cparams(("parallel", "parallel", "arbitrary")),
        name="diffattn",
    )(lam, q, k, v, kc, vc, subln_g)


FFT_SLABS = 64
FFT_STRIP = 16


def _dft_tables(nb):
    N = FFT_SLABS * nb
    fa = jnp.arange(FFT_SLABS, dtype=jnp.int32)[:, None, None]
    fb = jnp.arange(nb, dtype=jnp.int32)[None, :, None]
    b = jnp.arange(nb, dtype=jnp.int32)[None, None, :]
    ang = ((FFT_SLABS * b * fb + b * fa) % N).astype(F32) * (2.0 * math.pi / N)
    c, s = jnp.cos(ang), jnp.sin(ang)
    fwd = jnp.concatenate([jnp.concatenate([c, s], -1), jnp.concatenate([-s, c], -1)], axis=1)
    inv = jnp.swapaxes(fwd, 1, 2) / N
    return fwd.astype(BF16), inv.astype(BF16)


def _vadd(a, b):
    return b if a is None else (a if b is None else a + b)


def _vsub(a, b):
    if b is None:
        return a
    return -b if a is None else a - b


def _vscale(a, k):
    if a is None or k == 1.0:
        return a
    return -a if k == -1.0 else a * k


def _cmulc(x, c, s):
    re, im = x
    eps = 1e-9
    if abs(s) < eps:
        k = 1.0 if c > 0 else -1.0
        return _vscale(re, k), _vscale(im, k)
    if abs(c) < eps:
        k = 1.0 if s > 0 else -1.0
        return _vscale(im, -k), _vscale(re, k)
    if abs(abs(c) - abs(s)) < eps:
        kc = 1.0 if c > 0 else -1.0
        ks = 1.0 if s > 0 else -1.0
        return (_vscale(_vsub(_vscale(re, kc), _vscale(im, ks)), abs(c)),
                _vscale(_vadd(_vscale(re, ks), _vscale(im, kc)), abs(c)))
    return _vsub(_vscale(re, c), _vscale(im, s)), _vadd(_vscale(re, s), _vscale(im, c))


def _fft(xs, sign):
    n = len(xs)
    if n == 1:
        return xs
    ev, od = _fft(xs[0::2], sign), _fft(xs[1::2], sign)
    out = [None] * n
    for k in range(n // 2):
        ang = sign * 2.0 * math.pi * k / n
        t = _cmulc(od[k], math.cos(ang), math.sin(ang))
        out[k] = (_vadd(ev[k][0], t[0]), _vadd(ev[k][1], t[1]))
        out[k + n // 2] = (_vsub(ev[k][0], t[0]), _vsub(ev[k][1], t[1]))
    return out


def _put(ref, slot, rows, val):
    ref[slot, rows, :] = jnp.zeros((FFT_STRIP, ref.shape[2]), F32) if val is None else val


def _slab_fft_forward(load, s_ref, nb, n_in):
    def strip(i, carry):
        r0 = pl.multiple_of(i * FFT_STRIP, FFT_STRIP)
        rows_re, rows_im = pl.ds(r0, FFT_STRIP), pl.ds(nb + r0, FFT_STRIP)
        for a0 in range(8):
            xs = [load(8 * a1 + a0, r0) if a1 < n_in else (None, None) for a1 in range(8)]
            ys = _fft(xs, -1.0)
            for f1 in range(8):
                ang = -2.0 * math.pi * a0 * f1 / FFT_SLABS
                yr, yi = _cmulc(ys[f1], math.cos(ang), math.sin(ang))
                _put(s_ref, a0 * 8 + f1, rows_re, yr)
                _put(s_ref, a0 * 8 + f1, rows_im, yi)
        for f1 in range(8):
            xs = [(s_ref[a0 * 8 + f1, rows_re, :], s_ref[a0 * 8 + f1, rows_im, :]) for a0 in range(8)]
            ys = _fft(xs, -1.0)
            for f0 in range(8):
                s_ref[f1 + 8 * f0, rows_re, :] = ys[f0][0]
                s_ref[f1 + 8 * f0, rows_im, :] = ys[f0][1]
        return carry

    lax.fori_loop(0, nb // FFT_STRIP, strip, 0)


def _slab_fft_inverse(s_ref, nb, emit):
    def strip(i, carry):
        r0 = pl.multiple_of(i * FFT_STRIP, FFT_STRIP)
        rows_re, rows_im = pl.ds(r0, FFT_STRIP), pl.ds(nb + r0, FFT_STRIP)
        for q in range(8):
            xs = [(s_ref[8 * p + q, rows_re, :], s_ref[8 * p + q, rows_im, :]) for p in range(8)]
            us = _fft(xs, 1.0)
            for r in range(8):
                ang = 2.0 * math.pi * r * q / FFT_SLABS
                ur, ui = _cmulc(us[r], math.cos(ang), math.sin(ang))
                s_ref[8 * r + q, rows_re, :] = ur
                s_ref[8 * r + q, rows_im, :] = ui
        for r in range(8):
            xs = [(s_ref[8 * r + q, rows_re, :], s_ref[8 * r + q, rows_im, :]) for q in range(8)]
            ys = _fft(xs, 1.0)
            for sidx in range(4):
                emit(r + 8 * sidx, r0, ys[sidx][0], ys[sidx][1])
        return carry

    lax.fori_loop(0, nb // FFT_STRIP, strip, 0)


def _hyena_conv_kernel(v_ref, x_ref, k_ref, ff_ref, fi_ref, bias_ref, o_ref, s_ref, *, nb):
    def load(a, r0):
        rows = pl.ds(pl.multiple_of(a * nb + r0, FFT_STRIP), FFT_STRIP)
        return v_ref[0, rows, :].astype(F32), v_ref[1, rows, :].astype(F32)

    _slab_fft_forward(load, s_ref, nb, FFT_SLABS // 16)

    def freq(f, carry):
        z = jnp.dot(ff_ref[f], s_ref[f].astype(BF16), preferred_element_type=F32)
        zr, zi = z[:nb], z[nb:]
        kr, ki = k_ref[0, f], k_ref[1, f]
        y = jnp.concatenate([zr * kr - zi * ki, zr * ki + zi * kr], axis=0).astype(BF16)
        s_ref[f] = jnp.dot(fi_ref[f], y, preferred_element_type=F32)
        return carry

    lax.fori_loop(0, FFT_SLABS, freq, 0, unroll=8)
    bias = bias_ref[...]

    def emit(a, r0, re, im):
        rows = pl.ds(pl.multiple_of(a * nb + r0, FFT_STRIP), FFT_STRIP)
        for half, y in enumerate((re, im)):
            v = v_ref[half, rows, :].astype(F32)
            o_ref[half, rows, :] = (x_ref[half, rows, :].astype(F32) * (y + v * bias)).astype(BF16)

    _slab_fft_inverse(s_ref, nb, emit)


def _hyena_conv(uv, zv, k_hat, ff, fi, bias, order, in_col, gate_col, nb):
    _, P, L, _ = uv.shape
    W = HYENA_W
    nct = W // 128
    src, src_off = (uv, in_col * nct) if zv is None else (zv, 0)
    blk = (2, None, L, 128)
    return pl.pallas_call(
        functools.partial(_hyena_conv_kernel, nb=nb),
        grid=(nct, P),
        in_specs=[pl.BlockSpec(blk, lambda c, p: (0, p, 0, src_off + c)),
                  pl.BlockSpec(blk, lambda c, p: (0, p, 0, gate_col * nct + c)),
                  pl.BlockSpec((2, FFT_SLABS, nb, 128), lambda c, p: (0, 0, 0, order * nct + c),
                               pipeline_mode=pl.Buffered(1)),
                  pl.BlockSpec((FFT_SLABS, 2 * nb, 2 * nb), lambda c, p: (0, 0, 0), pipeline_mode=pl.Buffered(1)),
                  pl.BlockSpec((FFT_SLABS, 2 * nb, 2 * nb), lambda c, p: (0, 0, 0), pipeline_mode=pl.Buffered(1)),
                  pl.BlockSpec((1, 128), lambda c, p: (0, c))],
        out_specs=pl.BlockSpec(blk, lambda c, p: (0, p, 0, c)),
        out_shape=jax.ShapeDtypeStruct((2, P, L, W), BF16),
        scratch_shapes=[pltpu.VMEM((FFT_SLABS, 2 * nb, 128), F32)],
        compiler_params=_cparams(("parallel", "arbitrary")),
        name="hyena_conv%d" % order,
    )(src, uv, k_hat, ff, fi, bias)


def _filter_features(n):
    pos = jnp.arange(n, dtype=F32)[:, None]
    t = jnp.linspace(0.0, 1.0, n, dtype=F32)[:, None]
    w = 2 * math.pi * pos / n
    bands = jnp.linspace(1e-4, FILTER_BANDS - 1, FILTER_BANDS, dtype=F32)
    z = jnp.concatenate([t, jnp.cos(bands * w), -jnp.sin(bands * w)], axis=-1)
    z = jnp.pad(z, ((0, 0), (0, 128 - FILTER_EMB)))
    tap = jnp.arange(2 * n)
    return z[jnp.where(tap < n, tap, (2 * n - tap) % n)]


def _filt_taps_kernel(z_ref, w1_ref, b1_ref, fr_ref, w2_ref, b2_ref, w3_ref, dl_ref, k_ref, sum_ref):
    a = pl.program_id(0)
    z = z_ref[...]
    fr = fr_ref[...]
    hd = jnp.sin(fr * (jnp.dot(z, w1_ref[...], precision=HIGHEST, preferred_element_type=F32) + b1_ref[...]))
    hd = jnp.sin(fr * (jnp.dot(hd, w2_ref[...], precision=HIGHEST, preferred_element_type=F32) + b2_ref[...]))
    h = jnp.dot(hd, w3_ref[...], precision=HIGHEST, preferred_element_type=F32) * jnp.exp(-z[:, 0:1] * dl_ref[...])
    rows = lax.broadcasted_iota(jnp.int32, (z.shape[0], 1), 0)
    h = jnp.where(jnp.logical_and(a == FFT_SLABS // 2, rows == 0), 0.0, h)
    k_ref[...] = h

    @pl.when(a == 0)
    def _():
        sum_ref[...] = jnp.zeros_like(sum_ref)

    sum_ref[0:1, :] += jnp.sum(jnp.abs(h), axis=0, keepdims=True)


def _filt_spectrum_kernel(t_ref, inv_ref, ff_ref, k_ref, s_ref, *, nb):
    def load(a, r0):
        return t_ref[pl.ds(pl.multiple_of(a * nb + r0, FFT_STRIP), FFT_STRIP), :], None

    _slab_fft_forward(load, s_ref, nb, FFT_SLABS // 8)
    inv = inv_ref[...]

    def freq(f, carry):
        z = jnp.dot(ff_ref[f], s_ref[f].astype(BF16), preferred_element_type=F32)
        k_ref[0, f] = z[:nb] * inv
        k_ref[1, f] = z[nb:] * inv
        return carry

    lax.fori_loop(0, FFT_SLABS, freq, 0, unroll=8)


def _hyena_filters(n, filt, ff, nb):
    w1, b1, freq, w2, b2, w3 = filt
    W = HYENA_W
    hid = w1.shape[1]
    z = _filter_features(n)
    w1p = jnp.pad(w1.astype(F32), ((0, 128 - FILTER_EMB), (0, 0)))
    w3d = w3.astype(F32).reshape(hid, 2, 2, W)
    w3sel = jnp.stack([w3d[:, :, d, :].reshape(hid, 2 * W) for d in range(2)])
    deltas = jnp.abs(jnp.linspace(math.log(DECAY_TARGET) / SLOW_DECAY_PCT,
                                  math.log(DECAY_TARGET) / FAST_DECAY_PCT, W, dtype=F32))
    deltas = jnp.tile(deltas.reshape(1, W), (1, 2))
    const = lambda a: (0, 0)
    taps, sums = pl.pallas_call(
        _filt_taps_kernel,
        grid=(FFT_SLABS,),
        in_specs=[pl.BlockSpec((nb, 128), lambda a: (a, 0)),
                  pl.BlockSpec((128, hid), const), pl.BlockSpec((1, hid), const), pl.BlockSpec((1, hid), const),
                  pl.BlockSpec((hid, hid), const), pl.BlockSpec((1, hid), const),
                  pl.BlockSpec((None, hid, 2 * W), lambda a: (a // (FFT_SLABS // 2), 0, 0)),
                  pl.BlockSpec((1, 2 * W), const)],
        out_specs=[pl.BlockSpec((nb, 2 * W), lambda a: (a, 0)), pl.BlockSpec((8, 2 * W), const)],
        out_shape=[jax.ShapeDtypeStruct((2 * n, 2 * W), F32), jax.ShapeDtypeStruct((8, 2 * W), F32)],
        compiler_params=_cparams(("arbitrary",)),
        name="hyena_filter_taps",
    )(z, w1p, b1.reshape(1, hid), freq.reshape(1, hid), w2, b2.reshape(1, hid), w3sel, deltas)
    inv = 1.0 / sums[0:1]
    return pl.pallas_call(
        functools.partial(_filt_spectrum_kernel, nb=nb),
        grid=(2 * W // 128,),
        in_specs=[pl.BlockSpec((2 * n, 128), lambda j: (0, j)),
                  pl.BlockSpec((1, 128), lambda j: (0, j)),
                  pl.BlockSpec((FFT_SLABS, 2 * nb, 2 * nb), lambda j: (0, 0, 0), pipeline_mode=pl.Buffered(1))],
        out_specs=pl.BlockSpec((2, FFT_SLABS, nb, 128), lambda j: (0, 0, 0, j)),
        out_shape=jax.ShapeDtypeStruct((2, FFT_SLABS, nb, 2 * W), F32),
        scratch_shapes=[pltpu.VMEM((FFT_SLABS, 2 * nb, 128), F32)],
        compiler_params=_cparams(("parallel",)),
        name="hyena_filter_spectrum",
    )(taps, inv, ff)


def _hyena(u, filt, hy_bias):
    B, L, _ = u.shape
    P = B // 2
    nb = 2 * L // FFT_SLABS
    ff, fi = _dft_tables(nb)
    k_hat = _hyena_filters(L, filt, ff, nb)
    uv = u.reshape(2, P, L, 3 * HYENA_W)
    bias = hy_bias.astype(F32)
    z = _hyena_conv(uv, None, k_hat, ff, fi, bias[0:1], 0, 0, 1, nb)
    y = _hyena_conv(uv, z, k_hat, ff, fi, bias[1:2], 1, 0, 2, nb)
    return y.reshape(B, L, HYENA_W)


def _merge_kernel(a_ref, h_ref, gt_ref, x_ref, g1_ref, sh_ref, sc_ref, n2g_ref, wpa_ref, wph_ref, wo_ref, rw_ref,
                  x1_ref, hx_ref, lg_ref):
    ya = jnp.dot(a_ref[...], wpa_ref[...], preferred_element_type=F32)
    yh = jnp.dot(h_ref[...], wph_ref[...], preferred_element_type=F32)
    ga = gt_ref[:, 0:1024].astype(F32)
    gh = gt_ref[:, 1024:2048].astype(F32)
    y = _sigmoid(ga) * ya + _sigmoid(gh) * yh
    mix = jnp.dot(y.astype(BF16), wo_ref[...], preferred_element_type=F32)
    x1 = x_ref[...] + g1_ref[...] * mix
    x1_ref[...] = x1
    hx = (_rms(x1) * n2g_ref[...]) * (1.0 + sc_ref[...]) + sh_ref[...]
    _to_tiles(hx_ref, hx)
    lg_ref[...] = lax.dot_general(rw_ref[...], hx, (((1,), (1,)), ((), ())), precision=HIGHEST,
                                  preferred_element_type=F32)


def _merge(attn, hy, gates, x, g1, sh2, sc2, n2g, wpa, wph, wo, rwT, tm):
    B, L, D = x.shape
    nt = L // tm
    row = lambda b, i: (b, i, 0)
    per_b = lambda b, i: (b, 0, 0)
    const = lambda b, i: (0, 0)
    return pl.pallas_call(
        _merge_kernel,
        grid=(B, nt),
        in_specs=[pl.BlockSpec((None, tm, 1024), row), pl.BlockSpec((None, tm, 1024), row),
                  pl.BlockSpec((None, tm, 2048), row), pl.BlockSpec((None, tm, D), row),
                  pl.BlockSpec((None, 1, D), per_b), pl.BlockSpec((None, 1, D), per_b),
                  pl.BlockSpec((None, 1, D), per_b), pl.BlockSpec((1, D), const),
                  pl.BlockSpec((1024, D), const), pl.BlockSpec((1024, D), const), pl.BlockSpec((D, D), const),
                  pl.BlockSpec((N_EXPERTS, D), const)],
        out_specs=[pl.BlockSpec((None, tm, D), row),
                   pl.BlockSpec((tm * TILE_ROWS, 128), lambda b, i: (b * nt + i, 0)),
                   pl.BlockSpec((N_EXPERTS, tm), lambda b, i: (0, b * nt + i))],
        out_shape=[jax.ShapeDtypeStruct((B, L, D), F32),
                   jax.ShapeDtypeStruct((B * L * TILE_ROWS, 128), U32),
                   jax.ShapeDtypeStruct((N_EXPERTS, B * L), F32)],
        compiler_params=_cparams(("parallel", "arbitrary")),
        name="merge",
    )(attn, hy, gates, x, g1, sh2, sc2, n2g, wpa, wph, wo, rwT)


def _router_kernel(lg_ref, rb_ref, tri_ref, ei_ref, wt_ref, rk_ref, cnt_ref):
    scores = _sigmoid(lg_ref[...])
    choice = scores + rb_ref[...]
    E, T = scores.shape
    gsz = E // N_GROUPS
    neg = jnp.float32(-jnp.inf)
    big = jnp.int32(1 << 30)
    rows8 = lax.broadcasted_iota(jnp.int32, (gsz, T), 0)
    gscore = []
    for gi in range(N_GROUPS):
        c = choice[gi * gsz:(gi + 1) * gsz]
        m1 = jnp.max(c, axis=0, keepdims=True)
        i1 = jnp.min(jnp.where(c == m1, rows8, big), axis=0, keepdims=True)
        m2 = jnp.max(jnp.where(rows8 == i1, neg, c), axis=0, keepdims=True)
        gscore.append(m1 + m2)
    gs = jnp.concatenate(gscore, axis=0)
    rowsg = lax.broadcasted_iota(jnp.int32, (N_GROUPS, T), 0)
    gsel = jnp.zeros((N_GROUPS, T), jnp.bool_)
    for _ in range(TOPK_GROUPS):
        m = jnp.max(gs, axis=0, keepdims=True)
        ix = jnp.min(jnp.where(gs == m, rowsg, big), axis=0, keepdims=True)
        hit = rowsg == ix
        gsel = jnp.logical_or(gsel, hit)
        gs = jnp.where(hit, neg, gs)
    masked = jnp.concatenate(
        [jnp.where(gsel[gi:gi + 1], choice[gi * gsz:(gi + 1) * gsz], neg) for gi in range(N_GROUPS)], axis=0)
    rows = lax.broadcasted_iota(jnp.int32, (E, T), 0)
    ids, ws, hits = [], [], []
    sel = jnp.zeros((E, T), F32)
    for _ in range(TOP_K):
        m = jnp.max(masked, axis=0, keepdims=True)
        ix = jnp.min(jnp.where(masked == m, rows, big), axis=0, keepdims=True)
        hit = rows == ix
        ids.append(ix)
        hits.append(hit)
        ws.append(jnp.sum(jnp.where(hit, scores, 0.0), axis=0, keepdims=True))
        sel = jnp.where(hit, 1.0, sel)
        masked = jnp.where(hit, neg, masked)
    w = jnp.concatenate(ws, axis=0)
    w = w / jnp.sum(w, axis=0, keepdims=True) * ROUTED_SCALE
    ei_ref[...] = jnp.concatenate(ids, axis=0)
    wt_ref[...] = jnp.concatenate([w, jnp.zeros((128 - TOP_K, T), F32)], axis=0).T

    @pl.when(pl.program_id(0) == 0)
    def _():
        cnt_ref[...] = jnp.zeros_like(cnt_ref)

    before = cnt_ref[...][:, 0:1]
    csum = jnp.dot(sel.astype(BF16), tri_ref[...], preferred_element_type=F32)
    rank = before + csum - sel
    rk_ref[...] = jnp.concatenate(
        [jnp.sum(jnp.where(h, rank, 0.0), axis=0, keepdims=True) for h in hits], axis=0).astype(jnp.int32)
    cnt_ref[...] = jnp.broadcast_to(before + csum[:, T - 1:T], cnt_ref.shape)


def _router(logits_t, router_bias, tt):
    E, N = logits_t.shape
    tri = jnp.triu(jnp.ones((tt, tt), BF16))
    per_tok = pl.BlockSpec((TOP_K, tt), lambda i: (0, i))
    return pl.pallas_call(
        _router_kernel,
        grid=(N // tt,),
        in_specs=[pl.BlockSpec((E, tt), lambda i: (0, i)), pl.BlockSpec((E, 1), lambda i: (0, 0)),
                  pl.BlockSpec((tt, tt), lambda i: (0, 0))],
        out_specs=[per_tok, pl.BlockSpec((tt, 128), lambda i: (i, 0)), per_tok,
                   pl.BlockSpec((E, 128), lambda i: (0, 0))],
        out_shape=[jax.ShapeDtypeStruct((TOP_K, N), jnp.int32), jax.ShapeDtypeStruct((N, 128), F32),
                   jax.ShapeDtypeStruct((TOP_K, N), jnp.int32), jax.ShapeDtypeStruct((E, 128), F32)],
        compiler_params=_cparams(("arbitrary",)),
        name="router",
    )(logits_t, router_bias.reshape(E, 1).astype(F32), tri)


MOE_ROWS = 512
MOE_TOKENS = 128


def _plan_kernel(ei_ref, rk_ref, ps_ref, pos_ref):
    ei = ei_ref[...]
    ps = ps_ref[...]
    rows = lax.broadcasted_iota(jnp.int32, (ps.shape[0], ei.shape[1]), 0)
    start = [jnp.sum(jnp.where(rows == ei[k:k + 1], ps, 0.0), axis=0, keepdims=True) for k in range(ei.shape[0])]
    pos_ref[...] = rk_ref[...] + jnp.concatenate(start, axis=0).astype(jnp.int32)


def _dispatch_plan(eidx, rank, counts, bm, tm):
    K, N = eidx.shape
    E = N_EXPERTS
    cnt = counts[:, 0].astype(jnp.int32)
    padded = (cnt + bm - 1) // bm * bm
    pad_end = jnp.cumsum(padded)
    pad_start = pad_end - padded
    nb = -(-(N * K + E * (bm - 1)) // bm)
    tt = _tile(N, 2048)
    per_tok = pl.BlockSpec((K, tt), lambda i: (0, i))
    pos = pl.pallas_call(
        _plan_kernel,
        grid=(N // tt,),
        in_specs=[per_tok, per_tok, pl.BlockSpec((E, 1), lambda i: (0, 0))],
        out_specs=per_tok,
        out_shape=jax.ShapeDtypeStruct((K, N), jnp.int32),
        compiler_params=_cparams(("parallel",)),
        name="moe_plan",
    )(eidx, rank, pad_start.astype(F32).reshape(E, 1))
    blk_start = jnp.arange(nb, dtype=jnp.int32) * bm
    blk_e = jnp.minimum(jnp.sum(blk_start[:, None] >= pad_end[None, :], axis=1), E - 1)
    n_pad = nb * bm - N * K
    assert n_pad % (K * tm) == 0
    cpad = jnp.cumsum(padded - cnt)
    q = jnp.arange(n_pad, dtype=jnp.int32)
    seg = jnp.sum(q[:, None] >= cpad[None, :], axis=1)
    onehot = seg[:, None] == jnp.arange(E + 1, dtype=jnp.int32)[None, :]
    first = jnp.concatenate([pad_start + cnt, pad_end[-1:]])
    skipped = jnp.concatenate([jnp.zeros((1,), cpad.dtype), cpad])
    pad_pos = (q + jnp.sum(jnp.where(onehot, (first - skipped)[None, :], 0), axis=1)).astype(jnp.int32)
    nt = N // tm
    pos3 = pos.reshape(K, nt, tm).transpose(1, 0, 2).reshape(nt, 1, K * tm)
    return pos3, pad_pos.reshape(-1, 1, K * tm), blk_e.astype(jnp.int32), nb


TILE_ROWS = 4
_HI16 = 0xFFFF0000


def _bf16_bits(v):
    return lax.bitcast_convert_type(v.astype(BF16).astype(F32), U32)


def _to_tiles(ref, val):
    m, half = val.shape[0], val.shape[1] // 2
    words = (lax.shift_right_logical(_bf16_bits(val[:, :half]), jnp.uint32(16))
             | (_bf16_bits(val[:, half:]) & jnp.uint32(_HI16)))
    for j in range(TILE_ROWS):
        ref[pl.ds(j, m, stride=TILE_ROWS), :] = words[:, j * 128:(j + 1) * 128]


def _from_tiles(ref, start, m):
    words = jnp.concatenate([ref[pl.ds(start + j, m, stride=TILE_ROWS), :] for j in range(TILE_ROWS)], axis=1)
    lo = lax.bitcast_convert_type(lax.shift_left(words, jnp.uint32(16)), F32)
    hi = lax.bitcast_convert_type(words & jnp.uint32(_HI16), F32)
    return jnp.concatenate([lo, hi], axis=1)


def _row_copy(idx_ref, r, src_hbm, buf, base, sem):
    src = pl.multiple_of(idx_ref[0, r] * TILE_ROWS, TILE_ROWS)
    dst = base + r * TILE_ROWS
    if not isinstance(dst, int):
        dst = pl.multiple_of(dst, TILE_ROWS)
    return pltpu.make_async_copy(src_hbm.at[pl.ds(src, TILE_ROWS), :], buf.at[pl.ds(dst, TILE_ROWS), :], sem)


def _pipelined_gather(cur_ref, nxt_ref, src_hbm, buf, sem, nrows, consume):
    i = pl.program_id(0)
    span = nrows * TILE_ROWS

    def wait(slot):
        pltpu.make_async_copy(src_hbm.at[pl.ds(0, span), :], buf.at[pl.ds(slot * span, span), :], sem.at[slot]).wait()

    @pl.when(i == 0)
    def _():
        def body(r, carry):
            _row_copy(cur_ref, r, src_hbm, buf, 0, sem.at[0]).start()
            return carry
        lax.fori_loop(0, nrows, body, 0)

    for slot in range(2):
        @pl.when(i % 2 == slot)
        def _():
            for r in range(nrows):
                _row_copy(nxt_ref, r, src_hbm, buf, (1 - slot) * span, sem.at[1 - slot]).start(priority=r % 2)
            wait(slot)
            consume(slot * span)

            @pl.when(i == pl.num_programs(0) - 1)
            def _():
                wait(1 - slot)


def _dispatch_kernel(pos_ref, hx_ref, xs_hbm, stage, zero_tile, sem, *, tm, nt):
    i = pl.program_id(0)
    slot = i % 2
    span = tm * TILE_ROWS

    def wait_step(s):
        for _ in range(TOP_K):
            pltpu.make_async_copy(stage.at[pl.ds(0, span), :], xs_hbm.at[pl.ds(0, span), :], sem.at[s]).wait()

    def dst(r):
        return xs_hbm.at[pl.ds(pl.multiple_of(pos_ref[0, r] * TILE_ROWS, TILE_ROWS), TILE_ROWS), :]

    @pl.when(i < nt)
    def _():
        base = pl.multiple_of(slot * span, span)
        stage[pl.ds(base, span), :] = hx_ref[...]
        for r in range(TOP_K * tm):
            src = stage.at[pl.ds(pl.multiple_of(base + (r % tm) * TILE_ROWS, TILE_ROWS), TILE_ROWS), :]
            pltpu.make_async_copy(src, dst(r), sem.at[slot]).start(priority=r % 2)

    @pl.when(i == 0)
    def _():
        zero_tile[...] = jnp.zeros_like(zero_tile)

    @pl.when(i >= nt)
    def _():
        for r in range(TOP_K * tm):
            pltpu.make_async_copy(zero_tile, dst(r), sem.at[slot]).start(priority=r % 2)

    @pl.when(i > 0)
    def _():
        wait_step(1 - slot)

    @pl.when(i == pl.num_programs(0) - 1)
    def _():
        wait_step(slot)


def _dispatch(pos3, pad_pos, hx_tiles, rows_total, tm):
    nt = pos3.shape[0]
    idx = jnp.concatenate([pos3, pad_pos], axis=0)
    steps = idx.shape[0]
    return pl.pallas_call(
        functools.partial(_dispatch_kernel, tm=tm, nt=nt),
        grid=(steps,),
        in_specs=[pl.BlockSpec((None, 1, TOP_K * tm), lambda i: (i, 0, 0), memory_space=pltpu.SMEM),
                  pl.BlockSpec((tm * TILE_ROWS, 128), lambda i: (jnp.minimum(i, nt - 1), 0))],
        out_specs=pl.BlockSpec(memory_space=pl.ANY),
        out_shape=jax.ShapeDtypeStruct((rows_total * TILE_ROWS, 128), U32),
        scratch_shapes=[pltpu.VMEM((2 * tm * TILE_ROWS, 128), U32), pltpu.VMEM((TILE_ROWS, 128), U32),
                        pltpu.SemaphoreType.DMA((2,))],
        compiler_params=_cparams(("arbitrary",)),
        name="moe_dispatch",
    )(idx, hx_tiles)


def _expert_kernel(blk_e_ref, xs_ref, wg_ref, wu_ref, wd_ref, y_ref, *, bm):
    del blk_e_ref
    x = _from_tiles(xs_ref, 0, bm).astype(BF16)
    hg = jnp.dot(x, wg_ref[...].astype(BF16), preferred_element_type=F32)
    hu = jnp.dot(x, wu_ref[...].astype(BF16), preferred_element_type=F32)
    act = hg * _sigmoid(hg) * hu
    _to_tiles(y_ref, jnp.dot(act.astype(BF16), wd_ref[...].astype(BF16), preferred_element_type=F32))


def _experts(xs_tiles, blk_e, wg, wu, wd, nb, bm):
    D = wg.shape[1]
    grid_spec = pltpu.PrefetchScalarGridSpec(
        num_scalar_prefetch=1,
        grid=(nb,),
        in_specs=[pl.BlockSpec((bm * TILE_ROWS, 128), lambda i, be: (i, 0)),
                  pl.BlockSpec((None, D, EXPERT_HIDDEN), lambda i, be: (be[i], 0, 0)),
                  pl.BlockSpec((None, D, EXPERT_HIDDEN), lambda i, be: (be[i], 0, 0)),
                  pl.BlockSpec((None, EXPERT_HIDDEN, D), lambda i, be: (be[i], 0, 0))],
        out_specs=pl.BlockSpec((bm * TILE_ROWS, 128), lambda i, be: (i, 0)),
    )
    return pl.pallas_call(
        functools.partial(_expert_kernel, bm=bm),
        grid_spec=grid_spec,
        out_shape=jax.ShapeDtypeStruct((nb * bm * TILE_ROWS, 128), U32),
        compiler_params=_cparams(("arbitrary",)),
        name="moe_experts",
    )(blk_e, xs_tiles, wg, wu, wd)


def _combine_kernel(cur_ref, nxt_ref, y_hbm, hx_ref, wt_ref, sgu_ref, sd_ref, x1_ref, g2_ref, fg_ref, o_ref, ybuf,
                    sem, *, tm):
    def consume(here):
        wt = wt_ref[...]
        routed = None
        for k in range(TOP_K):
            wk = jnp.broadcast_to(wt[:, k:k + 1], (tm, 128))
            wk = jnp.concatenate([wk] * (x1_ref.shape[1] // 128), axis=1)
            term = wk * _from_tiles(ybuf, here + k * tm * TILE_ROWS, tm)
            routed = term if routed is None else routed + term
        hu = jnp.dot(_from_tiles(hx_ref, 0, tm).astype(BF16), sgu_ref[...], preferred_element_type=F32)
        hg = hu[:, :EXPERT_HIDDEN]
        act = hg * _sigmoid(hg) * hu[:, EXPERT_HIDDEN:]
        shared = jnp.dot(act.astype(BF16), sd_ref[...], preferred_element_type=F32)
        xo = x1_ref[...] + g2_ref[...] * (routed + shared)
        o_ref[...] = _rms(xo) * fg_ref[...]

    _pipelined_gather(cur_ref, nxt_ref, y_hbm, ybuf, sem, TOP_K * tm, consume)


def _combine(pos3, y_tiles, hx_tiles, w_tok, sgu, sd, x1, g2, fg, tm, tiles_per_batch):
    N, D = x1.shape
    nt = N // tm
    smem_blk = lambda f: pl.BlockSpec((None, 1, TOP_K * tm), f, memory_space=pltpu.SMEM)
    row = lambda i: (i, 0)
    const = lambda i: (0, 0)
    return pl.pallas_call(
        functools.partial(_combine_kernel, tm=tm),
        grid=(nt,),
        in_specs=[smem_blk(lambda i: (i, 0, 0)),
                  smem_blk(lambda i: (jnp.minimum(i + 1, nt - 1), 0, 0)),
                  pl.BlockSpec(memory_space=pl.ANY),
                  pl.BlockSpec((tm * TILE_ROWS, 128), row),
                  pl.BlockSpec((tm, 128), row),
                  pl.BlockSpec((D, 2 * EXPERT_HIDDEN), const),
                  pl.BlockSpec((EXPERT_HIDDEN, D), const),
                  pl.BlockSpec((tm, D), row),
                  pl.BlockSpec((None, 1, D), lambda i: (i // tiles_per_batch, 0, 0)),
                  pl.BlockSpec((1, D), const)],
        out_specs=pl.BlockSpec((tm, D), row),
        out_shape=jax.ShapeDtypeStruct((N, D), F32),
        scratch_shapes=[pltpu.VMEM((2 * TOP_K * tm * TILE_ROWS, 128), U32), pltpu.SemaphoreType.DMA((2,))],
        compiler_params=_cparams(("arbitrary",)),
        name="moe_combine",
    )(pos3, pos3, y_tiles, hx_tiles, w_tok, sgu, sd, x1, g2, fg)


def _rope_tables(L):
    rows = L // GRID_W
    row = jnp.repeat(jnp.arange(rows), GRID_W).astype(F32)
    col = jnp.tile(jnp.arange(GRID_W), rows).astype(F32)
    inv = ROPE_THETA ** (-jnp.arange(0, ROPE_AXIS_DIM, 2, dtype=F32) / ROPE_AXIS_DIM)
    ang = jnp.stack([row[:, None] * inv, col[:, None] * inv], axis=1)
    c, s = jnp.cos(ang), jnp.sin(ang)
    cos_h = jnp.stack([c, c], axis=2).reshape(L, HEAD_DIM)
    sin_h = jnp.stack([-s, s], axis=2).reshape(L, HEAD_DIM)
    return jnp.tile(cos_h, (1, 2)), jnp.tile(sin_h, (1, 2))


def _tile(n, pref):
    t = min(pref, n)
    while n % t:
        t //= 2
    return t


def kernel(x, c, ctx, c_ctx, ada_w, ada_b, norm1_g, norm2_g, w_in, lam_q1, lam_k1, lam_q2, lam_k2, subln_g, hy_conv_w, hy_conv_b, filt_w1, filt_b1, filt_freq, filt_w2, filt_b2, filt_w3, hy_bias, w_branch_attn, w_branch_hyena, w_out, router_w, router_bias, exp_w_gate, exp_w_up, exp_w_down, shared_w_gate, shared_w_up, shared_w_down, final_norm_g):
    B, L, D = x.shape
    assert ada_w.shape[0] == 1 and B % 2 == 0 and (2 * L) % (FFT_SLABS * FFT_STRIP) == 0
    lam = (jnp.exp(jnp.sum(lam_q1[0].astype(F32) * lam_k1[0].astype(F32)))
           - jnp.exp(jnp.sum(lam_q2[0].astype(F32) * lam_k2[0].astype(F32))) + LAM_INIT).reshape(1, 1)

    cc = jnp.zeros((16, D), F32).at[:B].set(c).at[B].set(c_ctx)
    mod = _adaln(cc, ada_w[0], ada_b[0])
    chunk = lambda r0, r1, j: mod[r0:r1, j * D:(j + 1) * D]
    sh1, sc1, g1, sh2, sc2, g2 = (chunk(0, B, j).reshape(B, 1, D) for j in range(6))
    csh, csc = chunk(B, B + 1, 0), chunk(B, B + 1, 1)

    w_bf = w_in[0].astype(BF16)
    n1g = norm1_g[0].reshape(1, D)
    cos_t, sin_t = _rope_tables(L)
    k, v, q, u, gates = _inproj(x, sh1, sc1, n1g, w_bf, cos_t, sin_t, hy_conv_w[0], hy_conv_b[0].reshape(1, -1),
                                _tile(L, 512))
    kc, vc = _ctxproj(ctx, csh, csc, n1g, w_bf)
    attn = _attention(lam, q, k, v, kc, vc, subln_g[0].reshape(1, V_DIM), _tile(L, 2048), _tile(L, 512))

    filt = (filt_w1[0], filt_b1[0], filt_freq[0], filt_w2[0], filt_b2[0], filt_w3[0])
    hy = _hyena(u, filt, hy_bias[0])

    x1, hx2, logits_t = _merge(attn, hy, gates, x, g1, sh2, sc2, norm2_g[0].reshape(1, D),
                               w_branch_attn[0].astype(BF16), w_branch_hyena[0].astype(BF16),
                               w_out[0].astype(BF16), router_w[0].T.astype(F32), _tile(L, 512))
    eidx, w_tok, rank, counts = _router(logits_t, router_bias[0], _tile(B * L, 1024))

    N = B * L
    tm = _tile(L, MOE_TOKENS)
    pos3, pad_pos, blk_e, nb = _dispatch_plan(eidx, rank, counts, MOE_ROWS, tm)
    xs = _dispatch(pos3, pad_pos, hx2, nb * MOE_ROWS, tm)
    y = _experts(xs, blk_e, exp_w_gate[0], exp_w_up[0], exp_w_down[0], nb, MOE_ROWS)
    sgu = jnp.concatenate([shared_w_gate[0], shared_w_up[0]], axis=-1).astype(BF16)
    out = _combine(pos3, y, hx2, w_tok, sgu, shared_w_down[0].astype(BF16), x1.reshape(N, D), g2,
                   final_norm_g.reshape(1, D), tm, L // tm)
    return out.reshape(B, L, D)
```

```python
import functools
import math

import jax
import jax.numpy as jnp
from jax import lax
from jax.experimental import pallas as pl
from jax.experimental.pallas import tpu as pltpu

F32 = jnp.float32
BF16 = jnp.bfloat16
U32 = jnp.uint32
HIGHEST = lax.Precision.HIGHEST

NORM_EPS = 1e-6
N_HEADS = 8
HEAD_DIM = 64
V_DIM = 2 * HEAD_DIM
GRID_W = 64
ROPE_THETA = 10000.0
ROPE_AXIS_DIM = HEAD_DIM // 2
HYENA_W = 1024
FILTER_EMB = 33
FILTER_BANDS = (FILTER_EMB - 1) // 2
DECAY_TARGET = 1e-2
FAST_DECAY_PCT = 0.3
SLOW_DECAY_PCT = 1.5
N_EXPERTS = 64
N_GROUPS = 8
TOPK_GROUPS = 4
TOP_K = 8
EXPERT_HIDDEN = 256
ROUTED_SCALE = 2.5
LAM_INIT = 0.8 - 0.6 * math.exp(-0.3 * 0)

V7X_VMEM_LIMIT = 56 * 1024 * 1024
LOG2E = 1.4426950408889634


def _cparams(sem):
    return pltpu.CompilerParams(dimension_semantics=sem, vmem_limit_bytes=V7X_VMEM_LIMIT)


def _sigmoid(v):
    return 1.0 / (1.0 + jnp.exp(-v))


def _rms(v):
    return v * lax.rsqrt(jnp.mean(v * v, axis=-1, keepdims=True) + NORM_EPS)


def _adaln_kernel(c_ref, w_ref, b_ref, o_ref):
    c = c_ref[...]
    s = c * _sigmoid(c)
    o_ref[...] = jnp.dot(s, w_ref[...], precision=HIGHEST, preferred_element_type=F32) + b_ref[...]


def _adaln(cc, w, b):
    rows, d = cc.shape
    n = w.shape[1]
    tn = 1536
    return pl.pallas_call(
        _adaln_kernel,
        grid=(n // tn,),
        in_specs=[pl.BlockSpec((rows, d), lambda j: (0, 0)),
                  pl.BlockSpec((d, tn), lambda j: (0, j)),
                  pl.BlockSpec((1, tn), lambda j: (0, j))],
        out_specs=pl.BlockSpec((rows, tn), lambda j: (0, j)),
        out_shape=jax.ShapeDtypeStruct((rows, n), F32),
        compiler_params=_cparams(("arbitrary",)),
        name="adaln",
    )(cc, w, b.reshape(1, n))


def _inproj_kernel(x_ref, xp_ref, xn_ref, sh_ref, sc_ref, g_ref, w_ref, cos_ref, sin_ref, cw_ref, cb_ref,
                   k_ref, v_ref, q_ref, u_ref, gt_ref, pscr, *, tm, nt, qscale):
    i = pl.program_id(1)
    g = g_ref[...]
    sh = sh_ref[...]
    sc = 1.0 + sc_ref[...]

    def norm_mod(xx):
        return (_rms(xx) * g) * sc + sh

    h = norm_mod(x_ref[...])
    hp = jnp.where(i > 0, norm_mod(xp_ref[...]), 0.0)
    hn = jnp.where(i < nt - 1, norm_mod(xn_ref[...]), 0.0)
    hb = h.astype(BF16)
    h_ext = jnp.concatenate([hp, h, hn], axis=0).astype(BF16)

    def proj(lhs, c0):
        return jnp.dot(lhs, w_ref[:, c0:c0 + 1024], preferred_element_type=F32)

    cosv = cos_ref[...]
    sinv = sin_ref[...]
    lane = lax.broadcasted_iota(jnp.int32, (1, V_DIM), 1)
    first = (lane % ROPE_AXIS_DIM) < (ROPE_AXIS_DIM // 2)

    def rope_store(a, o_ref, scale):
        for hh in range(N_HEADS):
            s = a[:, hh * V_DIM:(hh + 1) * V_DIM]
            partner = jnp.where(first, pltpu.roll(s, V_DIM - ROPE_AXIS_DIM // 2, 1),
                                pltpu.roll(s, ROPE_AXIS_DIM // 2, 1))
            r = s * cosv + partner * sinv
            if scale != 1.0:
                r = r * scale
            o_ref[:, hh * V_DIM:(hh + 1) * V_DIM] = r.astype(BF16)

    rope_store(proj(hb, 0), k_ref, 1.0)
    v_ref[...] = proj(hb, 1024).astype(BF16)
    rope_store(proj(hb, 2048), q_ref, qscale)
    for j in range(3):
        pscr[...] = proj(h_ext, 3072 + j * 1024)
        cw = cw_ref[:, j * 1024:(j + 1) * 1024]
        cb = cb_ref[:, j * 1024:(j + 1) * 1024]
        u = (pscr[pl.ds(7, tm), :] * cw[0:1] + pscr[pl.ds(8, tm), :] * cw[1:2]
             + pscr[pl.ds(9, tm), :] * cw[2:3] + cb)
        u_ref[:, j * 1024:(j + 1) * 1024] = u.astype(BF16)
    for j in range(2):
        gt_ref[:, j * 1024:(j + 1) * 1024] = proj(hb, 6144 + j * 1024).astype(BF16)


def _inproj(x, shift, scale, g, w_bf, cos_t, sin_t, conv_w, conv_b, tm):
    B, L, D = x.shape
    nt = L // tm
    nb8 = L // 8
    qscale = (HEAD_DIM ** -0.5) * LOG2E
    kern = functools.partial(_inproj_kernel, tm=tm, nt=nt, qscale=qscale)
    row = lambda b, i: (b, i, 0)
    per_b = lambda b, i: (b, 0, 0)
    const = lambda b, i: (0, 0)
    outs = pl.pallas_call(
        kern,
        grid=(B, nt),
        in_specs=[
            pl.BlockSpec((None, tm, D), row),
            pl.BlockSpec((None, 8, D), lambda b, i: (b, jnp.maximum(i * (tm // 8) - 1, 0), 0)),
            pl.BlockSpec((None, 8, D), lambda b, i: (b, jnp.minimum((i + 1) * (tm // 8), nb8 - 1), 0)),
            pl.BlockSpec((None, 1, D), per_b),
            pl.BlockSpec((None, 1, D), per_b),
            pl.BlockSpec((1, D), const),
            pl.BlockSpec((D, 8192), const, pipeline_mode=pl.Buffered(1)),
            pl.BlockSpec((tm, V_DIM), lambda b, i: (i, 0)),
            pl.BlockSpec((tm, V_DIM), lambda b, i: (i, 0)),
            pl.BlockSpec((3, 3 * HYENA_W), const),
            pl.BlockSpec((1, 3 * HYENA_W), const),
        ],
        out_specs=[
            pl.BlockSpec((None, tm, 1024), row),
            pl.BlockSpec((None, tm, 1024), row),
            pl.BlockSpec((None, tm, 1024), row),
            pl.BlockSpec((None, tm, 3 * HYENA_W), row),
            pl.BlockSpec((None, tm, 2048), row),
        ],
        out_shape=[
            jax.ShapeDtypeStruct((B, L, 1024), BF16),
            jax.ShapeDtypeStruct((B, L, 1024), BF16),
            jax.ShapeDtypeStruct((B, L, 1024), BF16),
            jax.ShapeDtypeStruct((B, L, 3 * HYENA_W), BF16),
            jax.ShapeDtypeStruct((B, L, 2048), BF16),
        ],
        scratch_shapes=[pltpu.VMEM((tm + 16, 1024), F32)],
        compiler_params=_cparams(("parallel", "arbitrary")),
        name="inproj",
    )(x, x, x, shift, scale, g, w_bf, cos_t, sin_t, conv_w, conv_b)
    return outs


def _ctxproj_kernel(x_ref, sh_ref, sc_ref, g_ref, w_ref, k_ref, v_ref):
    h = ((_rms(x_ref[...]) * g_ref[...]) * (1.0 + sc_ref[...]) + sh_ref[...]).astype(BF16)
    k_ref[...] = jnp.dot(h, w_ref[:, 0:1024], preferred_element_type=F32).astype(BF16)
    v_ref[...] = jnp.dot(h, w_ref[:, 1024:2048], preferred_element_type=F32).astype(BF16)


def _ctxproj(ctx, shift, scale, g, w_bf):
    B, Lc, D = ctx.shape
    return pl.pallas_call(
        _ctxproj_kernel,
        grid=(B,),
        in_specs=[pl.BlockSpec((None, Lc, D), lambda b: (b, 0, 0)),
                  pl.BlockSpec((1, D), lambda b: (0, 0)),
                  pl.BlockSpec((1, D), lambda b: (0, 0)),
                  pl.BlockSpec((1, D), lambda b: (0, 0)),
                  pl.BlockSpec((D, 2048), lambda b: (0, 0))],
        out_specs=[pl.BlockSpec((None, Lc, 1024), lambda b: (b, 0, 0)),
                   pl.BlockSpec((None, Lc, 1024), lambda b: (b, 0, 0))],
        out_shape=[jax.ShapeDtypeStruct((B, Lc, 1024), BF16),
                   jax.ShapeDtypeStruct((B, Lc, 1024), BF16)],
        compiler_params=_cparams(("arbitrary",)),
        name="ctxproj",
    )(ctx, shift, scale, g, w_bf)


def _attn_kernel(lam_ref, q_ref, k_ref, v_ref, kc_ref, vc_ref, g_ref, o_ref, m_s, acc_s, *, tk, nk):
    q = q_ref[...]
    lane = lax.broadcasted_iota(jnp.int32, (1, V_DIM), 1)
    zero = jnp.zeros_like(q)
    qs = (jnp.where(lane < HEAD_DIM, q, zero), jnp.where(lane >= HEAD_DIM, q, zero))
    def chunk(kk, vv, first=False):
        reps = kk.shape[0] // V_DIM
        v1 = jnp.concatenate([vv, jnp.ones_like(vv)], axis=1)
        for m in range(2):
            s = lax.dot_general(qs[m], kk, (((1,), (1,)), ((), ())), preferred_element_type=F32)
            m_new = jnp.broadcast_to(jnp.max(s, axis=-1, keepdims=True), m_s.shape[1:])
            if not first:
                m_prev = m_s[m]
                m_new = jnp.maximum(m_prev, m_new)
                alpha = jnp.exp2(m_prev - m_new)
            p = jnp.exp2(s - jnp.concatenate([m_new] * reps, axis=1))
            pv = jnp.dot(p.astype(BF16), v1, preferred_element_type=F32)
            acc_s[m] = pv if first else jnp.concatenate([alpha, alpha], axis=1) * acc_s[m] + pv
            m_s[m] = m_new

    def body(j, carry):
        off = pl.multiple_of(j * tk, tk)
        chunk(k_ref[pl.ds(off, tk), :], v_ref[pl.ds(off, tk), :])
        return carry

    chunk(kc_ref[...], vc_ref[...], first=True)
    lax.fori_loop(0, nk, body, 0, unroll=2 if nk % 2 == 0 else 1)
    lam = lam_ref[0, 0]
    a0, a1 = acc_s[0], acc_s[1]
    o = a0[:, :V_DIM] / a0[:, V_DIM:] - lam * (a1[:, :V_DIM] / a1[:, V_DIM:])
    o = _rms(o) * g_ref[...] * (1.0 - LAM_INIT)
    o_ref[...] = o.astype(BF16)


def _attention(lam, q, k, v, kc, vc, subln_g, tq, tk):
    B, L, _ = q.shape
    Lc = kc.shape[1]
    kern = functools.partial(_attn_kernel, tk=tk, nk=L // tk)
    kv = lambda b, h, i: (b, 0, h)
    return pl.pallas_call(
        kern,
        grid=(B, N_HEADS, L // tq),
        in_specs=[
            pl.BlockSpec(memory_space=pltpu.SMEM),
            pl.BlockSpec((None, tq, V_DIM), lambda b, h, i: (b, i, h)),
            pl.BlockSpec((None, L, V_DIM), kv),
            pl.BlockSpec((None, L, V_DIM), kv),
            pl.BlockSpec((None, Lc, V_DIM), kv),
            pl.BlockSpec((None, Lc, V_DIM), kv),
            pl.BlockSpec((1, V_DIM), lambda b, h, i: (0, 0)),
        ],
        out_specs=pl.BlockSpec((None, tq, V_DIM), lambda b, h, i: (b, i, h)),
        out_shape=jax.ShapeDtypeStruct((B, L, N_HEADS * V_DIM), BF16),
        scratch_shapes=[pltpu.VMEM((2, tq, V_DIM), F32), pltpu.VMEM((2, tq, 2 * V_DIM), F32)],
        compiler_params=_cparams(("parallel", "parallel", "arbitrary")),
        name="diffattn",
    )(lam, q, k, v, kc, vc, subln_g)


FFT_SLABS = 64
FFT_STRIP = 16


def _dft_tables(nb):
    N = FFT_SLABS * nb
    fa = jnp.arange(FFT_SLABS, dtype=jnp.int32)[:, None, None]
    fb = jnp.arange(nb, dtype=jnp.int32)[None, :, None]
    b = jnp.arange(nb, dtype=jnp.int32)[None, None, :]
    ang = ((FFT_SLABS * b * fb + b * fa) % N).astype(F32) * (2.0 * math.pi / N)
    c, s = jnp.cos(ang), jnp.sin(ang)
    fwd = jnp.concatenate([jnp.concatenate([c, s], -1), jnp.concatenate([-s, c], -1)], axis=1)
    inv = jnp.swapaxes(fwd, 1, 2) / N
    return fwd.astype(BF16), inv.astype(BF16)


def _vadd(a, b):
    return b if a is None else (a if b is None else a + b)


def _vsub(a, b):
    if b is None:
        return a
    return -b if a is None else a - b


def _vscale(a, k):
    if a is None or k == 1.0:
        return a
    return -a if k == -1.0 else a * k


def _cmulc(x, c, s):
    re, im = x
    eps = 1e-9
    if abs(s) < eps:
        k = 1.0 if c > 0 else -1.0
        return _vscale(re, k), _vscale(im, k)
    if abs(c) < eps:
        k = 1.0 if s > 0 else -1.0
        return _vscale(im, -k), _vscale(re, k)
    if abs(abs(c) - abs(s)) < eps:
        kc = 1.0 if c > 0 else -1.0
        ks = 1.0 if s > 0 else -1.0
        return (_vscale(_vsub(_vscale(re, kc), _vscale(im, ks)), abs(c)),
                _vscale(_vadd(_vscale(re, ks), _vscale(im, kc)), abs(c)))
    return _vsub(_vscale(re, c), _vscale(im, s)), _vadd(_vscale(re, s), _vscale(im, c))


def _fft(xs, sign):
    n = len(xs)
    if n == 1:
        return xs
    ev, od = _fft(xs[0::2], sign), _fft(xs[1::2], sign)
    out = [None] * n
    for k in range(n // 2):
        ang = sign * 2.0 * math.pi * k / n
        t = _cmulc(od[k], math.cos(ang), math.sin(ang))
        out[k] = (_vadd(ev[k][0], t[0]), _vadd(ev[k][1], t[1]))
        out[k + n // 2] = (_vsub(ev[k][0], t[0]), _vsub(ev[k][1], t[1]))
    return out


def _put(ref, slot, rows, val):
    ref[slot, rows, :] = jnp.zeros((FFT_STRIP, ref.shape[2]), F32) if val is None else val


def _slab_fft_forward(load, s_ref, nb, n_in):
    def strip(i, carry):
        r0 = pl.multiple_of(i * FFT_STRIP, FFT_STRIP)
        rows_re, rows_im = pl.ds(r0, FFT_STRIP), pl.ds(nb + r0, FFT_STRIP)
        for a0 in range(8):
            xs = [load(8 * a1 + a0, r0) if a1 < n_in else (None, None) for a1 in range(8)]
            ys = _fft(xs, -1.0)
            for f1 in range(8):
                ang = -2.0 * math.pi * a0 * f1 / FFT_SLABS
                yr, yi = _cmulc(ys[f1], math.cos(ang), math.sin(ang))
                _put(s_ref, a0 * 8 + f1, rows_re, yr)
                _put(s_ref, a0 * 8 + f1, rows_im, yi)
        for f1 in range(8):
            xs = [(s_ref[a0 * 8 + f1, rows_re, :], s_ref[a0 * 8 + f1, rows_im, :]) for a0 in range(8)]
            ys = _fft(xs, -1.0)
            for f0 in range(8):
                s_ref[f1 + 8 * f0, rows_re, :] = ys[f0][0]
                s_ref[f1 + 8 * f0, rows_im, :] = ys[f0][1]
        return carry

    lax.fori_loop(0, nb // FFT_STRIP, strip, 0)


def _slab_fft_inverse(s_ref, nb, emit):
    def strip(i, carry):
        r0 = pl.multiple_of(i * FFT_STRIP, FFT_STRIP)
        rows_re, rows_im = pl.ds(r0, FFT_STRIP), pl.ds(nb + r0, FFT_STRIP)
        for q in range(8):
            xs = [(s_ref[8 * p + q, rows_re, :], s_ref[8 * p + q, rows_im, :]) for p in range(8)]
            us = _fft(xs, 1.0)
            for r in range(8):
                ang = 2.0 * math.pi * r * q / FFT_SLABS
                ur, ui = _cmulc(us[r], math.cos(ang), math.sin(ang))
                s_ref[8 * r + q, rows_re, :] = ur
                s_ref[8 * r + q, rows_im, :] = ui
        for r in range(8):
            xs = [(s_ref[8 * r + q, rows_re, :], s_ref[8 * r + q, rows_im, :]) for q in range(8)]
            ys = _fft(xs, 1.0)
            for sidx in range(4):
                emit(r + 8 * sidx, r0, ys[sidx][0], ys[sidx][1])
        return carry

    lax.fori_loop(0, nb // FFT_STRIP, strip, 0)


def _hyena_conv_kernel(v_ref, x_ref, k_ref, ff_ref, fi_ref, bias_ref, o_ref, s_ref, *, nb):
    def load(a, r0):
        rows = pl.ds(pl.multiple_of(a * nb + r0, FFT_STRIP), FFT_STRIP)
        return v_ref[0, rows, :].astype(F32), v_ref[1, rows, :].astype(F32)

    _slab_fft_forward(load, s_ref, nb, FFT_SLABS // 16)

    def freq(f, carry):
        z = jnp.dot(ff_ref[f], s_ref[f].astype(BF16), preferred_element_type=F32)
        zr, zi = z[:nb], z[nb:]
        kr, ki = k_ref[0, f], k_ref[1, f]
        y = jnp.concatenate([zr * kr - zi * ki, zr * ki + zi * kr], axis=0).astype(BF16)
        s_ref[f] = jnp.dot(fi_ref[f], y, preferred_element_type=F32)
        return carry

    lax.fori_loop(0, FFT_SLABS, freq, 0, unroll=8)
    bias = bias_ref[...]

    def emit(a, r0, re, im):
        rows = pl.ds(pl.multiple_of(a * nb + r0, FFT_STRIP), FFT_STRIP)
        for half, y in enumerate((re, im)):
            v = v_ref[half, rows, :].astype(F32)
            o_ref[half, rows, :] = (x_ref[half, rows, :].astype(F32) * (y + v * bias)).astype(BF16)

    _slab_fft_inverse(s_ref, nb, emit)


def _hyena_conv(uv, zv, k_hat, ff, fi, bias, order, in_col, gate_col, nb):
    _, P, L, _ = uv.shape
    W = HYENA_W
    nct = W // 128
    src, src_off = (uv, in_col * nct) if zv is None else (zv, 0)
    blk = (2, None, L, 128)
    return pl.pallas_call(
        functools.partial(_hyena_conv_kernel, nb=nb),
        grid=(nct, P),
        in_specs=[pl.BlockSpec(blk, lambda c, p: (0, p, 0, src_off + c)),
                  pl.BlockSpec(blk, lambda c, p: (0, p, 0, gate_col * nct + c)),
                  pl.BlockSpec((2, FFT_SLABS, nb, 128), lambda c, p: (0, 0, 0, order * nct + c),
                               pipeline_mode=pl.Buffered(1)),
                  pl.BlockSpec((FFT_SLABS, 2 * nb, 2 * nb), lambda c, p: (0, 0, 0), pipeline_mode=pl.Buffered(1)),
                  pl.BlockSpec((FFT_SLABS, 2 * nb, 2 * nb), lambda c, p: (0, 0, 0), pipeline_mode=pl.Buffered(1)),
                  pl.BlockSpec((1, 128), lambda c, p: (0, c))],
        out_specs=pl.BlockSpec(blk, lambda c, p: (0, p, 0, c)),
        out_shape=jax.ShapeDtypeStruct((2, P, L, W), BF16),
        scratch_shapes=[pltpu.VMEM((FFT_SLABS, 2 * nb, 128), F32)],
        compiler_params=_cparams(("parallel", "arbitrary")),
        name="hyena_conv%d" % order,
    )(src, uv, k_hat, ff, fi, bias)


def _filter_features(n):
    pos = jnp.arange(n, dtype=F32)[:, None]
    t = jnp.linspace(0.0, 1.0, n, dtype=F32)[:, None]
    w = 2 * math.pi * pos / n
    bands = jnp.linspace(1e-4, FILTER_BANDS - 1, FILTER_BANDS, dtype=F32)
    z = jnp.concatenate([t, jnp.cos(bands * w), -jnp.sin(bands * w)], axis=-1)
    z = jnp.pad(z, ((0, 0), (0, 128 - FILTER_EMB)))
    tap = jnp.arange(2 * n)
    return z[jnp.where(tap < n, tap, (2 * n - tap) % n)]


def _filt_taps_kernel(z_ref, w1_ref, b1_ref, fr_ref, w2_ref, b2_ref, w3_ref, dl_ref, k_ref, sum_ref):
    a = pl.program_id(0)
    z = z_ref[...]
    fr = fr_ref[...]
    hd = jnp.sin(fr * (jnp.dot(z, w1_ref[...], precision=HIGHEST, preferred_element_type=F32) + b1_ref[...]))
    hd = jnp.sin(fr * (jnp.dot(hd, w2_ref[...], precision=HIGHEST, preferred_element_type=F32) + b2_ref[...]))
    h = jnp.dot(hd, w3_ref[...], precision=HIGHEST, preferred_element_type=F32) * jnp.exp(-z[:, 0:1] * dl_ref[...])
    rows = lax.broadcasted_iota(jnp.int32, (z.shape[0], 1), 0)
    h = jnp.where(jnp.logical_and(a == FFT_SLABS // 2, rows == 0), 0.0, h)
    k_ref[...] = h

    @pl.when(a == 0)
    def _():
        sum_ref[...] = jnp.zeros_like(sum_ref)

    sum_ref[0:1, :] += jnp.sum(jnp.abs(h), axis=0, keepdims=True)


def _filt_spectrum_kernel(t_ref, inv_ref, ff_ref, k_ref, s_ref, *, nb):
    def load(a, r0):
        return t_ref[pl.ds(pl.multiple_of(a * nb + r0, FFT_STRIP), FFT_STRIP), :], None

    _slab_fft_forward(load, s_ref, nb, FFT_SLABS // 8)
    inv = inv_ref[...]

    def freq(f, carry):
        z = jnp.dot(ff_ref[f], s_ref[f].astype(BF16), preferred_element_type=F32)
        k_ref[0, f] = z[:nb] * inv
        k_ref[1, f] = z[nb:] * inv
        return carry

    lax.fori_loop(0, FFT_SLABS, freq, 0, unroll=8)


def _hyena_filters(n, filt, ff, nb):
    w1, b1, freq, w2, b2, w3 = filt
    W = HYENA_W
    hid = w1.shape[1]
    z = _filter_features(n)
    w1p = jnp.pad(w1.astype(F32), ((0, 128 - FILTER_EMB), (0, 0)))
    w3d = w3.astype(F32).reshape(hid, 2, 2, W)
    w3sel = jnp.stack([w3d[:, :, d, :].reshape(hid, 2 * W) for d in range(2)])
    deltas = jnp.abs(jnp.linspace(math.log(DECAY_TARGET) / SLOW_DECAY_PCT,
                                  math.log(DECAY_TARGET) / FAST_DECAY_PCT, W, dtype=F32))
    deltas = jnp.tile(deltas.reshape(1, W), (1, 2))
    const = lambda a: (0, 0)
    taps, sums = pl.pallas_call(
        _filt_taps_kernel,
        grid=(FFT_SLABS,),
        in_specs=[pl.BlockSpec((nb, 128), lambda a: (a, 0)),
                  pl.BlockSpec((128, hid), const), pl.BlockSpec((1, hid), const), pl.BlockSpec((1, hid), const),
                  pl.BlockSpec((hid, hid), const), pl.BlockSpec((1, hid), const),
                  pl.BlockSpec((None, hid, 2 * W), lambda a: (a // (FFT_SLABS // 2), 0, 0)),
                  pl.BlockSpec((1, 2 * W), const)],
        out_specs=[pl.BlockSpec((nb, 2 * W), lambda a: (a, 0)), pl.BlockSpec((8, 2 * W), const)],
        out_shape=[jax.ShapeDtypeStruct((2 * n, 2 * W), F32), jax.ShapeDtypeStruct((8, 2 * W), F32)],
        compiler_params=_cparams(("arbitrary",)),
        name="hyena_filter_taps",
    )(z, w1p, b1.reshape(1, hid), freq.reshape(1, hid), w2, b2.reshape(1, hid), w3sel, deltas)
    inv = 1.0 / sums[0:1]
    return pl.pallas_call(
        functools.partial(_filt_spectrum_kernel, nb=nb),
        grid=(2 * W // 128,),
        in_specs=[pl.BlockSpec((2 * n, 128), lambda j: (0, j)),
                  pl.BlockSpec((1, 128), lambda j: (0, j)),
                  pl.BlockSpec((FFT_SLABS, 2 * nb, 2 * nb), lambda j: (0, 0, 0), pipeline_mode=pl.Buffered(1))],
        out_specs=pl.BlockSpec((2, FFT_SLABS, nb, 128), lambda j: (0, 0, 0, j)),
        out_shape=jax.ShapeDtypeStruct((2, FFT_SLABS, nb, 2 * W), F32),
        scratch_shapes=[pltpu.VMEM((FFT_SLABS, 2 * nb, 128), F32)],
        compiler_params=_cparams(("parallel",)),
        name="hyena_filter_spectrum",
    )(taps, inv, ff)


def _hyena(u, filt, hy_bias):
    B, L, _ = u.shape
    P = B // 2
    nb = 2 * L // FFT_SLABS
    ff, fi = _dft_tables(nb)
    k_hat = _hyena_filters(L, filt, ff, nb)
    uv = u.reshape(2, P, L, 3 * HYENA_W)
    bias = hy_bias.astype(F32)
    z = _hyena_conv(uv, None, k_hat, ff, fi, bias[0:1], 0, 0, 1, nb)
    y = _hyena_conv(uv, z, k_hat, ff, fi, bias[1:2], 1, 0, 2, nb)
    return y.reshape(B, L, HYENA_W)


def _merge_kernel(a_ref, h_ref, gt_ref, x_ref, g1_ref, sh_ref, sc_ref, n2g_ref, wpa_ref, wph_ref, wo_ref, rw_ref,
                  x1_ref, hx_ref, lg_ref):
    ya = jnp.dot(a_ref[...], wpa_ref[...], preferred_element_type=F32)
    yh = jnp.dot(h_ref[...], wph_ref[...], preferred_element_type=F32)
    ga = gt_ref[:, 0:1024].astype(F32)
    gh = gt_ref[:, 1024:2048].astype(F32)
    y = _sigmoid(ga) * ya + _sigmoid(gh) * yh
    mix = jnp.dot(y.astype(BF16), wo_ref[...], preferred_element_type=F32)
    x1 = x_ref[...] + g1_ref[...] * mix
    x1_ref[...] = x1
    hx = (_rms(x1) * n2g_ref[...]) * (1.0 + sc_ref[...]) + sh_ref[...]
    _to_tiles(hx_ref, hx)
    lg_ref[...] = lax.dot_general(rw_ref[...], hx, (((1,), (1,)), ((), ())), precision=HIGHEST,
                                  preferred_element_type=F32)


def _merge(attn, hy, gates, x, g1, sh2, sc2, n2g, wpa, wph, wo, rwT, tm):
    B, L, D = x.shape
    nt = L // tm
    row = lambda b, i: (b, i, 0)
    per_b = lambda b, i: (b, 0, 0)
    const = lambda b, i: (0, 0)
    return pl.pallas_call(
        _merge_kernel,
        grid=(B, nt),
        in_specs=[pl.BlockSpec((None, tm, 1024), row), pl.BlockSpec((None, tm, 1024), row),
                  pl.BlockSpec((None, tm, 2048), row), pl.BlockSpec((None, tm, D), row),
                  pl.BlockSpec((None, 1, D), per_b), pl.BlockSpec((None, 1, D), per_b),
                  pl.BlockSpec((None, 1, D), per_b), pl.BlockSpec((1, D), const),
                  pl.BlockSpec((1024, D), const), pl.BlockSpec((1024, D), const), pl.BlockSpec((D, D), const),
                  pl.BlockSpec((N_EXPERTS, D), const)],
        out_specs=[pl.BlockSpec((None, tm, D), row),
                   pl.BlockSpec((tm * TILE_ROWS, 128), lambda b, i: (b * nt + i, 0)),
                   pl.BlockSpec((N_EXPERTS, tm), lambda b, i: (0, b * nt + i))],
        out_shape=[jax.ShapeDtypeStruct((B, L, D), F32),
                   jax.ShapeDtypeStruct((B * L * TILE_ROWS, 128), U32),
                   jax.ShapeDtypeStruct((N_EXPERTS, B * L), F32)],
        compiler_params=_cparams(("parallel", "arbitrary")),
        name="merge",
    )(attn, hy, gates, x, g1, sh2, sc2, n2g, wpa, wph, wo, rwT)


def _router_kernel(lg_ref, rb_ref, tri_ref, ei_ref, wt_ref, rk_ref, cnt_ref):
    scores = _sigmoid(lg_ref[...])
    choice = scores + rb_ref[...]
    E, T = scores.shape
    gsz = E // N_GROUPS
    neg = jnp.float32(-jnp.inf)
    big = jnp.int32(1 << 30)
    rows8 = lax.broadcasted_iota(jnp.int32, (gsz, T), 0)
    gscore = []
    for gi in range(N_GROUPS):
        c = choice[gi * gsz:(gi + 1) * gsz]
        m1 = jnp.max(c, axis=0, keepdims=True)
        i1 = jnp.min(jnp.where(c == m1, rows8, big), axis=0, keepdims=True)
        m2 = jnp.max(jnp.where(rows8 == i1, neg, c), axis=0, keepdims=True)
        gscore.append(m1 + m2)
    gs = jnp.concatenate(gscore, axis=0)
    rowsg = lax.broadcasted_iota(jnp.int32, (N_GROUPS, T), 0)
    gsel = jnp.zeros((N_GROUPS, T), jnp.bool_)
    for _ in range(TOPK_GROUPS):
        m = jnp.max(gs, axis=0, keepdims=True)
        ix = jnp.min(jnp.where(gs == m, rowsg, big), axis=0, keepdims=True)
        hit = rowsg == ix
        gsel = jnp.logical_or(gsel, hit)
        gs = jnp.where(hit, neg, gs)
    masked = jnp.concatenate(
        [jnp.where(gsel[gi:gi + 1], choice[gi * gsz:(gi + 1) * gsz], neg) for gi in range(N_GROUPS)], axis=0)
    rows = lax.broadcasted_iota(jnp.int32, (E, T), 0)
    ids, ws, hits = [], [], []
    sel = jnp.zeros((E, T), F32)
    for _ in range(TOP_K):
        m = jnp.max(masked, axis=0, keepdims=True)
        ix = jnp.min(jnp.where(masked == m, rows, big), axis=0, keepdims=True)
        hit = rows == ix
        ids.append(ix)
        hits.append(hit)
        ws.append(jnp.sum(jnp.where(hit, scores, 0.0), axis=0, keepdims=True))
        sel = jnp.where(hit, 1.0, sel)
        masked = jnp.where(hit, neg, masked)
    w = jnp.concatenate(ws, axis=0)
    w = w / jnp.sum(w, axis=0, keepdims=True) * ROUTED_SCALE
    ei_ref[...] = jnp.concatenate(ids, axis=0)
    wt_ref[...] = jnp.concatenate([w, jnp.zeros((128 - TOP_K, T), F32)], axis=0).T

    @pl.when(pl.program_id(0) == 0)
    def _():
        cnt_ref[...] = jnp.zeros_like(cnt_ref)

    before = cnt_ref[...][:, 0:1]
    csum = jnp.dot(sel.astype(BF16), tri_ref[...], preferred_element_type=F32)
    rank = before + csum - sel
    rk_ref[...] = jnp.concatenate(
        [jnp.sum(jnp.where(h, rank, 0.0), axis=0, keepdims=True) for h in hits], axis=0).astype(jnp.int32)
    cnt_ref[...] = jnp.broadcast_to(before + csum[:, T - 1:T], cnt_ref.shape)


def _router(logits_t, router_bias, tt):
    E, N = logits_t.shape
    tri = jnp.triu(jnp.ones((tt, tt), BF16))
    per_tok = pl.BlockSpec((TOP_K, tt), lambda i: (0, i))
    return pl.pallas_call(
        _router_kernel,
        grid=(N // tt,),
        in_specs=[pl.BlockSpec((E, tt), lambda i: (0, i)), pl.BlockSpec((E, 1), lambda i: (0, 0)),
                  pl.BlockSpec((tt, tt), lambda i: (0, 0))],
        out_specs=[per_tok, pl.BlockSpec((tt, 128), lambda i: (i, 0)), per_tok,
                   pl.BlockSpec((E, 128), lambda i: (0, 0))],
        out_shape=[jax.ShapeDtypeStruct((TOP_K, N), jnp.int32), jax.ShapeDtypeStruct((N, 128), F32),
                   jax.ShapeDtypeStruct((TOP_K, N), jnp.int32), jax.ShapeDtypeStruct((E, 128), F32)],
        compiler_params=_cparams(("arbitrary",)),
        name="router",
    )(logits_t, router_bias.reshape(E, 1).astype(F32), tri)


MOE_ROWS = 512
MOE_TOKENS = 128


def _plan_kernel(ei_ref, rk_ref, ps_ref, pos_ref):
    ei = ei_ref[...]
    ps = ps_ref[...]
    rows = lax.broadcasted_iota(jnp.int32, (ps.shape[0], ei.shape[1]), 0)
    start = [jnp.sum(jnp.where(rows == ei[k:k + 1], ps, 0.0), axis=0, keepdims=True) for k in range(ei.shape[0])]
    pos_ref[...] = rk_ref[...] + jnp.concatenate(start, axis=0).astype(jnp.int32)


def _dispatch_plan(eidx, rank, counts, bm, tm):
    K, N = eidx.shape
    E = N_EXPERTS
    cnt = counts[:, 0].astype(jnp.int32)
    padded = (cnt + bm - 1) // bm * bm
    pad_end = jnp.cumsum(padded)
    pad_start = pad_end - padded
    nb = -(-(N * K + E * (bm - 1)) // bm)
    tt = _tile(N, 2048)
    per_tok = pl.BlockSpec((K, tt), lambda i: (0, i))
    pos = pl.pallas_call(
        _plan_kernel,
        grid=(N // tt,),
        in_specs=[per_tok, per_tok, pl.BlockSpec((E, 1), lambda i: (0, 0))],
        out_specs=per_tok,
        out_shape=jax.ShapeDtypeStruct((K, N), jnp.int32),
        compiler_params=_cparams(("parallel",)),
        name="moe_plan",
    )(eidx, rank, pad_start.astype(F32).reshape(E, 1))
    blk_start = jnp.arange(nb, dtype=jnp.int32) * bm
    blk_e = jnp.minimum(jnp.sum(blk_start[:, None] >= pad_end[None, :], axis=1), E - 1)
    n_pad = nb * bm - N * K
    assert n_pad % (K * tm) == 0
    cpad = jnp.cumsum(padded - cnt)
    q = jnp.arange(n_pad, dtype=jnp.int32)
    seg = jnp.sum(q[:, None] >= cpad[None, :], axis=1)
    onehot = seg[:, None] == jnp.arange(E + 1, dtype=jnp.int32)[None, :]
    first = jnp.concatenate([pad_start + cnt, pad_end[-1:]])
    skipped = jnp.concatenate([jnp.zeros((1,), cpad.dtype), cpad])
    pad_pos = (q + jnp.sum(jnp.where(onehot, (first - skipped)[None, :], 0), axis=1)).astype(jnp.int32)
    nt = N // tm
    pos3 = pos.reshape(K, nt, tm).transpose(1, 0, 2).reshape(nt, 1, K * tm)
    return pos3, pad_pos.reshape(-1, 1, K * tm), blk_e.astype(jnp.int32), nb


TILE_ROWS = 4
_HI16 = 0xFFFF0000


def _bf16_bits(v):
    return lax.bitcast_convert_type(v.astype(BF16).astype(F32), U32)


def _to_tiles(ref, val):
    m, half = val.shape[0], val.shape[1] // 2
    words = (lax.shift_right_logical(_bf16_bits(val[:, :half]), jnp.uint32(16))
             | (_bf16_bits(val[:, half:]) & jnp.uint32(_HI16)))
    for j in range(TILE_ROWS):
        ref[pl.ds(j, m, stride=TILE_ROWS), :] = words[:, j * 128:(j + 1) * 128]


def _from_tiles(ref, start, m):
    words = jnp.concatenate([ref[pl.ds(start + j, m, stride=TILE_ROWS), :] for j in range(TILE_ROWS)], axis=1)
    lo = lax.bitcast_convert_type(lax.shift_left(words, jnp.uint32(16)), F32)
    hi = lax.bitcast_convert_type(words & jnp.uint32(_HI16), F32)
    return jnp.concatenate([lo, hi], axis=1)


def _row_copy(idx_ref, r, src_hbm, buf, base, sem):
    src = pl.multiple_of(idx_ref[0, r] * TILE_ROWS, TILE_ROWS)
    dst = base + r * TILE_ROWS
    if not isinstance(dst, int):
        dst = pl.multiple_of(dst, TILE_ROWS)
    return pltpu.make_async_copy(src_hbm.at[pl.ds(src, TILE_ROWS), :], buf.at[pl.ds(dst, TILE_ROWS), :], sem)


def _pipelined_gather(cur_ref, nxt_ref, src_hbm, buf, sem, nrows, consume):
    i = pl.program_id(0)
    span = nrows * TILE_ROWS

    def wait(slot):
        pltpu.make_async_copy(src_hbm.at[pl.ds(0, span), :], buf.at[pl.ds(slot * span, span), :], sem.at[slot]).wait()

    @pl.when(i == 0)
    def _():
        def body(r, carry):
            _row_copy(cur_ref, r, src_hbm, buf, 0, sem.at[0]).start()
            return carry
        lax.fori_loop(0, nrows, body, 0)

    for slot in range(2):
        @pl.when(i % 2 == slot)
        def _():
            for r in range(nrows):
                _row_copy(nxt_ref, r, src_hbm, buf, (1 - slot) * span, sem.at[1 - slot]).start(priority=r % 2)
            wait(slot)
            consume(slot * span)

            @pl.when(i == pl.num_programs(0) - 1)
            def _():
                wait(1 - slot)


def _dispatch_kernel(pos_ref, hx_ref, xs_hbm, stage, zero_tile, sem, *, tm, nt):
    i = pl.program_id(0)
    slot = i % 2
    span = tm * TILE_ROWS

    def wait_step(s):
        for _ in range(TOP_K):
            pltpu.make_async_copy(stage.at[pl.ds(0, span), :], xs_hbm.at[pl.ds(0, span), :], sem.at[s]).wait()

    def dst(r):
        return xs_hbm.at[pl.ds(pl.multiple_of(pos_ref[0, r] * TILE_ROWS, TILE_ROWS), TILE_ROWS), :]

    @pl.when(i < nt)
    def _():
        base = pl.multiple_of(slot * span, span)
        stage[pl.ds(base, span), :] = hx_ref[...]
        for r in range(TOP_K * tm):
            src = stage.at[pl.ds(pl.multiple_of(base + (r % tm) * TILE_ROWS, TILE_ROWS), TILE_ROWS), :]
            pltpu.make_async_copy(src, dst(r), sem.at[slot]).start(priority=r % 2)

    @pl.when(i == 0)
    def _():
        zero_tile[...] = jnp.zeros_like(zero_tile)

    @pl.when(i >= nt)
    def _():
        for r in range(TOP_K * tm):
            pltpu.make_async_copy(zero_tile, dst(r), sem.at[slot]).start(priority=r % 2)

    @pl.when(i > 0)
    def _():
        wait_step(1 - slot)

    @pl.when(i == pl.num_programs(0) - 1)
    def _():
        wait_step(slot)


def _dispatch(pos3, pad_pos, hx_tiles, rows_total, tm):
    nt = pos3.shape[0]
    idx = jnp.concatenate([pos3, pad_pos], axis=0)
    steps = idx.shape[0]
    return pl.pallas_call(
        functools.partial(_dispatch_kernel, tm=tm, nt=nt),
        grid=(steps,),
        in_specs=[pl.BlockSpec((None, 1, TOP_K * tm), lambda i: (i, 0, 0), memory_space=pltpu.SMEM),
                  pl.BlockSpec((tm * TILE_ROWS, 128), lambda i: (jnp.minimum(i, nt - 1), 0))],
        out_specs=pl.BlockSpec(memory_space=pl.ANY),
        out_shape=jax.ShapeDtypeStruct((rows_total * TILE_ROWS, 128), U32),
        scratch_shapes=[pltpu.VMEM((2 * tm * TILE_ROWS, 128), U32), pltpu.VMEM((TILE_ROWS, 128), U32),
                        pltpu.SemaphoreType.DMA((2,))],
        compiler_params=_cparams(("arbitrary",)),
        name="moe_dispatch",
    )(idx, hx_tiles)


def _expert_kernel(blk_e_ref, xs_ref, wg_ref, wu_ref, wd_ref, y_ref, *, bm):
    del blk_e_ref
    x = _from_tiles(xs_ref, 0, bm).astype(BF16)
    hg = jnp.dot(x, wg_ref[...].astype(BF16), preferred_element_type=F32)
    hu = jnp.dot(x, wu_ref[...].astype(BF16), preferred_element_type=F32)
    act = hg * _sigmoid(hg) * hu
    _to_tiles(y_ref, jnp.dot(act.astype(BF16), wd_ref[...].astype(BF16), preferred_element_type=F32))


def _experts(xs_tiles, blk_e, wg, wu, wd, nb, bm):
    D = wg.shape[1]
    grid_spec = pltpu.PrefetchScalarGridSpec(
        num_scalar_prefetch=1,
        grid=(nb,),
        in_specs=[pl.BlockSpec((bm * TILE_ROWS, 128), lambda i, be: (i, 0)),
                  pl.BlockSpec((None, D, EXPERT_HIDDEN), lambda i, be: (be[i], 0, 0)),
                  pl.BlockSpec((None, D, EXPERT_HIDDEN), lambda i, be: (be[i], 0, 0)),
                  pl.BlockSpec((None, EXPERT_HIDDEN, D), lambda i, be: (be[i], 0, 0))],
        out_specs=pl.BlockSpec((bm * TILE_ROWS, 128), lambda i, be: (i, 0)),
    )
    return pl.pallas_call(
        functools.partial(_expert_kernel, bm=bm),
        grid_spec=grid_spec,
        out_shape=jax.ShapeDtypeStruct((nb * bm * TILE_ROWS, 128), U32),
        compiler_params=_cparams(("arbitrary",)),
        name="moe_experts",
    )(blk_e, xs_tiles, wg, wu, wd)


def _combine_kernel(cur_ref, nxt_ref, y_hbm, hx_ref, wt_ref, sgu_ref, sd_ref, x1_ref, g2_ref, fg_ref, o_ref, ybuf,
                    sem, *, tm):
    def consume(here):
        wt = wt_ref[...]
        routed = None
        for k in range(TOP_K):
            wk = jnp.broadcast_to(wt[:, k:k + 1], (tm, 128))
            wk = jnp.concatenate([wk] * (x1_ref.shape[1] // 128), axis=1)
            term = wk * _from_tiles(ybuf, here + k * tm * TILE_ROWS, tm)
            routed = term if routed is None else routed + term
        hu = jnp.dot(_from_tiles(hx_ref, 0, tm).astype(BF16), sgu_ref[...], preferred_element_type=F32)
        hg = hu[:, :EXPERT_HIDDEN]
        act = hg * _sigmoid(hg) * hu[:, EXPERT_HIDDEN:]
        shared = jnp.dot(act.astype(BF16), sd_ref[...], preferred_element_type=F32)
        xo = x1_ref[...] + g2_ref[...] * (routed + shared)
        o_ref[...] = _rms(xo) * fg_ref[...]

    _pipelined_gather(cur_ref, nxt_ref, y_hbm, ybuf, sem, TOP_K * tm, consume)


def _combine(pos3, y_tiles, hx_tiles, w_tok, sgu, sd, x1, g2, fg, tm, tiles_per_batch):
    N, D = x1.shape
    nt = N // tm
    smem_blk = lambda f: pl.BlockSpec((None, 1, TOP_K * tm), f, memory_space=pltpu.SMEM)
    row = lambda i: (i, 0)
    const = lambda i: (0, 0)
    return pl.pallas_call(
        functools.partial(_combine_kernel, tm=tm),
        grid=(nt,),
        in_specs=[smem_blk(lambda i: (i, 0, 0)),
                  smem_blk(lambda i: (jnp.minimum(i + 1, nt - 1), 0, 0)),
                  pl.BlockSpec(memory_space=pl.ANY),
                  pl.BlockSpec((tm * TILE_ROWS, 128), row),
                  pl.BlockSpec((tm, 128), row),
                  pl.BlockSpec((D, 2 * EXPERT_HIDDEN), const),
                  pl.BlockSpec((EXPERT_HIDDEN, D), const),
                  pl.BlockSpec((tm, D), row),
                  pl.BlockSpec((None, 1, D), lambda i: (i // tiles_per_batch, 0, 0)),
                  pl.BlockSpec((1, D), const)],
        out_specs=pl.BlockSpec((tm, D), row),
        out_shape=jax.ShapeDtypeStruct((N, D), F32),
        scratch_shapes=[pltpu.VMEM((2 * TOP_K * tm * TILE_ROWS, 128), U32), pltpu.SemaphoreType.DMA((2,))],
        compiler_params=_cparams(("arbitrary",)),
        name="moe_combine",
    )(pos3, pos3, y_tiles, hx_tiles, w_tok, sgu, sd, x1, g2, fg)


def _rope_tables(L):
    rows = L // GRID_W
    row = jnp.repeat(jnp.arange(rows), GRID_W).astype(F32)
    col = jnp.tile(jnp.arange(GRID_W), rows).astype(F32)
    inv = ROPE_THETA ** (-jnp.arange(0, ROPE_AXIS_DIM, 2, dtype=F32) / ROPE_AXIS_DIM)
    ang = jnp.stack([row[:, None] * inv, col[:, None] * inv], axis=1)
    c, s = jnp.cos(ang), jnp.sin(ang)
    cos_h = jnp.stack([c, c], axis=2).reshape(L, HEAD_DIM)
    sin_h = jnp.stack([-s, s], axis=2).reshape(L, HEAD_DIM)
    return jnp.tile(cos_h, (1, 2)), jnp.tile(sin_h, (1, 2))


def _tile(n, pref):
    t = min(pref, n)
    while n % t:
        t //= 2
    return t


def kernel(x, c, ctx, c_ctx, ada_w, ada_b, norm1_g, norm2_g, w_in, lam_q1, lam_k1, lam_q2, lam_k2, subln_g, hy_conv_w, hy_conv_b, filt_w1, filt_b1, filt_freq, filt_w2, filt_b2, filt_w3, hy_bias, w_branch_attn, w_branch_hyena, w_out, router_w, router_bias, exp_w_gate, exp_w_up, exp_w_down, shared_w_gate, shared_w_up, shared_w_down, final_norm_g):
    B, L, D = x.shape
    assert ada_w.shape[0] == 1 and B % 2 == 0 and (2 * L) % (FFT_SLABS * FFT_STRIP) == 0
    lam = (jnp.exp(jnp.sum(lam_q1[0].astype(F32) * lam_k1[0].astype(F32)))
           - jnp.exp(jnp.sum(lam_q2[0].astype(F32) * lam_k2[0].astype(F32))) + LAM_INIT).reshape(1, 1)

    cc = jnp.zeros((16, D), F32).at[:B].set(c).at[B].set(c_ctx)
    mod = _adaln(cc, ada_w[0], ada_b[0])
    chunk = lambda r0, r1, j: mod[r0:r1, j * D:(j + 1) * D]
    sh1, sc1, g1, sh2, sc2, g2 = (chunk(0, B, j).reshape(B, 1, D) for j in range(6))
    csh, csc = chunk(B, B + 1, 0), chunk(B, B + 1, 1)

    w_bf = w_in[0].astype(BF16)
    n1g = norm1_g[0].reshape(1, D)
    cos_t, sin_t = _rope_tables(L)
    k, v, q, u, gates = _inproj(x, sh1, sc1, n1g, w_bf, cos_t, sin_t, hy_conv_w[0], hy_conv_b[0].reshape(1, -1),
                                _tile(L, 512))
    kc, vc = _ctxproj(ctx, csh, csc, n1g, w_bf)
    attn = _attention(lam, q, k, v, kc, vc, subln_g[0].reshape(1, V_DIM), _tile(L, 2048), _tile(L, 512))

    filt = (filt_w1[0], filt_b1[0], filt_freq[0], filt_w2[0], filt_b2[0], filt_w3[0])
    hy = _hyena(u, filt, hy_bias[0])

    x1, hx2, logits_t = _merge(attn, hy, gates, x, g1, sh2, sc2, norm2_g[0].reshape(1, D),
                               w_branch_attn[0].astype(BF16), w_branch_hyena[0].astype(BF16),
                               w_out[0].astype(BF16), router_w[0].T.astype(F32), _tile(L, 1024))
    eidx, w_tok, rank, counts = _router(logits_t, router_bias[0], _tile(B * L, 1024))

    N = B * L
    tm = _tile(L, MOE_TOKENS)
    pos3, pad_pos, blk_e, nb = _dispatch_plan(eidx, rank, counts, MOE_ROWS, tm)
    xs = _dispatch(pos3, pad_pos, hx2, nb * MOE_ROWS, tm)
    y = _experts(xs, blk_e, exp_w_gate[0], exp_w_up[0], exp_w_down[0], nb, MOE_ROWS)
    sgu = jnp.concatenate([shared_w_gate[0], shared_w_up[0]], axis=-1).astype(BF16)
    out = _combine(pos3, y, hx2, w_tok, sgu, shared_w_down[0].astype(BF16), x1.reshape(N, D), g2,
                   final_norm_g.reshape(1, D), tm, L // tm)
    return out.reshape(B, L, D)
```

```python
import functools
import math

import jax
import jax.numpy as jnp
from jax import lax
from jax.experimental import pallas as pl
from jax.experimental.pallas import tpu as pltpu

F32 = jnp.float32
BF16 = jnp.bfloat16
U32 = jnp.uint32
HIGHEST = lax.Precision.HIGHEST

NORM_EPS = 1e-6
N_HEADS = 8
HEAD_DIM = 64
V_DIM = 2 * HEAD_DIM
GRID_W = 64
ROPE_THETA = 10000.0
ROPE_AXIS_DIM = HEAD_DIM // 2
HYENA_W = 1024
FILTER_EMB = 33
FILTER_BANDS = (FILTER_EMB - 1) // 2
DECAY_TARGET = 1e-2
FAST_DECAY_PCT = 0.3
SLOW_DECAY_PCT = 1.5
N_EXPERTS = 64
N_GROUPS = 8
TOPK_GROUPS = 4
TOP_K = 8
EXPERT_HIDDEN = 256
ROUTED_SCALE = 2.5
LAM_INIT = 0.8 - 0.6 * math.exp(-0.3 * 0)

V7X_VMEM_LIMIT = 56 * 1024 * 1024
LOG2E = 1.4426950408889634


def _cparams(sem):
    return pltpu.CompilerParams(dimension_semantics=sem, vmem_limit_bytes=V7X_VMEM_LIMIT)


def _sigmoid(v):
    return 1.0 / (1.0 + jnp.exp(-v))


def _rms(v):
    return v * lax.rsqrt(jnp.mean(v * v, axis=-1, keepdims=True) + NORM_EPS)


def _adaln_kernel(c_ref, w_ref, b_ref, o_ref):
    c = c_ref[...]
    s = c * _sigmoid(c)
    o_ref[...] = jnp.dot(s, w_ref[...], precision=HIGHEST, preferred_element_type=F32) + b_ref[...]


def _adaln(cc, w, b):
    rows, d = cc.shape
    n = w.shape[1]
    tn = 1536
    return pl.pallas_call(
        _adaln_kernel,
        grid=(n // tn,),
        in_specs=[pl.BlockSpec((rows, d), lambda j: (0, 0)),
                  pl.BlockSpec((d, tn), lambda j: (0, j)),
                  pl.BlockSpec((1, tn), lambda j: (0, j))],
        out_specs=pl.BlockSpec((rows, tn), lambda j: (0, j)),
        out_shape=jax.ShapeDtypeStruct((rows, n), F32),
        compiler_params=_cparams(("arbitrary",)),
        name="adaln",
    )(cc, w, b.reshape(1, n))


def _inproj_kernel(x_ref, xp_ref, xn_ref, sh_ref, sc_ref, g_ref, w_ref, cos_ref, sin_ref, cw_ref, cb_ref,
                   k_ref, v_ref, q_ref, u_ref, gt_ref, pscr, *, tm, nt, qscale):
    i = pl.program_id(1)
    g = g_ref[...]
    sh = sh_ref[...]
    sc = 1.0 + sc_ref[...]

    def norm_mod(xx):
        return (_rms(xx) * g) * sc + sh

    h = norm_mod(x_ref[...])
    hp = jnp.where(i > 0, norm_mod(xp_ref[...]), 0.0)
    hn = jnp.where(i < nt - 1, norm_mod(xn_ref[...]), 0.0)
    hb = h.astype(BF16)
    h_ext = jnp.concatenate([hp, h, hn], axis=0).astype(BF16)

    def proj(lhs, c0):
        return jnp.dot(lhs, w_ref[:, c0:c0 + 1024], preferred_element_type=F32)

    cosv = cos_ref[...]
    sinv = sin_ref[...]
    lane = lax.broadcasted_iota(jnp.int32, (1, V_DIM), 1)
    first = (lane % ROPE_AXIS_DIM) < (ROPE_AXIS_DIM // 2)

    def rope_store(a, o_ref, scale):
        for hh in range(N_HEADS):
            s = a[:, hh * V_DIM:(hh + 1) * V_DIM]
            partner = jnp.where(first, pltpu.roll(s, V_DIM - ROPE_AXIS_DIM // 2, 1),
                                pltpu.roll(s, ROPE_AXIS_DIM // 2, 1))
            r = s * cosv + partner * sinv
            if scale != 1.0:
                r = r * scale
            o_ref[:, hh * V_DIM:(hh + 1) * V_DIM] = r.astype(BF16)

    rope_store(proj(hb, 0), k_ref, 1.0)
    v_ref[...] = proj(hb, 1024).astype(BF16)
    rope_store(proj(hb, 2048), q_ref, qscale)
    for j in range(3):
        pscr[...] = proj(h_ext, 3072 + j * 1024)
        cw = cw_ref[:, j * 1024:(j + 1) * 1024]
        cb = cb_ref[:, j * 1024:(j + 1) * 1024]
        u = (pscr[pl.ds(7, tm), :] * cw[0:1] + pscr[pl.ds(8, tm), :] * cw[1:2]
             + pscr[pl.ds(9, tm), :] * cw[2:3] + cb)
        u_ref[:, j * 1024:(j + 1) * 1024] = u.astype(BF16)
    for j in range(2):
        gt_ref[:, j * 1024:(j + 1) * 1024] = proj(hb, 6144 + j * 1024).astype(BF16)


def _inproj(x, shift, scale, g, w_bf, cos_t, sin_t, conv_w, conv_b, tm):
    B, L, D = x.shape
    nt = L // tm
    nb8 = L // 8
    qscale = (HEAD_DIM ** -0.5) * LOG2E
    kern = functools.partial(_inproj_kernel, tm=tm, nt=nt, qscale=qscale)
    row = lambda b, i: (b, i, 0)
    per_b = lambda b, i: (b, 0, 0)
    const = lambda b, i: (0, 0)
    outs = pl.pallas_call(
        kern,
        grid=(B, nt),
        in_specs=[
            pl.BlockSpec((None, tm, D), row),
            pl.BlockSpec((None, 8, D), lambda b, i: (b, jnp.maximum(i * (tm // 8) - 1, 0), 0)),
            pl.BlockSpec((None, 8, D), lambda b, i: (b, jnp.minimum((i + 1) * (tm // 8), nb8 - 1), 0)),
            pl.BlockSpec((None, 1, D), per_b),
            pl.BlockSpec((None, 1, D), per_b),
            pl.BlockSpec((1, D), const),
            pl.BlockSpec((D, 8192), const, pipeline_mode=pl.Buffered(1)),
            pl.BlockSpec((tm, V_DIM), lambda b, i: (i, 0)),
            pl.BlockSpec((tm, V_DIM), lambda b, i: (i, 0)),
            pl.BlockSpec((3, 3 * HYENA_W), const),
            pl.BlockSpec((1, 3 * HYENA_W), const),
        ],
        out_specs=[
            pl.BlockSpec((None, tm, 1024), row),
            pl.BlockSpec((None, tm, 1024), row),
            pl.BlockSpec((None, tm, 1024), row),
            pl.BlockSpec((None, tm, 3 * HYENA_W), row),
            pl.BlockSpec((None, tm, 2048), row),
        ],
        out_shape=[
            jax.ShapeDtypeStruct((B, L, 1024), BF16),
            jax.ShapeDtypeStruct((B, L, 1024), BF16),
            jax.ShapeDtypeStruct((B, L, 1024), BF16),
            jax.ShapeDtypeStruct((B, L, 3 * HYENA_W), BF16),
            jax.ShapeDtypeStruct((B, L, 2048), BF16),
        ],
        scratch_shapes=[pltpu.VMEM((tm + 16, 1024), F32)],
        compiler_params=_cparams(("parallel", "arbitrary")),
        name="inproj",
    )(x, x, x, shift, scale, g, w_bf, cos_t, sin_t, conv_w, conv_b)
    return outs


def _ctxproj_kernel(x_ref, sh_ref, sc_ref, g_ref, w_ref, k_ref, v_ref):
    h = ((_rms(x_ref[...]) * g_ref[...]) * (1.0 + sc_ref[...]) + sh_ref[...]).astype(BF16)
    k_ref[...] = jnp.dot(h, w_ref[:, 0:1024], preferred_element_type=F32).astype(BF16)
    v_ref[...] = jnp.dot(h, w_ref[:, 1024:2048], preferred_element_type=F32).astype(BF16)


def _ctxproj(ctx, shift, scale, g, w_bf):
    B, Lc, D = ctx.shape
    return pl.pallas_call(
        _ctxproj_kernel,
        grid=(B,),
        in_specs=[pl.BlockSpec((None, Lc, D), lambda b: (b, 0, 0)),
                  pl.BlockSpec((1, D), lambda b: (0, 0)),
                  pl.BlockSpec((1, D), lambda b: (0, 0)),
                  pl.BlockSpec((1, D), lambda b: (0, 0)),
                  pl.BlockSpec((D, 2048), lambda b: (0, 0))],
        out_specs=[pl.BlockSpec((None, Lc, 1024), lambda b: (b, 0, 0)),
                   pl.BlockSpec((None, Lc, 1024), lambda b: (b, 0, 0))],
        out_shape=[jax.ShapeDtypeStruct((B, Lc, 1024), BF16),
                   jax.ShapeDtypeStruct((B, Lc, 1024), BF16)],
        compiler_params=_cparams(("arbitrary",)),
        name="ctxproj",
    )(ctx, shift, scale, g, w_bf)


def _attn_kernel(lam_ref, q_ref, k_ref, v_ref, kc_ref, vc_ref, g_ref, o_ref, m_s, acc_s, *, tk, nk):
    q = q_ref[...]
    lane = lax.broadcasted_iota(jnp.int32, (1, V_DIM), 1)
    zero = jnp.zeros_like(q)
    qs = (jnp.where(lane < HEAD_DIM, q, zero), jnp.where(lane >= HEAD_DIM, q, zero))
    def chunk(kk, vv, first=False):
        reps = kk.shape[0] // V_DIM
        v1 = jnp.concatenate([vv, jnp.ones_like(vv)], axis=1)
        for m in range(2):
            s = lax.dot_general(qs[m], kk, (((1,), (1,)), ((), ())), preferred_element_type=F32)
            m_new = jnp.broadcast_to(jnp.max(s, axis=-1, keepdims=True), m_s.shape[1:])
            if not first:
                m_prev = m_s[m]
                m_new = jnp.maximum(m_prev, m_new)
                alpha = jnp.exp2(m_prev - m_new)
            p = jnp.exp2(s - jnp.concatenate([m_new] * reps, axis=1))
            pv = jnp.dot(p.astype(BF16), v1, preferred_element_type=F32)
            acc_s[m] = pv if first else jnp.concatenate([alpha, alpha], axis=1) * acc_s[m] + pv
            m_s[m] = m_new

    def body(j, carry):
        off = pl.multiple_of(j * tk, tk)
        chunk(k_ref[pl.ds(off, tk), :], v_ref[pl.ds(off, tk), :])
        return carry

    chunk(kc_ref[...], vc_ref[...], first=True)
    lax.fori_loop(0, nk, body, 0, unroll=2 if nk % 2 == 0 else 1)
    lam = lam_ref[0, 0]
    a0, a1 = acc_s[0], acc_s[1]
    o = a0[:, :V_DIM] / a0[:, V_DIM:] - lam * (a1[:, :V_DIM] / a1[:, V_DIM:])
    o = _rms(o) * g_ref[...] * (1.0 - LAM_INIT)
    o_ref[...] = o.astype(BF16)


def _attention(lam, q, k, v, kc, vc, subln_g, tq, tk):
    B, L, _ = q.shape
    Lc = kc.shape[1]
    kern = functools.partial(_attn_kernel, tk=tk, nk=L // tk)
    kv = lambda b, h, i: (b, 0, h)
    return pl.pallas_call(
        kern,
        grid=(B, N_HEADS, L // tq),
        in_specs=[
            pl.BlockSpec(memory_space=pltpu.SMEM),
            pl.BlockSpec((None, tq, V_DIM), lambda b, h, i: (b, i, h)),
            pl.BlockSpec((None, L, V_DIM), kv),
            pl.BlockSpec((None, L, V_DIM), kv),
            pl.BlockSpec((None, Lc, V_DIM), kv),
            pl.BlockSpec((None, Lc, V_DIM), kv),
            pl.BlockSpec((1, V_DIM), lambda b, h, i: (0, 0)),
        ],
        out_specs=pl.BlockSpec((None, tq, V_DIM), lambda b, h, i: (b, i, h)),
        out_shape=jax.ShapeDtypeStruct((B, L, N_HEADS * V_DIM), BF16),
        scratch_shapes=[pltpu.VMEM((2, tq, V_DIM), F32), pltpu.VMEM((2, tq, 2 * V_DIM), F32)],
        compiler_params=_cparams(("parallel", "parallel", "arbitrary")),
        name="diffattn",
    )(lam, q, k, v, kc, vc, subln_g)


FFT_SLABS = 64
FFT_STRIP = 16


def _dft_tables(nb):
    N = FFT_SLABS * nb
    fa = jnp.arange(FFT_SLABS, dtype=jnp.int32)[:, None, None]
    fb = jnp.arange(nb, dtype=jnp.int32)[None, :, None]
    b = jnp.arange(nb, dtype=jnp.int32)[None, None, :]
    ang = ((FFT_SLABS * b * fb + b * fa) % N).astype(F32) * (2.0 * math.pi / N)
    c, s = jnp.cos(ang), jnp.sin(ang)
    fwd = jnp.concatenate([jnp.concatenate([c, s], -1), jnp.concatenate([-s, c], -1)], axis=1)
    inv = jnp.swapaxes(fwd, 1, 2) / N
    return fwd.astype(BF16), inv.astype(BF16)


def _vadd(a, b):
    return b if a is None else (a if b is None else a + b)


def _vsub(a, b):
    if b is None:
        return a
    return -b if a is None else a - b


def _vscale(a, k):
    if a is None or k == 1.0:
        return a
    return -a if k == -1.0 else a * k


def _cmulc(x, c, s):
    re, im = x
    eps = 1e-9
    if abs(s) < eps:
        k = 1.0 if c > 0 else -1.0
        return _vscale(re, k), _vscale(im, k)
    if abs(c) < eps:
        k = 1.0 if s > 0 else -1.0
        return _vscale(im, -k), _vscale(re, k)
    if abs(abs(c) - abs(s)) < eps:
        kc = 1.0 if c > 0 else -1.0
        ks = 1.0 if s > 0 else -1.0
        return (_vscale(_vsub(_vscale(re, kc), _vscale(im, ks)), abs(c)),
                _vscale(_vadd(_vscale(re, ks), _vscale(im, kc)), abs(c)))
    return _vsub(_vscale(re, c), _vscale(im, s)), _vadd(_vscale(re, s), _vscale(im, c))


def _fft(xs, sign):
    n = len(xs)
    if n == 1:
        return xs
    ev, od = _fft(xs[0::2], sign), _fft(xs[1::2], sign)
    out = [None] * n
    for k in range(n // 2):
        ang = sign * 2.0 * math.pi * k / n
        t = _cmulc(od[k], math.cos(ang), math.sin(ang))
        out[k] = (_vadd(ev[k][0], t[0]), _vadd(ev[k][1], t[1]))
        out[k + n // 2] = (_vsub(ev[k][0], t[0]), _vsub(ev[k][1], t[1]))
    return out


def _put(ref, slot, rows, val):
    ref[slot, rows, :] = jnp.zeros((FFT_STRIP, ref.shape[2]), F32) if val is None else val


def _slab_fft_forward(load, s_ref, nb, n_in):
    def strip(i, carry):
        r0 = pl.multiple_of(i * FFT_STRIP, FFT_STRIP)
        rows_re, rows_im = pl.ds(r0, FFT_STRIP), pl.ds(nb + r0, FFT_STRIP)
        for a0 in range(8):
            xs = [load(8 * a1 + a0, r0) if a1 < n_in else (None, None) for a1 in range(8)]
            ys = _fft(xs, -1.0)
            for f1 in range(8):
                ang = -2.0 * math.pi * a0 * f1 / FFT_SLABS
                yr, yi = _cmulc(ys[f1], math.cos(ang), math.sin(ang))
                _put(s_ref, a0 * 8 + f1, rows_re, yr)
                _put(s_ref, a0 * 8 + f1, rows_im, yi)
        for f1 in range(8):
            xs = [(s_ref[a0 * 8 + f1, rows_re, :], s_ref[a0 * 8 + f1, rows_im, :]) for a0 in range(8)]
            ys = _fft(xs, -1.0)
            for f0 in range(8):
                s_ref[f1 + 8 * f0, rows_re, :] = ys[f0][0]
                s_ref[f1 + 8 * f0, rows_im, :] = ys[f0][1]
        return carry

    lax.fori_loop(0, nb // FFT_STRIP, strip, 0)


def _slab_fft_inverse(s_ref, nb, emit):
    def strip(i, carry):
        r0 = pl.multiple_of(i * FFT_STRIP, FFT_STRIP)
        rows_re, rows_im = pl.ds(r0, FFT_STRIP), pl.ds(nb + r0, FFT_STRIP)
        for q in range(8):
            xs = [(s_ref[8 * p + q, rows_re, :], s_ref[8 * p + q, rows_im, :]) for p in range(8)]
            us = _fft(xs, 1.0)
            for r in range(8):
                ang = 2.0 * math.pi * r * q / FFT_SLABS
                ur, ui = _cmulc(us[r], math.cos(ang), math.sin(ang))
                s_ref[8 * r + q, rows_re, :] = ur
                s_ref[8 * r + q, rows_im, :] = ui
        for r in range(8):
            xs = [(s_ref[8 * r + q, rows_re, :], s_ref[8 * r + q, rows_im, :]) for q in range(8)]
            ys = _fft(xs, 1.0)
            for sidx in range(4):
                emit(r + 8 * sidx, r0, ys[sidx][0], ys[sidx][1])
        return carry

    lax.fori_loop(0, nb // FFT_STRIP, strip, 0)


def _hyena_conv_kernel(v_ref, x_ref, k_ref, ff_ref, fi_ref, bias_ref, o_ref, s_ref, *, nb):
    def load(a, r0):
        rows = pl.ds(pl.multiple_of(a * nb + r0, FFT_STRIP), FFT_STRIP)
        return v_ref[0, rows, :].astype(F32), v_ref[1, rows, :].astype(F32)

    _slab_fft_forward(load, s_ref, nb, FFT_SLABS // 16)

    def freq(f, carry):
        z = jnp.dot(ff_ref[f], s_ref[f].astype(BF16), preferred_element_type=F32)
        zr, zi = z[:nb], z[nb:]
        kr, ki = k_ref[0, f], k_ref[1, f]
        y = jnp.concatenate([zr * kr - zi * ki, zr * ki + zi * kr], axis=0).astype(BF16)
        s_ref[f] = jnp.dot(fi_ref[f], y, preferred_element_type=F32)
        return carry

    lax.fori_loop(0, FFT_SLABS, freq, 0, unroll=8)
    bias = bias_ref[...]

    def emit(a, r0, re, im):
        rows = pl.ds(pl.multiple_of(a * nb + r0, FFT_STRIP), FFT_STRIP)
        for half, y in enumerate((re, im)):
            v = v_ref[half, rows, :].astype(F32)
            o_ref[half, rows, :] = (x_ref[half, rows, :].astype(F32) * (y + v * bias)).astype(BF16)

    _slab_fft_inverse(s_ref, nb, emit)


def _hyena_conv(uv, zv, k_hat, ff, fi, bias, order, in_col, gate_col, nb):
    _, P, L, _ = uv.shape
    W = HYENA_W
    nct = W // 128
    src, src_off = (uv, in_col * nct) if zv is None else (zv, 0)
    blk = (2, None, L, 128)
    return pl.pallas_call(
        functools.partial(_hyena_conv_kernel, nb=nb),
        grid=(nct, P),
        in_specs=[pl.BlockSpec(blk, lambda c, p: (0, p, 0, src_off + c)),
                  pl.BlockSpec(blk, lambda c, p: (0, p, 0, gate_col * nct + c)),
                  pl.BlockSpec((2, FFT_SLABS, nb, 128), lambda c, p: (0, 0, 0, order * nct + c),
                               pipeline_mode=pl.Buffered(1)),
                  pl.BlockSpec((FFT_SLABS, 2 * nb, 2 * nb), lambda c, p: (0, 0, 0), pipeline_mode=pl.Buffered(1)),
                  pl.BlockSpec((FFT_SLABS, 2 * nb, 2 * nb), lambda c, p: (0, 0, 0), pipeline_mode=pl.Buffered(1)),
                  pl.BlockSpec((1, 128), lambda c, p: (0, c))],
        out_specs=pl.BlockSpec(blk, lambda c, p: (0, p, 0, c)),
        out_shape=jax.ShapeDtypeStruct((2, P, L, W), BF16),
        scratch_shapes=[pltpu.VMEM((FFT_SLABS, 2 * nb, 128), F32)],
        compiler_params=_cparams(("parallel", "arbitrary")),
        name="hyena_conv%d" % order,
    )(src, uv, k_hat, ff, fi, bias)


def _filter_features(n):
    pos = jnp.arange(n, dtype=F32)[:, None]
    t = jnp.linspace(0.0, 1.0, n, dtype=F32)[:, None]
    w = 2 * math.pi * pos / n
    bands = jnp.linspace(1e-4, FILTER_BANDS - 1, FILTER_BANDS, dtype=F32)
    z = jnp.concatenate([t, jnp.cos(bands * w), -jnp.sin(bands * w)], axis=-1)
    z = jnp.pad(z, ((0, 0), (0, 128 - FILTER_EMB)))
    tap = jnp.arange(2 * n)
    return z[jnp.where(tap < n, tap, (2 * n - tap) % n)]


def _filt_taps_kernel(z_ref, w1_ref, b1_ref, fr_ref, w2_ref, b2_ref, w3_ref, dl_ref, k_ref, sum_ref):
    a = pl.program_id(0)
    z = z_ref[...]
    fr = fr_ref[...]
    hd = jnp.sin(fr * (jnp.dot(z, w1_ref[...], precision=HIGHEST, preferred_element_type=F32) + b1_ref[...]))
    hd = jnp.sin(fr * (jnp.dot(hd, w2_ref[...], precision=HIGHEST, preferred_element_type=F32) + b2_ref[...]))
    h = jnp.dot(hd, w3_ref[...], precision=HIGHEST, preferred_element_type=F32) * jnp.exp(-z[:, 0:1] * dl_ref[...])
    rows = lax.broadcasted_iota(jnp.int32, (z.shape[0], 1), 0)
    h = jnp.where(jnp.logical_and(a == FFT_SLABS // 2, rows == 0), 0.0, h)
    k_ref[...] = h

    @pl.when(a == 0)
    def _():
        sum_ref[...] = jnp.zeros_like(sum_ref)

    sum_ref[0:1, :] += jnp.sum(jnp.abs(h), axis=0, keepdims=True)


def _filt_spectrum_kernel(t_ref, inv_ref, ff_ref, k_ref, s_ref, *, nb):
    def load(a, r0):
        return t_ref[pl.ds(pl.multiple_of(a * nb + r0, FFT_STRIP), FFT_STRIP), :], None

    _slab_fft_forward(load, s_ref, nb, FFT_SLABS // 8)
    inv = inv_ref[...]

    def freq(f, carry):
        z = jnp.dot(ff_ref[f], s_ref[f].astype(BF16), preferred_element_type=F32)
        k_ref[0, f] = z[:nb] * inv
        k_ref[1, f] = z[nb:] * inv
        return carry

    lax.fori_loop(0, FFT_SLABS, freq, 0, unroll=8)


def _hyena_filters(n, filt, ff, nb):
    w1, b1, freq, w2, b2, w3 = filt
    W = HYENA_W
    hid = w1.shape[1]
    z = _filter_features(n)
    w1p = jnp.pad(w1.astype(F32), ((0, 128 - FILTER_EMB), (0, 0)))
    w3d = w3.astype(F32).reshape(hid, 2, 2, W)
    w3sel = jnp.stack([w3d[:, :, d, :].reshape(hid, 2 * W) for d in range(2)])
    deltas = jnp.abs(jnp.linspace(math.log(DECAY_TARGET) / SLOW_DECAY_PCT,
                                  math.log(DECAY_TARGET) / FAST_DECAY_PCT, W, dtype=F32))
    deltas = jnp.tile(deltas.reshape(1, W), (1, 2))
    const = lambda a: (0, 0)
    taps, sums = pl.pallas_call(
        _filt_taps_kernel,
        grid=(FFT_SLABS,),
        in_specs=[pl.BlockSpec((nb, 128), lambda a: (a, 0)),
                  pl.BlockSpec((128, hid), const), pl.BlockSpec((1, hid), const), pl.BlockSpec((1, hid), const),
                  pl.BlockSpec((hid, hid), const), pl.BlockSpec((1, hid), const),
                  pl.BlockSpec((None, hid, 2 * W), lambda a: (a // (FFT_SLABS // 2), 0, 0)),
                  pl.BlockSpec((1, 2 * W), const)],
        out_specs=[pl.BlockSpec((nb, 2 * W), lambda a: (a, 0)), pl.BlockSpec((8, 2 * W), const)],
        out_shape=[jax.ShapeDtypeStruct((2 * n, 2 * W), F32), jax.ShapeDtypeStruct((8, 2 * W), F32)],
        compiler_params=_cparams(("arbitrary",)),
        name="hyena_filter_taps",
    )(z, w1p, b1.reshape(1, hid), freq.reshape(1, hid), w2, b2.reshape(1, hid), w3sel, deltas)
    inv = 1.0 / sums[0:1]
    return pl.pallas_call(
        functools.partial(_filt_spectrum_kernel, nb=nb),
        grid=(2 * W // 128,),
        in_specs=[pl.BlockSpec((2 * n, 128), lambda j: (0, j)),
                  pl.BlockSpec((1, 128), lambda j: (0, j)),
                  pl.BlockSpec((FFT_SLABS, 2 * nb, 2 * nb), lambda j: (0, 0, 0), pipeline_mode=pl.Buffered(1))],
        out_specs=pl.BlockSpec((2, FFT_SLABS, nb, 128), lambda j: (0, 0, 0, j)),
        out_shape=jax.ShapeDtypeStruct((2, FFT_SLABS, nb, 2 * W), F32),
        scratch_shapes=[pltpu.VMEM((FFT_SLABS, 2 * nb, 128), F32)],
        compiler_params=_cparams(("parallel",)),
        name="hyena_filter_spectrum",
    )(taps, inv, ff)


def _hyena(u, filt, hy_bias):
    B, L, _ = u.shape
    P = B // 2
    nb = 2 * L // FFT_SLABS
    ff, fi = _dft_tables(nb)
    k_hat = _hyena_filters(L, filt, ff, nb)
    uv = u.reshape(2, P, L, 3 * HYENA_W)
    bias = hy_bias.astype(F32)
    z = _hyena_conv(uv, None, k_hat, ff, fi, bias[0:1], 0, 0, 1, nb)
    y = _hyena_conv(uv, z, k_hat, ff, fi, bias[1:2], 1, 0, 2, nb)
    return y.reshape(B, L, HYENA_W)


def _merge_kernel(a_ref, h_ref, gt_ref, x_ref, g1_ref, sh_ref, sc_ref, n2g_ref, wpa_ref, wph_ref, wo_ref, rw_ref,
                  x1_ref, hx_ref, lg_ref):
    ya = jnp.dot(a_ref[...], wpa_ref[...], preferred_element_type=F32)
    yh = jnp.dot(h_ref[...], wph_ref[...], preferred_element_type=F32)
    ga = gt_ref[:, 0:1024].astype(F32)
    gh = gt_ref[:, 1024:2048].astype(F32)
    y = _sigmoid(ga) * ya + _sigmoid(gh) * yh
    mix = jnp.dot(y.astype(BF16), wo_ref[...], preferred_element_type=F32)
    x1 = x_ref[...] + g1_ref[...] * mix
    x1_ref[...] = x1
    hx = (_rms(x1) * n2g_ref[...]) * (1.0 + sc_ref[...]) + sh_ref[...]
    _to_tiles(hx_ref, hx)
    lg_ref[...] = lax.dot_general(rw_ref[...], hx, (((1,), (1,)), ((), ())), precision=HIGHEST,
                                  preferred_element_type=F32)


def _merge(attn, hy, gates, x, g1, sh2, sc2, n2g, wpa, wph, wo, rwT, tm):
    B, L, D = x.shape
    nt = L // tm
    row = lambda b, i: (b, i, 0)
    per_b = lambda b, i: (b, 0, 0)
    const = lambda b, i: (0, 0)
    return pl.pallas_call(
        _merge_kernel,
        grid=(B, nt),
        in_specs=[pl.BlockSpec((None, tm, 1024), row), pl.BlockSpec((None, tm, 1024), row),
                  pl.BlockSpec((None, tm, 2048), row), pl.BlockSpec((None, tm, D), row),
                  pl.BlockSpec((None, 1, D), per_b), pl.BlockSpec((None, 1, D), per_b),
                  pl.BlockSpec((None, 1, D), per_b), pl.BlockSpec((1, D), const),
                  pl.BlockSpec((1024, D), const), pl.BlockSpec((1024, D), const), pl.BlockSpec((D, D), const),
                  pl.BlockSpec((N_EXPERTS, D), const)],
        out_specs=[pl.BlockSpec((None, tm, D), row),
                   pl.BlockSpec((tm * TILE_ROWS, 128), lambda b, i: (b * nt + i, 0)),
                   pl.BlockSpec((N_EXPERTS, tm), lambda b, i: (0, b * nt + i))],
        out_shape=[jax.ShapeDtypeStruct((B, L, D), F32),
                   jax.ShapeDtypeStruct((B * L * TILE_ROWS, 128), U32),
                   jax.ShapeDtypeStruct((N_EXPERTS, B * L), F32)],
        compiler_params=_cparams(("parallel", "arbitrary")),
        name="merge",
    )(attn, hy, gates, x, g1, sh2, sc2, n2g, wpa, wph, wo, rwT)


def _router_kernel(lg_ref, rb_ref, tri_ref, ei_ref, wt_ref, rk_ref, cnt_ref):
    scores = _sigmoid(lg_ref[...])
    choice = scores + rb_ref[...]
    E, T = scores.shape
    gsz = E // N_GROUPS
    neg = jnp.float32(-jnp.inf)
    big = jnp.int32(1 << 30)
    rows8 = lax.broadcasted_iota(jnp.int32, (gsz, T), 0)
    gscore = []
    for gi in range(N_GROUPS):
        c = choice[gi * gsz:(gi + 1) * gsz]
        m1 = jnp.max(c, axis=0, keepdims=True)
        i1 = jnp.min(jnp.where(c == m1, rows8, big), axis=0, keepdims=True)
        m2 = jnp.max(jnp.where(rows8 == i1, neg, c), axis=0, keepdims=True)
        gscore.append(m1 + m2)
    gs = jnp.concatenate(gscore, axis=0)
    rowsg = lax.broadcasted_iota(jnp.int32, (N_GROUPS, T), 0)
    gsel = jnp.zeros((N_GROUPS, T), jnp.bool_)
    for _ in range(TOPK_GROUPS):
        m = jnp.max(gs, axis=0, keepdims=True)
        ix = jnp.min(jnp.where(gs == m, rowsg, big), axis=0, keepdims=True)
        hit = rowsg == ix
        gsel = jnp.logical_or(gsel, hit)
        gs = jnp.where(hit, neg, gs)
    masked = jnp.concatenate(
        [jnp.where(gsel[gi:gi + 1], choice[gi * gsz:(gi + 1) * gsz], neg) for gi in range(N_GROUPS)], axis=0)
    rows = lax.broadcasted_iota(jnp.int32, (E, T), 0)
    ids, ws, hits = [], [], []
    sel = jnp.zeros((E, T), F32)
    for _ in range(TOP_K):
        m = jnp.max(masked, axis=0, keepdims=True)
        ix = jnp.min(jnp.where(masked == m, rows, big), axis=0, keepdims=True)
        hit = rows == ix
        ids.append(ix)
        hits.append(hit)
        ws.append(jnp.sum(jnp.where(hit, scores, 0.0), axis=0, keepdims=True))
        sel = jnp.where(hit, 1.0, sel)
        masked = jnp.where(hit, neg, masked)
    w = jnp.concatenate(ws, axis=0)
    w = w / jnp.sum(w, axis=0, keepdims=True) * ROUTED_SCALE
    ei_ref[...] = jnp.concatenate(ids, axis=0)
    wt_ref[...] = jnp.concatenate([w, jnp.zeros((128 - TOP_K, T), F32)], axis=0).T

    @pl.when(pl.program_id(0) == 0)
    def _():
        cnt_ref[...] = jnp.zeros_like(cnt_ref)

    before = cnt_ref[...][:, 0:1]
    csum = jnp.dot(sel.astype(BF16), tri_ref[...], preferred_element_type=F32)
    rank = before + csum - sel
    rk_ref[...] = jnp.concatenate(
        [jnp.sum(jnp.where(h, rank, 0.0), axis=0, keepdims=True) for h in hits], axis=0).astype(jnp.int32)
    cnt_ref[...] = jnp.broadcast_to(before + csum[:, T - 1:T], cnt_ref.shape)


def _router(logits_t, router_bias, tt):
    E, N = logits_t.shape
    tri = jnp.triu(jnp.ones((tt, tt), BF16))
    per_tok = pl.BlockSpec((TOP_K, tt), lambda i: (0, i))
    return pl.pallas_call(
        _router_kernel,
        grid=(N // tt,),
        in_specs=[pl.BlockSpec((E, tt), lambda i: (0, i)), pl.BlockSpec((E, 1), lambda i: (0, 0)),
                  pl.BlockSpec((tt, tt), lambda i: (0, 0))],
        out_specs=[per_tok, pl.BlockSpec((tt, 128), lambda i: (i, 0)), per_tok,
                   pl.BlockSpec((E, 128), lambda i: (0, 0))],
        out_shape=[jax.ShapeDtypeStruct((TOP_K, N), jnp.int32), jax.ShapeDtypeStruct((N, 128), F32),
                   jax.ShapeDtypeStruct((TOP_K, N), jnp.int32), jax.ShapeDtypeStruct((E, 128), F32)],
        compiler_params=_cparams(("arbitrary",)),
        name="router",
    )(logits_t, router_bias.reshape(E, 1).astype(F32), tri)


MOE_ROWS = 512
MOE_TOKENS = 128


def _plan_kernel(ei_ref, rk_ref, ps_ref, pos_ref):
    ei = ei_ref[...]
    ps = ps_ref[...]
    rows = lax.broadcasted_iota(jnp.int32, (ps.shape[0], ei.shape[1]), 0)
    start = [jnp.sum(jnp.where(rows == ei[k:k + 1], ps, 0.0), axis=0, keepdims=True) for k in range(ei.shape[0])]
    pos_ref[...] = rk_ref[...] + jnp.concatenate(start, axis=0).astype(jnp.int32)


def _dispatch_plan(eidx, rank, counts, bm, tm):
    K, N = eidx.shape
    E = N_EXPERTS
    cnt = counts[:, 0].astype(jnp.int32)
    padded = (cnt + bm - 1) // bm * bm
    pad_end = jnp.cumsum(padded)
    pad_start = pad_end - padded
    nb = -(-(N * K + E * (bm - 1)) // bm)
    tt = _tile(N, 2048)
    per_tok = pl.BlockSpec((K, tt), lambda i: (0, i))
    pos = pl.pallas_call(
        _plan_kernel,
        grid=(N // tt,),
        in_specs=[per_tok, per_tok, pl.BlockSpec((E, 1), lambda i: (0, 0))],
        out_specs=per_tok,
        out_shape=jax.ShapeDtypeStruct((K, N), jnp.int32),
        compiler_params=_cparams(("parallel",)),
        name="moe_plan",
    )(eidx, rank, pad_start.astype(F32).reshape(E, 1))
    blk_start = jnp.arange(nb, dtype=jnp.int32) * bm
    blk_e = jnp.minimum(jnp.sum(blk_start[:, None] >= pad_end[None, :], axis=1), E - 1)
    n_pad = nb * bm - N * K
    assert n_pad % (K * tm) == 0
    cpad = jnp.cumsum(padded - cnt)
    q = jnp.arange(n_pad, dtype=jnp.int32)
    seg = jnp.sum(q[:, None] >= cpad[None, :], axis=1)
    onehot = seg[:, None] == jnp.arange(E + 1, dtype=jnp.int32)[None, :]
    first = jnp.concatenate([pad_start + cnt, pad_end[-1:]])
    skipped = jnp.concatenate([jnp.zeros((1,), cpad.dtype), cpad])
    pad_pos = (q + jnp.sum(jnp.where(onehot, (first - skipped)[None, :], 0), axis=1)).astype(jnp.int32)
    nt = N // tm
    pos3 = pos.reshape(K, nt, tm).transpose(1, 0, 2).reshape(nt, 1, K * tm)
    return pos3, pad_pos.reshape(-1, 1, K * tm), blk_e.astype(jnp.int32), nb


TILE_ROWS = 4
_HI16 = 0xFFFF0000


def _bf16_bits(v):
    return lax.bitcast_convert_type(v.astype(BF16).astype(F32), U32)


def _to_tiles(ref, val):
    m, half = val.shape[0], val.shape[1] // 2
    words = (lax.shift_right_logical(_bf16_bits(val[:, :half]), jnp.uint32(16))
             | (_bf16_bits(val[:, half:]) & jnp.uint32(_HI16)))
    for j in range(TILE_ROWS):
        ref[pl.ds(j, m, stride=TILE_ROWS), :] = words[:, j * 128:(j + 1) * 128]


def _from_tiles(ref, start, m):
    words = jnp.concatenate([ref[pl.ds(start + j, m, stride=TILE_ROWS), :] for j in range(TILE_ROWS)], axis=1)
    lo = lax.bitcast_convert_type(lax.shift_left(words, jnp.uint32(16)), F32)
    hi = lax.bitcast_convert_type(words & jnp.uint32(_HI16), F32)
    return jnp.concatenate([lo, hi], axis=1)


def _row_copy(idx_ref, r, src_hbm, buf, base, sem):
    src = pl.multiple_of(idx_ref[0, r] * TILE_ROWS, TILE_ROWS)
    dst = base + r * TILE_ROWS
    if not isinstance(dst, int):
        dst = pl.multiple_of(dst, TILE_ROWS)
    return pltpu.make_async_copy(src_hbm.at[pl.ds(src, TILE_ROWS), :], buf.at[pl.ds(dst, TILE_ROWS), :], sem)


def _pipelined_gather(cur_ref, nxt_ref, src_hbm, buf, sem, nrows, consume, parts):
    i = pl.program_id(0)
    span = nrows * TILE_ROWS

    def wait(slot):
        pltpu.make_async_copy(src_hbm.at[pl.ds(0, span), :], buf.at[pl.ds(slot * span, span), :], sem.at[slot]).wait()

    @pl.when(i == 0)
    def _():
        def body(r, carry):
            _row_copy(cur_ref, r, src_hbm, buf, 0, sem.at[0]).start()
            return carry
        lax.fori_loop(0, nrows, body, 0)

    for slot in range(2):
        @pl.when(i % 2 == slot)
        def _():
            def start_part(c):
                for r in range(c * nrows // parts, (c + 1) * nrows // parts):
                    _row_copy(nxt_ref, r, src_hbm, buf, (1 - slot) * span, sem.at[1 - slot]).start(priority=r % 2)

            wait(slot)
            consume(slot * span, start_part)

            @pl.when(i == pl.num_programs(0) - 1)
            def _():
                wait(1 - slot)


def _dispatch_kernel(pos_ref, hx_ref, xs_hbm, stage, zero_tile, sem, *, tm, nt):
    i = pl.program_id(0)
    slot = i % 2
    span = tm * TILE_ROWS

    def wait_step(s):
        for _ in range(TOP_K):
            pltpu.make_async_copy(stage.at[pl.ds(0, span), :], xs_hbm.at[pl.ds(0, span), :], sem.at[s]).wait()

    def dst(r):
        return xs_hbm.at[pl.ds(pl.multiple_of(pos_ref[0, r] * TILE_ROWS, TILE_ROWS), TILE_ROWS), :]

    @pl.when(i < nt)
    def _():
        base = pl.multiple_of(slot * span, span)
        stage[pl.ds(base, span), :] = hx_ref[...]
        for r in range(TOP_K * tm):
            src = stage.at[pl.ds(pl.multiple_of(base + (r % tm) * TILE_ROWS, TILE_ROWS), TILE_ROWS), :]
            pltpu.make_async_copy(src, dst(r), sem.at[slot]).start(priority=r % 2)

    @pl.when(i == 0)
    def _():
        zero_tile[...] = jnp.zeros_like(zero_tile)

    @pl.when(i >= nt)
    def _():
        for r in range(TOP_K * tm):
            pltpu.make_async_copy(zero_tile, dst(r), sem.at[slot]).start(priority=r % 2)

    @pl.when(i > 0)
    def _():
        wait_step(1 - slot)

    @pl.when(i == pl.num_programs(0) - 1)
    def _():
        wait_step(slot)


def _dispatch(pos3, pad_pos, hx_tiles, rows_total, tm):
    nt = pos3.shape[0]
    idx = jnp.concatenate([pos3, pad_pos], axis=0)
    steps = idx.shape[0]
    return pl.pallas_call(
        functools.partial(_dispatch_kernel, tm=tm, nt=nt),
        grid=(steps,),
        in_specs=[pl.BlockSpec((None, 1, TOP_K * tm), lambda i: (i, 0, 0), memory_space=pltpu.SMEM),
                  pl.BlockSpec((tm * TILE_ROWS, 128), lambda i: (jnp.minimum(i, nt - 1), 0))],
        out_specs=pl.BlockSpec(memory_space=pl.ANY),
        out_shape=jax.ShapeDtypeStruct((rows_total * TILE_ROWS, 128), U32),
        scratch_shapes=[pltpu.VMEM((2 * tm * TILE_ROWS, 128), U32), pltpu.VMEM((TILE_ROWS, 128), U32),
                        pltpu.SemaphoreType.DMA((2,))],
        compiler_params=_cparams(("arbitrary",)),
        name="moe_dispatch",
    )(idx, hx_tiles)


def _expert_kernel(blk_e_ref, xs_ref, wg_ref, wu_ref, wd_ref, y_ref, *, bm):
    del blk_e_ref
    x = _from_tiles(xs_ref, 0, bm).astype(BF16)
    hg = jnp.dot(x, wg_ref[...].astype(BF16), preferred_element_type=F32)
    hu = jnp.dot(x, wu_ref[...].astype(BF16), preferred_element_type=F32)
    act = hg * _sigmoid(hg) * hu
    _to_tiles(y_ref, jnp.dot(act.astype(BF16), wd_ref[...].astype(BF16), preferred_element_type=F32))


def _experts(xs_tiles, blk_e, wg, wu, wd, nb, bm):
    D = wg.shape[1]
    grid_spec = pltpu.PrefetchScalarGridSpec(
        num_scalar_prefetch=1,
        grid=(nb,),
        in_specs=[pl.BlockSpec((bm * TILE_ROWS, 128), lambda i, be: (i, 0)),
                  pl.BlockSpec((None, D, EXPERT_HIDDEN), lambda i, be: (be[i], 0, 0)),
                  pl.BlockSpec((None, D, EXPERT_HIDDEN), lambda i, be: (be[i], 0, 0)),
                  pl.BlockSpec((None, EXPERT_HIDDEN, D), lambda i, be: (be[i], 0, 0))],
        out_specs=pl.BlockSpec((bm * TILE_ROWS, 128), lambda i, be: (i, 0)),
    )
    return pl.pallas_call(
        functools.partial(_expert_kernel, bm=bm),
        grid_spec=grid_spec,
        out_shape=jax.ShapeDtypeStruct((nb * bm * TILE_ROWS, 128), U32),
        compiler_params=_cparams(("arbitrary",)),
        name="moe_experts",
    )(blk_e, xs_tiles, wg, wu, wd)


def _combine_kernel(cur_ref, nxt_ref, y_hbm, hx_ref, wt_ref, sgu_ref, sd_ref, x1_ref, g2_ref, fg_ref, o_ref, ybuf,
                    sem, *, tm):
    def consume(here, start_part):
        wt = wt_ref[...]
        routed = None
        for k in range(TOP_K):
            start_part(k)
            wk = jnp.broadcast_to(wt[:, k:k + 1], (tm, 128))
            wk = jnp.concatenate([wk] * (x1_ref.shape[1] // 128), axis=1)
            term = wk * _from_tiles(ybuf, here + k * tm * TILE_ROWS, tm)
            routed = term if routed is None else routed + term
        hu = jnp.dot(_from_tiles(hx_ref, 0, tm).astype(BF16), sgu_ref[...], preferred_element_type=F32)
        hg = hu[:, :EXPERT_HIDDEN]
        act = hg * _sigmoid(hg) * hu[:, EXPERT_HIDDEN:]
        shared = jnp.dot(act.astype(BF16), sd_ref[...], preferred_element_type=F32)
        xo = x1_ref[...] + g2_ref[...] * (routed + shared)
        o_ref[...] = _rms(xo) * fg_ref[...]

    _pipelined_gather(cur_ref, nxt_ref, y_hbm, ybuf, sem, TOP_K * tm, consume, TOP_K)


def _combine(pos3, y_tiles, hx_tiles, w_tok, sgu, sd, x1, g2, fg, tm, tiles_per_batch):
    N, D = x1.shape
    nt = N // tm
    smem_blk = lambda f: pl.BlockSpec((None, 1, TOP_K * tm), f, memory_space=pltpu.SMEM)
    row = lambda i: (i, 0)
    const = lambda i: (0, 0)
    return pl.pallas_call(
        functools.partial(_combine_kernel, tm=tm),
        grid=(nt,),
        in_specs=[smem_blk(lambda i: (i, 0, 0)),
                  smem_blk(lambda i: (jnp.minimum(i + 1, nt - 1), 0, 0)),
                  pl.BlockSpec(memory_space=pl.ANY),
                  pl.BlockSpec((tm * TILE_ROWS, 128), row),
                  pl.BlockSpec((tm, 128), row),
                  pl.BlockSpec((D, 2 * EXPERT_HIDDEN), const),
                  pl.BlockSpec((EXPERT_HIDDEN, D), const),
                  pl.BlockSpec((tm, D), row),
                  pl.BlockSpec((None, 1, D), lambda i: (i // tiles_per_batch, 0, 0)),
                  pl.BlockSpec((1, D), const)],
        out_specs=pl.BlockSpec((tm, D), row),
        out_shape=jax.ShapeDtypeStruct((N, D), F32),
        scratch_shapes=[pltpu.VMEM((2 * TOP_K * tm * TILE_ROWS, 128), U32), pltpu.SemaphoreType.DMA((2,))],
        compiler_params=_cparams(("arbitrary",)),
        name="moe_combine",
    )(pos3, pos3, y_tiles, hx_tiles, w_tok, sgu, sd, x1, g2, fg)


def _rope_tables(L):
    rows = L // GRID_W
    row = jnp.repeat(jnp.arange(rows), GRID_W).astype(F32)
    col = jnp.tile(jnp.arange(GRID_W), rows).astype(F32)
    inv = ROPE_THETA ** (-jnp.arange(0, ROPE_AXIS_DIM, 2, dtype=F32) / ROPE_AXIS_DIM)
    ang = jnp.stack([row[:, None] * inv, col[:, None] * inv], axis=1)
    c, s = jnp.cos(ang), jnp.sin(ang)
    cos_h = jnp.stack([c, c], axis=2).reshape(L, HEAD_DIM)
    sin_h = jnp.stack([-s, s], axis=2).reshape(L, HEAD_DIM)
    return jnp.tile(cos_h, (1, 2)), jnp.tile(sin_h, (1, 2))


def _tile(n, pref):
    t = min(pref, n)
    while n % t:
        t //= 2
    return t


def kernel(x, c, ctx, c_ctx, ada_w, ada_b, norm1_g, norm2_g, w_in, lam_q1, lam_k1, lam_q2, lam_k2, subln_g, hy_conv_w, hy_conv_b, filt_w1, filt_b1, filt_freq, filt_w2, filt_b2, filt_w3, hy_bias, w_branch_attn, w_branch_hyena, w_out, router_w, router_bias, exp_w_gate, exp_w_up, exp_w_down, shared_w_gate, shared_w_up, shared_w_down, final_norm_g):
    B, L, D = x.shape
    assert ada_w.shape[0] == 1 and B % 2 == 0 and (2 * L) % (FFT_SLABS * FFT_STRIP) == 0
    lam = (jnp.exp(jnp.sum(lam_q1[0].astype(F32) * lam_k1[0].astype(F32)))
           - jnp.exp(jnp.sum(lam_q2[0].astype(F32) * lam_k2[0].astype(F32))) + LAM_INIT).reshape(1, 1)

    cc = jnp.zeros((16, D), F32).at[:B].set(c).at[B].set(c_ctx)
    mod = _adaln(cc, ada_w[0], ada_b[0])
    chunk = lambda r0, r1, j: mod[r0:r1, j * D:(j + 1) * D]
    sh1, sc1, g1, sh2, sc2, g2 = (chunk(0, B, j).reshape(B, 1, D) for j in range(6))
    csh, csc = chunk(B, B + 1, 0), chunk(B, B + 1, 1)

    w_bf = w_in[0].astype(BF16)
    n1g = norm1_g[0].reshape(1, D)
    cos_t, sin_t = _rope_tables(L)
    k, v, q, u, gates = _inproj(x, sh1, sc1, n1g, w_bf, cos_t, sin_t, hy_conv_w[0], hy_conv_b[0].reshape(1, -1),
                                _tile(L, 512))
    kc, vc = _ctxproj(ctx, csh, csc, n1g, w_bf)
    attn = _attention(lam, q, k, v, kc, vc, subln_g[0].reshape(1, V_DIM), _tile(L, 2048), _tile(L, 512))

    filt = (filt_w1[0], filt_b1[0], filt_freq[0], filt_w2[0], filt_b2[0], filt_w3[0])
    hy = _hyena(u, filt, hy_bias[0])

    x1, hx2, logits_t = _merge(attn, hy, gates, x, g1, sh2, sc2, norm2_g[0].reshape(1, D),
                               w_branch_attn[0].astype(BF16), w_branch_hyena[0].astype(BF16),
                               w_out[0].astype(BF16), router_w[0].T.astype(F32), _tile(L, 1024))
    eidx, w_tok, rank, counts = _router(logits_t, router_bias[0], _tile(B * L, 1024))

    N = B * L
    tm = _tile(L, MOE_TOKENS)
    pos3, pad_pos, blk_e, nb = _dispatch_plan(eidx, rank, counts, MOE_ROWS, tm)
    xs = _dispatch(pos3, pad_pos, hx2, nb * MOE_ROWS, tm)
    y = _experts(xs, blk_e, exp_w_gate[0], exp_w_up[0], exp_w_down[0], nb, MOE_ROWS)
    sgu = jnp.concatenate([shared_w_gate[0], shared_w_up[0]], axis=-1).astype(BF16)
    out = _combine(pos3, y, hx2, w_tok, sgu, shared_w_down[0].astype(BF16), x1.reshape(N, D), g2,
                   final_norm_g.reshape(1, D), tm, L // tm)
    return out.reshape(B, L, D)
```

```python
import functools
import math

import jax
import jax.numpy as jnp
from jax import lax
from jax.experimental import pallas as pl
from jax.experimental.pallas import tpu as pltpu

F32 = jnp.float32
BF16 = jnp.bfloat16
U32 = jnp.uint32
HIGHEST = lax.Precision.HIGHEST

NORM_EPS = 1e-6
N_HEADS = 8
HEAD_DIM = 64
V_DIM = 2 * HEAD_DIM
GRID_W = 64
ROPE_THETA = 10000.0
ROPE_AXIS_DIM = HEAD_DIM // 2
HYENA_W = 1024
FILTER_EMB = 33
FILTER_BANDS = (FILTER_EMB - 1) // 2
DECAY_TARGET = 1e-2
FAST_DECAY_PCT = 0.3
SLOW_DECAY_PCT = 1.5
N_EXPERTS = 64
N_GROUPS = 8
TOPK_GROUPS = 4
TOP_K = 8
EXPERT_HIDDEN = 256
ROUTED_SCALE = 2.5
LAM_INIT = 0.8 - 0.6 * math.exp(-0.3 * 0)

V7X_VMEM_LIMIT = 56 * 1024 * 1024
LOG2E = 1.4426950408889634


def _cparams(sem):
    return pltpu.CompilerParams(dimension_semantics=sem, vmem_limit_bytes=V7X_VMEM_LIMIT)


def _sigmoid(v):
    return 1.0 / (1.0 + jnp.exp(-v))


def _rms(v):
    return v * lax.rsqrt(jnp.mean(v * v, axis=-1, keepdims=True) + NORM_EPS)


def _adaln_kernel(c_ref, w_ref, b_ref, o_ref):
    c = c_ref[...]
    s = c * _sigmoid(c)
    o_ref[...] = jnp.dot(s, w_ref[...], precision=HIGHEST, preferred_element_type=F32) + b_ref[...]


def _adaln(cc, w, b):
    rows, d = cc.shape
    n = w.shape[1]
    tn = 1536
    return pl.pallas_call(
        _adaln_kernel,
        grid=(n // tn,),
        in_specs=[pl.BlockSpec((rows, d), lambda j: (0, 0)),
                  pl.BlockSpec((d, tn), lambda j: (0, j)),
                  pl.BlockSpec((1, tn), lambda j: (0, j))],
        out_specs=pl.BlockSpec((rows, tn), lambda j: (0, j)),
        out_shape=jax.ShapeDtypeStruct((rows, n), F32),
        compiler_params=_cparams(("arbitrary",)),
        name="adaln",
    )(cc, w, b.reshape(1, n))


def _inproj_kernel(x_ref, xp_ref, xn_ref, sh_ref, sc_ref, g_ref, w_ref, cos_ref, sin_ref, cw_ref, cb_ref,
                   k_ref, v_ref, q_ref, u_ref, gt_ref, pscr, *, tm, nt, qscale):
    i = pl.program_id(1)
    g = g_ref[...]
    sh = sh_ref[...]
    sc = 1.0 + sc_ref[...]

    def norm_mod(xx):
        return (_rms(xx) * g) * sc + sh

    h = norm_mod(x_ref[...])
    hp = jnp.where(i > 0, norm_mod(xp_ref[...]), 0.0)
    hn = jnp.where(i < nt - 1, norm_mod(xn_ref[...]), 0.0)
    hb = h.astype(BF16)
    h_ext = jnp.concatenate([hp, h, hn], axis=0).astype(BF16)

    def proj(lhs, c0):
        return jnp.dot(lhs, w_ref[:, c0:c0 + 1024], preferred_element_type=F32)

    cosv = cos_ref[...]
    sinv = sin_ref[...]
    lane = lax.broadcasted_iota(jnp.int32, (1, V_DIM), 1)
    first = (lane % ROPE_AXIS_DIM) < (ROPE_AXIS_DIM // 2)

    def rope_store(a, o_ref, scale):
        for hh in range(N_HEADS):
            s = a[:, hh * V_DIM:(hh + 1) * V_DIM]
            partner = jnp.where(first, pltpu.roll(s, V_DIM - ROPE_AXIS_DIM // 2, 1),
                                pltpu.roll(s, ROPE_AXIS_DIM // 2, 1))
            r = s * cosv + partner * sinv
            if scale != 1.0:
                r = r * scale
            o_ref[:, hh * V_DIM:(hh + 1) * V_DIM] = r.astype(BF16)

    rope_store(proj(hb, 0), k_ref, 1.0)
    v_ref[...] = proj(hb, 1024).astype(BF16)
    rope_store(proj(hb, 2048), q_ref, qscale)
    for j in range(3):
        pscr[...] = proj(h_ext, 3072 + j * 1024)
        cw = cw_ref[:, j * 1024:(j + 1) * 1024]
        cb = cb_ref[:, j * 1024:(j + 1) * 1024]
        u = (pscr[pl.ds(7, tm), :] * cw[0:1] + pscr[pl.ds(8, tm), :] * cw[1:2]
             + pscr[pl.ds(9, tm), :] * cw[2:3] + cb)
        u_ref[:, j * 1024:(j + 1) * 1024] = u.astype(BF16)
    for j in range(2):
        gt_ref[:, j * 1024:(j + 1) * 1024] = proj(hb, 6144 + j * 1024).astype(BF16)


def _inproj(x, shift, scale, g, w_bf, cos_t, sin_t, conv_w, conv_b, tm):
    B, L, D = x.shape
    nt = L // tm
    nb8 = L // 8
    qscale = (HEAD_DIM ** -0.5) * LOG2E
    kern = functools.partial(_inproj_kernel, tm=tm, nt=nt, qscale=qscale)
    row = lambda b, i: (b, i, 0)
    per_b = lambda b, i: (b, 0, 0)
    const = lambda b, i: (0, 0)
    outs = pl.pallas_call(
        kern,
        grid=(B, nt),
        in_specs=[
            pl.BlockSpec((None, tm, D), row),
            pl.BlockSpec((None, 8, D), lambda b, i: (b, jnp.maximum(i * (tm // 8) - 1, 0), 0)),
            pl.BlockSpec((None, 8, D), lambda b, i: (b, jnp.minimum((i + 1) * (tm // 8), nb8 - 1), 0)),
            pl.BlockSpec((None, 1, D), per_b),
            pl.BlockSpec((None, 1, D), per_b),
            pl.BlockSpec((1, D), const),
            pl.BlockSpec((D, 8192), const, pipeline_mode=pl.Buffered(1)),
            pl.BlockSpec((tm, V_DIM), lambda b, i: (i, 0)),
            pl.BlockSpec((tm, V_DIM), lambda b, i: (i, 0)),
            pl.BlockSpec((3, 3 * HYENA_W), const),
            pl.BlockSpec((1, 3 * HYENA_W), const),
        ],
        out_specs=[
            pl.BlockSpec((None, tm, 1024), row),
            pl.BlockSpec((None, tm, 1024), row),
            pl.BlockSpec((None, tm, 1024), row),
            pl.BlockSpec((None, tm, 3 * HYENA_W), row),
            pl.BlockSpec((None, tm, 2048), row),
        ],
        out_shape=[
            jax.ShapeDtypeStruct((B, L, 1024), BF16),
            jax.ShapeDtypeStruct((B, L, 1024), BF16),
            jax.ShapeDtypeStruct((B, L, 1024), BF16),
            jax.ShapeDtypeStruct((B, L, 3 * HYENA_W), BF16),
            jax.ShapeDtypeStruct((B, L, 2048), BF16),
        ],
        scratch_shapes=[pltpu.VMEM((tm + 16, 1024), F32)],
        compiler_params=_cparams(("parallel", "arbitrary")),
        name="inproj",
    )(x, x, x, shift, scale, g, w_bf, cos_t, sin_t, conv_w, conv_b)
    return outs


def _ctxproj_kernel(x_ref, sh_ref, sc_ref, g_ref, w_ref, k_ref, v_ref):
    h = ((_rms(x_ref[...]) * g_ref[...]) * (1.0 + sc_ref[...]) + sh_ref[...]).astype(BF16)
    k_ref[...] = jnp.dot(h, w_ref[:, 0:1024], preferred_element_type=F32).astype(BF16)
    v_ref[...] = jnp.dot(h, w_ref[:, 1024:2048], preferred_element_type=F32).astype(BF16)


def _ctxproj(ctx, shift, scale, g, w_bf):
    B, Lc, D = ctx.shape
    return pl.pallas_call(
        _ctxproj_kernel,
        grid=(B,),
        in_specs=[pl.BlockSpec((None, Lc, D), lambda b: (b, 0, 0)),
                  pl.BlockSpec((1, D), lambda b: (0, 0)),
                  pl.BlockSpec((1, D), lambda b: (0, 0)),
                  pl.BlockSpec((1, D), lambda b: (0, 0)),
                  pl.BlockSpec((D, 2048), lambda b: (0, 0))],
        out_specs=[pl.BlockSpec((None, Lc, 1024), lambda b: (b, 0, 0)),
                   pl.BlockSpec((None, Lc, 1024), lambda b: (b, 0, 0))],
        out_shape=[jax.ShapeDtypeStruct((B, Lc, 1024), BF16),
                   jax.ShapeDtypeStruct((B, Lc, 1024), BF16)],
        compiler_params=_cparams(("arbitrary",)),
        name="ctxproj",
    )(ctx, shift, scale, g, w_bf)


def _attn_kernel(lam_ref, q_ref, k_ref, v_ref, kc_ref, vc_ref, g_ref, o_ref, m_s, acc_s, *, tk, nk):
    q = q_ref[...]
    lane = lax.broadcasted_iota(jnp.int32, (1, V_DIM), 1)
    zero = jnp.zeros_like(q)
    qs = (jnp.where(lane < HEAD_DIM, q, zero), jnp.where(lane >= HEAD_DIM, q, zero))
    def chunk(kk, vv, first=False):
        reps = kk.shape[0] // V_DIM
        v1 = jnp.concatenate([vv, jnp.ones_like(vv)], axis=1)
        for m in range(2):
            s = lax.dot_general(qs[m], kk, (((1,), (1,)), ((), ())), preferred_element_type=F32)
            m_new = jnp.broadcast_to(jnp.max(s, axis=-1, keepdims=True), m_s.shape[1:])
            if not first:
                m_prev = m_s[m]
                m_new = jnp.maximum(m_prev, m_new)
                alpha = jnp.exp2(m_prev - m_new)
            p = jnp.exp2(s - jnp.concatenate([m_new] * reps, axis=1))
            pv = jnp.dot(p.astype(BF16), v1, preferred_element_type=F32)
            acc_s[m] = pv if first else jnp.concatenate([alpha, alpha], axis=1) * acc_s[m] + pv
            m_s[m] = m_new

    def body(j, carry):
        off = pl.multiple_of(j * tk, tk)
        chunk(k_ref[pl.ds(off, tk), :], v_ref[pl.ds(off, tk), :])
        return carry

    chunk(kc_ref[...], vc_ref[...], first=True)
    lax.fori_loop(0, nk, body, 0, unroll=2 if nk % 2 == 0 else 1)
    lam = lam_ref[0, 0]
    a0, a1 = acc_s[0], acc_s[1]
    o = a0[:, :V_DIM] / a0[:, V_DIM:] - lam * (a1[:, :V_DIM] / a1[:, V_DIM:])
    o = _rms(o) * g_ref[...] * (1.0 - LAM_INIT)
    o_ref[...] = o.astype(BF16)


def _attention(lam, q, k, v, kc, vc, subln_g, tq, tk):
    B, L, _ = q.shape
    Lc = kc.shape[1]
    kern = functools.partial(_attn_kernel, tk=tk, nk=L // tk)
    kv = lambda b, h, i: (b, 0, h)
    return pl.pallas_call(
        kern,
        grid=(B, N_HEADS, L // tq),
        in_specs=[
            pl.BlockSpec(memory_space=pltpu.SMEM),
            pl.BlockSpec((None, tq, V_DIM), lambda b, h, i: (b, i, h)),
            pl.BlockSpec((None, L, V_DIM), kv),
            pl.BlockSpec((None, L, V_DIM), kv),
            pl.BlockSpec((None, Lc, V_DIM), kv),
            pl.BlockSpec((None, Lc, V_DIM), kv),
            pl.BlockSpec((1, V_DIM), lambda b, h, i: (0, 0)),
        ],
        out_specs=pl.BlockSpec((None, tq, V_DIM), lambda b, h, i: (b, i, h)),
        out_shape=jax.ShapeDtypeStruct((B, L, N_HEADS * V_DIM), BF16),
        scratch_shapes=[pltpu.VMEM((2, tq, V_DIM), F32), pltpu.VMEM((2, tq, 2 * V_DIM), F32)],
        compiler_params=_cparams(("parallel", "parallel", "arbitrary")),
        name="diffattn",
    )(lam, q, k, v, kc, vc, subln_g)


FFT_SLABS = 64
FFT_STRIP = 16
FILTER_TAP_STEPS = 16


def _dft_tables(nb):
    N = FFT_SLABS * nb
    fa = jnp.arange(FFT_SLABS, dtype=jnp.int32)[:, None, None]
    fb = jnp.arange(nb, dtype=jnp.int32)[None, :, None]
    b = jnp.arange(nb, dtype=jnp.int32)[None, None, :]
    ang = ((FFT_SLABS * b * fb + b * fa) % N).astype(F32) * (2.0 * math.pi / N)
    c, s = jnp.cos(ang), jnp.sin(ang)
    fwd = jnp.concatenate([jnp.concatenate([c, s], -1), jnp.concatenate([-s, c], -1)], axis=1)
    inv = jnp.swapaxes(fwd, 1, 2) / N
    return fwd.astype(BF16), inv.astype(BF16)


def _vadd(a, b):
    return b if a is None else (a if b is None else a + b)


def _vsub(a, b):
    if b is None:
        return a
    return -b if a is None else a - b


def _vscale(a, k):
    if a is None or k == 1.0:
        return a
    return -a if k == -1.0 else a * k


def _cmulc(x, c, s):
    re, im = x
    eps = 1e-9
    if abs(s) < eps:
        k = 1.0 if c > 0 else -1.0
        return _vscale(re, k), _vscale(im, k)
    if abs(c) < eps:
        k = 1.0 if s > 0 else -1.0
        return _vscale(im, -k), _vscale(re, k)
    if abs(abs(c) - abs(s)) < eps:
        kc = 1.0 if c > 0 else -1.0
        ks = 1.0 if s > 0 else -1.0
        return (_vscale(_vsub(_vscale(re, kc), _vscale(im, ks)), abs(c)),
                _vscale(_vadd(_vscale(re, ks), _vscale(im, kc)), abs(c)))
    return _vsub(_vscale(re, c), _vscale(im, s)), _vadd(_vscale(re, s), _vscale(im, c))


def _fft(xs, sign):
    n = len(xs)
    if n == 1:
        return xs
    ev, od = _fft(xs[0::2], sign), _fft(xs[1::2], sign)
    out = [None] * n
    for k in range(n // 2):
        ang = sign * 2.0 * math.pi * k / n
        t = _cmulc(od[k], math.cos(ang), math.sin(ang))
        out[k] = (_vadd(ev[k][0], t[0]), _vadd(ev[k][1], t[1]))
        out[k + n // 2] = (_vsub(ev[k][0], t[0]), _vsub(ev[k][1], t[1]))
    return out


def _put(ref, slot, rows, val):
    ref[slot, rows, :] = jnp.zeros((FFT_STRIP, ref.shape[2]), F32) if val is None else val


def _slab_fft_forward(load, s_ref, nb, n_in):
    def strip(i, carry):
        r0 = pl.multiple_of(i * FFT_STRIP, FFT_STRIP)
        rows_re, rows_im = pl.ds(r0, FFT_STRIP), pl.ds(nb + r0, FFT_STRIP)
        for a0 in range(8):
            xs = [load(8 * a1 + a0, r0) if a1 < n_in else (None, None) for a1 in range(8)]
            ys = _fft(xs, -1.0)
            for f1 in range(8):
                ang = -2.0 * math.pi * a0 * f1 / FFT_SLABS
                yr, yi = _cmulc(ys[f1], math.cos(ang), math.sin(ang))
                _put(s_ref, a0 * 8 + f1, rows_re, yr)
                _put(s_ref, a0 * 8 + f1, rows_im, yi)
        for f1 in range(8):
            xs = [(s_ref[a0 * 8 + f1, rows_re, :], s_ref[a0 * 8 + f1, rows_im, :]) for a0 in range(8)]
            ys = _fft(xs, -1.0)
            for f0 in range(8):
                s_ref[f1 + 8 * f0, rows_re, :] = ys[f0][0]
                s_ref[f1 + 8 * f0, rows_im, :] = ys[f0][1]
        return carry

    lax.fori_loop(0, nb // FFT_STRIP, strip, 0)


def _slab_fft_inverse(s_ref, nb, emit):
    def strip(i, carry):
        r0 = pl.multiple_of(i * FFT_STRIP, FFT_STRIP)
        rows_re, rows_im = pl.ds(r0, FFT_STRIP), pl.ds(nb + r0, FFT_STRIP)
        for q in range(8):
            xs = [(s_ref[8 * p + q, rows_re, :], s_ref[8 * p + q, rows_im, :]) for p in range(8)]
            us = _fft(xs, 1.0)
            for r in range(8):
                ang = 2.0 * math.pi * r * q / FFT_SLABS
                ur, ui = _cmulc(us[r], math.cos(ang), math.sin(ang))
                s_ref[8 * r + q, rows_re, :] = ur
                s_ref[8 * r + q, rows_im, :] = ui
        for r in range(8):
            xs = [(s_ref[8 * r + q, rows_re, :], s_ref[8 * r + q, rows_im, :]) for q in range(8)]
            ys = _fft(xs, 1.0)
            for sidx in range(4):
                emit(r + 8 * sidx, r0, ys[sidx][0], ys[sidx][1])
        return carry

    lax.fori_loop(0, nb // FFT_STRIP, strip, 0)


def _hyena_conv_kernel(v_ref, x_ref, k_ref, ff_ref, fi_ref, bias_ref, o_ref, s_ref, *, nb):
    def load(a, r0):
        rows = pl.ds(pl.multiple_of(a * nb + r0, FFT_STRIP), FFT_STRIP)
        return v_ref[0, rows, :].astype(F32), v_ref[1, rows, :].astype(F32)

    _slab_fft_forward(load, s_ref, nb, FFT_SLABS // 16)

    def freq(f, carry):
        z = jnp.dot(ff_ref[f], s_ref[f].astype(BF16), preferred_element_type=F32)
        zr, zi = z[:nb], z[nb:]
        kr, ki = k_ref[0, f], k_ref[1, f]
        y = jnp.concatenate([zr * kr - zi * ki, zr * ki + zi * kr], axis=0).astype(BF16)
        s_ref[f] = jnp.dot(fi_ref[f], y, preferred_element_type=F32)
        return carry

    lax.fori_loop(0, FFT_SLABS, freq, 0, unroll=8)
    bias = bias_ref[...]

    def emit(a, r0, re, im):
        rows = pl.ds(pl.multiple_of(a * nb + r0, FFT_STRIP), FFT_STRIP)
        for half, y in enumerate((re, im)):
            v = v_ref[half, rows, :].astype(F32)
            o_ref[half, rows, :] = (x_ref[half, rows, :].astype(F32) * (y + v * bias)).astype(BF16)

    _slab_fft_inverse(s_ref, nb, emit)


def _hyena_conv(uv, zv, k_hat, ff, fi, bias, order, in_col, gate_col, nb):
    _, P, L, _ = uv.shape
    W = HYENA_W
    nct = W // 128
    src, src_off = (uv, in_col * nct) if zv is None else (zv, 0)
    blk = (2, None, L, 128)
    return pl.pallas_call(
        functools.partial(_hyena_conv_kernel, nb=nb),
        grid=(nct, P),
        in_specs=[pl.BlockSpec(blk, lambda c, p: (0, p, 0, src_off + c)),
                  pl.BlockSpec(blk, lambda c, p: (0, p, 0, gate_col * nct + c)),
                  pl.BlockSpec((2, FFT_SLABS, nb, 128), lambda c, p: (0, 0, 0, order * nct + c),
                               pipeline_mode=pl.Buffered(1)),
                  pl.BlockSpec((FFT_SLABS, 2 * nb, 2 * nb), lambda c, p: (0, 0, 0), pipeline_mode=pl.Buffered(1)),
                  pl.BlockSpec((FFT_SLABS, 2 * nb, 2 * nb), lambda c, p: (0, 0, 0), pipeline_mode=pl.Buffered(1)),
                  pl.BlockSpec((1, 128), lambda c, p: (0, c))],
        out_specs=pl.BlockSpec(blk, lambda c, p: (0, p, 0, c)),
        out_shape=jax.ShapeDtypeStruct((2, P, L, W), BF16),
        scratch_shapes=[pltpu.VMEM((FFT_SLABS, 2 * nb, 128), F32)],
        compiler_params=_cparams(("parallel", "arbitrary")),
        name="hyena_conv%d" % order,
    )(src, uv, k_hat, ff, fi, bias)


def _filter_features(n):
    pos = jnp.arange(n, dtype=F32)[:, None]
    t = jnp.linspace(0.0, 1.0, n, dtype=F32)[:, None]
    w = 2 * math.pi * pos / n
    bands = jnp.linspace(1e-4, FILTER_BANDS - 1, FILTER_BANDS, dtype=F32)
    z = jnp.concatenate([t, jnp.cos(bands * w), -jnp.sin(bands * w)], axis=-1)
    z = jnp.pad(z, ((0, 0), (0, 128 - FILTER_EMB)))
    return jnp.concatenate([z, z[0:1], z[:0:-1]], axis=0)


def _dot_bf16x3(a, b):
    ah, bh = a.astype(BF16), b.astype(BF16)
    al, bl = (a - ah.astype(F32)).astype(BF16), (b - bh.astype(F32)).astype(BF16)
    dot = functools.partial(jnp.dot, preferred_element_type=F32)
    return dot(ah, bh) + (dot(ah, bl) + dot(al, bh))


def _filt_taps_kernel(z_ref, w1_ref, b1_ref, fr_ref, w2_ref, b2_ref, w3_ref, dl_ref, k_ref, sum_ref):
    a = pl.program_id(0)
    z = z_ref[...]
    fr = fr_ref[...]
    hd = jnp.sin(fr * (jnp.dot(z, w1_ref[...], precision=HIGHEST, preferred_element_type=F32) + b1_ref[...]))
    hd = jnp.sin(fr * (jnp.dot(hd, w2_ref[...], precision=HIGHEST, preferred_element_type=F32) + b2_ref[...]))
    h = _dot_bf16x3(hd, w3_ref[...]) * jnp.exp(-z[:, 0:1] * dl_ref[...])
    rows = lax.broadcasted_iota(jnp.int32, (z.shape[0], 1), 0)
    h = jnp.where(jnp.logical_and(a == pl.num_programs(0) // 2, rows == 0), 0.0, h)
    k_ref[...] = h

    @pl.when(a == 0)
    def _():
        sum_ref[...] = jnp.zeros_like(sum_ref)

    sum_ref[0:1, :] += jnp.sum(jnp.abs(h), axis=0, keepdims=True)


def _filt_spectrum_kernel(t_ref, inv_ref, ff_ref, k_ref, s_ref, *, nb):
    def load(a, r0):
        return t_ref[pl.ds(pl.multiple_of(a * nb + r0, FFT_STRIP), FFT_STRIP), :], None

    _slab_fft_forward(load, s_ref, nb, FFT_SLABS // 8)
    inv = inv_ref[...]

    def freq(f, carry):
        z = jnp.dot(ff_ref[f], s_ref[f].astype(BF16), preferred_element_type=F32)
        k_ref[0, f] = z[:nb] * inv
        k_ref[1, f] = z[nb:] * inv
        return carry

    lax.fori_loop(0, FFT_SLABS, freq, 0, unroll=8)


def _hyena_filters(n, filt, ff, nb):
    w1, b1, freq, w2, b2, w3 = filt
    W = HYENA_W
    hid = w1.shape[1]
    z = _filter_features(n)
    w1p = jnp.pad(w1.astype(F32), ((0, 128 - FILTER_EMB), (0, 0)))
    w3d = w3.astype(F32).reshape(hid, 2, 2, W)
    w3sel = jnp.stack([w3d[:, :, d, :].reshape(hid, 2 * W) for d in range(2)])
    deltas = jnp.abs(jnp.linspace(math.log(DECAY_TARGET) / SLOW_DECAY_PCT,
                                  math.log(DECAY_TARGET) / FAST_DECAY_PCT, W, dtype=F32))
    deltas = jnp.tile(deltas.reshape(1, W), (1, 2))
    const = lambda a: (0, 0)
    steps = FILTER_TAP_STEPS
    rs = 2 * n // steps
    taps, sums = pl.pallas_call(
        _filt_taps_kernel,
        grid=(steps,),
        in_specs=[pl.BlockSpec((rs, 128), lambda a: (a, 0)),
                  pl.BlockSpec((128, hid), const), pl.BlockSpec((1, hid), const), pl.BlockSpec((1, hid), const),
                  pl.BlockSpec((hid, hid), const), pl.BlockSpec((1, hid), const),
                  pl.BlockSpec((None, hid, 2 * W), lambda a: (a // (steps // 2), 0, 0)),
                  pl.BlockSpec((1, 2 * W), const)],
        out_specs=[pl.BlockSpec((rs, 2 * W), lambda a: (a, 0)), pl.BlockSpec((8, 2 * W), const)],
        out_shape=[jax.ShapeDtypeStruct((2 * n, 2 * W), F32), jax.ShapeDtypeStruct((8, 2 * W), F32)],
        compiler_params=_cparams(("arbitrary",)),
        name="hyena_filter_taps",
    )(z, w1p, b1.reshape(1, hid), freq.reshape(1, hid), w2, b2.reshape(1, hid), w3sel, deltas)
    inv = 1.0 / sums[0:1]
    return pl.pallas_call(
        functools.partial(_filt_spectrum_kernel, nb=nb),
        grid=(2 * W // 128,),
        in_specs=[pl.BlockSpec((2 * n, 128), lambda j: (0, j)),
                  pl.BlockSpec((1, 128), lambda j: (0, j)),
                  pl.BlockSpec((FFT_SLABS, 2 * nb, 2 * nb), lambda j: (0, 0, 0), pipeline_mode=pl.Buffered(1))],
        out_specs=pl.BlockSpec((2, FFT_SLABS, nb, 128), lambda j: (0, 0, 0, j)),
        out_shape=jax.ShapeDtypeStruct((2, FFT_SLABS, nb, 2 * W), F32),
        scratch_shapes=[pltpu.VMEM((FFT_SLABS, 2 * nb, 128), F32)],
        compiler_params=_cparams(("parallel",)),
        name="hyena_filter_spectrum",
    )(taps, inv, ff)


def _hyena(u, filt, hy_bias):
    B, L, _ = u.shape
    P = B // 2
    nb = 2 * L // FFT_SLABS
    ff, fi = _dft_tables(nb)
    k_hat = _hyena_filters(L, filt, ff, nb)
    uv = u.reshape(2, P, L, 3 * HYENA_W)
    bias = hy_bias.astype(F32)
    z = _hyena_conv(uv, None, k_hat, ff, fi, bias[0:1], 0, 0, 1, nb)
    y = _hyena_conv(uv, z, k_hat, ff, fi, bias[1:2], 1, 0, 2, nb)
    return y.reshape(B, L, HYENA_W)


def _merge_kernel(a_ref, h_ref, gt_ref, x_ref, g1_ref, sh_ref, sc_ref, n2g_ref, wpa_ref, wph_ref, wo_ref, rw_ref,
                  x1_ref, hx_ref, lg_ref):
    ya = jnp.dot(a_ref[...], wpa_ref[...], preferred_element_type=F32)
    yh = jnp.dot(h_ref[...], wph_ref[...], preferred_element_type=F32)
    ga = gt_ref[:, 0:1024].astype(F32)
    gh = gt_ref[:, 1024:2048].astype(F32)
    y = _sigmoid(ga) * ya + _sigmoid(gh) * yh
    mix = jnp.dot(y.astype(BF16), wo_ref[...], preferred_element_type=F32)
    x1 = x_ref[...] + g1_ref[...] * mix
    x1_ref[...] = x1
    hx = (_rms(x1) * n2g_ref[...]) * (1.0 + sc_ref[...]) + sh_ref[...]
    _to_tiles(hx_ref, hx)
    lg_ref[...] = lax.dot_general(rw_ref[...], hx, (((1,), (1,)), ((), ())), precision=HIGHEST,
                                  preferred_element_type=F32)


def _merge(attn, hy, gates, x, g1, sh2, sc2, n2g, wpa, wph, wo, rwT, tm):
    B, L, D = x.shape
    nt = L // tm
    row = lambda b, i: (b, i, 0)
    per_b = lambda b, i: (b, 0, 0)
    const = lambda b, i: (0, 0)
    return pl.pallas_call(
        _merge_kernel,
        grid=(B, nt),
        in_specs=[pl.BlockSpec((None, tm, 1024), row), pl.BlockSpec((None, tm, 1024), row),
                  pl.BlockSpec((None, tm, 2048), row), pl.BlockSpec((None, tm, D), row),
                  pl.BlockSpec((None, 1, D), per_b), pl.BlockSpec((None, 1, D), per_b),
                  pl.BlockSpec((None, 1, D), per_b), pl.BlockSpec((1, D), const),
                  pl.BlockSpec((1024, D), const), pl.BlockSpec((1024, D), const), pl.BlockSpec((D, D), const),
                  pl.BlockSpec((N_EXPERTS, D), const)],
        out_specs=[pl.BlockSpec((None, tm, D), row),
                   pl.BlockSpec((tm * TILE_ROWS, 128), lambda b, i: (b * nt + i, 0)),
                   pl.BlockSpec((N_EXPERTS, tm), lambda b, i: (0, b * nt + i))],
        out_shape=[jax.ShapeDtypeStruct((B, L, D), F32),
                   jax.ShapeDtypeStruct((B * L * TILE_ROWS, 128), U32),
                   jax.ShapeDtypeStruct((N_EXPERTS, B * L), F32)],
        compiler_params=_cparams(("parallel", "arbitrary")),
        name="merge",
    )(attn, hy, gates, x, g1, sh2, sc2, n2g, wpa, wph, wo, rwT)


def _router_kernel(lg_ref, rb_ref, tri_ref, ei_ref, wt_ref, rk_ref, cnt_ref):
    scores = _sigmoid(lg_ref[...])
    choice = scores + rb_ref[...]
    E, T = scores.shape
    gsz = E // N_GROUPS
    neg = jnp.float32(-jnp.inf)
    big = jnp.int32(1 << 30)
    rows8 = lax.broadcasted_iota(jnp.int32, (gsz, T), 0)
    gscore = []
    for gi in range(N_GROUPS):
        c = choice[gi * gsz:(gi + 1) * gsz]
        m1 = jnp.max(c, axis=0, keepdims=True)
        i1 = jnp.min(jnp.where(c == m1, rows8, big), axis=0, keepdims=True)
        m2 = jnp.max(jnp.where(rows8 == i1, neg, c), axis=0, keepdims=True)
        gscore.append(m1 + m2)
    gs = jnp.concatenate(gscore, axis=0)
    rowsg = lax.broadcasted_iota(jnp.int32, (N_GROUPS, T), 0)
    gsel = jnp.zeros((N_GROUPS, T), jnp.bool_)
    for _ in range(TOPK_GROUPS):
        m = jnp.max(gs, axis=0, keepdims=True)
        ix = jnp.min(jnp.where(gs == m, rowsg, big), axis=0, keepdims=True)
        hit = rowsg == ix
        gsel = jnp.logical_or(gsel, hit)
        gs = jnp.where(hit, neg, gs)
    masked = jnp.concatenate(
        [jnp.where(gsel[gi:gi + 1], choice[gi * gsz:(gi + 1) * gsz], neg) for gi in range(N_GROUPS)], axis=0)
    rows = lax.broadcasted_iota(jnp.int32, (E, T), 0)
    ids, ws, hits = [], [], []
    sel = jnp.zeros((E, T), F32)
    for _ in range(TOP_K):
        m = jnp.max(masked, axis=0, keepdims=True)
        ix = jnp.min(jnp.where(masked == m, rows, big), axis=0, keepdims=True)
        hit = rows == ix
        ids.append(ix)
        hits.append(hit)
        ws.append(jnp.sum(jnp.where(hit, scores, 0.0), axis=0, keepdims=True))
        sel = jnp.where(hit, 1.0, sel)
        masked = jnp.where(hit, neg, masked)
    w = jnp.concatenate(ws, axis=0)
    w = w / jnp.sum(w, axis=0, keepdims=True) * ROUTED_SCALE
    ei_ref[...] = jnp.concatenate(ids, axis=0)
    wt_ref[...] = jnp.concatenate([w, jnp.zeros((128 - TOP_K, T), F32)], axis=0).T

    @pl.when(pl.program_id(0) == 0)
    def _():
        cnt_ref[...] = jnp.zeros_like(cnt_ref)

    before = cnt_ref[...][:, 0:1]
    csum = jnp.dot(sel.astype(BF16), tri_ref[...], preferred_element_type=F32)
    rank = before + csum - sel
    rk_ref[...] = jnp.concatenate(
        [jnp.sum(jnp.where(h, rank, 0.0), axis=0, keepdims=True) for h in hits], axis=0).astype(jnp.int32)
    cnt_ref[...] = jnp.broadcast_to(before + csum[:, T - 1:T], cnt_ref.shape)


def _router(logits_t, router_bias, tt):
    E, N = logits_t.shape
    tri = jnp.triu(jnp.ones((tt, tt), BF16))
    per_tok = pl.BlockSpec((TOP_K, tt), lambda i: (0, i))
    return pl.pallas_call(
        _router_kernel,
        grid=(N // tt,),
        in_specs=[pl.BlockSpec((E, tt), lambda i: (0, i)), pl.BlockSpec((E, 1), lambda i: (0, 0)),
                  pl.BlockSpec((tt, tt), lambda i: (0, 0))],
        out_specs=[per_tok, pl.BlockSpec((tt, 128), lambda i: (i, 0)), per_tok,
                   pl.BlockSpec((E, 128), lambda i: (0, 0))],
        out_shape=[jax.ShapeDtypeStruct((TOP_K, N), jnp.int32), jax.ShapeDtypeStruct((N, 128), F32),
                   jax.ShapeDtypeStruct((TOP_K, N), jnp.int32), jax.ShapeDtypeStruct((E, 128), F32)],
        compiler_params=_cparams(("arbitrary",)),
        name="router",
    )(logits_t, router_bias.reshape(E, 1).astype(F32), tri)


MOE_ROWS = 512
MOE_TOKENS = 128


def _plan_kernel(ei_ref, rk_ref, ps_ref, pos_ref):
    ei = ei_ref[...]
    ps = ps_ref[...]
    rows = lax.broadcasted_iota(jnp.int32, (ps.shape[0], ei.shape[1]), 0)
    start = [jnp.sum(jnp.where(rows == ei[k:k + 1], ps, 0.0), axis=0, keepdims=True) for k in range(ei.shape[0])]
    pos_ref[...] = rk_ref[...] + jnp.concatenate(start, axis=0).astype(jnp.int32)


def _dispatch_plan(eidx, rank, counts, bm, tm):
    K, N = eidx.shape
    E = N_EXPERTS
    cnt = counts[:, 0].astype(jnp.int32)
    padded = (cnt + bm - 1) // bm * bm
    pad_end = jnp.cumsum(padded)
    pad_start = pad_end - padded
    nb = -(-(N * K + E * (bm - 1)) // bm)
    tt = _tile(N, 2048)
    per_tok = pl.BlockSpec((K, tt), lambda i: (0, i))
    pos = pl.pallas_call(
        _plan_kernel,
        grid=(N // tt,),
        in_specs=[per_tok, per_tok, pl.BlockSpec((E, 1), lambda i: (0, 0))],
        out_specs=per_tok,
        out_shape=jax.ShapeDtypeStruct((K, N), jnp.int32),
        compiler_params=_cparams(("parallel",)),
        name="moe_plan",
    )(eidx, rank, pad_start.astype(F32).reshape(E, 1))
    blk_start = jnp.arange(nb, dtype=jnp.int32) * bm
    blk_e = jnp.minimum(jnp.sum(blk_start[:, None] >= pad_end[None, :], axis=1), E - 1)
    n_pad = nb * bm - N * K
    assert n_pad % (K * tm) == 0
    cpad = jnp.cumsum(padded - cnt)
    q = jnp.arange(n_pad, dtype=jnp.int32)
    seg = jnp.sum(q[:, None] >= cpad[None, :], axis=1)
    onehot = seg[:, None] == jnp.arange(E + 1, dtype=jnp.int32)[None, :]
    first = jnp.concatenate([pad_start + cnt, pad_end[-1:]])
    skipped = jnp.concatenate([jnp.zeros((1,), cpad.dtype), cpad])
    pad_pos = (q + jnp.sum(jnp.where(onehot, (first - skipped)[None, :], 0), axis=1)).astype(jnp.int32)
    nt = N // tm
    pos3 = pos.reshape(K, nt, tm).transpose(1, 0, 2).reshape(nt, 1, K * tm)
    return pos3, pad_pos.reshape(-1, 1, K * tm), blk_e.astype(jnp.int32), nb


TILE_ROWS = 4
_HI16 = 0xFFFF0000


def _bf16_bits(v):
    return lax.bitcast_convert_type(v.astype(BF16).astype(F32), U32)


def _to_tiles(ref, val):
    m, half = val.shape[0], val.shape[1] // 2
    words = (lax.shift_right_logical(_bf16_bits(val[:, :half]), jnp.uint32(16))
             | (_bf16_bits(val[:, half:]) & jnp.uint32(_HI16)))
    for j in range(TILE_ROWS):
        ref[pl.ds(j, m, stride=TILE_ROWS), :] = words[:, j * 128:(j + 1) * 128]


def _from_tiles(ref, start, m):
    words = jnp.concatenate([ref[pl.ds(start + j, m, stride=TILE_ROWS), :] for j in range(TILE_ROWS)], axis=1)
    lo = lax.bitcast_convert_type(lax.shift_left(words, jnp.uint32(16)), F32)
    hi = lax.bitcast_convert_type(words & jnp.uint32(_HI16), F32)
    return jnp.concatenate([lo, hi], axis=1)


def _row_copy(idx_ref, r, src_hbm, buf, base, sem):
    src = pl.multiple_of(idx_ref[0, r] * TILE_ROWS, TILE_ROWS)
    dst = base + r * TILE_ROWS
    if not isinstance(dst, int):
        dst = pl.multiple_of(dst, TILE_ROWS)
    return pltpu.make_async_copy(src_hbm.at[pl.ds(src, TILE_ROWS), :], buf.at[pl.ds(dst, TILE_ROWS), :], sem)


def _pipelined_gather(cur_ref, nxt_ref, src_hbm, buf, sem, nrows, consume):
    i = pl.program_id(0)
    span = nrows * TILE_ROWS

    def wait(slot):
        pltpu.make_async_copy(src_hbm.at[pl.ds(0, span), :], buf.at[pl.ds(slot * span, span), :], sem.at[slot]).wait()

    @pl.when(i == 0)
    def _():
        def body(r, carry):
            _row_copy(cur_ref, r, src_hbm, buf, 0, sem.at[0]).start()
            return carry
        lax.fori_loop(0, nrows, body, 0)

    for slot in range(2):
        @pl.when(i % 2 == slot)
        def _():
            for r in range(nrows):
                _row_copy(nxt_ref, r, src_hbm, buf, (1 - slot) * span, sem.at[1 - slot]).start(priority=r % 2)
            wait(slot)
            consume(slot * span)

            @pl.when(i == pl.num_programs(0) - 1)
            def _():
                wait(1 - slot)


def _dispatch_kernel(pos_ref, hx_ref, xs_hbm, stage, zero_tile, sem, *, tm, nt):
    i = pl.program_id(0)
    slot = i % 2
    span = tm * TILE_ROWS

    def wait_step(s):
        for _ in range(TOP_K):
            pltpu.make_async_copy(stage.at[pl.ds(0, span), :], xs_hbm.at[pl.ds(0, span), :], sem.at[s]).wait()

    def dst(r):
        return xs_hbm.at[pl.ds(pl.multiple_of(pos_ref[0, r] * TILE_ROWS, TILE_ROWS), TILE_ROWS), :]

    @pl.when(i < nt)
    def _():
        base = pl.multiple_of(slot * span, span)
        stage[pl.ds(base, span), :] = hx_ref[...]
        for r in range(TOP_K * tm):
            src = stage.at[pl.ds(pl.multiple_of(base + (r % tm) * TILE_ROWS, TILE_ROWS), TILE_ROWS), :]
            pltpu.make_async_copy(src, dst(r), sem.at[slot]).start(priority=r % 2)

    @pl.when(i == 0)
    def _():
        zero_tile[...] = jnp.zeros_like(zero_tile)

    @pl.when(i >= nt)
    def _():
        for r in range(TOP_K * tm):
            pltpu.make_async_copy(zero_tile, dst(r), sem.at[slot]).start(priority=r % 2)

    @pl.when(i > 0)
    def _():
        wait_step(1 - slot)

    @pl.when(i == pl.num_programs(0) - 1)
    def _():
        wait_step(slot)


def _dispatch(pos3, pad_pos, hx_tiles, rows_total, tm):
    nt = pos3.shape[0]
    idx = jnp.concatenate([pos3, pad_pos], axis=0)
    steps = idx.shape[0]
    return pl.pallas_call(
        functools.partial(_dispatch_kernel, tm=tm, nt=nt),
        grid=(steps,),
        in_specs=[pl.BlockSpec((None, 1, TOP_K * tm), lambda i: (i, 0, 0), memory_space=pltpu.SMEM),
                  pl.BlockSpec((tm * TILE_ROWS, 128), lambda i: (jnp.minimum(i, nt - 1), 0))],
        out_specs=pl.BlockSpec(memory_space=pl.ANY),
        out_shape=jax.ShapeDtypeStruct((rows_total * TILE_ROWS, 128), U32),
        scratch_shapes=[pltpu.VMEM((2 * tm * TILE_ROWS, 128), U32), pltpu.VMEM((TILE_ROWS, 128), U32),
                        pltpu.SemaphoreType.DMA((2,))],
        compiler_params=_cparams(("arbitrary",)),
        name="moe_dispatch",
    )(idx, hx_tiles)


def _expert_kernel(blk_e_ref, xs_ref, wg_ref, wu_ref, wd_ref, y_ref, *, bm):
    del blk_e_ref
    x = _from_tiles(xs_ref, 0, bm).astype(BF16)
    hg = jnp.dot(x, wg_ref[...].astype(BF16), preferred_element_type=F32)
    hu = jnp.dot(x, wu_ref[...].astype(BF16), preferred_element_type=F32)
    act = hg * _sigmoid(hg) * hu
    _to_tiles(y_ref, jnp.dot(act.astype(BF16), wd_ref[...].astype(BF16), preferred_element_type=F32))


def _experts(xs_tiles, blk_e, wg, wu, wd, nb, bm):
    D = wg.shape[1]
    grid_spec = pltpu.PrefetchScalarGridSpec(
        num_scalar_prefetch=1,
        grid=(nb,),
        in_specs=[pl.BlockSpec((bm * TILE_ROWS, 128), lambda i, be: (i, 0)),
                  pl.BlockSpec((None, D, EXPERT_HIDDEN), lambda i, be: (be[i], 0, 0)),
                  pl.BlockSpec((None, D, EXPERT_HIDDEN), lambda i, be: (be[i], 0, 0)),
                  pl.BlockSpec((None, EXPERT_HIDDEN, D), lambda i, be: (be[i], 0, 0))],
        out_specs=pl.BlockSpec((bm * TILE_ROWS, 128), lambda i, be: (i, 0)),
    )
    return pl.pallas_call(
        functools.partial(_expert_kernel, bm=bm),
        grid_spec=grid_spec,
        out_shape=jax.ShapeDtypeStruct((nb * bm * TILE_ROWS, 128), U32),
        compiler_params=_cparams(("arbitrary",)),
        name="moe_experts",
    )(blk_e, xs_tiles, wg, wu, wd)


def _combine_kernel(cur_ref, nxt_ref, y_hbm, hx_ref, wt_ref, sgu_ref, sd_ref, x1_ref, g2_ref, fg_ref, o_ref, ybuf,
                    sem, *, tm):
    def consume(here):
        wt = wt_ref[...]
        routed = None
        for k in range(TOP_K):
            wk = jnp.broadcast_to(wt[:, k:k + 1], (tm, 128))
            wk = jnp.concatenate([wk] * (x1_ref.shape[1] // 128), axis=1)
            term = wk * _from_tiles(ybuf, here + k * tm * TILE_ROWS, tm)
            routed = term if routed is None else routed + term
        hu = jnp.dot(_from_tiles(hx_ref, 0, tm).astype(BF16), sgu_ref[...], preferred_element_type=F32)
        hg = hu[:, :EXPERT_HIDDEN]
        act = hg * _sigmoid(hg) * hu[:, EXPERT_HIDDEN:]
        shared = jnp.dot(act.astype(BF16), sd_ref[...], preferred_element_type=F32)
        xo = x1_ref[...] + g2_ref[...] * (routed + shared)
        o_ref[...] = _rms(xo) * fg_ref[...]

    _pipelined_gather(cur_ref, nxt_ref, y_hbm, ybuf, sem, TOP_K * tm, consume)


def _combine(pos3, y_tiles, hx_tiles, w_tok, sgu, sd, x1, g2, fg, tm, tiles_per_batch):
    N, D = x1.shape
    nt = N // tm
    smem_blk = lambda f: pl.BlockSpec((None, 1, TOP_K * tm), f, memory_space=pltpu.SMEM)
    row = lambda i: (i, 0)
    const = lambda i: (0, 0)
    return pl.pallas_call(
        functools.partial(_combine_kernel, tm=tm),
        grid=(nt,),
        in_specs=[smem_blk(lambda i: (i, 0, 0)),
                  smem_blk(lambda i: (jnp.minimum(i + 1, nt - 1), 0, 0)),
                  pl.BlockSpec(memory_space=pl.ANY),
                  pl.BlockSpec((tm * TILE_ROWS, 128), row),
                  pl.BlockSpec((tm, 128), row),
                  pl.BlockSpec((D, 2 * EXPERT_HIDDEN), const),
                  pl.BlockSpec((EXPERT_HIDDEN, D), const),
                  pl.BlockSpec((tm, D), row),
                  pl.BlockSpec((None, 1, D), lambda i: (i // tiles_per_batch, 0, 0)),
                  pl.BlockSpec((1, D), const)],
        out_specs=pl.BlockSpec((tm, D), row),
        out_shape=jax.ShapeDtypeStruct((N, D), F32),
        scratch_shapes=[pltpu.VMEM((2 * TOP_K * tm * TILE_ROWS, 128), U32), pltpu.SemaphoreType.DMA((2,))],
        compiler_params=_cparams(("arbitrary",)),
        name="moe_combine",
    )(pos3, pos3, y_tiles, hx_tiles, w_tok, sgu, sd, x1, g2, fg)


def _rope_tables(L):
    rows = L // GRID_W
    row = jnp.repeat(jnp.arange(rows), GRID_W).astype(F32)
    col = jnp.tile(jnp.arange(GRID_W), rows).astype(F32)
    inv = ROPE_THETA ** (-jnp.arange(0, ROPE_AXIS_DIM, 2, dtype=F32) / ROPE_AXIS_DIM)
    ang = jnp.stack([row[:, None] * inv, col[:, None] * inv], axis=1)
    c, s = jnp.cos(ang), jnp.sin(ang)
    cos_h = jnp.stack([c, c], axis=2).reshape(L, HEAD_DIM)
    sin_h = jnp.stack([-s, s], axis=2).reshape(L, HEAD_DIM)
    return jnp.tile(cos_h, (1, 2)), jnp.tile(sin_h, (1, 2))


def _tile(n, pref):
    t = min(pref, n)
    while n % t:
        t //= 2
    return t


def kernel(x, c, ctx, c_ctx, ada_w, ada_b, norm1_g, norm2_g, w_in, lam_q1, lam_k1, lam_q2, lam_k2, subln_g, hy_conv_w, hy_conv_b, filt_w1, filt_b1, filt_freq, filt_w2, filt_b2, filt_w3, hy_bias, w_branch_attn, w_branch_hyena, w_out, router_w, router_bias, exp_w_gate, exp_w_up, exp_w_down, shared_w_gate, shared_w_up, shared_w_down, final_norm_g):
    B, L, D = x.shape
    assert ada_w.shape[0] == 1 and B % 2 == 0 and (2 * L) % (FFT_SLABS * FFT_STRIP) == 0
    lam = (jnp.exp(jnp.sum(lam_q1[0].astype(F32) * lam_k1[0].astype(F32)))
           - jnp.exp(jnp.sum(lam_q2[0].astype(F32) * lam_k2[0].astype(F32))) + LAM_INIT).reshape(1, 1)

    cc = jnp.zeros((16, D), F32).at[:B].set(c).at[B].set(c_ctx)
    mod = _adaln(cc, ada_w[0], ada_b[0])
    chunk = lambda r0, r1, j: mod[r0:r1, j * D:(j + 1) * D]
    sh1, sc1, g1, sh2, sc2, g2 = (chunk(0, B, j).reshape(B, 1, D) for j in range(6))
    csh, csc = chunk(B, B + 1, 0), chunk(B, B + 1, 1)

    w_bf = w_in[0].astype(BF16)
    n1g = norm1_g[0].reshape(1, D)
    cos_t, sin_t = _rope_tables(L)
    k, v, q, u, gates = _inproj(x, sh1, sc1, n1g, w_bf, cos_t, sin_t, hy_conv_w[0], hy_conv_b[0].reshape(1, -1),
                                _tile(L, 512))
    kc, vc = _ctxproj(ctx, csh, csc, n1g, w_bf)
    attn = _attention(lam, q, k, v, kc, vc, subln_g[0].reshape(1, V_DIM), _tile(L, 2048), _tile(L, 512))

    filt = (filt_w1[0], filt_b1[0], filt_freq[0], filt_w2[0], filt_b2[0], filt_w3[0])
    hy = _hyena(u, filt, hy_bias[0])

    x1, hx2, logits_t = _merge(attn, hy, gates, x, g1, sh2, sc2, norm2_g[0].reshape(1, D),
                               w_branch_attn[0].astype(BF16), w_branch_hyena[0].astype(BF16),
                               w_out[0].astype(BF16), router_w[0].T.astype(F32), _tile(L, 1024))
    eidx, w_tok, rank, counts = _router(logits_t, router_bias[0], _tile(B * L, 1024))

    N = B * L
    tm = _tile(L, MOE_TOKENS)
    pos3, pad_pos, blk_e, nb = _dispatch_plan(eidx, rank, counts, MOE_ROWS, tm)
    xs = _dispatch(pos3, pad_pos, hx2, nb * MOE_ROWS, tm)
    y = _experts(xs, blk_e, exp_w_gate[0], exp_w_up[0], exp_w_down[0], nb, MOE_ROWS)
    sgu = jnp.concatenate([shared_w_gate[0], shared_w_up[0]], axis=-1).astype(BF16)
    out = _combine(pos3, y, hx2, w_tok, sgu, shared_w_down[0].astype(BF16), x1.reshape(N, D), g2,
                   final_norm_g.reshape(1, D), tm, L // tm)
    return out.reshape(B, L, D)
```

```python
import functools
import math

import jax
import jax.numpy as jnp
from jax import lax
from jax.experimental import pallas as pl
from jax.experimental.pallas import tpu as pltpu

F32 = jnp.float32
BF16 = jnp.bfloat16
U32 = jnp.uint32
HIGHEST = lax.Precision.HIGHEST

NORM_EPS = 1e-6
N_HEADS = 8
HEAD_DIM = 64
V_DIM = 2 * HEAD_DIM
GRID_W = 64
ROPE_THETA = 10000.0
ROPE_AXIS_DIM = HEAD_DIM // 2
HYENA_W = 1024
FILTER_EMB = 33
FILTER_BANDS = (FILTER_EMB - 1) // 2
DECAY_TARGET = 1e-2
FAST_DECAY_PCT = 0.3
SLOW_DECAY_PCT = 1.5
N_EXPERTS = 64
N_GROUPS = 8
TOPK_GROUPS = 4
TOP_K = 8
EXPERT_HIDDEN = 256
ROUTED_SCALE = 2.5
LAM_INIT = 0.8 - 0.6 * math.exp(-0.3 * 0)

V7X_VMEM_LIMIT = 56 * 1024 * 1024
LOG2E = 1.4426950408889634


def _cparams(sem):
    return pltpu.CompilerParams(dimension_semantics=sem, vmem_limit_bytes=V7X_VMEM_LIMIT)


def _sigmoid(v):
    return 1.0 / (1.0 + jnp.exp(-v))


def _rms(v):
    return v * lax.rsqrt(jnp.mean(v * v, axis=-1, keepdims=True) + NORM_EPS)


def _adaln_kernel(c_ref, w_ref, b_ref, o_ref):
    c = c_ref[...]
    s = c * _sigmoid(c)
    o_ref[...] = jnp.dot(s, w_ref[...], precision=HIGHEST, preferred_element_type=F32) + b_ref[...]


def _adaln(cc, w, b):
    rows, d = cc.shape
    n = w.shape[1]
    tn = 1536
    return pl.pallas_call(
        _adaln_kernel,
        grid=(n // tn,),
        in_specs=[pl.BlockSpec((rows, d), lambda j: (0, 0)),
                  pl.BlockSpec((d, tn), lambda j: (0, j)),
                  pl.BlockSpec((1, tn), lambda j: (0, j))],
        out_specs=pl.BlockSpec((rows, tn), lambda j: (0, j)),
        out_shape=jax.ShapeDtypeStruct((rows, n), F32),
        compiler_params=_cparams(("arbitrary",)),
        name="adaln",
    )(cc, w, b.reshape(1, n))


def _inproj_kernel(x_ref, xp_ref, xn_ref, sh_ref, sc_ref, g_ref, w_ref, cos_ref, sin_ref, cw_ref, cb_ref,
                   k_ref, v_ref, q_ref, u_ref, gt_ref, pscr, *, tm, nt, qscale):
    i = pl.program_id(1)
    g = g_ref[...]
    sh = sh_ref[...]
    sc = 1.0 + sc_ref[...]

    def norm_mod(xx):
        return (_rms(xx) * g) * sc + sh

    h = norm_mod(x_ref[...])
    hp = jnp.where(i > 0, norm_mod(xp_ref[...]), 0.0)
    hn = jnp.where(i < nt - 1, norm_mod(xn_ref[...]), 0.0)
    hb = h.astype(BF16)
    h_ext = jnp.concatenate([hp, h, hn], axis=0).astype(BF16)

    def proj(lhs, c0):
        return jnp.dot(lhs, w_ref[:, c0:c0 + 1024], preferred_element_type=F32)

    cosv = cos_ref[...]
    sinv = sin_ref[...]
    lane = lax.broadcasted_iota(jnp.int32, (1, V_DIM), 1)
    first = (lane % ROPE_AXIS_DIM) < (ROPE_AXIS_DIM // 2)

    def rope_store(a, o_ref, scale):
        for hh in range(N_HEADS):
            s = a[:, hh * V_DIM:(hh + 1) * V_DIM]
            partner = jnp.where(first, pltpu.roll(s, V_DIM - ROPE_AXIS_DIM // 2, 1),
                                pltpu.roll(s, ROPE_AXIS_DIM // 2, 1))
            r = s * cosv + partner * sinv
            if scale != 1.0:
                r = r * scale
            o_ref[:, hh * V_DIM:(hh + 1) * V_DIM] = r.astype(BF16)

    rope_store(proj(hb, 0), k_ref, 1.0)
    v_ref[...] = proj(hb, 1024).astype(BF16)
    rope_store(proj(hb, 2048), q_ref, qscale)
    for j in range(3):
        pscr[...] = proj(h_ext, 3072 + j * 1024)
        cw = cw_ref[:, j * 1024:(j + 1) * 1024]
        cb = cb_ref[:, j * 1024:(j + 1) * 1024]
        u = (pscr[pl.ds(7, tm), :] * cw[0:1] + pscr[pl.ds(8, tm), :] * cw[1:2]
             + pscr[pl.ds(9, tm), :] * cw[2:3] + cb)
        u_ref[:, j * 1024:(j + 1) * 1024] = u.astype(BF16)
    for j in range(2):
        gt_ref[:, j * 1024:(j + 1) * 1024] = proj(hb, 6144 + j * 1024).astype(BF16)


def _inproj(x, shift, scale, g, w_bf, cos_t, sin_t, conv_w, conv_b, tm):
    B, L, D = x.shape
    nt = L // tm
    nb8 = L // 8
    qscale = (HEAD_DIM ** -0.5) * LOG2E
    kern = functools.partial(_inproj_kernel, tm=tm, nt=nt, qscale=qscale)
    row = lambda b, i: (b, i, 0)
    per_b = lambda b, i: (b, 0, 0)
    const = lambda b, i: (0, 0)
    outs = pl.pallas_call(
        kern,
        grid=(B, nt),
        in_specs=[
            pl.BlockSpec((None, tm, D), row),
            pl.BlockSpec((None, 8, D), lambda b, i: (b, jnp.maximum(i * (tm // 8) - 1, 0), 0)),
            pl.BlockSpec((None, 8, D), lambda b, i: (b, jnp.minimum((i + 1) * (tm // 8), nb8 - 1), 0)),
            pl.BlockSpec((None, 1, D), per_b),
            pl.BlockSpec((None, 1, D), per_b),
            pl.BlockSpec((1, D), const),
            pl.BlockSpec((D, 8192), const, pipeline_mode=pl.Buffered(1)),
            pl.BlockSpec((tm, V_DIM), lambda b, i: (i, 0)),
            pl.BlockSpec((tm, V_DIM), lambda b, i: (i, 0)),
            pl.BlockSpec((3, 3 * HYENA_W), const),
            pl.BlockSpec((1, 3 * HYENA_W), const),
        ],
        out_specs=[
            pl.BlockSpec((None, tm, 1024), row),
            pl.BlockSpec((None, tm, 1024), row),
            pl.BlockSpec((None, tm, 1024), row),
            pl.BlockSpec((None, tm, 3 * HYENA_W), row),
            pl.BlockSpec((None, tm, 2048), row),
        ],
        out_shape=[
            jax.ShapeDtypeStruct((B, L, 1024), BF16),
            jax.ShapeDtypeStruct((B, L, 1024), BF16),
            jax.ShapeDtypeStruct((B, L, 1024), BF16),
            jax.ShapeDtypeStruct((B, L, 3 * HYENA_W), BF16),
            jax.ShapeDtypeStruct((B, L, 2048), BF16),
        ],
        scratch_shapes=[pltpu.VMEM((tm + 16, 1024), F32)],
        compiler_params=_cparams(("parallel", "arbitrary")),
        name="inproj",
    )(x, x, x, shift, scale, g, w_bf, cos_t, sin_t, conv_w, conv_b)
    return outs


def _ctxproj_kernel(x_ref, sh_ref, sc_ref, g_ref, w_ref, k_ref, v_ref):
    h = ((_rms(x_ref[...]) * g_ref[...]) * (1.0 + sc_ref[...]) + sh_ref[...]).astype(BF16)
    k_ref[...] = jnp.dot(h, w_ref[:, 0:1024], preferred_element_type=F32).astype(BF16)
    v_ref[...] = jnp.dot(h, w_ref[:, 1024:2048], preferred_element_type=F32).astype(BF16)


def _ctxproj(ctx, shift, scale, g, w_bf):
    B, Lc, D = ctx.shape
    return pl.pallas_call(
        _ctxproj_kernel,
        grid=(B,),
        in_specs=[pl.BlockSpec((None, Lc, D), lambda b: (b, 0, 0)),
                  pl.BlockSpec((1, D), lambda b: (0, 0)),
                  pl.BlockSpec((1, D), lambda b: (0, 0)),
                  pl.BlockSpec((1, D), lambda b: (0, 0)),
                  pl.BlockSpec((D, 2048), lambda b: (0, 0))],
        out_specs=[pl.BlockSpec((None, Lc, 1024), lambda b: (b, 0, 0)),
                   pl.BlockSpec((None, Lc, 1024), lambda b: (b, 0, 0))],
        out_shape=[jax.ShapeDtypeStruct((B, Lc, 1024), BF16),
                   jax.ShapeDtypeStruct((B, Lc, 1024), BF16)],
        compiler_params=_cparams(("arbitrary",)),
        name="ctxproj",
    )(ctx, shift, scale, g, w_bf)


def _attn_kernel(lam_ref, q_ref, k_ref, v_ref, kc_ref, vc_ref, g_ref, o_ref, m_s, acc_s, *, tk, nk):
    q = q_ref[...]
    lane = lax.broadcasted_iota(jnp.int32, (1, V_DIM), 1)
    zero = jnp.zeros_like(q)
    qs = (jnp.where(lane < HEAD_DIM, q, zero), jnp.where(lane >= HEAD_DIM, q, zero))
    def chunk(kk, vv, first=False):
        reps = kk.shape[0] // V_DIM
        v1 = jnp.concatenate([vv, jnp.ones_like(vv)], axis=1)
        for m in range(2):
            s = lax.dot_general(qs[m], kk, (((1,), (1,)), ((), ())), preferred_element_type=F32)
            m_new = jnp.broadcast_to(jnp.max(s, axis=-1, keepdims=True), m_s.shape[1:])
            if not first:
                m_prev = m_s[m]
                m_new = jnp.maximum(m_prev, m_new)
                alpha = jnp.exp2(m_prev - m_new)
            p = jnp.exp2(s - jnp.concatenate([m_new] * reps, axis=1))
            pv = jnp.dot(p.astype(BF16), v1, preferred_element_type=F32)
            acc_s[m] = pv if first else jnp.concatenate([alpha, alpha], axis=1) * acc_s[m] + pv
            m_s[m] = m_new

    def body(j, carry):
        off = pl.multiple_of(j * tk, tk)
        chunk(k_ref[pl.ds(off, tk), :], v_ref[pl.ds(off, tk), :])
        return carry

    chunk(kc_ref[...], vc_ref[...], first=True)
    lax.fori_loop(0, nk, body, 0, unroll=2 if nk % 2 == 0 else 1)
    lam = lam_ref[0, 0]
    a0, a1 = acc_s[0], acc_s[1]
    o = a0[:, :V_DIM] / a0[:, V_DIM:] - lam * (a1[:, :V_DIM] / a1[:, V_DIM:])
    o = _rms(o) * g_ref[...] * (1.0 - LAM_INIT)
    o_ref[...] = o.astype(BF16)


def _attention(lam, q, k, v, kc, vc, subln_g, tq, tk):
    B, L, _ = q.shape
    Lc = kc.shape[1]
    kern = functools.partial(_attn_kernel, tk=tk, nk=L // tk)
    kv = lambda b, h, i: (b, 0, h)
    return pl.pallas_call(
        kern,
        grid=(B, N_HEADS, L // tq),
        in_specs=[
            pl.BlockSpec(memory_space=pltpu.SMEM),
            pl.BlockSpec((None, tq, V_DIM), lambda b, h, i: (b, i, h)),
            pl.BlockSpec((None, L, V_DIM), kv),
            pl.BlockSpec((None, L, V_DIM), kv),
            pl.BlockSpec((None, Lc, V_DIM), kv),
            pl.BlockSpec((None, Lc, V_DIM), kv),
            pl.BlockSpec((1, V_DIM), lambda b, h, i: (0, 0)),
        ],
        out_specs=pl.BlockSpec((None, tq, V_DIM), lambda b, h, i: (b, i, h)),
        out_shape=jax.ShapeDtypeStruct((B, L, N_HEADS * V_DIM), BF16),
        scratch_shapes=[pltpu.VMEM((2, tq, V_DIM), F32), pltpu.VMEM((2, tq, 2 * V_DIM), F32)],
        compiler_params=_cparams(("parallel", "parallel", "arbitrary")),
        name="diffattn",
    )(lam, q, k, v, kc, vc, subln_g)


FFT_SLABS = 64
FFT_STRIP = 16
FILTER_TAP_STEPS = 16


def _dft_tables(nb):
    N = FFT_SLABS * nb
    fa = jnp.arange(FFT_SLABS, dtype=jnp.int32)[:, None, None]
    fb = jnp.arange(nb, dtype=jnp.int32)[None, :, None]
    b = jnp.arange(nb, dtype=jnp.int32)[None, None, :]
    ang = ((FFT_SLABS * b * fb + b * fa) % N).astype(F32) * (2.0 * math.pi / N)
    c, s = jnp.cos(ang), jnp.sin(ang)
    fwd = jnp.concatenate([jnp.concatenate([c, s], -1), jnp.concatenate([-s, c], -1)], axis=1)
    inv = jnp.swapaxes(fwd, 1, 2) / N
    return fwd.astype(BF16), inv.astype(BF16)


def _vadd(a, b):
    return b if a is None else (a if b is None else a + b)


def _vsub(a, b):
    if b is None:
        return a
    return -b if a is None else a - b


def _vscale(a, k):
    if a is None or k == 1.0:
        return a
    return -a if k == -1.0 else a * k


def _cmulc(x, c, s):
    re, im = x
    eps = 1e-9
    if abs(s) < eps:
        k = 1.0 if c > 0 else -1.0
        return _vscale(re, k), _vscale(im, k)
    if abs(c) < eps:
        k = 1.0 if s > 0 else -1.0
        return _vscale(im, -k), _vscale(re, k)
    if abs(abs(c) - abs(s)) < eps:
        kc = 1.0 if c > 0 else -1.0
        ks = 1.0 if s > 0 else -1.0
        return (_vscale(_vsub(_vscale(re, kc), _vscale(im, ks)), abs(c)),
                _vscale(_vadd(_vscale(re, ks), _vscale(im, kc)), abs(c)))
    return _vsub(_vscale(re, c), _vscale(im, s)), _vadd(_vscale(re, s), _vscale(im, c))


def _fft(xs, sign):
    n = len(xs)
    if n == 1:
        return xs
    ev, od = _fft(xs[0::2], sign), _fft(xs[1::2], sign)
    out = [None] * n
    for k in range(n // 2):
        ang = sign * 2.0 * math.pi * k / n
        t = _cmulc(od[k], math.cos(ang), math.sin(ang))
        out[k] = (_vadd(ev[k][0], t[0]), _vadd(ev[k][1], t[1]))
        out[k + n // 2] = (_vsub(ev[k][0], t[0]), _vsub(ev[k][1], t[1]))
    return out


def _put(ref, slot, rows, val):
    ref[slot, rows, :] = jnp.zeros((FFT_STRIP, ref.shape[2]), F32) if val is None else val


def _slab_fft_forward(load, s_ref, nb, n_in):
    def strip(i, carry):
        r0 = pl.multiple_of(i * FFT_STRIP, FFT_STRIP)
        rows_re, rows_im = pl.ds(r0, FFT_STRIP), pl.ds(nb + r0, FFT_STRIP)
        for a0 in range(8):
            xs = [load(8 * a1 + a0, r0) if a1 < n_in else (None, None) for a1 in range(8)]
            ys = _fft(xs, -1.0)
            for f1 in range(8):
                ang = -2.0 * math.pi * a0 * f1 / FFT_SLABS
                yr, yi = _cmulc(ys[f1], math.cos(ang), math.sin(ang))
                _put(s_ref, a0 * 8 + f1, rows_re, yr)
                _put(s_ref, a0 * 8 + f1, rows_im, yi)
        for f1 in range(8):
            xs = [(s_ref[a0 * 8 + f1, rows_re, :], s_ref[a0 * 8 + f1, rows_im, :]) for a0 in range(8)]
            ys = _fft(xs, -1.0)
            for f0 in range(8):
                s_ref[f1 + 8 * f0, rows_re, :] = ys[f0][0]
                s_ref[f1 + 8 * f0, rows_im, :] = ys[f0][1]
        return carry

    lax.fori_loop(0, nb // FFT_STRIP, strip, 0)


def _slab_fft_inverse(s_ref, nb, emit):
    def strip(i, carry):
        r0 = pl.multiple_of(i * FFT_STRIP, FFT_STRIP)
        rows_re, rows_im = pl.ds(r0, FFT_STRIP), pl.ds(nb + r0, FFT_STRIP)
        for q in range(8):
            xs = [(s_ref[8 * p + q, rows_re, :], s_ref[8 * p + q, rows_im, :]) for p in range(8)]
            us = _fft(xs, 1.0)
            for r in range(8):
                ang = 2.0 * math.pi * r * q / FFT_SLABS
                ur, ui = _cmulc(us[r], math.cos(ang), math.sin(ang))
                s_ref[8 * r + q, rows_re, :] = ur
                s_ref[8 * r + q, rows_im, :] = ui
        for r in range(8):
            xs = [(s_ref[8 * r + q, rows_re, :], s_ref[8 * r + q, rows_im, :]) for q in range(8)]
            ys = _fft(xs, 1.0)
            for sidx in range(4):
                emit(r + 8 * sidx, r0, ys[sidx][0], ys[sidx][1])
        return carry

    lax.fori_loop(0, nb // FFT_STRIP, strip, 0)


def _hyena_conv_kernel(v_ref, x_ref, k_ref, ff_ref, fi_ref, bias_ref, o_ref, s_ref, *, nb):
    def load(a, r0):
        rows = pl.ds(pl.multiple_of(a * nb + r0, FFT_STRIP), FFT_STRIP)
        return v_ref[0, rows, :].astype(F32), v_ref[1, rows, :].astype(F32)

    _slab_fft_forward(load, s_ref, nb, FFT_SLABS // 16)

    def freq(f, carry):
        z = jnp.dot(ff_ref[f], s_ref[f].astype(BF16), preferred_element_type=F32)
        zr, zi = z[:nb], z[nb:]
        kr, ki = k_ref[0, f], k_ref[1, f]
        y = jnp.concatenate([zr * kr - zi * ki, zr * ki + zi * kr], axis=0).astype(BF16)
        s_ref[f] = jnp.dot(fi_ref[f], y, preferred_element_type=F32)
        return carry

    lax.fori_loop(0, FFT_SLABS, freq, 0, unroll=8)
    bias = bias_ref[...]

    def emit(a, r0, re, im):
        rows = pl.ds(pl.multiple_of(a * nb + r0, FFT_STRIP), FFT_STRIP)
        for half, y in enumerate((re, im)):
            v = v_ref[half, rows, :].astype(F32)
            o_ref[half, rows, :] = (x_ref[half, rows, :].astype(F32) * (y + v * bias)).astype(BF16)

    _slab_fft_inverse(s_ref, nb, emit)


def _hyena_conv(uv, zv, k_hat, ff, fi, bias, order, in_col, gate_col, nb):
    _, P, L, _ = uv.shape
    W = HYENA_W
    nct = W // 128
    src, src_off = (uv, in_col * nct) if zv is None else (zv, 0)
    blk = (2, None, L, 128)
    return pl.pallas_call(
        functools.partial(_hyena_conv_kernel, nb=nb),
        grid=(nct, P),
        in_specs=[pl.BlockSpec(blk, lambda c, p: (0, p, 0, src_off + c)),
                  pl.BlockSpec(blk, lambda c, p: (0, p, 0, gate_col * nct + c)),
                  pl.BlockSpec((2, FFT_SLABS, nb, 128), lambda c, p: (0, 0, 0, order * nct + c),
                               pipeline_mode=pl.Buffered(1)),
                  pl.BlockSpec((FFT_SLABS, 2 * nb, 2 * nb), lambda c, p: (0, 0, 0), pipeline_mode=pl.Buffered(1)),
                  pl.BlockSpec((FFT_SLABS, 2 * nb, 2 * nb), lambda c, p: (0, 0, 0), pipeline_mode=pl.Buffered(1)),
                  pl.BlockSpec((1, 128), lambda c, p: (0, c))],
        out_specs=pl.BlockSpec(blk, lambda c, p: (0, p, 0, c)),
        out_shape=jax.ShapeDtypeStruct((2, P, L, W), BF16),
        scratch_shapes=[pltpu.VMEM((FFT_SLABS, 2 * nb, 128), F32)],
        compiler_params=_cparams(("parallel", "arbitrary")),
        name="hyena_conv%d" % order,
    )(src, uv, k_hat, ff, fi, bias)


def _filter_features(n):
    pos = jnp.arange(n, dtype=F32)[:, None]
    t = jnp.linspace(0.0, 1.0, n, dtype=F32)[:, None]
    w = 2 * math.pi * pos / n
    bands = jnp.linspace(1e-4, FILTER_BANDS - 1, FILTER_BANDS, dtype=F32)
    z = jnp.concatenate([t, jnp.cos(bands * w), -jnp.sin(bands * w)], axis=-1)
    z = jnp.pad(z, ((0, 0), (0, 128 - FILTER_EMB)))
    return jnp.concatenate([z, z[0:1], z[:0:-1]], axis=0)


def _dot_bf16x3(a, b):
    ah, bh = a.astype(BF16), b.astype(BF16)
    al, bl = (a - ah.astype(F32)).astype(BF16), (b - bh.astype(F32)).astype(BF16)
    dot = functools.partial(jnp.dot, preferred_element_type=F32)
    return dot(ah, bh) + (dot(ah, bl) + dot(al, bh))


def _filt_taps_kernel(z_ref, w1_ref, b1_ref, fr_ref, w2_ref, b2_ref, w3_ref, dl_ref, k_ref, sum_ref):
    a = pl.program_id(0)
    z = z_ref[...]
    fr = fr_ref[...]
    hd = jnp.sin(fr * (jnp.dot(z, w1_ref[...], precision=HIGHEST, preferred_element_type=F32) + b1_ref[...]))
    hd = jnp.sin(fr * (jnp.dot(hd, w2_ref[...], precision=HIGHEST, preferred_element_type=F32) + b2_ref[...]))
    h = _dot_bf16x3(hd, w3_ref[...]) * jnp.exp(-z[:, 0:1] * dl_ref[...])
    rows = lax.broadcasted_iota(jnp.int32, (z.shape[0], 1), 0)
    h = jnp.where(jnp.logical_and(a == pl.num_programs(0) // 2, rows == 0), 0.0, h)
    k_ref[...] = h

    @pl.when(a == 0)
    def _():
        sum_ref[...] = jnp.zeros_like(sum_ref)

    sum_ref[0:1, :] += jnp.sum(jnp.abs(h), axis=0, keepdims=True)


def _filt_spectrum_kernel(t_ref, inv_ref, ff_ref, k_ref, s_ref, *, nb):
    def load(a, r0):
        return t_ref[pl.ds(pl.multiple_of(a * nb + r0, FFT_STRIP), FFT_STRIP), :], None

    _slab_fft_forward(load, s_ref, nb, FFT_SLABS // 8)
    inv = inv_ref[...]

    def freq(f, carry):
        z = jnp.dot(ff_ref[f], s_ref[f].astype(BF16), preferred_element_type=F32)
        k_ref[0, f] = z[:nb] * inv
        k_ref[1, f] = z[nb:] * inv
        return carry

    lax.fori_loop(0, FFT_SLABS, freq, 0, unroll=8)


def _hyena_filters(n, filt, ff, nb):
    w1, b1, freq, w2, b2, w3 = filt
    W = HYENA_W
    hid = w1.shape[1]
    z = _filter_features(n)
    w1p = jnp.pad(w1.astype(F32), ((0, 128 - FILTER_EMB), (0, 0)))
    w3d = w3.astype(F32).reshape(hid, 2, 2, W)
    w3sel = jnp.stack([w3d[:, :, d, :].reshape(hid, 2 * W) for d in range(2)])
    deltas = jnp.abs(jnp.linspace(math.log(DECAY_TARGET) / SLOW_DECAY_PCT,
                                  math.log(DECAY_TARGET) / FAST_DECAY_PCT, W, dtype=F32))
    deltas = jnp.tile(deltas.reshape(1, W), (1, 2))
    const = lambda a: (0, 0)
    steps = FILTER_TAP_STEPS
    rs = 2 * n // steps
    taps, sums = pl.pallas_call(
        _filt_taps_kernel,
        grid=(steps,),
        in_specs=[pl.BlockSpec((rs, 128), lambda a: (a, 0)),
                  pl.BlockSpec((128, hid), const), pl.BlockSpec((1, hid), const), pl.BlockSpec((1, hid), const),
                  pl.BlockSpec((hid, hid), const), pl.BlockSpec((1, hid), const),
                  pl.BlockSpec((None, hid, 2 * W), lambda a: (a // (steps // 2), 0, 0)),
                  pl.BlockSpec((1, 2 * W), const)],
        out_specs=[pl.BlockSpec((rs, 2 * W), lambda a: (a, 0)), pl.BlockSpec((8, 2 * W), const)],
        out_shape=[jax.ShapeDtypeStruct((2 * n, 2 * W), F32), jax.ShapeDtypeStruct((8, 2 * W), F32)],
        compiler_params=_cparams(("arbitrary",)),
        name="hyena_filter_taps",
    )(z, w1p, b1.reshape(1, hid), freq.reshape(1, hid), w2, b2.reshape(1, hid), w3sel, deltas)
    inv = 1.0 / sums[0:1]
    return pl.pallas_call(
        functools.partial(_filt_spectrum_kernel, nb=nb),
        grid=(2 * W // 128,),
        in_specs=[pl.BlockSpec((2 * n, 128), lambda j: (0, j)),
                  pl.BlockSpec((1, 128), lambda j: (0, j)),
                  pl.BlockSpec((FFT_SLABS, 2 * nb, 2 * nb), lambda j: (0, 0, 0), pipeline_mode=pl.Buffered(1))],
        out_specs=pl.BlockSpec((2, FFT_SLABS, nb, 128), lambda j: (0, 0, 0, j)),
        out_shape=jax.ShapeDtypeStruct((2, FFT_SLABS, nb, 2 * W), F32),
        scratch_shapes=[pltpu.VMEM((FFT_SLABS, 2 * nb, 128), F32)],
        compiler_params=_cparams(("parallel",)),
        name="hyena_filter_spectrum",
    )(taps, inv, ff)


def _hyena(u, filt, hy_bias):
    B, L, _ = u.shape
    P = B // 2
    nb = 2 * L // FFT_SLABS
    ff, fi = _dft_tables(nb)
    k_hat = _hyena_filters(L, filt, ff, nb)
    uv = u.reshape(2, P, L, 3 * HYENA_W)
    bias = hy_bias.astype(F32)
    z = _hyena_conv(uv, None, k_hat, ff, fi, bias[0:1], 0, 0, 1, nb)
    y = _hyena_conv(uv, z, k_hat, ff, fi, bias[1:2], 1, 0, 2, nb)
    return y.reshape(B, L, HYENA_W)


def _merge_kernel(a_ref, h_ref, gt_ref, x_ref, g1_ref, sh_ref, sc_ref, n2g_ref, wpa_ref, wph_ref, wo_ref, rw_ref,
                  x1_ref, hx_ref, lg_ref):
    ya = jnp.dot(a_ref[...], wpa_ref[...], preferred_element_type=F32)
    yh = jnp.dot(h_ref[...], wph_ref[...], preferred_element_type=F32)
    ga = gt_ref[:, 0:1024].astype(F32)
    gh = gt_ref[:, 1024:2048].astype(F32)
    y = _sigmoid(ga) * ya + _sigmoid(gh) * yh
    mix = jnp.dot(y.astype(BF16), wo_ref[...], preferred_element_type=F32)
    x1 = x_ref[...] + g1_ref[...] * mix
    x1_ref[...] = x1
    hx = (_rms(x1) * n2g_ref[...]) * (1.0 + sc_ref[...]) + sh_ref[...]
    _to_tiles(hx_ref, hx)
    lg_ref[...] = lax.dot_general(rw_ref[...], hx, (((1,), (1,)), ((), ())), precision=HIGHEST,
                                  preferred_element_type=F32)


def _merge(attn, hy, gates, x, g1, sh2, sc2, n2g, wpa, wph, wo, rwT, tm):
    B, L, D = x.shape
    nt = L // tm
    row = lambda b, i: (b, i, 0)
    per_b = lambda b, i: (b, 0, 0)
    const = lambda b, i: (0, 0)
    return pl.pallas_call(
        _merge_kernel,
        grid=(B, nt),
        in_specs=[pl.BlockSpec((None, tm, 1024), row), pl.BlockSpec((None, tm, 1024), row),
                  pl.BlockSpec((None, tm, 2048), row), pl.BlockSpec((None, tm, D), row),
                  pl.BlockSpec((None, 1, D), per_b), pl.BlockSpec((None, 1, D), per_b),
                  pl.BlockSpec((None, 1, D), per_b), pl.BlockSpec((1, D), const),
                  pl.BlockSpec((1024, D), const), pl.BlockSpec((1024, D), const), pl.BlockSpec((D, D), const),
                  pl.BlockSpec((N_EXPERTS, D), const)],
        out_specs=[pl.BlockSpec((None, tm, D), row),
                   pl.BlockSpec((tm * TILE_ROWS, 128), lambda b, i: (b * nt + i, 0)),
                   pl.BlockSpec((N_EXPERTS, tm), lambda b, i: (0, b * nt + i))],
        out_shape=[jax.ShapeDtypeStruct((B, L, D), F32),
                   jax.ShapeDtypeStruct((B * L * TILE_ROWS, 128), U32),
                   jax.ShapeDtypeStruct((N_EXPERTS, B * L), F32)],
        compiler_params=_cparams(("parallel", "arbitrary")),
        name="merge",
    )(attn, hy, gates, x, g1, sh2, sc2, n2g, wpa, wph, wo, rwT)


def _router_kernel(lg_ref, rb_ref, tri_ref, ei_ref, wt_ref, rk_ref, cnt_ref):
    scores = _sigmoid(lg_ref[...])
    choice = scores + rb_ref[...]
    E, T = scores.shape
    gsz = E // N_GROUPS
    neg = jnp.float32(-jnp.inf)
    big = jnp.int32(1 << 30)
    rows8 = lax.broadcasted_iota(jnp.int32, (gsz, T), 0)
    gscore = []
    for gi in range(N_GROUPS):
        c = choice[gi * gsz:(gi + 1) * gsz]
        m1 = jnp.max(c, axis=0, keepdims=True)
        i1 = jnp.min(jnp.where(c == m1, rows8, big), axis=0, keepdims=True)
        m2 = jnp.max(jnp.where(rows8 == i1, neg, c), axis=0, keepdims=True)
        gscore.append(m1 + m2)
    gs = jnp.concatenate(gscore, axis=0)
    rowsg = lax.broadcasted_iota(jnp.int32, (N_GROUPS, T), 0)
    gsel = jnp.zeros((N_GROUPS, T), jnp.bool_)
    for _ in range(TOPK_GROUPS):
        m = jnp.max(gs, axis=0, keepdims=True)
        ix = jnp.min(jnp.where(gs == m, rowsg, big), axis=0, keepdims=True)
        hit = rowsg == ix
        gsel = jnp.logical_or(gsel, hit)
        gs = jnp.where(hit, neg, gs)
    masked = jnp.concatenate(
        [jnp.where(gsel[gi:gi + 1], choice[gi * gsz:(gi + 1) * gsz], neg) for gi in range(N_GROUPS)], axis=0)
    rows = lax.broadcasted_iota(jnp.int32, (E, T), 0)
    ids, ws, hits = [], [], []
    sel = jnp.zeros((E, T), F32)
    for _ in range(TOP_K):
        m = jnp.max(masked, axis=0, keepdims=True)
        ix = jnp.min(jnp.where(masked == m, rows, big), axis=0, keepdims=True)
        hit = rows == ix
        ids.append(ix)
        hits.append(hit)
        ws.append(jnp.sum(jnp.where(hit, scores, 0.0), axis=0, keepdims=True))
        sel = jnp.where(hit, 1.0, sel)
        masked = jnp.where(hit, neg, masked)
    w = jnp.concatenate(ws, axis=0)
    w = w / jnp.sum(w, axis=0, keepdims=True) * ROUTED_SCALE
    ei_ref[...] = jnp.concatenate(ids, axis=0)
    wt_ref[...] = jnp.concatenate([w, jnp.zeros((128 - TOP_K, T), F32)], axis=0).T

    @pl.when(pl.program_id(0) == 0)
    def _():
        cnt_ref[...] = jnp.zeros_like(cnt_ref)

    before = cnt_ref[...][:, 0:1]
    csum = jnp.dot(sel.astype(BF16), tri_ref[...], preferred_element_type=F32)
    rank = before + csum - sel
    rk_ref[...] = jnp.concatenate(
        [jnp.sum(jnp.where(h, rank, 0.0), axis=0, keepdims=True) for h in hits], axis=0).astype(jnp.int32)
    cnt_ref[...] = jnp.broadcast_to(before + csum[:, T - 1:T], cnt_ref.shape)


def _router(logits_t, router_bias, tt):
    E, N = logits_t.shape
    tri = jnp.triu(jnp.ones((tt, tt), BF16))
    per_tok = pl.BlockSpec((TOP_K, tt), lambda i: (0, i))
    return pl.pallas_call(
        _router_kernel,
        grid=(N // tt,),
        in_specs=[pl.BlockSpec((E, tt), lambda i: (0, i)), pl.BlockSpec((E, 1), lambda i: (0, 0)),
                  pl.BlockSpec((tt, tt), lambda i: (0, 0))],
        out_specs=[per_tok, pl.BlockSpec((tt, 128), lambda i: (i, 0)), per_tok,
                   pl.BlockSpec((E, 128), lambda i: (0, 0))],
        out_shape=[jax.ShapeDtypeStruct((TOP_K, N), jnp.int32), jax.ShapeDtypeStruct((N, 128), F32),
                   jax.ShapeDtypeStruct((TOP_K, N), jnp.int32), jax.ShapeDtypeStruct((E, 128), F32)],
        compiler_params=_cparams(("arbitrary",)),
        name="router",
    )(logits_t, router_bias.reshape(E, 1).astype(F32), tri)


MOE_ROWS = 512
MOE_TOKENS = 128


def _plan_kernel(ei_ref, rk_ref, ps_ref, pos_ref):
    ei = ei_ref[...]
    ps = ps_ref[...]
    rows = lax.broadcasted_iota(jnp.int32, (ps.shape[0], ei.shape[1]), 0)
    start = [jnp.sum(jnp.where(rows == ei[k:k + 1], ps, 0.0), axis=0, keepdims=True) for k in range(ei.shape[0])]
    pos_ref[...] = rk_ref[...] + jnp.concatenate(start, axis=0).astype(jnp.int32)


def _dispatch_plan(eidx, rank, counts, bm, tm):
    K, N = eidx.shape
    E = N_EXPERTS
    cnt = counts[:, 0].astype(jnp.int32)
    padded = (cnt + bm - 1) // bm * bm
    pad_end = jnp.cumsum(padded)
    pad_start = pad_end - padded
    nb = -(-(N * K + E * (bm - 1)) // bm)
    tt = _tile(N, 2048)
    per_tok = pl.BlockSpec((K, tt), lambda i: (0, i))
    pos = pl.pallas_call(
        _plan_kernel,
        grid=(N // tt,),
        in_specs=[per_tok, per_tok, pl.BlockSpec((E, 1), lambda i: (0, 0))],
        out_specs=per_tok,
        out_shape=jax.ShapeDtypeStruct((K, N), jnp.int32),
        compiler_params=_cparams(("parallel",)),
        name="moe_plan",
    )(eidx, rank, pad_start.astype(F32).reshape(E, 1))
    blk_start = jnp.arange(nb, dtype=jnp.int32) * bm
    blk_e = jnp.minimum(jnp.sum(blk_start[:, None] >= pad_end[None, :], axis=1), E - 1)
    n_pad = nb * bm - N * K
    assert n_pad % (K * tm) == 0
    cpad = jnp.cumsum(padded - cnt)
    q = jnp.arange(n_pad, dtype=jnp.int32)
    seg = jnp.sum(q[:, None] >= cpad[None, :], axis=1)
    onehot = seg[:, None] == jnp.arange(E + 1, dtype=jnp.int32)[None, :]
    first = jnp.concatenate([pad_start + cnt, pad_end[-1:]])
    skipped = jnp.concatenate([jnp.zeros((1,), cpad.dtype), cpad])
    pad_pos = (q + jnp.sum(jnp.where(onehot, (first - skipped)[None, :], 0), axis=1)).astype(jnp.int32)
    nt = N // tm
    pos3 = pos.reshape(K, nt, tm).transpose(1, 0, 2).reshape(nt, 1, K * tm)
    return pos3, pad_pos.reshape(-1, 1, K * tm), blk_e.astype(jnp.int32), nb


TILE_ROWS = 4
_HI16 = 0xFFFF0000


def _bf16_bits(v):
    return lax.bitcast_convert_type(v.astype(BF16).astype(F32), U32)


def _to_tiles(ref, val):
    m, half = val.shape[0], val.shape[1] // 2
    words = (lax.shift_right_logical(_bf16_bits(val[:, :half]), jnp.uint32(16))
             | (_bf16_bits(val[:, half:]) & jnp.uint32(_HI16)))
    for j in range(TILE_ROWS):
        ref[pl.ds(j, m, stride=TILE_ROWS), :] = words[:, j * 128:(j + 1) * 128]


def _from_tiles(ref, start, m):
    words = jnp.concatenate([ref[pl.ds(start + j, m, stride=TILE_ROWS), :] for j in range(TILE_ROWS)], axis=1)
    lo = lax.bitcast_convert_type(lax.shift_left(words, jnp.uint32(16)), F32)
    hi = lax.bitcast_convert_type(words & jnp.uint32(_HI16), F32)
    return jnp.concatenate([lo, hi], axis=1)


def _row_copy(idx_ref, r, src_hbm, buf, base, sem):
    src = pl.multiple_of(idx_ref[0, r] * TILE_ROWS, TILE_ROWS)
    dst = base + r * TILE_ROWS
    if not isinstance(dst, int):
        dst = pl.multiple_of(dst, TILE_ROWS)
    return pltpu.make_async_copy(src_hbm.at[pl.ds(src, TILE_ROWS), :], buf.at[pl.ds(dst, TILE_ROWS), :], sem)


def _pipelined_gather(cur_ref, nxt_ref, src_hbm, buf, sem, nrows, consume):
    i = pl.program_id(0)
    span = nrows * TILE_ROWS

    def wait(slot):
        pltpu.make_async_copy(src_hbm.at[pl.ds(0, span), :], buf.at[pl.ds(slot * span, span), :], sem.at[slot]).wait()

    @pl.when(i == 0)
    def _():
        def body(r, carry):
            _row_copy(cur_ref, r, src_hbm, buf, 0, sem.at[0]).start()
            return carry
        lax.fori_loop(0, nrows, body, 0)

    for slot in range(2):
        @pl.when(i % 2 == slot)
        def _():
            for r in range(nrows):
                _row_copy(nxt_ref, r, src_hbm, buf, (1 - slot) * span, sem.at[1 - slot]).start(priority=r % 2)
            wait(slot)
            consume(slot * span)

            @pl.when(i == pl.num_programs(0) - 1)
            def _():
                wait(1 - slot)


def _dispatch_kernel(pos_ref, hx_ref, xs_hbm, stage, zero_tile, sem, *, tm, nt):
    i = pl.program_id(0)
    slot = i % 2
    span = tm * TILE_ROWS

    def wait_step(s):
        for _ in range(TOP_K):
            pltpu.make_async_copy(stage.at[pl.ds(0, span), :], xs_hbm.at[pl.ds(0, span), :], sem.at[s]).wait()

    def dst(r):
        return xs_hbm.at[pl.ds(pl.multiple_of(pos_ref[0, r] * TILE_ROWS, TILE_ROWS), TILE_ROWS), :]

    @pl.when(i < nt)
    def _():
        base = pl.multiple_of(slot * span, span)
        stage[pl.ds(base, span), :] = hx_ref[...]
        for r in range(TOP_K * tm):
            src = stage.at[pl.ds(pl.multiple_of(base + (r % tm) * TILE_ROWS, TILE_ROWS), TILE_ROWS), :]
            pltpu.make_async_copy(src, dst(r), sem.at[slot]).start(priority=r % 2)

    @pl.when(i == 0)
    def _():
        zero_tile[...] = jnp.zeros_like(zero_tile)

    @pl.when(i >= nt)
    def _():
        for r in range(TOP_K * tm):
            pltpu.make_async_copy(zero_tile, dst(r), sem.at[slot]).start(priority=r % 2)

    @pl.when(i > 0)
    def _():
        wait_step(1 - slot)

    @pl.when(i == pl.num_programs(0) - 1)
    def _():
        wait_step(slot)


def _dispatch(pos3, pad_pos, hx_tiles, rows_total, tm):
    nt = pos3.shape[0]
    idx = jnp.concatenate([pos3, pad_pos], axis=0)
    steps = idx.shape[0]
    return pl.pallas_call(
        functools.partial(_dispatch_kernel, tm=tm, nt=nt),
        grid=(steps,),
        in_specs=[pl.BlockSpec((None, 1, TOP_K * tm), lambda i: (i, 0, 0), memory_space=pltpu.SMEM),
                  pl.BlockSpec((tm * TILE_ROWS, 128), lambda i: (jnp.minimum(i, nt - 1), 0))],
        out_specs=pl.BlockSpec(memory_space=pl.ANY),
        out_shape=jax.ShapeDtypeStruct((rows_total * TILE_ROWS, 128), U32),
        scratch_shapes=[pltpu.VMEM((2 * tm * TILE_ROWS, 128), U32), pltpu.VMEM((TILE_ROWS, 128), U32),
                        pltpu.SemaphoreType.DMA((2,))],
        compiler_params=_cparams(("arbitrary",)),
        name="moe_dispatch",
    )(idx, hx_tiles)


def _expert_kernel(blk_e_ref, xs_ref, wg_ref, wu_ref, wd_ref, y_ref, *, bm):
    del blk_e_ref
    x = _from_tiles(xs_ref, 0, bm).astype(BF16)
    hg = jnp.dot(x, wg_ref[...].astype(BF16), preferred_element_type=F32)
    hu = jnp.dot(x, wu_ref[...].astype(BF16), preferred_element_type=F32)
    act = hg * _sigmoid(hg) * hu
    _to_tiles(y_ref, jnp.dot(act.astype(BF16), wd_ref[...].astype(BF16), preferred_element_type=F32))


def _experts(xs_tiles, blk_e, wg, wu, wd, nb, bm):
    D = wg.shape[1]
    grid_spec = pltpu.PrefetchScalarGridSpec(
        num_scalar_prefetch=1,
        grid=(nb,),
        in_specs=[pl.BlockSpec((bm * TILE_ROWS, 128), lambda i, be: (i, 0)),
                  pl.BlockSpec((None, D, EXPERT_HIDDEN), lambda i, be: (be[i], 0, 0)),
                  pl.BlockSpec((None, D, EXPERT_HIDDEN), lambda i, be: (be[i], 0, 0)),
                  pl.BlockSpec((None, EXPERT_HIDDEN, D), lambda i, be: (be[i], 0, 0))],
        out_specs=pl.BlockSpec((bm * TILE_ROWS, 128), lambda i, be: (i, 0)),
    )
    return pl.pallas_call(
        functools.partial(_expert_kernel, bm=bm),
        grid_spec=grid_spec,
        out_shape=jax.ShapeDtypeStruct((nb * bm * TILE_ROWS, 128), U32),
        compiler_params=_cparams(("arbitrary",)),
        name="moe_experts",
    )(blk_e, xs_tiles, wg, wu, wd)


def _combine_kernel(cur_ref, nxt_ref, y_hbm, hx_ref, wt_ref, sgu_ref, sd_ref, x1_ref, g2_ref, fg_ref, o_ref, ybuf,
                    sem, *, tm):
    def consume(here):
        wt = wt_ref[...]
        routed = None
        for k in range(TOP_K):
            wk = jnp.broadcast_to(wt[:, k:k + 1], (tm, 128))
            wk = jnp.concatenate([wk] * (x1_ref.shape[1] // 128), axis=1)
            term = wk * _from_tiles(ybuf, here + k * tm * TILE_ROWS, tm)
            routed = term if routed is None else routed + term
        hu = jnp.dot(_from_tiles(hx_ref, 0, tm).astype(BF16), sgu_ref[...], preferred_element_type=F32)
        hg = hu[:, :EXPERT_HIDDEN]
        act = hg * _sigmoid(hg) * hu[:, EXPERT_HIDDEN:]
        shared = jnp.dot(act.astype(BF16), sd_ref[...], preferred_element_type=F32)
        xo = x1_ref[...] + g2_ref[...] * (routed + shared)
        o_ref[...] = _rms(xo) * fg_ref[...]

    _pipelined_gather(cur_ref, nxt_ref, y_hbm, ybuf, sem, TOP_K * tm, consume)


def _combine(pos3, y_tiles, hx_tiles, w_tok, sgu, sd, x1, g2, fg, tm, tiles_per_batch):
    N, D = x1.shape
    nt = N // tm
    smem_blk = lambda f: pl.BlockSpec((None, 1, TOP_K * tm), f, memory_space=pltpu.SMEM)
    row = lambda i: (i, 0)
    const = lambda i: (0, 0)
    return pl.pallas_call(
        functools.partial(_combine_kernel, tm=tm),
        grid=(nt,),
        in_specs=[smem_blk(lambda i: (i, 0, 0)),
                  smem_blk(lambda i: (jnp.minimum(i + 1, nt - 1), 0, 0)),
                  pl.BlockSpec(memory_space=pl.ANY),
                  pl.BlockSpec((tm * TILE_ROWS, 128), row),
                  pl.BlockSpec((tm, 128), row),
                  pl.BlockSpec((D, 2 * EXPERT_HIDDEN), const),
                  pl.BlockSpec((EXPERT_HIDDEN, D), const),
                  pl.BlockSpec((tm, D), row),
                  pl.BlockSpec((None, 1, D), lambda i: (i // tiles_per_batch, 0, 0)),
                  pl.BlockSpec((1, D), const)],
        out_specs=pl.BlockSpec((tm, D), row),
        out_shape=jax.ShapeDtypeStruct((N, D), F32),
        scratch_shapes=[pltpu.VMEM((2 * TOP_K * tm * TILE_ROWS, 128), U32), pltpu.SemaphoreType.DMA((2,))],
        compiler_params=_cparams(("arbitrary",)),
        name="moe_combine",
    )(pos3, pos3, y_tiles, hx_tiles, w_tok, sgu, sd, x1, g2, fg)


def _rope_tables(L):
    rows = L // GRID_W
    row = jnp.repeat(jnp.arange(rows), GRID_W).astype(F32)
    col = jnp.tile(jnp.arange(GRID_W), rows).astype(F32)
    inv = ROPE_THETA ** (-jnp.arange(0, ROPE_AXIS_DIM, 2, dtype=F32) / ROPE_AXIS_DIM)
    ang = jnp.stack([row[:, None] * inv, col[:, None] * inv], axis=1)
    c, s = jnp.cos(ang), jnp.sin(ang)
    cos_h = jnp.stack([c, c], axis=2).reshape(L, HEAD_DIM)
    sin_h = jnp.stack([-s, s], axis=2).reshape(L, HEAD_DIM)
    return jnp.tile(cos_h, (1, 2)), jnp.tile(sin_h, (1, 2))


def _tile(n, pref):
    t = min(pref, n)
    while n % t:
        t //= 2
    return t


def kernel(x, c, ctx, c_ctx, ada_w, ada_b, norm1_g, norm2_g, w_in, lam_q1, lam_k1, lam_q2, lam_k2, subln_g, hy_conv_w, hy_conv_b, filt_w1, filt_b1, filt_freq, filt_w2, filt_b2, filt_w3, hy_bias, w_branch_attn, w_branch_hyena, w_out, router_w, router_bias, exp_w_gate, exp_w_up, exp_w_down, shared_w_gate, shared_w_up, shared_w_down, final_norm_g):
    B, L, D = x.shape
    assert ada_w.shape[0] == 1 and B % 2 == 0 and (2 * L) % (FFT_SLABS * FFT_STRIP) == 0
    lam = (jnp.exp(jnp.sum(lam_q1[0].astype(F32) * lam_k1[0].astype(F32)))
           - jnp.exp(jnp.sum(lam_q2[0].astype(F32) * lam_k2[0].astype(F32))) + LAM_INIT).reshape(1, 1)

    cc = jnp.zeros((16, D), F32).at[:B].set(c).at[B].set(c_ctx)
    mod = _adaln(cc, ada_w[0], ada_b[0])
    chunk = lambda r0, r1, j: mod[r0:r1, j * D:(j + 1) * D]
    sh1, sc1, g1, sh2, sc2, g2 = (chunk(0, B, j).reshape(B, 1, D) for j in range(6))
    csh, csc = chunk(B, B + 1, 0), chunk(B, B + 1, 1)

    w_bf = w_in[0].astype(BF16)
    n1g = norm1_g[0].reshape(1, D)
    cos_t, sin_t = _rope_tables(L)
    k, v, q, u, gates = _inproj(x, sh1, sc1, n1g, w_bf, cos_t, sin_t, hy_conv_w[0], hy_conv_b[0].reshape(1, -1),
                                _tile(L, 512))
    kc, vc = _ctxproj(ctx, csh, csc, n1g, w_bf)
    attn = _attention(lam, q, k, v, kc, vc, subln_g[0].reshape(1, V_DIM), _tile(L, 4096), _tile(L, 512))

    filt = (filt_w1[0], filt_b1[0], filt_freq[0], filt_w2[0], filt_b2[0], filt_w3[0])
    hy = _hyena(u, filt, hy_bias[0])

    x1, hx2, logits_t = _merge(attn, hy, gates, x, g1, sh2, sc2, norm2_g[0].reshape(1, D),
                               w_branch_attn[0].astype(BF16), w_branch_hyena[0].astype(BF16),
                               w_out[0].astype(BF16), router_w[0].T.astype(F32), _tile(L, 1024))
    eidx, w_tok, rank, counts = _router(logits_t, router_bias[0], _tile(B * L, 1024))

    N = B * L
    tm = _tile(L, MOE_TOKENS)
    pos3, pad_pos, blk_e, nb = _dispatch_plan(eidx, rank, counts, MOE_ROWS, tm)
    xs = _dispatch(pos3, pad_pos, hx2, nb * MOE_ROWS, tm)
    y = _experts(xs, blk_e, exp_w_gate[0], exp_w_up[0], exp_w_down[0], nb, MOE_ROWS)
    sgu = jnp.concatenate([shared_w_gate[0], shared_w_up[0]], axis=-1).astype(BF16)
    out = _combine(pos3, y, hx2, w_tok, sgu, shared_w_down[0].astype(BF16), x1.reshape(N, D), g2,
                   final_norm_g.reshape(1, D), tm, L // tm)
    return out.reshape(B, L, D)
```

```python
import functools
import math

import jax
import jax.numpy as jnp
from jax import lax
from jax.experimental import pallas as pl
from jax.experimental.pallas import tpu as pltpu

F32 = jnp.float32
BF16 = jnp.bfloat16
U32 = jnp.uint32
HIGHEST = lax.Precision.HIGHEST

NORM_EPS = 1e-6
N_HEADS = 8
HEAD_DIM = 64
V_DIM = 2 * HEAD_DIM
GRID_W = 64
ROPE_THETA = 10000.0
ROPE_AXIS_DIM = HEAD_DIM // 2
HYENA_W = 1024
FILTER_EMB = 33
FILTER_BANDS = (FILTER_EMB - 1) // 2
DECAY_TARGET = 1e-2
FAST_DECAY_PCT = 0.3
SLOW_DECAY_PCT = 1.5
N_EXPERTS = 64
N_GROUPS = 8
TOPK_GROUPS = 4
TOP_K = 8
EXPERT_HIDDEN = 256
ROUTED_SCALE = 2.5
LAM_INIT = 0.8 - 0.6 * math.exp(-0.3 * 0)

V7X_VMEM_LIMIT = 56 * 1024 * 1024
LOG2E = 1.4426950408889634


def _cparams(sem):
    return pltpu.CompilerParams(dimension_semantics=sem, vmem_limit_bytes=V7X_VMEM_LIMIT)


def _sigmoid(v):
    return 1.0 / (1.0 + jnp.exp(-v))


def _rms(v):
    return v * lax.rsqrt(jnp.mean(v * v, axis=-1, keepdims=True) + NORM_EPS)


def _adaln_kernel(c_ref, w_ref, b_ref, o_ref):
    c = c_ref[...]
    s = c * _sigmoid(c)
    o_ref[...] = jnp.dot(s, w_ref[...], precision=HIGHEST, preferred_element_type=F32) + b_ref[...]


def _adaln(cc, w, b):
    rows, d = cc.shape
    n = w.shape[1]
    tn = 1536
    return pl.pallas_call(
        _adaln_kernel,
        grid=(n // tn,),
        in_specs=[pl.BlockSpec((rows, d), lambda j: (0, 0)),
                  pl.BlockSpec((d, tn), lambda j: (0, j)),
                  pl.BlockSpec((1, tn), lambda j: (0, j))],
        out_specs=pl.BlockSpec((rows, tn), lambda j: (0, j)),
        out_shape=jax.ShapeDtypeStruct((rows, n), F32),
        compiler_params=_cparams(("arbitrary",)),
        name="adaln",
    )(cc, w, b.reshape(1, n))


def _inproj_kernel(x_ref, xp_ref, xn_ref, sh_ref, sc_ref, g_ref, w_ref, cos_ref, sin_ref, cw_ref, cb_ref,
                   k_ref, v_ref, q_ref, u_ref, gt_ref, pscr, *, tm, nt, qscale):
    i = pl.program_id(1)
    g = g_ref[...]
    sh = sh_ref[...]
    sc = 1.0 + sc_ref[...]

    def norm_mod(xx):
        return (_rms(xx) * g) * sc + sh

    h = norm_mod(x_ref[...])
    hp = jnp.where(i > 0, norm_mod(xp_ref[...]), 0.0)
    hn = jnp.where(i < nt - 1, norm_mod(xn_ref[...]), 0.0)
    hb = h.astype(BF16)
    h_ext = jnp.concatenate([hp, h, hn], axis=0).astype(BF16)

    def proj(lhs, c0):
        return jnp.dot(lhs, w_ref[:, c0:c0 + 1024], preferred_element_type=F32)

    cosv = cos_ref[...]
    sinv = sin_ref[...]
    lane = lax.broadcasted_iota(jnp.int32, (1, V_DIM), 1)
    first = (lane % ROPE_AXIS_DIM) < (ROPE_AXIS_DIM // 2)

    def rope_store(a, o_ref, scale):
        for hh in range(N_HEADS):
            s = a[:, hh * V_DIM:(hh + 1) * V_DIM]
            partner = jnp.where(first, pltpu.roll(s, V_DIM - ROPE_AXIS_DIM // 2, 1),
                                pltpu.roll(s, ROPE_AXIS_DIM // 2, 1))
            r = s * cosv + partner * sinv
            if scale != 1.0:
                r = r * scale
            o_ref[:, hh * V_DIM:(hh + 1) * V_DIM] = r.astype(BF16)

    rope_store(proj(hb, 0), k_ref, 1.0)
    v_ref[...] = proj(hb, 1024).astype(BF16)
    rope_store(proj(hb, 2048), q_ref, qscale)
    for j in range(3):
        pscr[...] = proj(h_ext, 3072 + j * 1024)
        cw = cw_ref[:, j * 1024:(j + 1) * 1024]
        cb = cb_ref[:, j * 1024:(j + 1) * 1024]
        u = (pscr[pl.ds(7, tm), :] * cw[0:1] + pscr[pl.ds(8, tm), :] * cw[1:2]
             + pscr[pl.ds(9, tm), :] * cw[2:3] + cb)
        u_ref[:, j * 1024:(j + 1) * 1024] = u.astype(BF16)
    for j in range(2):
        gt_ref[:, j * 1024:(j + 1) * 1024] = proj(hb, 6144 + j * 1024).astype(BF16)


def _inproj(x, shift, scale, g, w_bf, cos_t, sin_t, conv_w, conv_b, tm):
    B, L, D = x.shape
    nt = L // tm
    nb8 = L // 8
    qscale = (HEAD_DIM ** -0.5) * LOG2E
    kern = functools.partial(_inproj_kernel, tm=tm, nt=nt, qscale=qscale)
    row = lambda b, i: (b, i, 0)
    per_b = lambda b, i: (b, 0, 0)
    const = lambda b, i: (0, 0)
    outs = pl.pallas_call(
        kern,
        grid=(B, nt),
        in_specs=[
            pl.BlockSpec((None, tm, D), row),
            pl.BlockSpec((None, 8, D), lambda b, i: (b, jnp.maximum(i * (tm // 8) - 1, 0), 0)),
            pl.BlockSpec((None, 8, D), lambda b, i: (b, jnp.minimum((i + 1) * (tm // 8), nb8 - 1), 0)),
            pl.BlockSpec((None, 1, D), per_b),
            pl.BlockSpec((None, 1, D), per_b),
            pl.BlockSpec((1, D), const),
            pl.BlockSpec((D, 8192), const, pipeline_mode=pl.Buffered(1)),
            pl.BlockSpec((tm, V_DIM), lambda b, i: (i, 0)),
            pl.BlockSpec((tm, V_DIM), lambda b, i: (i, 0)),
            pl.BlockSpec((3, 3 * HYENA_W), const),
            pl.BlockSpec((1, 3 * HYENA_W), const),
        ],
        out_specs=[
            pl.BlockSpec((None, tm, 1024), row),
            pl.BlockSpec((None, tm, 1024), row),
            pl.BlockSpec((None, tm, 1024), row),
            pl.BlockSpec((None, tm, 3 * HYENA_W), row),
            pl.BlockSpec((None, tm, 2048), row),
        ],
        out_shape=[
            jax.ShapeDtypeStruct((B, L, 1024), BF16),
            jax.ShapeDtypeStruct((B, L, 1024), BF16),
            jax.ShapeDtypeStruct((B, L, 1024), BF16),
            jax.ShapeDtypeStruct((B, L, 3 * HYENA_W), BF16),
            jax.ShapeDtypeStruct((B, L, 2048), BF16),
        ],
        scratch_shapes=[pltpu.VMEM((tm + 16, 1024), F32)],
        compiler_params=_cparams(("parallel", "arbitrary")),
        name="inproj",
    )(x, x, x, shift, scale, g, w_bf, cos_t, sin_t, conv_w, conv_b)
    return outs


def _ctxproj_kernel(x_ref, sh_ref, sc_ref, g_ref, w_ref, k_ref, v_ref):
    h = ((_rms(x_ref[...]) * g_ref[...]) * (1.0 + sc_ref[...]) + sh_ref[...]).astype(BF16)
    k_ref[...] = jnp.dot(h, w_ref[:, 0:1024], preferred_element_type=F32).astype(BF16)
    v_ref[...] = jnp.dot(h, w_ref[:, 1024:2048], preferred_element_type=F32).astype(BF16)


def _ctxproj(ctx, shift, scale, g, w_bf):
    B, Lc, D = ctx.shape
    return pl.pallas_call(
        _ctxproj_kernel,
        grid=(B,),
        in_specs=[pl.BlockSpec((None, Lc, D), lambda b: (b, 0, 0)),
                  pl.BlockSpec((1, D), lambda b: (0, 0)),
                  pl.BlockSpec((1, D), lambda b: (0, 0)),
                  pl.BlockSpec((1, D), lambda b: (0, 0)),
                  pl.BlockSpec((D, 2048), lambda b: (0, 0))],
        out_specs=[pl.BlockSpec((None, Lc, 1024), lambda b: (b, 0, 0)),
                   pl.BlockSpec((None, Lc, 1024), lambda b: (b, 0, 0))],
        out_shape=[jax.ShapeDtypeStruct((B, Lc, 1024), BF16),
                   jax.ShapeDtypeStruct((B, Lc, 1024), BF16)],
        compiler_params=_cparams(("arbitrary",)),
        name="ctxproj",
    )(ctx, shift, scale, g, w_bf)


def _attn_kernel(lam_ref, q_ref, k_ref, v_ref, kc_ref, vc_ref, g_ref, o_ref, m_s, acc_s, *, tk, nk):
    q = q_ref[...]
    lane = lax.broadcasted_iota(jnp.int32, (1, V_DIM), 1)
    zero = jnp.zeros_like(q)
    qs = (jnp.where(lane < HEAD_DIM, q, zero), jnp.where(lane >= HEAD_DIM, q, zero))
    def chunk(kk, vv, first=False):
        reps = kk.shape[0] // V_DIM
        v1 = jnp.concatenate([vv, jnp.ones_like(vv)], axis=1)
        for m in range(2):
            s = lax.dot_general(qs[m], kk, (((1,), (1,)), ((), ())), preferred_element_type=F32)
            m_new = jnp.broadcast_to(jnp.max(s, axis=-1, keepdims=True), m_s.shape[1:])
            if not first:
                m_prev = m_s[m]
                m_new = jnp.maximum(m_prev, m_new)
                alpha = jnp.exp2(m_prev - m_new)
            p = jnp.exp2(s - jnp.concatenate([m_new] * reps, axis=1))
            pv = jnp.dot(p.astype(BF16), v1, preferred_element_type=F32)
            acc_s[m] = pv if first else jnp.concatenate([alpha, alpha], axis=1) * acc_s[m] + pv
            m_s[m] = m_new

    def body(j, carry):
        off = pl.multiple_of(j * tk, tk)
        chunk(k_ref[pl.ds(off, tk), :], v_ref[pl.ds(off, tk), :])
        return carry

    chunk(kc_ref[...], vc_ref[...], first=True)
    lax.fori_loop(0, nk, body, 0, unroll=2 if nk % 2 == 0 else 1)
    lam = lam_ref[0, 0]
    a0, a1 = acc_s[0], acc_s[1]
    o = a0[:, :V_DIM] / a0[:, V_DIM:] - lam * (a1[:, :V_DIM] / a1[:, V_DIM:])
    o = _rms(o) * g_ref[...] * (1.0 - LAM_INIT)
    o_ref[...] = o.astype(BF16)


def _attention(lam, q, k, v, kc, vc, subln_g, tq, tk):
    B, L, _ = q.shape
    Lc = kc.shape[1]
    kern = functools.partial(_attn_kernel, tk=tk, nk=L // tk)
    kv = lambda b, h, i: (b, 0, h)
    return pl.pallas_call(
        kern,
        grid=(B, N_HEADS, L // tq),
        in_specs=[
            pl.BlockSpec(memory_space=pltpu.SMEM),
            pl.BlockSpec((None, tq, V_DIM), lambda b, h, i: (b, i, h)),
            pl.BlockSpec((None, L, V_DIM), kv),
            pl.BlockSpec((None, L, V_DIM), kv),
            pl.BlockSpec((None, Lc, V_DIM), kv),
            pl.BlockSpec((None, Lc, V_DIM), kv),
            pl.BlockSpec((1, V_DIM), lambda b, h, i: (0, 0)),
        ],
        out_specs=pl.BlockSpec((None, tq, V_DIM), lambda b, h, i: (b, i, h)),
        out_shape=jax.ShapeDtypeStruct((B, L, N_HEADS * V_DIM), BF16),
        scratch_shapes=[pltpu.VMEM((2, tq, V_DIM), F32), pltpu.VMEM((2, tq, 2 * V_DIM), F32)],
        compiler_params=_cparams(("parallel", "parallel", "arbitrary")),
        name="diffattn",
    )(lam, q, k, v, kc, vc, subln_g)


FFT_SLABS = 64
FFT_STRIP = 16
FILTER_TAP_STEPS = 16


def _dft_tables(nb):
    N = FFT_SLABS * nb
    fa = jnp.arange(FFT_SLABS, dtype=jnp.int32)[:, None, None]
    fb = jnp.arange(nb, dtype=jnp.int32)[None, :, None]
    b = jnp.arange(nb, dtype=jnp.int32)[None, None, :]
    ang = ((FFT_SLABS * b * fb + b * fa) % N).astype(F32) * (2.0 * math.pi / N)
    c, s = jnp.cos(ang), jnp.sin(ang)
    fwd = jnp.concatenate([jnp.concatenate([c, s], -1), jnp.concatenate([-s, c], -1)], axis=1)
    inv = jnp.swapaxes(fwd, 1, 2) / N
    return fwd.astype(BF16), inv.astype(BF16)


def _vadd(a, b):
    return b if a is None else (a if b is None else a + b)


def _vsub(a, b):
    if b is None:
        return a
    return -b if a is None else a - b


def _vscale(a, k):
    if a is None or k == 1.0:
        return a
    return -a if k == -1.0 else a * k


def _cmulc(x, c, s):
    re, im = x
    eps = 1e-9
    if abs(s) < eps:
        k = 1.0 if c > 0 else -1.0
        return _vscale(re, k), _vscale(im, k)
    if abs(c) < eps:
        k = 1.0 if s > 0 else -1.0
        return _vscale(im, -k), _vscale(re, k)
    if abs(abs(c) - abs(s)) < eps:
        kc = 1.0 if c > 0 else -1.0
        ks = 1.0 if s > 0 else -1.0
        return (_vscale(_vsub(_vscale(re, kc), _vscale(im, ks)), abs(c)),
                _vscale(_vadd(_vscale(re, ks), _vscale(im, kc)), abs(c)))
    return _vsub(_vscale(re, c), _vscale(im, s)), _vadd(_vscale(re, s), _vscale(im, c))


def _fft(xs, sign):
    n = len(xs)
    if n == 1:
        return xs
    ev, od = _fft(xs[0::2], sign), _fft(xs[1::2], sign)
    out = [None] * n
    for k in range(n // 2):
        ang = sign * 2.0 * math.pi * k / n
        t = _cmulc(od[k], math.cos(ang), math.sin(ang))
        out[k] = (_vadd(ev[k][0], t[0]), _vadd(ev[k][1], t[1]))
        out[k + n // 2] = (_vsub(ev[k][0], t[0]), _vsub(ev[k][1], t[1]))
    return out


def _put(ref, slot, rows, val):
    ref[slot, rows, :] = jnp.zeros((FFT_STRIP, ref.shape[2]), F32) if val is None else val


def _slab_fft_forward(load, s_ref, nb, n_in):
    def strip(i, carry):
        r0 = pl.multiple_of(i * FFT_STRIP, FFT_STRIP)
        rows_re, rows_im = pl.ds(r0, FFT_STRIP), pl.ds(nb + r0, FFT_STRIP)
        for a0 in range(8):
            xs = [load(8 * a1 + a0, r0) if a1 < n_in else (None, None) for a1 in range(8)]
            ys = _fft(xs, -1.0)
            for f1 in range(8):
                ang = -2.0 * math.pi * a0 * f1 / FFT_SLABS
                yr, yi = _cmulc(ys[f1], math.cos(ang), math.sin(ang))
                _put(s_ref, a0 * 8 + f1, rows_re, yr)
                _put(s_ref, a0 * 8 + f1, rows_im, yi)
        for f1 in range(8):
            xs = [(s_ref[a0 * 8 + f1, rows_re, :], s_ref[a0 * 8 + f1, rows_im, :]) for a0 in range(8)]
            ys = _fft(xs, -1.0)
            for f0 in range(8):
                s_ref[f1 + 8 * f0, rows_re, :] = ys[f0][0]
                s_ref[f1 + 8 * f0, rows_im, :] = ys[f0][1]
        return carry

    lax.fori_loop(0, nb // FFT_STRIP, strip, 0)


def _slab_fft_inverse(s_ref, nb, emit):
    def strip(i, carry):
        r0 = pl.multiple_of(i * FFT_STRIP, FFT_STRIP)
        rows_re, rows_im = pl.ds(r0, FFT_STRIP), pl.ds(nb + r0, FFT_STRIP)
        for q in range(8):
            xs = [(s_ref[8 * p + q, rows_re, :], s_ref[8 * p + q, rows_im, :]) for p in range(8)]
            us = _fft(xs, 1.0)
            for r in range(8):
                ang = 2.0 * math.pi * r * q / FFT_SLABS
                ur, ui = _cmulc(us[r], math.cos(ang), math.sin(ang))
                s_ref[8 * r + q, rows_re, :] = ur
                s_ref[8 * r + q, rows_im, :] = ui
        for r in range(8):
            xs = [(s_ref[8 * r + q, rows_re, :], s_ref[8 * r + q, rows_im, :]) for q in range(8)]
            ys = _fft(xs, 1.0)
            for sidx in range(4):
                emit(r + 8 * sidx, r0, ys[sidx][0], ys[sidx][1])
        return carry

    lax.fori_loop(0, nb // FFT_STRIP, strip, 0)


def _hyena_conv_kernel(v_ref, x_ref, k_ref, ff_ref, fi_ref, bias_ref, o_ref, s_ref, *, nb):
    def load(a, r0):
        rows = pl.ds(pl.multiple_of(a * nb + r0, FFT_STRIP), FFT_STRIP)
        return v_ref[0, rows, :].astype(F32), v_ref[1, rows, :].astype(F32)

    _slab_fft_forward(load, s_ref, nb, FFT_SLABS // 16)

    def freq(f, carry):
        z = jnp.dot(ff_ref[f], s_ref[f].astype(BF16), preferred_element_type=F32)
        zr, zi = z[:nb], z[nb:]
        kr, ki = k_ref[0, f], k_ref[1, f]
        y = jnp.concatenate([zr * kr - zi * ki, zr * ki + zi * kr], axis=0).astype(BF16)
        s_ref[f] = jnp.dot(fi_ref[f], y, preferred_element_type=F32)
        return carry

    lax.fori_loop(0, FFT_SLABS, freq, 0, unroll=32)
    bias = bias_ref[...]

    def emit(a, r0, re, im):
        rows = pl.ds(pl.multiple_of(a * nb + r0, FFT_STRIP), FFT_STRIP)
        for half, y in enumerate((re, im)):
            v = v_ref[half, rows, :].astype(F32)
            o_ref[half, rows, :] = (x_ref[half, rows, :].astype(F32) * (y + v * bias)).astype(BF16)

    _slab_fft_inverse(s_ref, nb, emit)


def _hyena_conv(uv, zv, k_hat, ff, fi, bias, order, in_col, gate_col, nb):
    _, P, L, _ = uv.shape
    W = HYENA_W
    nct = W // 128
    src, src_off = (uv, in_col * nct) if zv is None else (zv, 0)
    blk = (2, None, L, 128)
    return pl.pallas_call(
        functools.partial(_hyena_conv_kernel, nb=nb),
        grid=(nct, P),
        in_specs=[pl.BlockSpec(blk, lambda c, p: (0, p, 0, src_off + c)),
                  pl.BlockSpec(blk, lambda c, p: (0, p, 0, gate_col * nct + c)),
                  pl.BlockSpec((2, FFT_SLABS, nb, 128), lambda c, p: (0, 0, 0, order * nct + c),
                               pipeline_mode=pl.Buffered(1)),
                  pl.BlockSpec((FFT_SLABS, 2 * nb, 2 * nb), lambda c, p: (0, 0, 0), pipeline_mode=pl.Buffered(1)),
                  pl.BlockSpec((FFT_SLABS, 2 * nb, 2 * nb), lambda c, p: (0, 0, 0), pipeline_mode=pl.Buffered(1)),
                  pl.BlockSpec((1, 128), lambda c, p: (0, c))],
        out_specs=pl.BlockSpec(blk, lambda c, p: (0, p, 0, c)),
        out_shape=jax.ShapeDtypeStruct((2, P, L, W), BF16),
        scratch_shapes=[pltpu.VMEM((FFT_SLABS, 2 * nb, 128), F32)],
        compiler_params=_cparams(("parallel", "arbitrary")),
        name="hyena_conv%d" % order,
    )(src, uv, k_hat, ff, fi, bias)


def _filter_features(n):
    pos = jnp.arange(n, dtype=F32)[:, None]
    t = jnp.linspace(0.0, 1.0, n, dtype=F32)[:, None]
    w = 2 * math.pi * pos / n
    bands = jnp.linspace(1e-4, FILTER_BANDS - 1, FILTER_BANDS, dtype=F32)
    z = jnp.concatenate([t, jnp.cos(bands * w), -jnp.sin(bands * w)], axis=-1)
    z = jnp.pad(z, ((0, 0), (0, 128 - FILTER_EMB)))
    return jnp.concatenate([z, z[0:1], z[:0:-1]], axis=0)


def _dot_bf16x3(a, b):
    ah, bh = a.astype(BF16), b.astype(BF16)
    al, bl = (a - ah.astype(F32)).astype(BF16), (b - bh.astype(F32)).astype(BF16)
    dot = functools.partial(jnp.dot, preferred_element_type=F32)
    return dot(ah, bh) + (dot(ah, bl) + dot(al, bh))


def _filt_taps_kernel(z_ref, w1_ref, b1_ref, fr_ref, w2_ref, b2_ref, w3_ref, dl_ref, k_ref, sum_ref):
    a = pl.program_id(0)
    z = z_ref[...]
    fr = fr_ref[...]
    hd = jnp.sin(fr * (jnp.dot(z, w1_ref[...], precision=HIGHEST, preferred_element_type=F32) + b1_ref[...]))
    hd = jnp.sin(fr * (jnp.dot(hd, w2_ref[...], precision=HIGHEST, preferred_element_type=F32) + b2_ref[...]))
    h = _dot_bf16x3(hd, w3_ref[...]) * jnp.exp(-z[:, 0:1] * dl_ref[...])
    rows = lax.broadcasted_iota(jnp.int32, (z.shape[0], 1), 0)
    h = jnp.where(jnp.logical_and(a == pl.num_programs(0) // 2, rows == 0), 0.0, h)
    k_ref[...] = h

    @pl.when(a == 0)
    def _():
        sum_ref[...] = jnp.zeros_like(sum_ref)

    sum_ref[0:1, :] += jnp.sum(jnp.abs(h), axis=0, keepdims=True)


def _filt_spectrum_kernel(t_ref, inv_ref, ff_ref, k_ref, s_ref, *, nb):
    def load(a, r0):
        return t_ref[pl.ds(pl.multiple_of(a * nb + r0, FFT_STRIP), FFT_STRIP), :], None

    _slab_fft_forward(load, s_ref, nb, FFT_SLABS // 8)
    inv = inv_ref[...]

    def freq(f, carry):
        z = jnp.dot(ff_ref[f], s_ref[f].astype(BF16), preferred_element_type=F32)
        k_ref[0, f] = z[:nb] * inv
        k_ref[1, f] = z[nb:] * inv
        return carry

    lax.fori_loop(0, FFT_SLABS, freq, 0, unroll=32)


def _hyena_filters(n, filt, ff, nb):
    w1, b1, freq, w2, b2, w3 = filt
    W = HYENA_W
    hid = w1.shape[1]
    z = _filter_features(n)
    w1p = jnp.pad(w1.astype(F32), ((0, 128 - FILTER_EMB), (0, 0)))
    w3d = w3.astype(F32).reshape(hid, 2, 2, W)
    w3sel = jnp.stack([w3d[:, :, d, :].reshape(hid, 2 * W) for d in range(2)])
    deltas = jnp.abs(jnp.linspace(math.log(DECAY_TARGET) / SLOW_DECAY_PCT,
                                  math.log(DECAY_TARGET) / FAST_DECAY_PCT, W, dtype=F32))
    deltas = jnp.tile(deltas.reshape(1, W), (1, 2))
    const = lambda a: (0, 0)
    steps = FILTER_TAP_STEPS
    rs = 2 * n // steps
    taps, sums = pl.pallas_call(
        _filt_taps_kernel,
        grid=(steps,),
        in_specs=[pl.BlockSpec((rs, 128), lambda a: (a, 0)),
                  pl.BlockSpec((128, hid), const), pl.BlockSpec((1, hid), const), pl.BlockSpec((1, hid), const),
                  pl.BlockSpec((hid, hid), const), pl.BlockSpec((1, hid), const),
                  pl.BlockSpec((None, hid, 2 * W), lambda a: (a // (steps // 2), 0, 0)),
                  pl.BlockSpec((1, 2 * W), const)],
        out_specs=[pl.BlockSpec((rs, 2 * W), lambda a: (a, 0)), pl.BlockSpec((8, 2 * W), const)],
        out_shape=[jax.ShapeDtypeStruct((2 * n, 2 * W), F32), jax.ShapeDtypeStruct((8, 2 * W), F32)],
        compiler_params=_cparams(("arbitrary",)),
        name="hyena_filter_taps",
    )(z, w1p, b1.reshape(1, hid), freq.reshape(1, hid), w2, b2.reshape(1, hid), w3sel, deltas)
    inv = 1.0 / sums[0:1]
    return pl.pallas_call(
        functools.partial(_filt_spectrum_kernel, nb=nb),
        grid=(2 * W // 128,),
        in_specs=[pl.BlockSpec((2 * n, 128), lambda j: (0, j)),
                  pl.BlockSpec((1, 128), lambda j: (0, j)),
                  pl.BlockSpec((FFT_SLABS, 2 * nb, 2 * nb), lambda j: (0, 0, 0), pipeline_mode=pl.Buffered(1))],
        out_specs=pl.BlockSpec((2, FFT_SLABS, nb, 128), lambda j: (0, 0, 0, j)),
        out_shape=jax.ShapeDtypeStruct((2, FFT_SLABS, nb, 2 * W), F32),
        scratch_shapes=[pltpu.VMEM((FFT_SLABS, 2 * nb, 128), F32)],
        compiler_params=_cparams(("parallel",)),
        name="hyena_filter_spectrum",
    )(taps, inv, ff)


def _hyena(u, filt, hy_bias):
    B, L, _ = u.shape
    P = B // 2
    nb = 2 * L // FFT_SLABS
    ff, fi = _dft_tables(nb)
    k_hat = _hyena_filters(L, filt, ff, nb)
    uv = u.reshape(2, P, L, 3 * HYENA_W)
    bias = hy_bias.astype(F32)
    z = _hyena_conv(uv, None, k_hat, ff, fi, bias[0:1], 0, 0, 1, nb)
    y = _hyena_conv(uv, z, k_hat, ff, fi, bias[1:2], 1, 0, 2, nb)
    return y.reshape(B, L, HYENA_W)


def _merge_kernel(a_ref, h_ref, gt_ref, x_ref, g1_ref, sh_ref, sc_ref, n2g_ref, wpa_ref, wph_ref, wo_ref, rw_ref,
                  x1_ref, hx_ref, lg_ref):
    ya = jnp.dot(a_ref[...], wpa_ref[...], preferred_element_type=F32)
    yh = jnp.dot(h_ref[...], wph_ref[...], preferred_element_type=F32)
    ga = gt_ref[:, 0:1024].astype(F32)
    gh = gt_ref[:, 1024:2048].astype(F32)
    y = _sigmoid(ga) * ya + _sigmoid(gh) * yh
    mix = jnp.dot(y.astype(BF16), wo_ref[...], preferred_element_type=F32)
    x1 = x_ref[...] + g1_ref[...] * mix
    x1_ref[...] = x1
    hx = (_rms(x1) * n2g_ref[...]) * (1.0 + sc_ref[...]) + sh_ref[...]
    _to_tiles(hx_ref, hx)
    lg_ref[...] = lax.dot_general(rw_ref[...], hx, (((1,), (1,)), ((), ())), precision=HIGHEST,
                                  preferred_element_type=F32)


def _merge(attn, hy, gates, x, g1, sh2, sc2, n2g, wpa, wph, wo, rwT, tm):
    B, L, D = x.shape
    nt = L // tm
    row = lambda b, i: (b, i, 0)
    per_b = lambda b, i: (b, 0, 0)
    const = lambda b, i: (0, 0)
    return pl.pallas_call(
        _merge_kernel,
        grid=(B, nt),
        in_specs=[pl.BlockSpec((None, tm, 1024), row), pl.BlockSpec((None, tm, 1024), row),
                  pl.BlockSpec((None, tm, 2048), row), pl.BlockSpec((None, tm, D), row),
                  pl.BlockSpec((None, 1, D), per_b), pl.BlockSpec((None, 1, D), per_b),
                  pl.BlockSpec((None, 1, D), per_b), pl.BlockSpec((1, D), const),
                  pl.BlockSpec((1024, D), const), pl.BlockSpec((1024, D), const), pl.BlockSpec((D, D), const),
                  pl.BlockSpec((N_EXPERTS, D), const)],
        out_specs=[pl.BlockSpec((None, tm, D), row),
                   pl.BlockSpec((tm * TILE_ROWS, 128), lambda b, i: (b * nt + i, 0)),
                   pl.BlockSpec((N_EXPERTS, tm), lambda b, i: (0, b * nt + i))],
        out_shape=[jax.ShapeDtypeStruct((B, L, D), F32),
                   jax.ShapeDtypeStruct((B * L * TILE_ROWS, 128), U32),
                   jax.ShapeDtypeStruct((N_EXPERTS, B * L), F32)],
        compiler_params=_cparams(("parallel", "arbitrary")),
        name="merge",
    )(attn, hy, gates, x, g1, sh2, sc2, n2g, wpa, wph, wo, rwT)


def _router_kernel(lg_ref, rb_ref, tri_ref, ei_ref, wt_ref, rk_ref, cnt_ref):
    scores = _sigmoid(lg_ref[...])
    choice = scores + rb_ref[...]
    E, T = scores.shape
    gsz = E // N_GROUPS
    neg = jnp.float32(-jnp.inf)
    big = jnp.int32(1 << 30)
    rows8 = lax.broadcasted_iota(jnp.int32, (gsz, T), 0)
    gscore = []
    for gi in range(N_GROUPS):
        c = choice[gi * gsz:(gi + 1) * gsz]
        m1 = jnp.max(c, axis=0, keepdims=True)
        i1 = jnp.min(jnp.where(c == m1, rows8, big), axis=0, keepdims=True)
        m2 = jnp.max(jnp.where(rows8 == i1, neg, c), axis=0, keepdims=True)
        gscore.append(m1 + m2)
    gs = jnp.concatenate(gscore, axis=0)
    rowsg = lax.broadcasted_iota(jnp.int32, (N_GROUPS, T), 0)
    gsel = jnp.zeros((N_GROUPS, T), jnp.bool_)
    for _ in range(TOPK_GROUPS):
        m = jnp.max(gs, axis=0, keepdims=True)
        ix = jnp.min(jnp.where(gs == m, rowsg, big), axis=0, keepdims=True)
        hit = rowsg == ix
        gsel = jnp.logical_or(gsel, hit)
        gs = jnp.where(hit, neg, gs)
    masked = jnp.concatenate(
        [jnp.where(gsel[gi:gi + 1], choice[gi * gsz:(gi + 1) * gsz], neg) for gi in range(N_GROUPS)], axis=0)
    rows = lax.broadcasted_iota(jnp.int32, (E, T), 0)
    ids, ws, hits = [], [], []
    sel = jnp.zeros((E, T), F32)
    for _ in range(TOP_K):
        m = jnp.max(masked, axis=0, keepdims=True)
        ix = jnp.min(jnp.where(masked == m, rows, big), axis=0, keepdims=True)
        hit = rows == ix
        ids.append(ix)
        hits.append(hit)
        ws.append(jnp.sum(jnp.where(hit, scores, 0.0), axis=0, keepdims=True))
        sel = jnp.where(hit, 1.0, sel)
        masked = jnp.where(hit, neg, masked)
    w = jnp.concatenate(ws, axis=0)
    w = w / jnp.sum(w, axis=0, keepdims=True) * ROUTED_SCALE
    ei_ref[...] = jnp.concatenate(ids, axis=0)
    wt_ref[...] = jnp.concatenate([w, jnp.zeros((128 - TOP_K, T), F32)], axis=0).T

    @pl.when(pl.program_id(0) == 0)
    def _():
        cnt_ref[...] = jnp.zeros_like(cnt_ref)

    before = cnt_ref[...][:, 0:1]
    csum = jnp.dot(sel.astype(BF16), tri_ref[...], preferred_element_type=F32)
    rank = before + csum - sel
    rk_ref[...] = jnp.concatenate(
        [jnp.sum(jnp.where(h, rank, 0.0), axis=0, keepdims=True) for h in hits], axis=0).astype(jnp.int32)
    cnt_ref[...] = jnp.broadcast_to(before + csum[:, T - 1:T], cnt_ref.shape)


def _router(logits_t, router_bias, tt):
    E, N = logits_t.shape
    tri = jnp.triu(jnp.ones((tt, tt), BF16))
    per_tok = pl.BlockSpec((TOP_K, tt), lambda i: (0, i))
    return pl.pallas_call(
        _router_kernel,
        grid=(N // tt,),
        in_specs=[pl.BlockSpec((E, tt), lambda i: (0, i)), pl.BlockSpec((E, 1), lambda i: (0, 0)),
                  pl.BlockSpec((tt, tt), lambda i: (0, 0))],
        out_specs=[per_tok, pl.BlockSpec((tt, 128), lambda i: (i, 0)), per_tok,
                   pl.BlockSpec((E, 128), lambda i: (0, 0))],
        out_shape=[jax.ShapeDtypeStruct((TOP_K, N), jnp.int32), jax.ShapeDtypeStruct((N, 128), F32),
                   jax.ShapeDtypeStruct((TOP_K, N), jnp.int32), jax.ShapeDtypeStruct((E, 128), F32)],
        compiler_params=_cparams(("arbitrary",)),
        name="router",
    )(logits_t, router_bias.reshape(E, 1).astype(F32), tri)


MOE_ROWS = 512
MOE_TOKENS = 128


def _plan_kernel(ei_ref, rk_ref, ps_ref, pos_ref):
    ei = ei_ref[...]
    ps = ps_ref[...]
    rows = lax.broadcasted_iota(jnp.int32, (ps.shape[0], ei.shape[1]), 0)
    start = [jnp.sum(jnp.where(rows == ei[k:k + 1], ps, 0.0), axis=0, keepdims=True) for k in range(ei.shape[0])]
    pos_ref[...] = rk_ref[...] + jnp.concatenate(start, axis=0).astype(jnp.int32)


def _dispatch_plan(eidx, rank, counts, bm, tm):
    K, N = eidx.shape
    E = N_EXPERTS
    cnt = counts[:, 0].astype(jnp.int32)
    padded = (cnt + bm - 1) // bm * bm
    pad_end = jnp.cumsum(padded)
    pad_start = pad_end - padded
    nb = -(-(N * K + E * (bm - 1)) // bm)
    tt = _tile(N, 2048)
    per_tok = pl.BlockSpec((K, tt), lambda i: (0, i))
    pos = pl.pallas_call(
        _plan_kernel,
        grid=(N // tt,),
        in_specs=[per_tok, per_tok, pl.BlockSpec((E, 1), lambda i: (0, 0))],
        out_specs=per_tok,
        out_shape=jax.ShapeDtypeStruct((K, N), jnp.int32),
        compiler_params=_cparams(("parallel",)),
        name="moe_plan",
    )(eidx, rank, pad_start.astype(F32).reshape(E, 1))
    blk_start = jnp.arange(nb, dtype=jnp.int32) * bm
    blk_e = jnp.minimum(jnp.sum(blk_start[:, None] >= pad_end[None, :], axis=1), E - 1)
    n_pad = nb * bm - N * K
    assert n_pad % (K * tm) == 0
    cpad = jnp.cumsum(padded - cnt)
    q = jnp.arange(n_pad, dtype=jnp.int32)
    seg = jnp.sum(q[:, None] >= cpad[None, :], axis=1)
    onehot = seg[:, None] == jnp.arange(E + 1, dtype=jnp.int32)[None, :]
    first = jnp.concatenate([pad_start + cnt, pad_end[-1:]])
    skipped = jnp.concatenate([jnp.zeros((1,), cpad.dtype), cpad])
    pad_pos = (q + jnp.sum(jnp.where(onehot, (first - skipped)[None, :], 0), axis=1)).astype(jnp.int32)
    nt = N // tm
    pos3 = pos.reshape(K, nt, tm).transpose(1, 0, 2).reshape(nt, 1, K * tm)
    return pos3, pad_pos.reshape(-1, 1, K * tm), blk_e.astype(jnp.int32), nb


TILE_ROWS = 4
_HI16 = 0xFFFF0000


def _bf16_bits(v):
    return lax.bitcast_convert_type(v.astype(BF16).astype(F32), U32)


def _to_tiles(ref, val):
    m, half = val.shape[0], val.shape[1] // 2
    words = (lax.shift_right_logical(_bf16_bits(val[:, :half]), jnp.uint32(16))
             | (_bf16_bits(val[:, half:]) & jnp.uint32(_HI16)))
    for j in range(TILE_ROWS):
        ref[pl.ds(j, m, stride=TILE_ROWS), :] = words[:, j * 128:(j + 1) * 128]


def _from_tiles(ref, start, m):
    words = jnp.concatenate([ref[pl.ds(start + j, m, stride=TILE_ROWS), :] for j in range(TILE_ROWS)], axis=1)
    lo = lax.bitcast_convert_type(lax.shift_left(words, jnp.uint32(16)), F32)
    hi = lax.bitcast_convert_type(words & jnp.uint32(_HI16), F32)
    return jnp.concatenate([lo, hi], axis=1)


def _row_copy(idx_ref, r, src_hbm, buf, base, sem):
    src = pl.multiple_of(idx_ref[0, r] * TILE_ROWS, TILE_ROWS)
    dst = base + r * TILE_ROWS
    if not isinstance(dst, int):
        dst = pl.multiple_of(dst, TILE_ROWS)
    return pltpu.make_async_copy(src_hbm.at[pl.ds(src, TILE_ROWS), :], buf.at[pl.ds(dst, TILE_ROWS), :], sem)


def _pipelined_gather(cur_ref, nxt_ref, src_hbm, buf, sem, nrows, consume):
    i = pl.program_id(0)
    span = nrows * TILE_ROWS

    def wait(slot):
        pltpu.make_async_copy(src_hbm.at[pl.ds(0, span), :], buf.at[pl.ds(slot * span, span), :], sem.at[slot]).wait()

    @pl.when(i == 0)
    def _():
        def body(r, carry):
            _row_copy(cur_ref, r, src_hbm, buf, 0, sem.at[0]).start()
            return carry
        lax.fori_loop(0, nrows, body, 0)

    for slot in range(2):
        @pl.when(i % 2 == slot)
        def _():
            for r in range(nrows):
                _row_copy(nxt_ref, r, src_hbm, buf, (1 - slot) * span, sem.at[1 - slot]).start(priority=r % 2)
            wait(slot)
            consume(slot * span)

            @pl.when(i == pl.num_programs(0) - 1)
            def _():
                wait(1 - slot)


def _dispatch_kernel(pos_ref, hx_ref, xs_hbm, stage, zero_tile, sem, *, tm, nt):
    i = pl.program_id(0)
    slot = i % 2
    span = tm * TILE_ROWS

    def wait_step(s):
        for _ in range(TOP_K):
            pltpu.make_async_copy(stage.at[pl.ds(0, span), :], xs_hbm.at[pl.ds(0, span), :], sem.at[s]).wait()

    def dst(r):
        return xs_hbm.at[pl.ds(pl.multiple_of(pos_ref[0, r] * TILE_ROWS, TILE_ROWS), TILE_ROWS), :]

    @pl.when(i < nt)
    def _():
        base = pl.multiple_of(slot * span, span)
        stage[pl.ds(base, span), :] = hx_ref[...]
        for r in range(TOP_K * tm):
            src = stage.at[pl.ds(pl.multiple_of(base + (r % tm) * TILE_ROWS, TILE_ROWS), TILE_ROWS), :]
            pltpu.make_async_copy(src, dst(r), sem.at[slot]).start(priority=r % 2)

    @pl.when(i == 0)
    def _():
        zero_tile[...] = jnp.zeros_like(zero_tile)

    @pl.when(i >= nt)
    def _():
        for r in range(TOP_K * tm):
            pltpu.make_async_copy(zero_tile, dst(r), sem.at[slot]).start(priority=r % 2)

    @pl.when(i > 0)
    def _():
        wait_step(1 - slot)

    @pl.when(i == pl.num_programs(0) - 1)
    def _():
        wait_step(slot)


def _dispatch(pos3, pad_pos, hx_tiles, rows_total, tm):
    nt = pos3.shape[0]
    idx = jnp.concatenate([pos3, pad_pos], axis=0)
    steps = idx.shape[0]
    return pl.pallas_call(
        functools.partial(_dispatch_kernel, tm=tm, nt=nt),
        grid=(steps,),
        in_specs=[pl.BlockSpec((None, 1, TOP_K * tm), lambda i: (i, 0, 0), memory_space=pltpu.SMEM),
                  pl.BlockSpec((tm * TILE_ROWS, 128), lambda i: (jnp.minimum(i, nt - 1), 0))],
        out_specs=pl.BlockSpec(memory_space=pl.ANY),
        out_shape=jax.ShapeDtypeStruct((rows_total * TILE_ROWS, 128), U32),
        scratch_shapes=[pltpu.VMEM((2 * tm * TILE_ROWS, 128), U32), pltpu.VMEM((TILE_ROWS, 128), U32),
                        pltpu.SemaphoreType.DMA((2,))],
        compiler_params=_cparams(("arbitrary",)),
        name="moe_dispatch",
    )(idx, hx_tiles)


def _expert_kernel(blk_e_ref, xs_ref, wg_ref, wu_ref, wd_ref, y_ref, *, bm):
    del blk_e_ref
    x = _from_tiles(xs_ref, 0, bm).astype(BF16)
    hg = jnp.dot(x, wg_ref[...].astype(BF16), preferred_element_type=F32)
    hu = jnp.dot(x, wu_ref[...].astype(BF16), preferred_element_type=F32)
    act = hg * _sigmoid(hg) * hu
    _to_tiles(y_ref, jnp.dot(act.astype(BF16), wd_ref[...].astype(BF16), preferred_element_type=F32))


def _experts(xs_tiles, blk_e, wg, wu, wd, nb, bm):
    D = wg.shape[1]
    grid_spec = pltpu.PrefetchScalarGridSpec(
        num_scalar_prefetch=1,
        grid=(nb,),
        in_specs=[pl.BlockSpec((bm * TILE_ROWS, 128), lambda i, be: (i, 0)),
                  pl.BlockSpec((None, D, EXPERT_HIDDEN), lambda i, be: (be[i], 0, 0)),
                  pl.BlockSpec((None, D, EXPERT_HIDDEN), lambda i, be: (be[i], 0, 0)),
                  pl.BlockSpec((None, EXPERT_HIDDEN, D), lambda i, be: (be[i], 0, 0))],
        out_specs=pl.BlockSpec((bm * TILE_ROWS, 128), lambda i, be: (i, 0)),
    )
    return pl.pallas_call(
        functools.partial(_expert_kernel, bm=bm),
        grid_spec=grid_spec,
        out_shape=jax.ShapeDtypeStruct((nb * bm * TILE_ROWS, 128), U32),
        compiler_params=_cparams(("arbitrary",)),
        name="moe_experts",
    )(blk_e, xs_tiles, wg, wu, wd)


def _combine_kernel(cur_ref, nxt_ref, y_hbm, hx_ref, wt_ref, sgu_ref, sd_ref, x1_ref, g2_ref, fg_ref, o_ref, ybuf,
                    sem, *, tm):
    def consume(here):
        wt = wt_ref[...]
        routed = None
        for k in range(TOP_K):
            wk = jnp.broadcast_to(wt[:, k:k + 1], (tm, 128))
            wk = jnp.concatenate([wk] * (x1_ref.shape[1] // 128), axis=1)
            term = wk * _from_tiles(ybuf, here + k * tm * TILE_ROWS, tm)
            routed = term if routed is None else routed + term
        hu = jnp.dot(_from_tiles(hx_ref, 0, tm).astype(BF16), sgu_ref[...], preferred_element_type=F32)
        hg = hu[:, :EXPERT_HIDDEN]
        act = hg * _sigmoid(hg) * hu[:, EXPERT_HIDDEN:]
        shared = jnp.dot(act.astype(BF16), sd_ref[...], preferred_element_type=F32)
        xo = x1_ref[...] + g2_ref[...] * (routed + shared)
        o_ref[...] = _rms(xo) * fg_ref[...]

    _pipelined_gather(cur_ref, nxt_ref, y_hbm, ybuf, sem, TOP_K * tm, consume)


def _combine(pos3, y_tiles, hx_tiles, w_tok, sgu, sd, x1, g2, fg, tm, tiles_per_batch):
    N, D = x1.shape
    nt = N // tm
    smem_blk = lambda f: pl.BlockSpec((None, 1, TOP_K * tm), f, memory_space=pltpu.SMEM)
    row = lambda i: (i, 0)
    const = lambda i: (0, 0)
    return pl.pallas_call(
        functools.partial(_combine_kernel, tm=tm),
        grid=(nt,),
        in_specs=[smem_blk(lambda i: (i, 0, 0)),
                  smem_blk(lambda i: (jnp.minimum(i + 1, nt - 1), 0, 0)),
                  pl.BlockSpec(memory_space=pl.ANY),
                  pl.BlockSpec((tm * TILE_ROWS, 128), row),
                  pl.BlockSpec((tm, 128), row),
                  pl.BlockSpec((D, 2 * EXPERT_HIDDEN), const),
                  pl.BlockSpec((EXPERT_HIDDEN, D), const),
                  pl.BlockSpec((tm, D), row),
                  pl.BlockSpec((None, 1, D), lambda i: (i // tiles_per_batch, 0, 0)),
                  pl.BlockSpec((1, D), const)],
        out_specs=pl.BlockSpec((tm, D), row),
        out_shape=jax.ShapeDtypeStruct((N, D), F32),
        scratch_shapes=[pltpu.VMEM((2 * TOP_K * tm * TILE_ROWS, 128), U32), pltpu.SemaphoreType.DMA((2,))],
        compiler_params=_cparams(("arbitrary",)),
        name="moe_combine",
    )(pos3, pos3, y_tiles, hx_tiles, w_tok, sgu, sd, x1, g2, fg)


def _rope_tables(L):
    rows = L // GRID_W
    row = jnp.repeat(jnp.arange(rows), GRID_W).astype(F32)
    col = jnp.tile(jnp.arange(GRID_W), rows).astype(F32)
    inv = ROPE_THETA ** (-jnp.arange(0, ROPE_AXIS_DIM, 2, dtype=F32) / ROPE_AXIS_DIM)
    ang = jnp.stack([row[:, None] * inv, col[:, None] * inv], axis=1)
    c, s = jnp.cos(ang), jnp.sin(ang)
    cos_h = jnp.stack([c, c], axis=2).reshape(L, HEAD_DIM)
    sin_h = jnp.stack([-s, s], axis=2).reshape(L, HEAD_DIM)
    return jnp.tile(cos_h, (1, 2)), jnp.tile(sin_h, (1, 2))


def _tile(n, pref):
    t = min(pref, n)
    while n % t:
        t //= 2
    return t


def kernel(x, c, ctx, c_ctx, ada_w, ada_b, norm1_g, norm2_g, w_in, lam_q1, lam_k1, lam_q2, lam_k2, subln_g, hy_conv_w, hy_conv_b, filt_w1, filt_b1, filt_freq, filt_w2, filt_b2, filt_w3, hy_bias, w_branch_attn, w_branch_hyena, w_out, router_w, router_bias, exp_w_gate, exp_w_up, exp_w_down, shared_w_gate, shared_w_up, shared_w_down, final_norm_g):
    B, L, D = x.shape
    assert ada_w.shape[0] == 1 and B % 2 == 0 and (2 * L) % (FFT_SLABS * FFT_STRIP) == 0
    lam = (jnp.exp(jnp.sum(lam_q1[0].astype(F32) * lam_k1[0].astype(F32)))
           - jnp.exp(jnp.sum(lam_q2[0].astype(F32) * lam_k2[0].astype(F32))) + LAM_INIT).reshape(1, 1)

    cc = jnp.zeros((16, D), F32).at[:B].set(c).at[B].set(c_ctx)
    mod = _adaln(cc, ada_w[0], ada_b[0])
    chunk = lambda r0, r1, j: mod[r0:r1, j * D:(j + 1) * D]
    sh1, sc1, g1, sh2, sc2, g2 = (chunk(0, B, j).reshape(B, 1, D) for j in range(6))
    csh, csc = chunk(B, B + 1, 0), chunk(B, B + 1, 1)

    w_bf = w_in[0].astype(BF16)
    n1g = norm1_g[0].reshape(1, D)
    cos_t, sin_t = _rope_tables(L)
    k, v, q, u, gates = _inproj(x, sh1, sc1, n1g, w_bf, cos_t, sin_t, hy_conv_w[0], hy_conv_b[0].reshape(1, -1),
                                _tile(L, 512))
    kc, vc = _ctxproj(ctx, csh, csc, n1g, w_bf)
    attn = _attention(lam, q, k, v, kc, vc, subln_g[0].reshape(1, V_DIM), _tile(L, 4096), _tile(L, 512))

    filt = (filt_w1[0], filt_b1[0], filt_freq[0], filt_w2[0], filt_b2[0], filt_w3[0])
    hy = _hyena(u, filt, hy_bias[0])

    x1, hx2, logits_t = _merge(attn, hy, gates, x, g1, sh2, sc2, norm2_g[0].reshape(1, D),
                               w_branch_attn[0].astype(BF16), w_branch_hyena[0].astype(BF16),
                               w_out[0].astype(BF16), router_w[0].T.astype(F32), _tile(L, 1024))
    eidx, w_tok, rank, counts = _router(logits_t, router_bias[0], _tile(B * L, 1024))

    N = B * L
    tm = _tile(L, MOE_TOKENS)
    pos3, pad_pos, blk_e, nb = _dispatch_plan(eidx, rank, counts, MOE_ROWS, tm)
    xs = _dispatch(pos3, pad_pos, hx2, nb * MOE_ROWS, tm)
    y = _experts(xs, blk_e, exp_w_gate[0], exp_w_up[0], exp_w_down[0], nb, MOE_ROWS)
    sgu = jnp.concatenate([shared_w_gate[0], shared_w_up[0]], axis=-1).astype(BF16)
    out = _combine(pos3, y, hx2, w_tok, sgu, shared_w_down[0].astype(BF16), x1.reshape(N, D), g2,
                   final_norm_g.reshape(1, D), tm, L // tm)
    return out.reshape(B, L, D)
```

```python
import functools
import math

import jax
import jax.numpy as jnp
from jax import lax
from jax.experimental import pallas as pl
from jax.experimental.pallas import tpu as pltpu

F32 = jnp.float32
BF16 = jnp.bfloat16
U32 = jnp.uint32
HIGHEST = lax.Precision.HIGHEST

NORM_EPS = 1e-6
N_HEADS = 8
HEAD_DIM = 64
V_DIM = 2 * HEAD_DIM
GRID_W = 64
ROPE_THETA = 10000.0
ROPE_AXIS_DIM = HEAD_DIM // 2
HYENA_W = 1024
FILTER_EMB = 33
FILTER_BANDS = (FILTER_EMB - 1) // 2
DECAY_TARGET = 1e-2
FAST_DECAY_PCT = 0.3
SLOW_DECAY_PCT = 1.5
N_EXPERTS = 64
N_GROUPS = 8
TOPK_GROUPS = 4
TOP_K = 8
EXPERT_HIDDEN = 256
ROUTED_SCALE = 2.5
LAM_INIT = 0.8 - 0.6 * math.exp(-0.3 * 0)

V7X_VMEM_LIMIT = 56 * 1024 * 1024
LOG2E = 1.4426950408889634


def _cparams(sem):
    return pltpu.CompilerParams(dimension_semantics=sem, vmem_limit_bytes=V7X_VMEM_LIMIT)


def _sigmoid(v):
    return 1.0 / (1.0 + jnp.exp(-v))


def _rms(v):
    return v * lax.rsqrt(jnp.mean(v * v, axis=-1, keepdims=True) + NORM_EPS)


def _adaln_kernel(c_ref, w_ref, b_ref, o_ref):
    c = c_ref[...]
    s = c * _sigmoid(c)
    o_ref[...] = jnp.dot(s, w_ref[...], precision=HIGHEST, preferred_element_type=F32) + b_ref[...]


def _adaln(cc, w, b):
    rows, d = cc.shape
    n = w.shape[1]
    tn = 1536
    return pl.pallas_call(
        _adaln_kernel,
        grid=(n // tn,),
        in_specs=[pl.BlockSpec((rows, d), lambda j: (0, 0)),
                  pl.BlockSpec((d, tn), lambda j: (0, j)),
                  pl.BlockSpec((1, tn), lambda j: (0, j))],
        out_specs=pl.BlockSpec((rows, tn), lambda j: (0, j)),
        out_shape=jax.ShapeDtypeStruct((rows, n), F32),
        compiler_params=_cparams(("arbitrary",)),
        name="adaln",
    )(cc, w, b.reshape(1, n))


def _inproj_kernel(x_ref, xp_ref, xn_ref, sh_ref, sc_ref, g_ref, w_ref, cos_ref, sin_ref, cw_ref, cb_ref,
                   k_ref, v_ref, q_ref, u_ref, gt_ref, pscr, *, tm, nt, qscale):
    i = pl.program_id(1)
    g = g_ref[...]
    sh = sh_ref[...]
    sc = 1.0 + sc_ref[...]

    def norm_mod(xx):
        return (_rms(xx) * g) * sc + sh

    h = norm_mod(x_ref[...])
    hp = jnp.where(i > 0, norm_mod(xp_ref[...]), 0.0)
    hn = jnp.where(i < nt - 1, norm_mod(xn_ref[...]), 0.0)
    hb = h.astype(BF16)
    h_ext = jnp.concatenate([hp, h, hn], axis=0).astype(BF16)

    def proj(lhs, c0):
        return jnp.dot(lhs, w_ref[:, c0:c0 + 1024], preferred_element_type=F32)

    cosv = cos_ref[...]
    sinv = sin_ref[...]
    lane = lax.broadcasted_iota(jnp.int32, (1, V_DIM), 1)
    first = (lane % ROPE_AXIS_DIM) < (ROPE_AXIS_DIM // 2)

    def rope_store(a, o_ref, scale):
        for hh in range(N_HEADS):
            s = a[:, hh * V_DIM:(hh + 1) * V_DIM]
            partner = jnp.where(first, pltpu.roll(s, V_DIM - ROPE_AXIS_DIM // 2, 1),
                                pltpu.roll(s, ROPE_AXIS_DIM // 2, 1))
            r = s * cosv + partner * sinv
            if scale != 1.0:
                r = r * scale
            o_ref[:, hh * V_DIM:(hh + 1) * V_DIM] = r.astype(BF16)

    rope_store(proj(hb, 0), k_ref, 1.0)
    v_ref[...] = proj(hb, 1024).astype(BF16)
    rope_store(proj(hb, 2048), q_ref, qscale)
    for j in range(3):
        pscr[...] = proj(h_ext, 3072 + j * 1024)
        cw = cw_ref[:, j * 1024:(j + 1) * 1024]
        cb = cb_ref[:, j * 1024:(j + 1) * 1024]
        u = (pscr[pl.ds(7, tm), :] * cw[0:1] + pscr[pl.ds(8, tm), :] * cw[1:2]
             + pscr[pl.ds(9, tm), :] * cw[2:3] + cb)
        u_ref[:, j * 1024:(j + 1) * 1024] = u.astype(BF16)
    for j in range(2):
        gt_ref[:, j * 1024:(j + 1) * 1024] = proj(hb, 6144 + j * 1024).astype(BF16)


def _inproj(x, shift, scale, g, w_bf, cos_t, sin_t, conv_w, conv_b, tm):
    B, L, D = x.shape
    nt = L // tm
    nb8 = L // 8
    qscale = (HEAD_DIM ** -0.5) * LOG2E
    kern = functools.partial(_inproj_kernel, tm=tm, nt=nt, qscale=qscale)
    row = lambda b, i: (b, i, 0)
    per_b = lambda b, i: (b, 0, 0)
    const = lambda b, i: (0, 0)
    outs = pl.pallas_call(
        kern,
        grid=(B, nt),
        in_specs=[
            pl.BlockSpec((None, tm, D), row),
            pl.BlockSpec((None, 8, D), lambda b, i: (b, jnp.maximum(i * (tm // 8) - 1, 0), 0)),
            pl.BlockSpec((None, 8, D), lambda b, i: (b, jnp.minimum((i + 1) * (tm // 8), nb8 - 1), 0)),
            pl.BlockSpec((None, 1, D), per_b),
            pl.BlockSpec((None, 1, D), per_b),
            pl.BlockSpec((1, D), const),
            pl.BlockSpec((D, 8192), const, pipeline_mode=pl.Buffered(1)),
            pl.BlockSpec((tm, V_DIM), lambda b, i: (i, 0)),
            pl.BlockSpec((tm, V_DIM), lambda b, i: (i, 0)),
            pl.BlockSpec((3, 3 * HYENA_W), const),
            pl.BlockSpec((1, 3 * HYENA_W), const),
        ],
        out_specs=[
            pl.BlockSpec((None, tm, 1024), row),
            pl.BlockSpec((None, tm, 1024), row),
            pl.BlockSpec((None, tm, 1024), row),
            pl.BlockSpec((None, tm, 3 * HYENA_W), row),
            pl.BlockSpec((None, tm, 2048), row),
        ],
        out_shape=[
            jax.ShapeDtypeStruct((B, L, 1024), BF16),
            jax.ShapeDtypeStruct((B, L, 1024), BF16),
            jax.ShapeDtypeStruct((B, L, 1024), BF16),
            jax.ShapeDtypeStruct((B, L, 3 * HYENA_W), BF16),
            jax.ShapeDtypeStruct((B, L, 2048), BF16),
        ],
        scratch_shapes=[pltpu.VMEM((tm + 16, 1024), F32)],
        compiler_params=_cparams(("parallel", "arbitrary")),
        name="inproj",
    )(x, x, x, shift, scale, g, w_bf, cos_t, sin_t, conv_w, conv_b)
    return outs


def _ctxproj_kernel(x_ref, sh_ref, sc_ref, g_ref, w_ref, k_ref, v_ref):
    h = ((_rms(x_ref[...]) * g_ref[...]) * (1.0 + sc_ref[...]) + sh_ref[...]).astype(BF16)
    k_ref[...] = jnp.dot(h, w_ref[:, 0:1024], preferred_element_type=F32).astype(BF16)
    v_ref[...] = jnp.dot(h, w_ref[:, 1024:2048], preferred_element_type=F32).astype(BF16)


def _ctxproj(ctx, shift, scale, g, w_bf):
    B, Lc, D = ctx.shape
    return pl.pallas_call(
        _ctxproj_kernel,
        grid=(B,),
        in_specs=[pl.BlockSpec((None, Lc, D), lambda b: (b, 0, 0)),
                  pl.BlockSpec((1, D), lambda b: (0, 0)),
                  pl.BlockSpec((1, D), lambda b: (0, 0)),
                  pl.BlockSpec((1, D), lambda b: (0, 0)),
                  pl.BlockSpec((D, 2048), lambda b: (0, 0))],
        out_specs=[pl.BlockSpec((None, Lc, 1024), lambda b: (b, 0, 0)),
                   pl.BlockSpec((None, Lc, 1024), lambda b: (b, 0, 0))],
        out_shape=[jax.ShapeDtypeStruct((B, Lc, 1024), BF16),
                   jax.ShapeDtypeStruct((B, Lc, 1024), BF16)],
        compiler_params=_cparams(("arbitrary",)),
        name="ctxproj",
    )(ctx, shift, scale, g, w_bf)


def _attn_kernel(lam_ref, q_ref, k_ref, v_ref, kc_ref, vc_ref, g_ref, o_ref, m_s, acc_s, *, tk, nk):
    q = q_ref[...]
    lane = lax.broadcasted_iota(jnp.int32, (1, V_DIM), 1)
    zero = jnp.zeros_like(q)
    qs = (jnp.where(lane < HEAD_DIM, q, zero), jnp.where(lane >= HEAD_DIM, q, zero))
    def chunk(kk, vv, first=False):
        reps = kk.shape[0] // V_DIM
        v1 = jnp.concatenate([vv, jnp.ones_like(vv)], axis=1)
        for m in range(2):
            s = lax.dot_general(qs[m], kk, (((1,), (1,)), ((), ())), preferred_element_type=F32)
            m_new = jnp.broadcast_to(jnp.max(s, axis=-1, keepdims=True), m_s.shape[1:])
            if not first:
                m_prev = m_s[m]
                m_new = jnp.maximum(m_prev, m_new)
                alpha = jnp.exp2(m_prev - m_new)
            p = jnp.exp2(s - jnp.concatenate([m_new] * reps, axis=1))
            pv = jnp.dot(p.astype(BF16), v1, preferred_element_type=F32)
            acc_s[m] = pv if first else jnp.concatenate([alpha, alpha], axis=1) * acc_s[m] + pv
            m_s[m] = m_new

    def body(j, carry):
        off = pl.multiple_of(j * tk, tk)
        chunk(k_ref[pl.ds(off, tk), :], v_ref[pl.ds(off, tk), :])
        return carry

    chunk(kc_ref[...], vc_ref[...], first=True)
    lax.fori_loop(0, nk, body, 0, unroll=2 if nk % 2 == 0 else 1)
    lam = lam_ref[0, 0]
    a0, a1 = acc_s[0], acc_s[1]
    o = a0[:, :V_DIM] / a0[:, V_DIM:] - lam * (a1[:, :V_DIM] / a1[:, V_DIM:])
    o = _rms(o) * g_ref[...] * (1.0 - LAM_INIT)
    o_ref[...] = o.astype(BF16)


def _attention(lam, q, k, v, kc, vc, subln_g, tq, tk):
    B, L, _ = q.shape
    Lc = kc.shape[1]
    kern = functools.partial(_attn_kernel, tk=tk, nk=L // tk)
    kv = lambda b, h, i: (b, 0, h)
    return pl.pallas_call(
        kern,
        grid=(B, N_HEADS, L // tq),
        in_specs=[
            pl.BlockSpec(memory_space=pltpu.SMEM),
            pl.BlockSpec((None, tq, V_DIM), lambda b, h, i: (b, i, h)),
            pl.BlockSpec((None, L, V_DIM), kv),
            pl.BlockSpec((None, L, V_DIM), kv),
            pl.BlockSpec((None, Lc, V_DIM), kv),
            pl.BlockSpec((None, Lc, V_DIM), kv),
            pl.BlockSpec((1, V_DIM), lambda b, h, i: (0, 0)),
        ],
        out_specs=pl.BlockSpec((None, tq, V_DIM), lambda b, h, i: (b, i, h)),
        out_shape=jax.ShapeDtypeStruct((B, L, N_HEADS * V_DIM), BF16),
        scratch_shapes=[pltpu.VMEM((2, tq, V_DIM), F32), pltpu.VMEM((2, tq, 2 * V_DIM), F32)],
        compiler_params=_cparams(("parallel", "parallel", "arbitrary")),
        name="diffattn",
    )(lam, q, k, v, kc, vc, subln_g)


FFT_SLABS = 64
FFT_STRIP = 16
FILTER_TAP_STEPS = 16


def _dft_tables(nb):
    N = FFT_SLABS * nb
    fa = jnp.arange(FFT_SLABS, dtype=jnp.int32)[:, None, None]
    fb = jnp.arange(nb, dtype=jnp.int32)[None, :, None]
    b = jnp.arange(nb, dtype=jnp.int32)[None, None, :]
    ang = ((FFT_SLABS * b * fb + b * fa) % N).astype(F32) * (2.0 * math.pi / N)
    c, s = jnp.cos(ang), jnp.sin(ang)
    fwd = jnp.concatenate([jnp.concatenate([c, s], -1), jnp.concatenate([-s, c], -1)], axis=1)
    inv = jnp.swapaxes(fwd, 1, 2) / N
    return fwd.astype(BF16), inv.astype(BF16)


def _vadd(a, b):
    return b if a is None else (a if b is None else a + b)


def _vsub(a, b):
    if b is None:
        return a
    return -b if a is None else a - b


def _vscale(a, k):
    if a is None or k == 1.0:
        return a
    return -a if k == -1.0 else a * k


def _cmulc(x, c, s):
    re, im = x
    eps = 1e-9
    if abs(s) < eps:
        k = 1.0 if c > 0 else -1.0
        return _vscale(re, k), _vscale(im, k)
    if abs(c) < eps:
        k = 1.0 if s > 0 else -1.0
        return _vscale(im, -k), _vscale(re, k)
    if abs(abs(c) - abs(s)) < eps:
        kc = 1.0 if c > 0 else -1.0
        ks = 1.0 if s > 0 else -1.0
        return (_vscale(_vsub(_vscale(re, kc), _vscale(im, ks)), abs(c)),
                _vscale(_vadd(_vscale(re, ks), _vscale(im, kc)), abs(c)))
    return _vsub(_vscale(re, c), _vscale(im, s)), _vadd(_vscale(re, s), _vscale(im, c))


def _fft(xs, sign):
    n = len(xs)
    if n == 1:
        return xs
    ev, od = _fft(xs[0::2], sign), _fft(xs[1::2], sign)
    out = [None] * n
    for k in range(n // 2):
        ang = sign * 2.0 * math.pi * k / n
        t = _cmulc(od[k], math.cos(ang), math.sin(ang))
        out[k] = (_vadd(ev[k][0], t[0]), _vadd(ev[k][1], t[1]))
        out[k + n // 2] = (_vsub(ev[k][0], t[0]), _vsub(ev[k][1], t[1]))
    return out


def _put(ref, slot, rows, val):
    ref[slot, rows, :] = jnp.zeros((FFT_STRIP, ref.shape[2]), F32) if val is None else val


def _slab_fft_forward(load, s_ref, nb, n_in):
    def strip(i, carry):
        r0 = pl.multiple_of(i * FFT_STRIP, FFT_STRIP)
        rows_re, rows_im = pl.ds(r0, FFT_STRIP), pl.ds(nb + r0, FFT_STRIP)
        for a0 in range(8):
            xs = [load(8 * a1 + a0, r0) if a1 < n_in else (None, None) for a1 in range(8)]
            ys = _fft(xs, -1.0)
            for f1 in range(8):
                ang = -2.0 * math.pi * a0 * f1 / FFT_SLABS
                yr, yi = _cmulc(ys[f1], math.cos(ang), math.sin(ang))
                _put(s_ref, a0 * 8 + f1, rows_re, yr)
                _put(s_ref, a0 * 8 + f1, rows_im, yi)
        for f1 in range(8):
            xs = [(s_ref[a0 * 8 + f1, rows_re, :], s_ref[a0 * 8 + f1, rows_im, :]) for a0 in range(8)]
            ys = _fft(xs, -1.0)
            for f0 in range(8):
                s_ref[f1 + 8 * f0, rows_re, :] = ys[f0][0]
                s_ref[f1 + 8 * f0, rows_im, :] = ys[f0][1]
        return carry

    lax.fori_loop(0, nb // FFT_STRIP, strip, 0)


def _slab_fft_inverse(s_ref, nb, emit):
    def strip(i, carry):
        r0 = pl.multiple_of(i * FFT_STRIP, FFT_STRIP)
        rows_re, rows_im = pl.ds(r0, FFT_STRIP), pl.ds(nb + r0, FFT_STRIP)
        for q in range(8):
            xs = [(s_ref[8 * p + q, rows_re, :], s_ref[8 * p + q, rows_im, :]) for p in range(8)]
            us = _fft(xs, 1.0)
            for r in range(8):
                ang = 2.0 * math.pi * r * q / FFT_SLABS
                ur, ui = _cmulc(us[r], math.cos(ang), math.sin(ang))
                s_ref[8 * r + q, rows_re, :] = ur
                s_ref[8 * r + q, rows_im, :] = ui
        for r in range(8):
            xs = [(s_ref[8 * r + q, rows_re, :], s_ref[8 * r + q, rows_im, :]) for q in range(8)]
            ys = _fft(xs, 1.0)
            for sidx in range(4):
                emit(r + 8 * sidx, r0, ys[sidx][0], ys[sidx][1])
        return carry

    lax.fori_loop(0, nb // FFT_STRIP, strip, 0)


def _hyena_conv_kernel(v_ref, x_ref, k_ref, ff_ref, fi_ref, bias_ref, o_ref, s_ref, *, nb):
    def load(a, r0):
        rows = pl.ds(pl.multiple_of(a * nb + r0, FFT_STRIP), FFT_STRIP)
        return v_ref[0, rows, :].astype(F32), v_ref[1, rows, :].astype(F32)

    _slab_fft_forward(load, s_ref, nb, FFT_SLABS // 16)

    def freq(f, carry):
        z = jnp.dot(ff_ref[f], s_ref[f].astype(BF16), preferred_element_type=F32)
        zr, zi = z[:nb], z[nb:]
        kr, ki = k_ref[0, f], k_ref[1, f]
        y = jnp.concatenate([zr * kr - zi * ki, zr * ki + zi * kr], axis=0).astype(BF16)
        s_ref[f] = jnp.dot(fi_ref[f], y, preferred_element_type=F32)
        return carry

    lax.fori_loop(0, FFT_SLABS, freq, 0, unroll=32)
    bias = bias_ref[...]

    def emit(a, r0, re, im):
        rows = pl.ds(pl.multiple_of(a * nb + r0, FFT_STRIP), FFT_STRIP)
        for half, y in enumerate((re, im)):
            v = v_ref[half, rows, :].astype(F32)
            o_ref[half, rows, :] = (x_ref[half, rows, :].astype(F32) * (y + v * bias)).astype(BF16)

    _slab_fft_inverse(s_ref, nb, emit)


def _hyena_conv(uv, zv, k_hat, ff, fi, bias, order, in_col, gate_col, nb):
    _, P, L, _ = uv.shape
    W = HYENA_W
    nct = W // 128
    src, src_off = (uv, in_col * nct) if zv is None else (zv, 0)
    blk = (2, None, L, 128)
    return pl.pallas_call(
        functools.partial(_hyena_conv_kernel, nb=nb),
        grid=(nct, P),
        in_specs=[pl.BlockSpec(blk, lambda c, p: (0, p, 0, src_off + c)),
                  pl.BlockSpec(blk, lambda c, p: (0, p, 0, gate_col * nct + c)),
                  pl.BlockSpec((2, FFT_SLABS, nb, 128), lambda c, p: (0, 0, 0, order * nct + c),
                               pipeline_mode=pl.Buffered(1)),
                  pl.BlockSpec((FFT_SLABS, 2 * nb, 2 * nb), lambda c, p: (0, 0, 0), pipeline_mode=pl.Buffered(1)),
                  pl.BlockSpec((FFT_SLABS, 2 * nb, 2 * nb), lambda c, p: (0, 0, 0), pipeline_mode=pl.Buffered(1)),
                  pl.BlockSpec((1, 128), lambda c, p: (0, c))],
        out_specs=pl.BlockSpec(blk, lambda c, p: (0, p, 0, c)),
        out_shape=jax.ShapeDtypeStruct((2, P, L, W), BF16),
        scratch_shapes=[pltpu.VMEM((FFT_SLABS, 2 * nb, 128), F32)],
        compiler_params=_cparams(("parallel", "arbitrary")),
        name="hyena_conv%d" % order,
    )(src, uv, k_hat, ff, fi, bias)


def _filter_features(n):
    pos = jnp.arange(n, dtype=F32)[:, None]
    t = jnp.linspace(0.0, 1.0, n, dtype=F32)[:, None]
    w = 2 * math.pi * pos / n
    bands = jnp.linspace(1e-4, FILTER_BANDS - 1, FILTER_BANDS, dtype=F32)
    z = jnp.concatenate([t, jnp.cos(bands * w), -jnp.sin(bands * w)], axis=-1)
    z = jnp.pad(z, ((0, 0), (0, 128 - FILTER_EMB)))
    return jnp.concatenate([z, z[0:1], z[:0:-1]], axis=0)


def _dot_bf16x3(a, b):
    ah, bh = a.astype(BF16), b.astype(BF16)
    al, bl = (a - ah.astype(F32)).astype(BF16), (b - bh.astype(F32)).astype(BF16)
    dot = functools.partial(jnp.dot, preferred_element_type=F32)
    return dot(ah, bh) + (dot(ah, bl) + dot(al, bh))


def _filt_taps_kernel(z_ref, w1_ref, b1_ref, fr_ref, w2_ref, b2_ref, w3_ref, dl_ref, k_ref, sum_ref):
    a = pl.program_id(0)
    z = z_ref[...]
    fr = fr_ref[...]
    hd = jnp.sin(fr * (jnp.dot(z, w1_ref[...], precision=HIGHEST, preferred_element_type=F32) + b1_ref[...]))
    hd = jnp.sin(fr * (jnp.dot(hd, w2_ref[...], precision=HIGHEST, preferred_element_type=F32) + b2_ref[...]))
    h = _dot_bf16x3(hd, w3_ref[...]) * jnp.exp(-z[:, 0:1] * dl_ref[...])
    rows = lax.broadcasted_iota(jnp.int32, (z.shape[0], 1), 0)
    h = jnp.where(jnp.logical_and(a == pl.num_programs(0) // 2, rows == 0), 0.0, h)
    k_ref[...] = h

    @pl.when(a == 0)
    def _():
        sum_ref[...] = jnp.zeros_like(sum_ref)

    sum_ref[0:1, :] += jnp.sum(jnp.abs(h), axis=0, keepdims=True)


def _filt_spectrum_kernel(t_ref, inv_ref, ff_ref, k_ref, s_ref, *, nb):
    def load(a, r0):
        return t_ref[pl.ds(pl.multiple_of(a * nb + r0, FFT_STRIP), FFT_STRIP), :], None

    _slab_fft_forward(load, s_ref, nb, FFT_SLABS // 8)
    inv = inv_ref[...]

    def freq(f, carry):
        z = jnp.dot(ff_ref[f], s_ref[f].astype(BF16), preferred_element_type=F32)
        k_ref[0, f] = z[:nb] * inv
        k_ref[1, f] = z[nb:] * inv
        return carry

    lax.fori_loop(0, FFT_SLABS, freq, 0, unroll=32)


def _hyena_filters(n, filt, ff, nb):
    w1, b1, freq, w2, b2, w3 = filt
    W = HYENA_W
    hid = w1.shape[1]
    z = _filter_features(n)
    w1p = jnp.pad(w1.astype(F32), ((0, 128 - FILTER_EMB), (0, 0)))
    w3d = w3.astype(F32).reshape(hid, 2, 2, W)
    w3sel = jnp.stack([w3d[:, :, d, :].reshape(hid, 2 * W) for d in range(2)])
    deltas = jnp.abs(jnp.linspace(math.log(DECAY_TARGET) / SLOW_DECAY_PCT,
                                  math.log(DECAY_TARGET) / FAST_DECAY_PCT, W, dtype=F32))
    deltas = jnp.tile(deltas.reshape(1, W), (1, 2))
    const = lambda a: (0, 0)
    steps = FILTER_TAP_STEPS
    rs = 2 * n // steps
    taps, sums = pl.pallas_call(
        _filt_taps_kernel,
        grid=(steps,),
        in_specs=[pl.BlockSpec((rs, 128), lambda a: (a, 0)),
                  pl.BlockSpec((128, hid), const), pl.BlockSpec((1, hid), const), pl.BlockSpec((1, hid), const),
                  pl.BlockSpec((hid, hid), const), pl.BlockSpec((1, hid), const),
                  pl.BlockSpec((None, hid, 2 * W), lambda a: (a // (steps // 2), 0, 0)),
                  pl.BlockSpec((1, 2 * W), const)],
        out_specs=[pl.BlockSpec((rs, 2 * W), lambda a: (a, 0)), pl.BlockSpec((8, 2 * W), const)],
        out_shape=[jax.ShapeDtypeStruct((2 * n, 2 * W), F32), jax.ShapeDtypeStruct((8, 2 * W), F32)],
        compiler_params=_cparams(("arbitrary",)),
        name="hyena_filter_taps",
    )(z, w1p, b1.reshape(1, hid), freq.reshape(1, hid), w2, b2.reshape(1, hid), w3sel, deltas)
    inv = 1.0 / sums[0:1]
    return pl.pallas_call(
        functools.partial(_filt_spectrum_kernel, nb=nb),
        grid=(2 * W // 128,),
        in_specs=[pl.BlockSpec((2 * n, 128), lambda j: (0, j)),
                  pl.BlockSpec((1, 128), lambda j: (0, j)),
                  pl.BlockSpec((FFT_SLABS, 2 * nb, 2 * nb), lambda j: (0, 0, 0), pipeline_mode=pl.Buffered(1))],
        out_specs=pl.BlockSpec((2, FFT_SLABS, nb, 128), lambda j: (0, 0, 0, j)),
        out_shape=jax.ShapeDtypeStruct((2, FFT_SLABS, nb, 2 * W), F32),
        scratch_shapes=[pltpu.VMEM((FFT_SLABS, 2 * nb, 128), F32)],
        compiler_params=_cparams(("parallel",)),
        name="hyena_filter_spectrum",
    )(taps, inv, ff)


def _hyena(u, filt, hy_bias):
    B, L, _ = u.shape
    P = B // 2
    nb = 2 * L // FFT_SLABS
    ff, fi = _dft_tables(nb)
    k_hat = _hyena_filters(L, filt, ff, nb)
    uv = u.reshape(2, P, L, 3 * HYENA_W)
    bias = hy_bias.astype(F32)
    z = _hyena_conv(uv, None, k_hat, ff, fi, bias[0:1], 0, 0, 1, nb)
    y = _hyena_conv(uv, z, k_hat, ff, fi, bias[1:2], 1, 0, 2, nb)
    return y.reshape(B, L, HYENA_W)


def _merge_kernel(a_ref, h_ref, gt_ref, x_ref, g1_ref, sh_ref, sc_ref, n2g_ref, wpa_ref, wph_ref, wo_ref, rw_ref,
                  x1_ref, hx_ref, lg_ref):
    ya = jnp.dot(a_ref[...], wpa_ref[...], preferred_element_type=F32)
    yh = jnp.dot(h_ref[...], wph_ref[...], preferred_element_type=F32)
    ga = gt_ref[:, 0:1024].astype(F32)
    gh = gt_ref[:, 1024:2048].astype(F32)
    y = _sigmoid(ga) * ya + _sigmoid(gh) * yh
    mix = jnp.dot(y.astype(BF16), wo_ref[...], preferred_element_type=F32)
    x1 = x_ref[...] + g1_ref[...] * mix
    x1_ref[...] = x1
    hx = (_rms(x1) * n2g_ref[...]) * (1.0 + sc_ref[...]) + sh_ref[...]
    _to_tiles(hx_ref, hx)
    lg_ref[...] = lax.dot_general(rw_ref[...], hx, (((1,), (1,)), ((), ())), precision=HIGHEST,
                                  preferred_element_type=F32)


def _merge(attn, hy, gates, x, g1, sh2, sc2, n2g, wpa, wph, wo, rwT, tm):
    B, L, D = x.shape
    nt = L // tm
    row = lambda b, i: (b, i, 0)
    per_b = lambda b, i: (b, 0, 0)
    const = lambda b, i: (0, 0)
    return pl.pallas_call(
        _merge_kernel,
        grid=(B, nt),
        in_specs=[pl.BlockSpec((None, tm, 1024), row), pl.BlockSpec((None, tm, 1024), row),
                  pl.BlockSpec((None, tm, 2048), row), pl.BlockSpec((None, tm, D), row),
                  pl.BlockSpec((None, 1, D), per_b), pl.BlockSpec((None, 1, D), per_b),
                  pl.BlockSpec((None, 1, D), per_b), pl.BlockSpec((1, D), const),
                  pl.BlockSpec((1024, D), const), pl.BlockSpec((1024, D), const), pl.BlockSpec((D, D), const),
                  pl.BlockSpec((N_EXPERTS, D), const)],
        out_specs=[pl.BlockSpec((None, tm, D), row),
                   pl.BlockSpec((tm * TILE_ROWS, 128), lambda b, i: (b * nt + i, 0)),
                   pl.BlockSpec((N_EXPERTS, tm), lambda b, i: (0, b * nt + i))],
        out_shape=[jax.ShapeDtypeStruct((B, L, D), F32),
                   jax.ShapeDtypeStruct((B * L * TILE_ROWS, 128), U32),
                   jax.ShapeDtypeStruct((N_EXPERTS, B * L), F32)],
        compiler_params=_cparams(("parallel", "arbitrary")),
        name="merge",
    )(attn, hy, gates, x, g1, sh2, sc2, n2g, wpa, wph, wo, rwT)


def _router_kernel(lg_ref, rb_ref, tri_ref, ei_ref, wt_ref, rk_ref, cnt_ref):
    scores = _sigmoid(lg_ref[...])
    choice = scores + rb_ref[...]
    E, T = scores.shape
    gsz = E // N_GROUPS
    neg = jnp.float32(-jnp.inf)
    big = jnp.int32(1 << 30)
    rows8 = lax.broadcasted_iota(jnp.int32, (gsz, T), 0)
    gscore = []
    for gi in range(N_GROUPS):
        c = choice[gi * gsz:(gi + 1) * gsz]
        m1 = jnp.max(c, axis=0, keepdims=True)
        i1 = jnp.min(jnp.where(c == m1, rows8, big), axis=0, keepdims=True)
        m2 = jnp.max(jnp.where(rows8 == i1, neg, c), axis=0, keepdims=True)
        gscore.append(m1 + m2)
    gs = jnp.concatenate(gscore, axis=0)
    rowsg = lax.broadcasted_iota(jnp.int32, (N_GROUPS, T), 0)
    gsel = jnp.zeros((N_GROUPS, T), jnp.bool_)
    for _ in range(TOPK_GROUPS):
        m = jnp.max(gs, axis=0, keepdims=True)
        ix = jnp.min(jnp.where(gs == m, rowsg, big), axis=0, keepdims=True)
        hit = rowsg == ix
        gsel = jnp.logical_or(gsel, hit)
        gs = jnp.where(hit, neg, gs)
    masked = jnp.concatenate(
        [jnp.where(gsel[gi:gi + 1], choice[gi * gsz:(gi + 1) * gsz], neg) for gi in range(N_GROUPS)], axis=0)
    rows = lax.broadcasted_iota(jnp.int32, (E, T), 0)
    ids, ws, hits = [], [], []
    sel = jnp.zeros((E, T), F32)
    for _ in range(TOP_K):
        m = jnp.max(masked, axis=0, keepdims=True)
        ix = jnp.min(jnp.where(masked == m, rows, big), axis=0, keepdims=True)
        hit = rows == ix
        ids.append(ix)
        hits.append(hit)
        ws.append(jnp.sum(jnp.where(hit, scores, 0.0), axis=0, keepdims=True))
        sel = jnp.where(hit, 1.0, sel)
        masked = jnp.where(hit, neg, masked)
    w = jnp.concatenate(ws, axis=0)
    w = w / jnp.sum(w, axis=0, keepdims=True) * ROUTED_SCALE
    ei_ref[...] = jnp.concatenate(ids, axis=0)
    wt_ref[...] = jnp.concatenate([w, jnp.zeros((128 - TOP_K, T), F32)], axis=0).T

    @pl.when(pl.program_id(0) == 0)
    def _():
        cnt_ref[...] = jnp.zeros_like(cnt_ref)

    before = cnt_ref[...][:, 0:1]
    csum = jnp.dot(sel.astype(BF16), tri_ref[...], preferred_element_type=F32)
    rank = before + csum - sel
    rk_ref[...] = jnp.concatenate(
        [jnp.sum(jnp.where(h, rank, 0.0), axis=0, keepdims=True) for h in hits], axis=0).astype(jnp.int32)
    cnt_ref[...] = jnp.broadcast_to(before + csum[:, T - 1:T], cnt_ref.shape)


def _router(logits_t, router_bias, tt):
    E, N = logits_t.shape
    tri = jnp.triu(jnp.ones((tt, tt), BF16))
    per_tok = pl.BlockSpec((TOP_K, tt), lambda i: (0, i))
    return pl.pallas_call(
        _router_kernel,
        grid=(N // tt,),
        in_specs=[pl.BlockSpec((E, tt), lambda i: (0, i)), pl.BlockSpec((E, 1), lambda i: (0, 0)),
                  pl.BlockSpec((tt, tt), lambda i: (0, 0))],
        out_specs=[per_tok, pl.BlockSpec((tt, 128), lambda i: (i, 0)), per_tok,
                   pl.BlockSpec((E, 128), lambda i: (0, 0))],
        out_shape=[jax.ShapeDtypeStruct((TOP_K, N), jnp.int32), jax.ShapeDtypeStruct((N, 128), F32),
                   jax.ShapeDtypeStruct((TOP_K, N), jnp.int32), jax.ShapeDtypeStruct((E, 128), F32)],
        compiler_params=_cparams(("arbitrary",)),
        name="router",
    )(logits_t, router_bias.reshape(E, 1).astype(F32), tri)


MOE_ROWS = 512
MOE_TOKENS = 256


def _plan_kernel(ei_ref, rk_ref, ps_ref, pos_ref):
    ei = ei_ref[...]
    ps = ps_ref[...]
    rows = lax.broadcasted_iota(jnp.int32, (ps.shape[0], ei.shape[1]), 0)
    start = [jnp.sum(jnp.where(rows == ei[k:k + 1], ps, 0.0), axis=0, keepdims=True) for k in range(ei.shape[0])]
    pos_ref[...] = rk_ref[...] + jnp.concatenate(start, axis=0).astype(jnp.int32)


def _dispatch_plan(eidx, rank, counts, bm, tm):
    K, N = eidx.shape
    E = N_EXPERTS
    cnt = counts[:, 0].astype(jnp.int32)
    padded = (cnt + bm - 1) // bm * bm
    pad_end = jnp.cumsum(padded)
    pad_start = pad_end - padded
    nb = -(-(N * K + E * (bm - 1)) // bm)
    tt = _tile(N, 2048)
    per_tok = pl.BlockSpec((K, tt), lambda i: (0, i))
    pos = pl.pallas_call(
        _plan_kernel,
        grid=(N // tt,),
        in_specs=[per_tok, per_tok, pl.BlockSpec((E, 1), lambda i: (0, 0))],
        out_specs=per_tok,
        out_shape=jax.ShapeDtypeStruct((K, N), jnp.int32),
        compiler_params=_cparams(("parallel",)),
        name="moe_plan",
    )(eidx, rank, pad_start.astype(F32).reshape(E, 1))
    blk_start = jnp.arange(nb, dtype=jnp.int32) * bm
    blk_e = jnp.minimum(jnp.sum(blk_start[:, None] >= pad_end[None, :], axis=1), E - 1)
    n_pad = nb * bm - N * K
    assert n_pad % (K * tm) == 0
    cpad = jnp.cumsum(padded - cnt)
    q = jnp.arange(n_pad, dtype=jnp.int32)
    seg = jnp.sum(q[:, None] >= cpad[None, :], axis=1)
    onehot = seg[:, None] == jnp.arange(E + 1, dtype=jnp.int32)[None, :]
    first = jnp.concatenate([pad_start + cnt, pad_end[-1:]])
    skipped = jnp.concatenate([jnp.zeros((1,), cpad.dtype), cpad])
    pad_pos = (q + jnp.sum(jnp.where(onehot, (first - skipped)[None, :], 0), axis=1)).astype(jnp.int32)
    nt = N // tm
    pos3 = pos.reshape(K, nt, tm).transpose(1, 0, 2).reshape(nt, 1, K * tm)
    return pos3, pad_pos.reshape(-1, 1, K * tm), blk_e.astype(jnp.int32), nb


TILE_ROWS = 4
_HI16 = 0xFFFF0000


def _bf16_bits(v):
    return lax.bitcast_convert_type(v.astype(BF16).astype(F32), U32)


def _to_tiles(ref, val):
    m, half = val.shape[0], val.shape[1] // 2
    words = (lax.shift_right_logical(_bf16_bits(val[:, :half]), jnp.uint32(16))
             | (_bf16_bits(val[:, half:]) & jnp.uint32(_HI16)))
    for j in range(TILE_ROWS):
        ref[pl.ds(j, m, stride=TILE_ROWS), :] = words[:, j * 128:(j + 1) * 128]


def _from_tiles(ref, start, m):
    words = jnp.concatenate([ref[pl.ds(start + j, m, stride=TILE_ROWS), :] for j in range(TILE_ROWS)], axis=1)
    lo = lax.bitcast_convert_type(lax.shift_left(words, jnp.uint32(16)), F32)
    hi = lax.bitcast_convert_type(words & jnp.uint32(_HI16), F32)
    return jnp.concatenate([lo, hi], axis=1)


def _row_copy(idx_ref, r, src_hbm, buf, base, sem):
    src = pl.multiple_of(idx_ref[0, r] * TILE_ROWS, TILE_ROWS)
    dst = base + r * TILE_ROWS
    if not isinstance(dst, int):
        dst = pl.multiple_of(dst, TILE_ROWS)
    return pltpu.make_async_copy(src_hbm.at[pl.ds(src, TILE_ROWS), :], buf.at[pl.ds(dst, TILE_ROWS), :], sem)


def _pipelined_gather(cur_ref, nxt_ref, src_hbm, buf, sem, nrows, consume):
    i = pl.program_id(0)
    span = nrows * TILE_ROWS

    def wait(slot):
        pltpu.make_async_copy(src_hbm.at[pl.ds(0, span), :], buf.at[pl.ds(slot * span, span), :], sem.at[slot]).wait()

    @pl.when(i == 0)
    def _():
        def body(r, carry):
            _row_copy(cur_ref, r, src_hbm, buf, 0, sem.at[0]).start()
            return carry
        lax.fori_loop(0, nrows, body, 0)

    for slot in range(2):
        @pl.when(i % 2 == slot)
        def _():
            for r in range(nrows):
                _row_copy(nxt_ref, r, src_hbm, buf, (1 - slot) * span, sem.at[1 - slot]).start(priority=r % 2)
            wait(slot)
            consume(slot * span)

            @pl.when(i == pl.num_programs(0) - 1)
            def _():
                wait(1 - slot)


def _dispatch_kernel(pos_ref, hx_ref, xs_hbm, stage, zero_tile, sem, *, tm, nt):
    i = pl.program_id(0)
    slot = i % 2
    span = tm * TILE_ROWS

    def wait_step(s):
        for _ in range(TOP_K):
            pltpu.make_async_copy(stage.at[pl.ds(0, span), :], xs_hbm.at[pl.ds(0, span), :], sem.at[s]).wait()

    def dst(r):
        return xs_hbm.at[pl.ds(pl.multiple_of(pos_ref[0, r] * TILE_ROWS, TILE_ROWS), TILE_ROWS), :]

    @pl.when(i < nt)
    def _():
        base = pl.multiple_of(slot * span, span)
        stage[pl.ds(base, span), :] = hx_ref[...]
        for r in range(TOP_K * tm):
            src = stage.at[pl.ds(pl.multiple_of(base + (r % tm) * TILE_ROWS, TILE_ROWS), TILE_ROWS), :]
            pltpu.make_async_copy(src, dst(r), sem.at[slot]).start(priority=r % 2)

    @pl.when(i == 0)
    def _():
        zero_tile[...] = jnp.zeros_like(zero_tile)

    @pl.when(i >= nt)
    def _():
        for r in range(TOP_K * tm):
            pltpu.make_async_copy(zero_tile, dst(r), sem.at[slot]).start(priority=r % 2)

    @pl.when(i > 0)
    def _():
        wait_step(1 - slot)

    @pl.when(i == pl.num_programs(0) - 1)
    def _():
        wait_step(slot)


def _dispatch(pos3, pad_pos, hx_tiles, rows_total, tm):
    nt = pos3.shape[0]
    idx = jnp.concatenate([pos3, pad_pos], axis=0)
    steps = idx.shape[0]
    return pl.pallas_call(
        functools.partial(_dispatch_kernel, tm=tm, nt=nt),
        grid=(steps,),
        in_specs=[pl.BlockSpec((None, 1, TOP_K * tm), lambda i: (i, 0, 0), memory_space=pltpu.SMEM),
                  pl.BlockSpec((tm * TILE_ROWS, 128), lambda i: (jnp.minimum(i, nt - 1), 0))],
        out_specs=pl.BlockSpec(memory_space=pl.ANY),
        out_shape=jax.ShapeDtypeStruct((rows_total * TILE_ROWS, 128), U32),
        scratch_shapes=[pltpu.VMEM((2 * tm * TILE_ROWS, 128), U32), pltpu.VMEM((TILE_ROWS, 128), U32),
                        pltpu.SemaphoreType.DMA((2,))],
        compiler_params=_cparams(("arbitrary",)),
        name="moe_dispatch",
    )(idx, hx_tiles)


def _expert_kernel(blk_e_ref, xs_ref, wg_ref, wu_ref, wd_ref, y_ref, *, bm):
    del blk_e_ref
    x = _from_tiles(xs_ref, 0, bm).astype(BF16)
    hg = jnp.dot(x, wg_ref[...].astype(BF16), preferred_element_type=F32)
    hu = jnp.dot(x, wu_ref[...].astype(BF16), preferred_element_type=F32)
    act = hg * _sigmoid(hg) * hu
    _to_tiles(y_ref, jnp.dot(act.astype(BF16), wd_ref[...].astype(BF16), preferred_element_type=F32))


def _experts(xs_tiles, blk_e, wg, wu, wd, nb, bm):
    D = wg.shape[1]
    grid_spec = pltpu.PrefetchScalarGridSpec(
        num_scalar_prefetch=1,
        grid=(nb,),
        in_specs=[pl.BlockSpec((bm * TILE_ROWS, 128), lambda i, be: (i, 0)),
                  pl.BlockSpec((None, D, EXPERT_HIDDEN), lambda i, be: (be[i], 0, 0)),
                  pl.BlockSpec((None, D, EXPERT_HIDDEN), lambda i, be: (be[i], 0, 0)),
                  pl.BlockSpec((None, EXPERT_HIDDEN, D), lambda i, be: (be[i], 0, 0))],
        out_specs=pl.BlockSpec((bm * TILE_ROWS, 128), lambda i, be: (i, 0)),
    )
    return pl.pallas_call(
        functools.partial(_expert_kernel, bm=bm),
        grid_spec=grid_spec,
        out_shape=jax.ShapeDtypeStruct((nb * bm * TILE_ROWS, 128), U32),
        compiler_params=_cparams(("arbitrary",)),
        name="moe_experts",
    )(blk_e, xs_tiles, wg, wu, wd)


def _combine_kernel(cur_ref, nxt_ref, y_hbm, hx_ref, wt_ref, sgu_ref, sd_ref, x1_ref, g2_ref, fg_ref, o_ref, ybuf,
                    sem, *, tm):
    def consume(here):
        wt = wt_ref[...]
        routed = None
        for k in range(TOP_K):
            wk = jnp.broadcast_to(wt[:, k:k + 1], (tm, 128))
            wk = jnp.concatenate([wk] * (x1_ref.shape[1] // 128), axis=1)
            term = wk * _from_tiles(ybuf, here + k * tm * TILE_ROWS, tm)
            routed = term if routed is None else routed + term
        hu = jnp.dot(_from_tiles(hx_ref, 0, tm).astype(BF16), sgu_ref[...], preferred_element_type=F32)
        hg = hu[:, :EXPERT_HIDDEN]
        act = hg * _sigmoid(hg) * hu[:, EXPERT_HIDDEN:]
        shared = jnp.dot(act.astype(BF16), sd_ref[...], preferred_element_type=F32)
        xo = x1_ref[...] + g2_ref[...] * (routed + shared)
        o_ref[...] = _rms(xo) * fg_ref[...]

    _pipelined_gather(cur_ref, nxt_ref, y_hbm, ybuf, sem, TOP_K * tm, consume)


def _combine(pos3, y_tiles, hx_tiles, w_tok, sgu, sd, x1, g2, fg, tm, tiles_per_batch):
    N, D = x1.shape
    nt = N // tm
    smem_blk = lambda f: pl.BlockSpec((None, 1, TOP_K * tm), f, memory_space=pltpu.SMEM)
    row = lambda i: (i, 0)
    const = lambda i: (0, 0)
    return pl.pallas_call(
        functools.partial(_combine_kernel, tm=tm),
        grid=(nt,),
        in_specs=[smem_blk(lambda i: (i, 0, 0)),
                  smem_blk(lambda i: (jnp.minimum(i + 1, nt - 1), 0, 0)),
                  pl.BlockSpec(memory_space=pl.ANY),
                  pl.BlockSpec((tm * TILE_ROWS, 128), row),
                  pl.BlockSpec((tm, 128), row),
                  pl.BlockSpec((D, 2 * EXPERT_HIDDEN), const),
                  pl.BlockSpec((EXPERT_HIDDEN, D), const),
                  pl.BlockSpec((tm, D), row),
                  pl.BlockSpec((None, 1, D), lambda i: (i // tiles_per_batch, 0, 0)),
                  pl.BlockSpec((1, D), const)],
        out_specs=pl.BlockSpec((tm, D), row),
        out_shape=jax.ShapeDtypeStruct((N, D), F32),
        scratch_shapes=[pltpu.VMEM((2 * TOP_K * tm * TILE_ROWS, 128), U32), pltpu.SemaphoreType.DMA((2,))],
        compiler_params=_cparams(("arbitrary",)),
        name="moe_combine",
    )(pos3, pos3, y_tiles, hx_tiles, w_tok, sgu, sd, x1, g2, fg)


def _rope_tables(L):
    rows = L // GRID_W
    row = jnp.repeat(jnp.arange(rows), GRID_W).astype(F32)
    col = jnp.tile(jnp.arange(GRID_W), rows).astype(F32)
    inv = ROPE_THETA ** (-jnp.arange(0, ROPE_AXIS_DIM, 2, dtype=F32) / ROPE_AXIS_DIM)
    ang = jnp.stack([row[:, None] * inv, col[:, None] * inv], axis=1)
    c, s = jnp.cos(ang), jnp.sin(ang)
    cos_h = jnp.stack([c, c], axis=2).reshape(L, HEAD_DIM)
    sin_h = jnp.stack([-s, s], axis=2).reshape(L, HEAD_DIM)
    return jnp.tile(cos_h, (1, 2)), jnp.tile(sin_h, (1, 2))


def _tile(n, pref):
    t = min(pref, n)
    while n % t:
        t //= 2
    return t


def kernel(x, c, ctx, c_ctx, ada_w, ada_b, norm1_g, norm2_g, w_in, lam_q1, lam_k1, lam_q2, lam_k2, subln_g, hy_conv_w, hy_conv_b, filt_w1, filt_b1, filt_freq, filt_w2, filt_b2, filt_w3, hy_bias, w_branch_attn, w_branch_hyena, w_out, router_w, router_bias, exp_w_gate, exp_w_up, exp_w_down, shared_w_gate, shared_w_up, shared_w_down, final_norm_g):
    B, L, D = x.shape
    assert ada_w.shape[0] == 1 and B % 2 == 0 and (2 * L) % (FFT_SLABS * FFT_STRIP) == 0
    lam = (jnp.exp(jnp.sum(lam_q1[0].astype(F32) * lam_k1[0].astype(F32)))
           - jnp.exp(jnp.sum(lam_q2[0].astype(F32) * lam_k2[0].astype(F32))) + LAM_INIT).reshape(1, 1)

    cc = jnp.zeros((16, D), F32).at[:B].set(c).at[B].set(c_ctx)
    mod = _adaln(cc, ada_w[0], ada_b[0])
    chunk = lambda r0, r1, j: mod[r0:r1, j * D:(j + 1) * D]
    sh1, sc1, g1, sh2, sc2, g2 = (chunk(0, B, j).reshape(B, 1, D) for j in range(6))
    csh, csc = chunk(B, B + 1, 0), chunk(B, B + 1, 1)

    w_bf = w_in[0].astype(BF16)
    n1g = norm1_g[0].reshape(1, D)
    cos_t, sin_t = _rope_tables(L)
    k, v, q, u, gates = _inproj(x, sh1, sc1, n1g, w_bf, cos_t, sin_t, hy_conv_w[0], hy_conv_b[0].reshape(1, -1),
                                _tile(L, 512))
    kc, vc = _ctxproj(ctx, csh, csc, n1g, w_bf)
    attn = _attention(lam, q, k, v, kc, vc, subln_g[0].reshape(1, V_DIM), _tile(L, 4096), _tile(L, 512))

    filt = (filt_w1[0], filt_b1[0], filt_freq[0], filt_w2[0], filt_b2[0], filt_w3[0])
    hy = _hyena(u, filt, hy_bias[0])

    x1, hx2, logits_t = _merge(attn, hy, gates, x, g1, sh2, sc2, norm2_g[0].reshape(1, D),
                               w_branch_attn[0].astype(BF16), w_branch_hyena[0].astype(BF16),
                               w_out[0].astype(BF16), router_w[0].T.astype(F32), _tile(L, 1024))
    eidx, w_tok, rank, counts = _router(logits_t, router_bias[0], _tile(B * L, 1024))

    N = B * L
    tm = _tile(L, MOE_TOKENS)
    pos3, pad_pos, blk_e, nb = _dispatch_plan(eidx, rank, counts, MOE_ROWS, tm)
    xs = _dispatch(pos3, pad_pos, hx2, nb * MOE_ROWS, tm)
    y = _experts(xs, blk_e, exp_w_gate[0], exp_w_up[0], exp_w_down[0], nb, MOE_ROWS)
    sgu = jnp.concatenate([shared_w_gate[0], shared_w_up[0]], axis=-1).astype(BF16)
    out = _combine(pos3, y, hx2, w_tok, sgu, shared_w_down[0].astype(BF16), x1.reshape(N, D), g2,
                   final_norm_g.reshape(1, D), tm, L // tm)
    return out.reshape(B, L, D)
```
